```python
import math
import jax, jax.numpy as jnp
from jax import lax
import numpy as np

D_MODEL = 2048
BATCH = 16
SEQ = 2048
DEPTH = 4

D_MIX = D_MODEL
ATT_WIDTH = D_MIX // 2
SSM_WIDTH = D_MIX - ATT_WIDTH
ATT_HEAD_DIM = 128
ATT_HEADS = ATT_WIDTH // ATT_HEAD_DIM
ATT_BLOCK = 128
SSM_HEAD_DIM = 64
SSM_HEADS = SSM_WIDTH // SSM_HEAD_DIM
SSM_GROUPS = 2
SSM_HEADS_PER_GROUP = SSM_HEADS // SSM_GROUPS
SSM_STATE = 128
CONV_WIDTH = 4
CONV_DIM = SSM_WIDTH + 2 * SSM_GROUPS * SSM_STATE
SSD_CHUNK = 128
D_FF = -(-8 * D_MODEL // (3 * 256)) * 256
IN_DIM = 3 * ATT_WIDTH + SSM_WIDTH + CONV_DIM + SSM_HEADS
EPS = 1e-6

kernel_name = 'hymba_stickbreaking_ssd_swiglu'


def rmsnorm(x, g):
    xf = x.astype(jnp.float32)
    y = xf * lax.rsqrt(jnp.mean(xf * xf, axis=-1, keepdims=True) + EPS)
    return (y * g.astype(jnp.float32)).astype(x.dtype)


def stick_breaking_attention(q, k, v):
    bsz, s, h, dh = q.shape
    nb = s // ATT_BLOCK
    scale = dh ** -0.5
    qf = q.astype(jnp.float32).transpose(0, 2, 1, 3)
    kf = k.astype(jnp.float32).transpose(0, 2, 1, 3)
    vf = v.astype(jnp.float32).transpose(0, 2, 1, 3)
    q_blocks = qf.reshape(bsz, h, nb, ATT_BLOCK, dh).transpose(2, 0, 1, 3, 4)
    key_pos = jnp.arange(s)

    def one_block(args):
        qb, blk = args
        z = jnp.einsum('bhqd,bhkd->bhqk', qb, kf) * scale
        query_pos = blk * ATT_BLOCK + jnp.arange(ATT_BLOCK)
        mask = key_pos[None, :] < query_pos[:, None]
        log_beta = jax.nn.log_sigmoid(z)
        log_remain = jnp.where(mask, log_beta - z, 0.0)
        later = lax.cumsum(log_remain, axis=3, reverse=True) - log_remain
        weights = jnp.where(mask, jnp.exp(log_beta + later), 0.0)
        return jnp.einsum('bhqk,bhkd->bhqd', weights, vf)

    out = lax.map(one_block, (q_blocks, jnp.arange(nb)))
    out = out.transpose(1, 0, 3, 2, 4).reshape(bsz, s, h, dh)
    return out.astype(q.dtype)


def causal_depthwise_conv(u, w, bias):
    s = u.shape[1]
    up = jnp.pad(u, ((0, 0), (CONV_WIDTH - 1, 0), (0, 0)))
    out = bias
    for i in range(CONV_WIDTH):
        out = out + up[:, i:i + s] * w[i]
    return out


def ssd_chunked(x, dt, a, b_in, c_in, d_skip):
    bsz, s, h, p = x.shape
    nc = s // SSD_CHUNK
    g, hg, n = SSM_GROUPS, SSM_HEADS_PER_GROUP, SSM_STATE
    xf = x.astype(jnp.float32)
    dtf = dt.astype(jnp.float32)
    xdt = (xf * dtf[..., None]).reshape(bsz, nc, SSD_CHUNK, g, hg, p)
    bc = b_in.astype(jnp.float32).reshape(bsz, nc, SSD_CHUNK, g, n)
    cc = c_in.astype(jnp.float32).reshape(bsz, nc, SSD_CHUNK, g, n)
    da = (dtf * a.astype(jnp.float32)).reshape(bsz, nc, SSD_CHUNK, g, hg)
    a_cum = jnp.cumsum(da, axis=2)
    pos = jnp.arange(SSD_CHUNK)
    causal = (pos[:, None] >= pos[None, :])[:, :, None, None]
    seg = a_cum[:, :, :, None] - a_cum[:, :, None, :]
    decay = jnp.exp(jnp.where(causal, seg, -jnp.inf))
    cb = jnp.einsum('bclgn,bcsgn->bclsg', cc, bc)
    y_diag = jnp.einsum('bclsg,bclsgi,bcsgip->bclgip', cb, decay, xdt)
    decay_to_end = jnp.exp(a_cum[:, :, -1:] - a_cum)
    states = jnp.einsum('bcsgn,bcsgi,bcsgip->bcgipn', bc, decay_to_end, xdt)
    chunk_decay = jnp.exp(a_cum[:, :, -1])

    def step(h_prev, inp):
        st, dec = inp
        return h_prev * dec[..., None, None] + st, h_prev

    h0 = jnp.zeros((bsz, g, hg, p, n), jnp.float32)
    _, h_in = lax.scan(step, h0, (states.transpose(1, 0, 2, 3, 4, 5), chunk_decay.transpose(1, 0, 2, 3)))
    h_in = h_in.transpose(1, 0, 2, 3, 4, 5)
    y_off = jnp.einsum('bclgn,bcgipn,bclgi->bclgip', cc, h_in, jnp.exp(a_cum))
    y = (y_diag + y_off).reshape(bsz, s, h, p) + xf * d_skip.astype(jnp.float32)[:, None]
    return y


def hybrid_layer(x, norm_mix, w_in, q_gain, k_gain, conv_w, conv_b, dt_bias, a_log, d_skip,
                 attn_out_gain, ssm_out_gain, w_out, norm_ffn, w_gate, w_up, w_down):
    bsz, s, _ = x.shape
    h = rmsnorm(x, norm_mix)
    proj = h @ w_in
    splits = [ATT_WIDTH, 2 * ATT_WIDTH, 3 * ATT_WIDTH, 3 * ATT_WIDTH + SSM_WIDTH,
              3 * ATT_WIDTH + SSM_WIDTH + CONV_DIM]
    q, k, v, z, xbc, dt = jnp.split(proj, splits, axis=-1)

    q = rmsnorm(q.reshape(bsz, s, ATT_HEADS, ATT_HEAD_DIM), q_gain)
    k = rmsnorm(k.reshape(bsz, s, ATT_HEADS, ATT_HEAD_DIM), k_gain)
    v = v.reshape(bsz, s, ATT_HEADS, ATT_HEAD_DIM)
    o_att = stick_breaking_attention(q, k, v).reshape(bsz, s, ATT_WIDTH)
    o_att = rmsnorm(o_att, attn_out_gain)

    xbc = jax.nn.silu(causal_depthwise_conv(xbc, conv_w, conv_b))
    xs, bm, cm = jnp.split(xbc, [SSM_WIDTH, SSM_WIDTH + SSM_GROUPS * SSM_STATE], axis=-1)
    dt = jax.nn.softplus(dt.astype(jnp.float32) + dt_bias.astype(jnp.float32))
    a = -jnp.exp(a_log.astype(jnp.float32))
    y = ssd_chunked(xs.reshape(bsz, s, SSM_HEADS, SSM_HEAD_DIM), dt, a,
                    bm.reshape(bsz, s, SSM_GROUPS, SSM_STATE),
                    cm.reshape(bsz, s, SSM_GROUPS, SSM_STATE), d_skip)
    y = y.reshape(bsz, s, SSM_WIDTH) * jax.nn.silu(z.astype(jnp.float32))
    yg = y.reshape(bsz, s, SSM_GROUPS, SSM_WIDTH // SSM_GROUPS)
    yg = yg * lax.rsqrt(jnp.mean(yg * yg, axis=-1, keepdims=True) + EPS)
    o_ssm = (yg.reshape(bsz, s, SSM_WIDTH) * ssm_out_gain.astype(jnp.float32)).astype(x.dtype)

    x = x + jnp.concatenate([o_att, o_ssm], axis=-1) @ w_out

    h = rmsnorm(x, norm_ffn)
    x = x + (jax.nn.silu(h @ w_gate) * (h @ w_up)) @ w_down
    return x


def _fwd_setup_inputs(seed: int = 0) -> dict:
    key = jax.random.key(seed)
    ks = jax.random.split(key, 18)
    f32 = jnp.float32

    def normal(k, shape, scale):
        return jax.random.normal(k, shape, f32) * scale

    x = normal(ks[0], (BATCH, SEQ, D_MODEL), 1.0)
    norm_mix = 1.0 + normal(ks[1], (DEPTH, D_MODEL), 0.02)
    w_in = normal(ks[2], (DEPTH, D_MODEL, IN_DIM), D_MODEL ** -0.5)
    q_gain = 1.0 + normal(ks[3], (DEPTH, ATT_HEAD_DIM), 0.02)
    k_gain = 1.0 + normal(ks[4], (DEPTH, ATT_HEAD_DIM), 0.02)
    conv_w = normal(ks[5], (DEPTH, CONV_WIDTH, CONV_DIM), CONV_WIDTH ** -0.5)
    conv_b = normal(ks[6], (DEPTH, CONV_DIM), 0.01)
    dt0 = jnp.exp(jax.random.uniform(ks[7], (DEPTH, SSM_HEADS), f32, math.log(1e-3), math.log(1e-1)))
    dt_bias = dt0 + jnp.log(-jnp.expm1(-dt0))
    a_log = jnp.log(jax.random.uniform(ks[8], (DEPTH, SSM_HEADS), f32, 1.0, 16.0))
    d_skip = 1.0 + normal(ks[9], (DEPTH, SSM_HEADS), 0.02)
    attn_out_gain = 1.0 + normal(ks[10], (DEPTH, ATT_WIDTH), 0.02)
    ssm_out_gain = 1.0 + normal(ks[11], (DEPTH, SSM_WIDTH), 0.02)
    w_out = normal(ks[12], (DEPTH, D_MIX, D_MODEL), D_MIX ** -0.5)
    norm_ffn = 1.0 + normal(ks[13], (DEPTH, D_MODEL), 0.02)
    w_gate = normal(ks[14], (DEPTH, D_MODEL, D_FF), D_MODEL ** -0.5)
    w_up = normal(ks[15], (DEPTH, D_MODEL, D_FF), D_MODEL ** -0.5)
    w_down = normal(ks[16], (DEPTH, D_FF, D_MODEL), D_FF ** -0.5)
    return {'x': x, 'norm_mix': norm_mix, 'w_in': w_in, 'q_gain': q_gain, 'k_gain': k_gain,
            'conv_w': conv_w, 'conv_b': conv_b, 'dt_bias': dt_bias, 'a_log': a_log, 'd_skip': d_skip,
            'attn_out_gain': attn_out_gain, 'ssm_out_gain': ssm_out_gain, 'w_out': w_out,
            'norm_ffn': norm_ffn, 'w_gate': w_gate, 'w_up': w_up, 'w_down': w_down}


def _fwd_reference(x, norm_mix, w_in, q_gain, k_gain, conv_w, conv_b, dt_bias, a_log, d_skip,
              attn_out_gain, ssm_out_gain, w_out, norm_ffn, w_gate, w_up, w_down):
    for i in range(DEPTH):
        x = hybrid_layer(x, norm_mix[i], w_in[i], q_gain[i], k_gain[i], conv_w[i], conv_b[i],
                         dt_bias[i], a_log[i], d_skip[i], attn_out_gain[i], ssm_out_gain[i],
                         w_out[i], norm_ffn[i], w_gate[i], w_up[i], w_down[i])
    return x


import jax as _jax
import jax.numpy as _jnp

TWIN_FORMAT = 'train_step'
FWD_PARAMS = ['x', 'norm_mix', 'w_in', 'q_gain', 'k_gain', 'conv_w', 'conv_b', 'dt_bias', 'a_log', 'd_skip', 'attn_out_gain', 'ssm_out_gain', 'w_out', 'norm_ffn', 'w_gate', 'w_up', 'w_down']
TWIN_WEIGHTS = ['norm_mix', 'w_in', 'q_gain', 'k_gain', 'conv_w', 'conv_b', 'dt_bias', 'a_log', 'd_skip', 'attn_out_gain', 'ssm_out_gain', 'w_out', 'norm_ffn', 'w_gate', 'w_up', 'w_down']
TWIN_DIFF_INPUT = 'x'
TWIN_INPUTS = ['x', 'norm_mix', 'w_in', 'q_gain', 'k_gain', 'conv_w', 'conv_b', 'dt_bias', 'a_log', 'd_skip', 'attn_out_gain', 'ssm_out_gain', 'w_out', 'norm_ffn', 'w_gate', 'w_up', 'w_down', 'loss_target', 'm_norm_mix', 'm_w_in', 'm_q_gain', 'm_k_gain', 'm_conv_w', 'm_conv_b', 'm_dt_bias', 'm_a_log', 'm_d_skip', 'm_attn_out_gain', 'm_ssm_out_gain', 'm_w_out', 'm_norm_ffn', 'm_w_gate', 'm_w_up', 'm_w_down', 'v_norm_mix', 'v_w_in', 'v_q_gain', 'v_k_gain', 'v_conv_w', 'v_conv_b', 'v_dt_bias', 'v_a_log', 'v_d_skip', 'v_attn_out_gain', 'v_ssm_out_gain', 'v_w_out', 'v_norm_ffn', 'v_w_gate', 'v_w_up', 'v_w_down']
TWIN_OUTPUTS = ['loss', 'grad_x', 'grad_norm_mix', 'grad_w_in', 'grad_q_gain', 'grad_k_gain', 'grad_conv_w', 'grad_conv_b', 'grad_dt_bias', 'grad_a_log', 'grad_d_skip', 'grad_attn_out_gain', 'grad_ssm_out_gain', 'grad_w_out', 'grad_norm_ffn', 'grad_w_gate', 'grad_w_up', 'grad_w_down', 'delta_norm_mix', 'delta_w_in', 'delta_q_gain', 'delta_k_gain', 'delta_conv_w', 'delta_conv_b', 'delta_dt_bias', 'delta_a_log', 'delta_d_skip', 'delta_attn_out_gain', 'delta_ssm_out_gain', 'delta_w_out', 'delta_norm_ffn', 'delta_w_gate', 'delta_w_up', 'delta_w_down', 'new_m_norm_mix', 'new_m_w_in', 'new_m_q_gain', 'new_m_k_gain', 'new_m_conv_w', 'new_m_conv_b', 'new_m_dt_bias', 'new_m_a_log', 'new_m_d_skip', 'new_m_attn_out_gain', 'new_m_ssm_out_gain', 'new_m_w_out', 'new_m_norm_ffn', 'new_m_w_gate', 'new_m_w_up', 'new_m_w_down', 'new_v_norm_mix', 'new_v_w_in', 'new_v_q_gain', 'new_v_k_gain', 'new_v_conv_w', 'new_v_conv_b', 'new_v_dt_bias', 'new_v_a_log', 'new_v_d_skip', 'new_v_attn_out_gain', 'new_v_ssm_out_gain', 'new_v_w_out', 'new_v_norm_ffn', 'new_v_w_gate', 'new_v_w_up', 'new_v_w_down']
TWIN_LEAF_KINDS = {'loss': 'loss', 'grad_x': 'grad_x', 'grad_norm_mix': 'grad_w', 'grad_w_in': 'grad_w', 'grad_q_gain': 'grad_w', 'grad_k_gain': 'grad_w', 'grad_conv_w': 'grad_w', 'grad_conv_b': 'grad_w', 'grad_dt_bias': 'grad_w', 'grad_a_log': 'grad_w', 'grad_d_skip': 'grad_w', 'grad_attn_out_gain': 'grad_w', 'grad_ssm_out_gain': 'grad_w', 'grad_w_out': 'grad_w', 'grad_norm_ffn': 'grad_w', 'grad_w_gate': 'grad_w', 'grad_w_up': 'grad_w', 'grad_w_down': 'grad_w', 'delta_norm_mix': 'delta_w', 'delta_w_in': 'delta_w', 'delta_q_gain': 'delta_w', 'delta_k_gain': 'delta_w', 'delta_conv_w': 'delta_w', 'delta_conv_b': 'delta_w', 'delta_dt_bias': 'delta_w', 'delta_a_log': 'delta_w', 'delta_d_skip': 'delta_w', 'delta_attn_out_gain': 'delta_w', 'delta_ssm_out_gain': 'delta_w', 'delta_w_out': 'delta_w', 'delta_norm_ffn': 'delta_w', 'delta_w_gate': 'delta_w', 'delta_w_up': 'delta_w', 'delta_w_down': 'delta_w', 'new_m_norm_mix': 'new_m', 'new_m_w_in': 'new_m', 'new_m_q_gain': 'new_m', 'new_m_k_gain': 'new_m', 'new_m_conv_w': 'new_m', 'new_m_conv_b': 'new_m', 'new_m_dt_bias': 'new_m', 'new_m_a_log': 'new_m', 'new_m_d_skip': 'new_m', 'new_m_attn_out_gain': 'new_m', 'new_m_ssm_out_gain': 'new_m', 'new_m_w_out': 'new_m', 'new_m_norm_ffn': 'new_m', 'new_m_w_gate': 'new_m', 'new_m_w_up': 'new_m', 'new_m_w_down': 'new_m', 'new_v_norm_mix': 'new_v', 'new_v_w_in': 'new_v', 'new_v_q_gain': 'new_v', 'new_v_k_gain': 'new_v', 'new_v_conv_w': 'new_v', 'new_v_conv_b': 'new_v', 'new_v_dt_bias': 'new_v', 'new_v_a_log': 'new_v', 'new_v_d_skip': 'new_v', 'new_v_attn_out_gain': 'new_v', 'new_v_ssm_out_gain': 'new_v', 'new_v_w_out': 'new_v', 'new_v_norm_ffn': 'new_v', 'new_v_w_gate': 'new_v', 'new_v_w_up': 'new_v', 'new_v_w_down': 'new_v'}


def _forward(args):
    return _fwd_reference(*[args[k] for k in FWD_PARAMS])


def _output_shape():
    out = _jax.eval_shape(lambda: _forward(_fwd_setup_inputs(0)))
    return out.shape, out.dtype

N_MICROBATCH = 1
ADAM_LR = 0.001
ADAM_B1 = 0.9
ADAM_B2 = 0.999
ADAM_EPS = 1e-08
ADAM_WD = 0.01
ADAM_STEP = 10
PER_EXAMPLE_BATCH_AXIS = {'x': 0, 'loss_target': 0}
SHARED_INPUTS = []
_WEIGHT_DTYPES = {'norm_mix': _jnp.float32, 'w_in': _jnp.float32, 'q_gain': _jnp.float32, 'k_gain': _jnp.float32, 'conv_w': _jnp.float32, 'conv_b': _jnp.float32, 'dt_bias': _jnp.float32, 'a_log': _jnp.float32, 'd_skip': _jnp.float32, 'attn_out_gain': _jnp.float32, 'ssm_out_gain': _jnp.float32, 'w_out': _jnp.float32, 'norm_ffn': _jnp.float32, 'w_gate': _jnp.float32, 'w_up': _jnp.float32, 'w_down': _jnp.float32}
MOMENT_SCALE = {'norm_mix': 8.009192e-01, 'w_in': 4.710609e-01, 'q_gain': 5.521855e-01, 'k_gain': 5.501767e-01, 'conv_w': 7.356244e-01, 'conv_b': 2.453945e+00, 'dt_bias': 1.143116e+00, 'a_log': 7.850542e+00, 'd_skip': 5.353337e+00, 'attn_out_gain': 1.592762e+01, 'ssm_out_gain': 2.327745e+01, 'w_out': 1.050192e+00, 'norm_ffn': 1.229214e+01, 'w_gate': 2.014523e-01, 'w_up': 1.822202e-01, 'w_down': 2.949428e-01}


def _to_microbatches(a, axis):
    t = _jnp.moveaxis(a, axis, 0)
    t = t.reshape((N_MICROBATCH, t.shape[0] // N_MICROBATCH) + t.shape[1:])
    return _jnp.moveaxis(t, 1, axis + 1)


def setup_inputs(seed: int = 0) -> dict:
    inp = _fwd_setup_inputs(seed)
    key = _jax.random.fold_in(_jax.random.key(seed), 7919)
    shape, _ = _output_shape()
    out = dict(inp)
    out["loss_target"] = _jax.random.normal(_jax.random.fold_in(key, 0), shape, _jnp.float32)
    for i, name in enumerate(TWIN_WEIGHTS):
        w = inp[name].astype(_jnp.float32)
        if MOMENT_SCALE is None:
            s = _jnp.sqrt(_jnp.mean(_jnp.square(w)) + 1e-30)
        else:
            s = MOMENT_SCALE[name]
        km, kv = _jax.random.split(_jax.random.fold_in(key, i + 1))
        out[name] = w
        out["m_" + name] = s * _jax.random.normal(km, w.shape, _jnp.float32)
        out["v_" + name] = (s * s) * _jax.random.uniform(kv, w.shape, _jnp.float32, 0.5, 1.5)
    if N_MICROBATCH > 1:
        for name, axis in PER_EXAMPLE_BATCH_AXIS.items():
            out[name] = _to_microbatches(out[name], axis)
    return {'x': out['x'], 'norm_mix': out['norm_mix'], 'w_in': out['w_in'], 'q_gain': out['q_gain'], 'k_gain': out['k_gain'], 'conv_w': out['conv_w'], 'conv_b': out['conv_b'], 'dt_bias': out['dt_bias'], 'a_log': out['a_log'], 'd_skip': out['d_skip'], 'attn_out_gain': out['attn_out_gain'], 'ssm_out_gain': out['ssm_out_gain'], 'w_out': out['w_out'], 'norm_ffn': out['norm_ffn'], 'w_gate': out['w_gate'], 'w_up': out['w_up'], 'w_down': out['w_down'], 'loss_target': out['loss_target'], 'm_norm_mix': out['m_norm_mix'], 'm_w_in': out['m_w_in'], 'm_q_gain': out['m_q_gain'], 'm_k_gain': out['m_k_gain'], 'm_conv_w': out['m_conv_w'], 'm_conv_b': out['m_conv_b'], 'm_dt_bias': out['m_dt_bias'], 'm_a_log': out['m_a_log'], 'm_d_skip': out['m_d_skip'], 'm_attn_out_gain': out['m_attn_out_gain'], 'm_ssm_out_gain': out['m_ssm_out_gain'], 'm_w_out': out['m_w_out'], 'm_norm_ffn': out['m_norm_ffn'], 'm_w_gate': out['m_w_gate'], 'm_w_up': out['m_w_up'], 'm_w_down': out['m_w_down'], 'v_norm_mix': out['v_norm_mix'], 'v_w_in': out['v_w_in'], 'v_q_gain': out['v_q_gain'], 'v_k_gain': out['v_k_gain'], 'v_conv_w': out['v_conv_w'], 'v_conv_b': out['v_conv_b'], 'v_dt_bias': out['v_dt_bias'], 'v_a_log': out['v_a_log'], 'v_d_skip': out['v_d_skip'], 'v_attn_out_gain': out['v_attn_out_gain'], 'v_ssm_out_gain': out['v_ssm_out_gain'], 'v_w_out': out['v_w_out'], 'v_norm_ffn': out['v_norm_ffn'], 'v_w_gate': out['v_w_gate'], 'v_w_up': out['v_w_up'], 'v_w_down': out['v_w_down']}


def _loss(weights, diff, rest, loss_target):
    with _jax.named_scope("forward"):
        args = {**rest, TWIN_DIFF_INPUT: diff, **{k: w.astype(_WEIGHT_DTYPES[k]) for k, w in weights.items()}}
        y = _forward(args)
    with _jax.named_scope("loss_head"):
        err = _jnp.square(y.astype(_jnp.float32) - loss_target)
        return 0.5 * _jnp.sum(_jnp.mean(err, axis=-1)) if err.ndim else 0.5 * err


def _adamw(w, g, m, v):
    m = ADAM_B1 * m + (1.0 - ADAM_B1) * g
    v = ADAM_B2 * v + (1.0 - ADAM_B2) * _jnp.square(g)
    m_hat = m / (1.0 - ADAM_B1 ** ADAM_STEP)
    v_hat = v / (1.0 - ADAM_B2 ** ADAM_STEP)
    delta = -ADAM_LR * (m_hat / (_jnp.sqrt(v_hat) + ADAM_EPS) + ADAM_WD * w)
    return delta, m, v


def reference(x, norm_mix, w_in, q_gain, k_gain, conv_w, conv_b, dt_bias, a_log, d_skip, attn_out_gain, ssm_out_gain, w_out, norm_ffn, w_gate, w_up, w_down, loss_target, m_norm_mix, m_w_in, m_q_gain, m_k_gain, m_conv_w, m_conv_b, m_dt_bias, m_a_log, m_d_skip, m_attn_out_gain, m_ssm_out_gain, m_w_out, m_norm_ffn, m_w_gate, m_w_up, m_w_down, v_norm_mix, v_w_in, v_q_gain, v_k_gain, v_conv_w, v_conv_b, v_dt_bias, v_a_log, v_d_skip, v_attn_out_gain, v_ssm_out_gain, v_w_out, v_norm_ffn, v_w_gate, v_w_up, v_w_down):
    given = dict(x=x, norm_mix=norm_mix, w_in=w_in, q_gain=q_gain, k_gain=k_gain, conv_w=conv_w, conv_b=conv_b, dt_bias=dt_bias, a_log=a_log, d_skip=d_skip, attn_out_gain=attn_out_gain, ssm_out_gain=ssm_out_gain, w_out=w_out, norm_ffn=norm_ffn, w_gate=w_gate, w_up=w_up, w_down=w_down, loss_target=loss_target, m_norm_mix=m_norm_mix, m_w_in=m_w_in, m_q_gain=m_q_gain, m_k_gain=m_k_gain, m_conv_w=m_conv_w, m_conv_b=m_conv_b, m_dt_bias=m_dt_bias, m_a_log=m_a_log, m_d_skip=m_d_skip, m_attn_out_gain=m_attn_out_gain, m_ssm_out_gain=m_ssm_out_gain, m_w_out=m_w_out, m_norm_ffn=m_norm_ffn, m_w_gate=m_w_gate, m_w_up=m_w_up, m_w_down=m_w_down, v_norm_mix=v_norm_mix, v_w_in=v_w_in, v_q_gain=v_q_gain, v_k_gain=v_k_gain, v_conv_w=v_conv_w, v_conv_b=v_conv_b, v_dt_bias=v_dt_bias, v_a_log=v_a_log, v_d_skip=v_d_skip, v_attn_out_gain=v_attn_out_gain, v_ssm_out_gain=v_ssm_out_gain, v_w_out=v_w_out, v_norm_ffn=v_norm_ffn, v_w_gate=v_w_gate, v_w_up=v_w_up, v_w_down=v_w_down)
    weights = {n: given[n] for n in TWIN_WEIGHTS}
    shared = {n: given[n] for n in SHARED_INPUTS}
    per_example = {n: given[n] for n in ['x']}
    grad_fn = _jax.value_and_grad(_loss, argnums=(0, 1))

    def one_microbatch(ex, loss_target):
        ex = dict(ex)
        diff = ex.pop(TWIN_DIFF_INPUT)
        return grad_fn(weights, diff, {**shared, **ex}, loss_target)

    if N_MICROBATCH == 1:
        loss, (grad_w, grad_x) = one_microbatch(per_example, given["loss_target"])
    else:
        def body(carry, xs):
            loss_sum, grad_sum = carry
            l_k, (gw_k, gx_k) = one_microbatch(xs[0], xs[1])
            with _jax.named_scope("update"):
                return (loss_sum + l_k, _jax.tree.map(_jnp.add, grad_sum, gw_k)), gx_k

        init = (_jnp.zeros((), _jnp.float32), _jax.tree.map(_jnp.zeros_like, weights))
        (loss, grad_w), grad_x = _jax.lax.scan(body, init, (per_example, given["loss_target"]))
    with _jax.named_scope("update"):
        delta_w, new_m, new_v = {}, {}, {}
        for n in TWIN_WEIGHTS:
            delta_w[n], new_m[n], new_v[n] = _adamw(weights[n], grad_w[n], given["m_" + n], given["v_" + n])
    return (loss, grad_x, *[grad_w[n] for n in TWIN_WEIGHTS], *[delta_w[n] for n in TWIN_WEIGHTS],
            *[new_m[n] for n in TWIN_WEIGHTS], *[new_v[n] for n in TWIN_WEIGHTS])
```

```python
import functools
import math

import jax
import jax.numpy as jnp
from jax import lax
from jax.experimental import pallas as pl
from jax.experimental.pallas import tpu as pltpu

F32 = jnp.float32
BF16 = jnp.bfloat16
MESH = pl.DeviceIdType.MESH

N_DEV = 8
EPS = 1e-6
ATT_HEADS = 8
ATT_DH = 128
ATT_W = ATT_HEADS * ATT_DH
SSM_W = 1024
SSM_P = 64
SSM_N = 128
SSM_GROUPS = 2
SSM_HG = 8
SSM_HEADS = SSM_GROUPS * SSM_HG
CHUNK = 128
CONV_K = 4
CONV_DIM = SSM_W + 2 * SSM_GROUPS * SSM_N
LANE = 128
OFF_Q, OFF_K, OFF_V, OFF_Z, OFF_XS = 0, ATT_W, 2 * ATT_W, 3 * ATT_W, 4 * ATT_W
OFF_B = OFF_XS + SSM_W
OFF_C = OFF_B + SSM_GROUPS * SSM_N
OFF_DT = OFF_C + SSM_GROUPS * SSM_N
NPROJ = 6144
IN_DIM = OFF_DT + SSM_HEADS

ADAM_LR = 0.001
ADAM_B1 = 0.9
ADAM_B2 = 0.999
ADAM_EPS = 1e-08
ADAM_WD = 0.01
ADAM_STEP = 10

VMEM_LIMIT = 56 * 1024 * 1024


def _params(sem=None):
    return pltpu.CompilerParams(dimension_semantics=sem, vmem_limit_bytes=VMEM_LIMIT)


def _pick(dim, target):
    if dim <= target:
        return dim
    best = None
    for t in range(LANE, target + 1, LANE):
        if dim % t == 0:
            best = t
    assert best is not None, (dim, target)
    return best


def _dot(a, b, dims=((1,), (0,))):
    return lax.dot_general(a, b, (dims, ((), ())), preferred_element_type=F32)


def _dot_nt(a, b):
    return _dot(a, b, ((1,), (1,)))


def _dot_tn(a, b):
    return _dot(a, b, ((0,), (0,)))


def _split_dot(x, m, parts=2):
    acc = None
    rem = x
    for _ in range(parts):
        hi = rem.astype(BF16)
        d = _dot(hi, m)
        acc = d if acc is None else acc + d
        rem = rem - hi.astype(F32)
    return acc


def _sigmoid(x):
    return 1.0 / (1.0 + jnp.exp(-x))


def _softplus(x):
    return jnp.maximum(x, 0.0) + jnp.log(1.0 + jnp.exp(-jnp.abs(x)))


def _rstd(x):
    return lax.rsqrt(jnp.mean(x * x, axis=-1, keepdims=True) + EPS)


def _matmul(a, b, mode, out_dtype, name, residual=None, tm=512, tn=1024, tk=2048):
    if mode == "nn":
        (m, k), (k2, n) = a.shape, b.shape
    elif mode == "nt":
        (m, k), (n, k2) = a.shape, b.shape
    else:
        (k, m), (k2, n) = a.shape, b.shape
    assert k == k2, (a.shape, b.shape, mode)
    tm, tn, tk = _pick(m, tm), _pick(n, tn), _pick(k, tk)
    nk = k // tk
    dims = {"nn": ((1,), (0,)), "nt": ((1,), (1,)), "tn": ((0,), (0,))}[mode]
    has_res = residual is not None

    def body(*refs):
        if has_res:
            a_ref, b_ref, r_ref, o_ref = refs[:4]
        else:
            a_ref, b_ref, o_ref = refs[:3]
            r_ref = None
        prod = _dot(a_ref[...], b_ref[...], dims)

        def finish(r):
            if r_ref is not None:
                r = r + r_ref[...]
            o_ref[...] = r.astype(o_ref.dtype)

        if nk == 1:
            finish(prod)
        else:
            acc = refs[-1]
            kk = pl.program_id(2)

            @pl.when(kk == 0)
            def _():
                acc[...] = prod

            @pl.when(kk > 0)
            def _():
                acc[...] += prod

            @pl.when(kk == nk - 1)
            def _():
                finish(acc[...])

    if mode == "tn":
        a_spec = pl.BlockSpec((tk, tm), lambda i, j, kk: (kk, i))
    else:
        a_spec = pl.BlockSpec((tm, tk), lambda i, j, kk: (i, kk))
    if mode == "nt":
        b_spec = pl.BlockSpec((tn, tk), lambda i, j, kk: (j, kk))
    else:
        b_spec = pl.BlockSpec((tk, tn), lambda i, j, kk: (kk, j))
    o_spec = pl.BlockSpec((tm, tn), lambda i, j, kk: (i, j))
    in_specs = [a_spec, b_spec] + ([o_spec] if has_res else [])
    args = (a, b) + ((residual,) if has_res else ())
    return pl.pallas_call(
        body,
        name=name,
        grid=(m // tm, n // tn, nk),
        in_specs=in_specs,
        out_specs=o_spec,
        out_shape=jax.ShapeDtypeStruct((m, n), out_dtype),
        scratch_shapes=[pltpu.VMEM((tm, tn), F32)] if nk > 1 else [],
        compiler_params=_params(("parallel", "parallel", "arbitrary")),
    )(*args)


ROWS = 512


def _rms_fwd(x, g, name):
    t, d = x.shape

    def body(x_ref, g_ref, o_ref):
        xv = x_ref[...]
        o_ref[...] = (xv * _rstd(xv) * g_ref[...]).astype(BF16)

    row = pl.BlockSpec((ROWS, d), lambda i: (i, 0))
    return pl.pallas_call(
        body, name=name, grid=(t // ROWS,),
        in_specs=[row, pl.BlockSpec((1, d), lambda i: (0, 0))],
        out_specs=row, out_shape=jax.ShapeDtypeStruct((t, d), BF16),
        compiler_params=_params(("parallel",)),
    )(x, g)


def _rms_bwd(x, g, dh, dres, name):
    t, d = x.shape

    def body(x_ref, g_ref, dh_ref, dr_ref, dx_ref, dxb_ref, dg_ref):
        xv = x_ref[...]
        r = _rstd(xv)
        xh = xv * r
        dhv = dh_ref[...]

        @pl.when(pl.program_id(0) == 0)
        def _():
            dg_ref[...] = jnp.zeros_like(dg_ref)

        dg_ref[...] += jnp.sum(dhv * xh, axis=0, keepdims=True)
        dxh = dhv * g_ref[...]
        dx = r * (dxh - xh * jnp.mean(dxh * xh, axis=-1, keepdims=True)) + dr_ref[...]
        dx_ref[...] = dx
        dxb_ref[...] = dx.astype(BF16)

    row = pl.BlockSpec((ROWS // 2, d), lambda i: (i, 0))
    vec = pl.BlockSpec((1, d), lambda i: (0, 0))
    return pl.pallas_call(
        body, name=name, grid=(t // (ROWS // 2),),
        in_specs=[row, vec, row, row],
        out_specs=[row, row, vec],
        out_shape=[jax.ShapeDtypeStruct((t, d), F32), jax.ShapeDtypeStruct((t, d), BF16),
                   jax.ShapeDtypeStruct((1, d), F32)],
        compiler_params=_params(("arbitrary",)),
    )(x, g, dh, dres)


def _loss_fwd_bwd(y, target, name):
    t, d = y.shape
    inv = 1.0 / d

    def body(y_ref, t_ref, l_ref, dy_ref, dyb_ref):
        e = y_ref[...] - t_ref[...]

        @pl.when(pl.program_id(0) == 0)
        def _():
            l_ref[...] = jnp.zeros_like(l_ref)

        l_ref[...] += 0.5 * inv * jnp.sum(e * e)
        dy = e * inv
        dy_ref[...] = dy
        dyb_ref[...] = dy.astype(BF16)

    row = pl.BlockSpec((ROWS, d), lambda i: (i, 0))
    return pl.pallas_call(
        body, name=name, grid=(t // ROWS,),
        in_specs=[row, row],
        out_specs=[pl.BlockSpec((8, LANE), lambda i: (0, 0)), row, row],
        out_shape=[jax.ShapeDtypeStruct((8, LANE), F32), jax.ShapeDtypeStruct((t, d), F32),
                   jax.ShapeDtypeStruct((t, d), BF16)],
        compiler_params=_params(("arbitrary",)),
    )(y, target)


def _swiglu_fwd(gu, name):
    t, f2 = gu.shape
    f = f2 // 2
    tf = _pick(f, 1024)
    nf = f // tf

    def body(g_ref, u_ref, o_ref):
        gv = g_ref[...]
        o_ref[...] = (gv * _sigmoid(gv) * u_ref[...]).astype(BF16)

    return pl.pallas_call(
        body, name=name, grid=(t // ROWS, nf),
        in_specs=[pl.BlockSpec((ROWS, tf), lambda i, j: (i, j)), pl.BlockSpec((ROWS, tf), lambda i, j: (i, j + nf))],
        out_specs=pl.BlockSpec((ROWS, tf), lambda i, j: (i, j)),
        out_shape=jax.ShapeDtypeStruct((t, f), BF16),
        compiler_params=_params(("parallel", "parallel")),
    )(gu, gu)


def _swiglu_bwd(gu, dact, name):
    t, f2 = gu.shape
    f = f2 // 2
    tf = _pick(f, 1024)
    nf = f // tf

    def body(g_ref, u_ref, d_ref, dg_ref, du_ref):
        gv = g_ref[...]
        sg = _sigmoid(gv)
        dv = d_ref[...]
        dg_ref[...] = (dv * u_ref[...] * sg * (1.0 + gv * (1.0 - sg))).astype(BF16)
        du_ref[...] = (dv * gv * sg).astype(BF16)

    blk = pl.BlockSpec((ROWS, tf), lambda i, j: (i, j))
    dgate, dup = pl.pallas_call(
        body, name=name, grid=(t // ROWS, nf),
        in_specs=[blk, pl.BlockSpec((ROWS, tf), lambda i, j: (i, j + nf)), blk],
        out_specs=[blk, blk],
        out_shape=[jax.ShapeDtypeStruct((t, f), BF16), jax.ShapeDtypeStruct((t, f), BF16)],
        compiler_params=_params(("parallel", "parallel")),
    )(gu, gu, dact)
    return jnp.concatenate([dgate, dup], axis=1)


MIX_ROWS = 256


def _mix_fwd(o, y, proj, ga, gs, name):
    t = o.shape[0]
    gw = SSM_W // SSM_GROUPS

    def body(o_ref, y_ref, z_ref, ga_ref, gs_ref, c_ref):
        ov = o_ref[...]
        c_ref[:, 0:ATT_W] = (ov * _rstd(ov) * ga_ref[...]).astype(BF16)
        zv = z_ref[...]
        yz = y_ref[...] * (zv * _sigmoid(zv))
        for gi in range(SSM_GROUPS):
            seg = yz[:, gi * gw:(gi + 1) * gw]
            c_ref[:, ATT_W + gi * gw:ATT_W + (gi + 1) * gw] = (
                seg * _rstd(seg) * gs_ref[:, gi * gw:(gi + 1) * gw]).astype(BF16)

    half = pl.BlockSpec((MIX_ROWS, ATT_W), lambda i: (i, 0))
    vec = pl.BlockSpec((1, ATT_W), lambda i: (0, 0))
    return pl.pallas_call(
        body, name=name, grid=(t // MIX_ROWS,),
        in_specs=[half, half, pl.BlockSpec((MIX_ROWS, ATT_W), lambda i: (i, OFF_Z // ATT_W)), vec, vec],
        out_specs=pl.BlockSpec((MIX_ROWS, 2 * ATT_W), lambda i: (i, 0)),
        out_shape=jax.ShapeDtypeStruct((t, 2 * ATT_W), BF16),
        compiler_params=_params(("parallel",)),
    )(o, y, proj, ga, gs)


def _mix_bwd(dcat, o, y, proj, ga, gs, name):
    t = o.shape[0]
    gw = SSM_W // SSM_GROUPS

    def body(dc_ref, o_ref, y_ref, z_ref, ga_ref, gs_ref, do_ref, dy_ref, dz_ref, dga_ref, dgs_ref):
        @pl.when(pl.program_id(0) == 0)
        def _():
            dga_ref[...] = jnp.zeros_like(dga_ref)
            dgs_ref[...] = jnp.zeros_like(dgs_ref)

        ov = o_ref[...]
        r = _rstd(ov)
        oh = ov * r
        d_on = dc_ref[:, 0:ATT_W]
        dga_ref[...] += jnp.sum(d_on * oh, axis=0, keepdims=True)
        doh = d_on * ga_ref[...]
        do_ref[...] = r * (doh - oh * jnp.mean(doh * oh, axis=-1, keepdims=True))

        zv = z_ref[...]
        yv = y_ref[...]
        sz = _sigmoid(zv)
        silu = zv * sz
        yz = yv * silu
        for gi in range(SSM_GROUPS):
            sl = slice(gi * gw, (gi + 1) * gw)
            seg = yz[:, sl]
            rg = _rstd(seg)
            yh = seg * rg
            dyn = dc_ref[:, ATT_W + gi * gw:ATT_W + (gi + 1) * gw]
            dgs_ref[:, sl] += jnp.sum(dyn * yh, axis=0, keepdims=True)
            dyh = dyn * gs_ref[:, sl]
            dyz = rg * (dyh - yh * jnp.mean(dyh * yh, axis=-1, keepdims=True))
            dy_ref[:, sl] = dyz * silu[:, sl]
            dz_ref[:, sl] = (dyz * yv[:, sl] * (sz[:, sl] * (1.0 + zv[:, sl] * (1.0 - sz[:, sl])))).astype(BF16)

    half = pl.BlockSpec((MIX_ROWS, ATT_W), lambda i: (i, 0))
    vec = pl.BlockSpec((1, ATT_W), lambda i: (0, 0))
    return pl.pallas_call(
        body, name=name, grid=(t // MIX_ROWS,),
        in_specs=[pl.BlockSpec((MIX_ROWS, 2 * ATT_W), lambda i: (i, 0)), half, half,
                  pl.BlockSpec((MIX_ROWS, ATT_W), lambda i: (i, OFF_Z // ATT_W)), vec, vec],
        out_specs=[half, half, half, vec, vec],
        out_shape=[jax.ShapeDtypeStruct((t, ATT_W), F32), jax.ShapeDtypeStruct((t, SSM_W), F32),
                   jax.ShapeDtypeStruct((t, SSM_W), BF16), jax.ShapeDtypeStruct((1, ATT_W), F32),
                   jax.ShapeDtypeStruct((1, SSM_W), F32)],
        compiler_params=_params(("arbitrary",)),
    )(dcat, o, y, proj, ga, gs)


ATT_QB = 256
ATT_KB = 256


def _att_tile(q_t, k_t, qi, kj, row, col, m_strict, carry):
    z = _dot_nt(q_t, k_t)
    lse = jnp.log(1.0 + jnp.exp(-jnp.abs(z)))
    lb = jnp.minimum(z, 0.0) - lse
    lr = -jnp.maximum(z, 0.0) - lse
    mask = (kj * ATT_KB + col) < (qi * ATT_QB + row)
    lrm = jnp.where(mask, lr, 0.0)
    later = _split_dot(lrm, m_strict) + carry
    w = jnp.where(mask, jnp.exp(lb + later), 0.0)
    return mask, lb, lrm, w


def _attn_fwd(proj, gq, gk, nb, seq, name):
    nq = seq // ATT_QB
    scale = ATT_DH ** -0.5

    def body(q_ref, k_ref, v_ref, gq_ref, gk_ref, o_ref, qs, kn, vb):
        qv = q_ref[...]
        kv = k_ref[...]
        qs[...] = (qv * _rstd(qv) * gq_ref[...] * scale).astype(BF16)
        kn[...] = (kv * _rstd(kv) * gk_ref[...]).astype(BF16)
        vb[...] = v_ref[...].astype(BF16)
        row = lax.broadcasted_iota(jnp.int32, (ATT_QB, ATT_KB), 0)
        col = lax.broadcasted_iota(jnp.int32, (ATT_QB, ATT_KB), 1)
        m_strict = (row > col).astype(BF16)

        def q_loop(qi, _):
            q0 = pl.multiple_of(qi * ATT_QB, ATT_QB)
            q_t = qs[pl.ds(q0, ATT_QB), :]

            def k_loop(i, c):
                acc, carry = c
                kj = qi - i
                k0 = pl.multiple_of(kj * ATT_KB, ATT_KB)
                _, _, lrm, w = _att_tile(q_t, kn[pl.ds(k0, ATT_KB), :], qi, kj, row, col, m_strict, carry)
                acc = acc + _dot(w.astype(BF16), vb[pl.ds(k0, ATT_KB), :])
                return acc, carry + jnp.sum(lrm, axis=-1, keepdims=True)

            acc, _ = lax.fori_loop(0, qi + 1, k_loop,
                                   (jnp.zeros((ATT_QB, ATT_DH), F32), jnp.zeros((ATT_QB, 1), F32)))
            o_ref[pl.ds(q0, ATT_QB), :] = acc
            return 0

        lax.fori_loop(0, nq, q_loop, 0)

    def head(off):
        return pl.BlockSpec((seq, ATT_DH), lambda s: (s // ATT_HEADS, off // ATT_DH + s % ATT_HEADS))

    vec = pl.BlockSpec((1, ATT_DH), lambda s: (0, 0))
    return pl.pallas_call(
        body, name=name, grid=(nb * ATT_HEADS,),
        in_specs=[head(OFF_Q), head(OFF_K), head(OFF_V), vec, vec],
        out_specs=head(0),
        out_shape=jax.ShapeDtypeStruct((nb * seq, ATT_W), F32),
        scratch_shapes=[pltpu.VMEM((seq, ATT_DH), BF16)] * 3,
        compiler_params=_params(("parallel",)),
    )(proj, proj, proj, gq, gk)


def _attn_bwd(proj, do, gq, gk, nb, seq, name):
    nq = seq // ATT_QB
    scale = ATT_DH ** -0.5

    def body(q_ref, k_ref, v_ref, do_ref, gq_ref, gk_ref, dq_ref, dk_ref, dv_ref, dgq_ref, dgk_ref,
             qs, kn, vb, dob, dq_acc, dk_acc, dv_acc, gbuf, bbuf):
        @pl.when(pl.program_id(0) == 0)
        def _():
            dgq_ref[...] = jnp.zeros_like(dgq_ref)
            dgk_ref[...] = jnp.zeros_like(dgk_ref)

        qv = q_ref[...]
        kv = k_ref[...]
        rq = _rstd(qv)
        rk = _rstd(kv)
        qs[...] = (qv * rq * gq_ref[...] * scale).astype(BF16)
        kn[...] = (kv * rk * gk_ref[...]).astype(BF16)
        vb[...] = v_ref[...].astype(BF16)
        dob[...] = do_ref[...].astype(BF16)
        dk_acc[...] = jnp.zeros_like(dk_acc)
        dv_acc[...] = jnp.zeros_like(dv_acc)
        row = lax.broadcasted_iota(jnp.int32, (ATT_QB, ATT_KB), 0)
        col = lax.broadcasted_iota(jnp.int32, (ATT_QB, ATT_KB), 1)
        m_strict = (row > col).astype(BF16)
        m_prefix = (row < col).astype(BF16)

        def q_loop(qi, _):
            q0 = pl.multiple_of(qi * ATT_QB, ATT_QB)
            q_t = qs[pl.ds(q0, ATT_QB), :]
            do_t = dob[pl.ds(q0, ATT_QB), :]

            def down(i, carry):
                kj = qi - i
                k0 = pl.multiple_of(kj * ATT_KB, ATT_KB)
                _, lb, lrm, w = _att_tile(q_t, kn[pl.ds(k0, ATT_KB), :], qi, kj, row, col, m_strict, carry)
                dw = _dot_nt(do_t, vb[pl.ds(k0, ATT_KB), :])
                gbuf[kj] = w * dw
                bbuf[kj] = jnp.exp(lb)
                dv_acc[pl.ds(k0, ATT_KB), :] += _dot_tn(w.astype(BF16), do_t)
                return carry + jnp.sum(lrm, axis=-1, keepdims=True)

            lax.fori_loop(0, qi + 1, down, jnp.zeros((ATT_QB, 1), F32))

            def up(kj, c):
                acc, carry = c
                k0 = pl.multiple_of(kj * ATT_KB, ATT_KB)
                g = gbuf[kj]
                beta = bbuf[kj]
                big_g = _split_dot(g, m_prefix) + carry
                mask = (kj * ATT_KB + col) < (qi * ATT_QB + row)
                dz = jnp.where(mask, g * (1.0 - beta) - big_g * beta, 0.0).astype(BF16)
                acc = acc + _dot(dz, kn[pl.ds(k0, ATT_KB), :])
                dk_acc[pl.ds(k0, ATT_KB), :] += _dot_tn(dz, q_t)
                return acc, carry + jnp.sum(g, axis=-1, keepdims=True)

            acc, _ = lax.fori_loop(0, qi + 1, up,
                                   (jnp.zeros((ATT_QB, ATT_DH), F32), jnp.zeros((ATT_QB, 1), F32)))
            dq_acc[pl.ds(q0, ATT_QB), :] = acc
            return 0

        lax.fori_loop(0, nq, q_loop, 0)

        def norm_bwd(xv, r, gain, dyn):
            xh = xv * r
            dgain = jnp.sum(dyn * xh, axis=0, keepdims=True)
            dxh = dyn * gain
            return r * (dxh - xh * jnp.mean(dxh * xh, axis=-1, keepdims=True)), dgain

        dq, dgq = norm_bwd(qv, rq, gq_ref[...], dq_acc[...] * scale)
        dk, dgk = norm_bwd(kv, rk, gk_ref[...], dk_acc[...])
        dq_ref[...] = dq.astype(BF16)
        dk_ref[...] = dk.astype(BF16)
        dv_ref[...] = dv_acc[...].astype(BF16)
        dgq_ref[...] += dgq
        dgk_ref[...] += dgk

    def head(off):
        return pl.BlockSpec((seq, ATT_DH), lambda s: (s // ATT_HEADS, off // ATT_DH + s % ATT_HEADS))

    vec = pl.BlockSpec((1, ATT_DH), lambda s: (0, 0))
    big = jax.ShapeDtypeStruct((nb * seq, ATT_W), BF16)
    small = jax.ShapeDtypeStruct((1, ATT_DH), F32)
    return pl.pallas_call(
        body, name=name, grid=(nb * ATT_HEADS,),
        in_specs=[head(OFF_Q), head(OFF_K), head(OFF_V), head(0), vec, vec],
        out_specs=[head(0), head(0), head(0), vec, vec],
        out_shape=[big, big, big, small, small],
        scratch_shapes=[pltpu.VMEM((seq, ATT_DH), BF16)] * 4 + [pltpu.VMEM((seq, ATT_DH), F32)] * 3
        + [pltpu.VMEM((seq // ATT_KB, ATT_QB, ATT_KB), F32)] * 2,
        compiler_params=_params(("arbitrary",)),
    )(proj, proj, proj, do, gq, gk)


CONV_COLS = 256


def _pack_conv(conv_w, conv_b):
    return jnp.concatenate([conv_w, conv_b[None, :], jnp.zeros((3, CONV_DIM), F32)], axis=0)


def _conv_pre(raw, w8, rowi):
    pre = w8[CONV_K:CONV_K + 1, :] + raw * w8[CONV_K - 1:CONV_K, :]
    for k in range(1, CONV_K):
        sh = jnp.where(rowi >= k, pltpu.roll(raw, k, 0), 0.0)
        pre = pre + sh * w8[CONV_K - 1 - k:CONV_K - k, :]
    return pre


def _conv_fwd(proj, cw8, nb, seq, name):
    ncol = CONV_DIM // CONV_COLS

    def body(x_ref, w_ref, o_ref):
        rowi = lax.broadcasted_iota(jnp.int32, (seq, 1), 0)
        pre = _conv_pre(x_ref[...], w_ref[...], rowi)
        o_ref[...] = pre * _sigmoid(pre)

    return pl.pallas_call(
        body, name=name, grid=(nb, ncol),
        in_specs=[pl.BlockSpec((seq, CONV_COLS), lambda b, j: (b, OFF_XS // CONV_COLS + j)),
                  pl.BlockSpec((8, CONV_COLS), lambda b, j: (0, j))],
        out_specs=pl.BlockSpec((seq, CONV_COLS), lambda b, j: (b, j)),
        out_shape=jax.ShapeDtypeStruct((nb * seq, CONV_DIM), F32),
        compiler_params=_params(("parallel", "parallel")),
    )(proj, cw8)


def _conv_bwd(proj, dact, cw8, nb, seq, name):
    ncol = CONV_DIM // CONV_COLS

    def body(x_ref, d_ref, w_ref, dx_ref, dw_ref):
        @pl.when(pl.program_id(1) == 0)
        def _():
            dw_ref[...] = jnp.zeros_like(dw_ref)

        rowi = lax.broadcasted_iota(jnp.int32, (seq, 1), 0)
        raw = x_ref[...]
        w8 = w_ref[...]
        pre = _conv_pre(raw, w8, rowi)
        sg = _sigmoid(pre)
        dpre = d_ref[...] * (sg * (1.0 + pre * (1.0 - sg)))
        dw_ref[CONV_K:CONV_K + 1, :] += jnp.sum(dpre, axis=0, keepdims=True)
        dw_ref[CONV_K - 1:CONV_K, :] += jnp.sum(dpre * raw, axis=0, keepdims=True)
        draw = dpre * w8[CONV_K - 1:CONV_K, :]
        for k in range(1, CONV_K):
            sh = jnp.where(rowi >= k, pltpu.roll(raw, k, 0), 0.0)
            dw_ref[CONV_K - 1 - k:CONV_K - k, :] += jnp.sum(dpre * sh, axis=0, keepdims=True)
            up = jnp.where(rowi < seq - k, pltpu.roll(dpre, seq - k, 0), 0.0)
            draw = draw + up * w8[CONV_K - 1 - k:CONV_K - k, :]
        dx_ref[...] = draw.astype(BF16)

    return pl.pallas_call(
        body, name=name, grid=(ncol, nb),
        in_specs=[pl.BlockSpec((seq, CONV_COLS), lambda j, b: (b, OFF_XS // CONV_COLS + j)),
                  pl.BlockSpec((seq, CONV_COLS), lambda j, b: (b, j)),
                  pl.BlockSpec((8, CONV_COLS), lambda j, b: (0, j))],
        out_specs=[pl.BlockSpec((seq, CONV_COLS), lambda j, b: (b, j)),
                   pl.BlockSpec((8, CONV_COLS), lambda j, b: (0, j))],
        out_shape=[jax.ShapeDtypeStruct((nb * seq, CONV_DIM), BF16), jax.ShapeDtypeStruct((8, CONV_DIM), F32)],
        compiler_params=_params(("parallel", "arbitrary")),
    )(proj, dact, cw8)


def _pack_heads(dt_bias, a_log, d_skip):
    rows = jnp.stack([dt_bias, a_log, d_skip]).reshape(3, SSM_GROUPS, SSM_HG).transpose(1, 0, 2)
    return jnp.pad(rows, ((0, 0), (0, 8 - 3), (0, LANE - SSM_HG)))


def _split_dot_r(m, x, parts):
    acc = None
    rem = x
    for _ in range(parts):
        hi = rem.astype(BF16)
        d = _dot(m, hi)
        acc = d if acc is None else acc + d
        rem = rem - hi.astype(F32)
    return acc


def _ssd_specs(seq):
    gx = SSM_HG * SSM_P
    return dict(
        xs=pl.BlockSpec((seq, gx), lambda g, b: (b, g)),
        bm=pl.BlockSpec((seq, SSM_N), lambda g, b: (b, SSM_W // SSM_N + g)),
        cm=pl.BlockSpec((seq, SSM_N), lambda g, b: (b, SSM_W // SSM_N + SSM_GROUPS + g)),
        dt=pl.BlockSpec((seq, LANE), lambda g, b: (b, OFF_DT // LANE + g)),
        hp=pl.BlockSpec((1, 8, LANE), lambda g, b: (g, 0, 0)),
        head=pl.BlockSpec((seq, gx), lambda g, b: (b, g)),
        grp=pl.BlockSpec((seq, SSM_N), lambda g, b: (b, g)),
    )


def _ssd_fwd(act, proj, hp, nb, seq, name):
    nc = seq // CHUNK

    def body(xs_ref, b_ref, c_ref, dt_ref, hp_ref, y_ref, dt_s, da_s, hst):
        hpv = hp_ref[0]
        dt = _softplus(dt_ref[...] + hpv[0:1, :])
        a = -jnp.exp(hpv[1:2, :])
        dsk = hpv[2:3, :]
        dt_s[...] = dt
        da_s[...] = dt * a
        hst[...] = jnp.zeros_like(hst)
        li = lax.broadcasted_iota(jnp.int32, (CHUNK, CHUNK), 0)
        si = lax.broadcasted_iota(jnp.int32, (CHUNK, CHUNK), 1)
        tril = (si <= li).astype(BF16)
        causal = li >= si

        def chunk(c, _):
            rows = pl.ds(pl.multiple_of(c * CHUNK, CHUNK), CHUNK)
            acol = _split_dot_r(tril, da_s[rows, :], 3)
            arow = acol.T
            alast = acol[CHUNK - 1:CHUNK, :]
            bb = b_ref[rows, :].astype(BF16)
            cb = c_ref[rows, :].astype(BF16)
            cbm = _dot_nt(cb, bb)
            xc = xs_ref[rows, :]
            dtc = dt_s[rows, :]
            for j in range(SSM_HG):
                a_col = acol[:, j:j + 1]
                decay = jnp.where(causal, jnp.exp(jnp.minimum(a_col - arow[j:j + 1, :], 0.0)), 0.0)
                x_j = xc[:, j * SSM_P:(j + 1) * SSM_P]
                u = x_j * dtc[:, j:j + 1]
                h_in = hst[j]
                y = (_dot((cbm * decay).astype(BF16), u.astype(BF16))
                     + jnp.exp(a_col) * _dot_nt(cb, h_in.astype(BF16)) + dsk[:, j:j + 1] * x_j)
                y_ref[rows, j * SSM_P:(j + 1) * SSM_P] = y
                a_l = alast[:, j:j + 1]
                hst[j] = jnp.exp(a_l) * h_in + _dot_tn((u * jnp.exp(a_l - a_col)).astype(BF16), bb)
            return 0

        lax.fori_loop(0, nc, chunk, 0)

    sp = _ssd_specs(seq)
    return pl.pallas_call(
        body, name=name, grid=(SSM_GROUPS, nb),
        in_specs=[sp["xs"], sp["bm"], sp["cm"], sp["dt"], sp["hp"]],
        out_specs=sp["head"],
        out_shape=jax.ShapeDtypeStruct((nb * seq, SSM_W), F32),
        scratch_shapes=[pltpu.VMEM((seq, LANE), F32)] * 2 + [pltpu.VMEM((SSM_HG, SSM_P, SSM_N), F32)],
        compiler_params=_params(("parallel", "parallel")),
    )(act, act, act, proj, hp)


def _ssd_bwd(act, proj, dy, hp, nb, seq, name):
    nc = seq // CHUNK

    def body(xs_ref, b_ref, c_ref, dt_ref, hp_ref, dy_ref, dxs_ref, db_ref, dc_ref, ddt_ref, dhp_ref,
             dt_s, da_s, ddt_s, hs, lam):
        @pl.when(pl.program_id(1) == 0)
        def _():
            dhp_ref[...] = jnp.zeros_like(dhp_ref)

        hpv = hp_ref[0]
        a = -jnp.exp(hpv[1:2, :])
        dsk = hpv[2:3, :]
        dt_s[...] = _softplus(dt_ref[...] + hpv[0:1, :])
        da_s[...] = dt_s[...] * a
        li = lax.broadcasted_iota(jnp.int32, (CHUNK, CHUNK), 0)
        si = lax.broadcasted_iota(jnp.int32, (CHUNK, CHUNK), 1)
        tril = (si <= li).astype(BF16)
        triu = (si >= li).astype(BF16)
        causal = li >= si
        causal_t = si >= li
        lane = lax.broadcasted_iota(jnp.int32, (1, LANE), 1)
        tril_strict = (si < li).astype(BF16)

        def chunk_rows(c):
            return pl.ds(pl.multiple_of(c * CHUNK, CHUNK), CHUNK)

        hs[0:SSM_HG] = jnp.zeros((SSM_HG, SSM_P, SSM_N), F32)

        def fwd_chunk(c, _):
            rows = chunk_rows(c)
            acol = _split_dot_r(tril, da_s[rows, :], 3)
            alast = acol[CHUNK - 1:CHUNK, :]
            bb = b_ref[rows, :].astype(BF16)
            xc = xs_ref[rows, :]
            dtc = dt_s[rows, :]
            for j in range(SSM_HG):
                a_col = acol[:, j:j + 1]
                a_l = alast[:, j:j + 1]
                u = xc[:, j * SSM_P:(j + 1) * SSM_P] * dtc[:, j:j + 1]
                hs[(c + 1) * SSM_HG + j] = (jnp.exp(a_l) * hs[c * SSM_HG + j]
                                            + _dot_tn((u * jnp.exp(a_l - a_col)).astype(BF16), bb))
            return 0

        lax.fori_loop(0, nc - 1, fwd_chunk, 0)
        lam[...] = jnp.zeros_like(lam)

        def bwd_chunk(i, carry):
            dd_vec, da_vec = carry
            c = nc - 1 - i
            rows = chunk_rows(c)
            acol = _split_dot_r(tril, da_s[rows, :], 3)
            arow = acol.T
            alast = acol[CHUNK - 1:CHUNK, :]
            bb = b_ref[rows, :].astype(BF16)
            cb = c_ref[rows, :].astype(BF16)
            cbm = _dot_nt(cb, bb)
            cbt = _dot_nt(bb, cb)
            xc = xs_ref[rows, :]
            dtc = dt_s[rows, :]
            dyc = dy_ref[rows, :]
            zero = jnp.zeros((CHUNK, CHUNK), F32)
            dcb, dcbt, dc_acc, db_acc, d_a, f_a, dtu = zero, zero, zero, zero, zero, zero, zero
            c_a = jnp.zeros((1, LANE), F32)
            for j in range(SSM_HG):
                a_col = acol[:, j:j + 1]
                seg = a_col - arow[j:j + 1, :]
                decay = jnp.where(causal, jnp.exp(jnp.minimum(seg, 0.0)), 0.0)
                decay_t = jnp.where(causal_t, jnp.exp(jnp.minimum(-seg, 0.0)), 0.0)
                m = cbm * decay
                mt = cbt * decay_t
                sl = slice(j * SSM_P, (j + 1) * SSM_P)
                x_j = xc[:, sl]
                dt_j = dtc[:, j:j + 1]
                dy_j = dyc[:, sl]
                u = x_j * dt_j
                ub = u.astype(BF16)
                dyb = dy_j.astype(BF16)
                h_in = hs[c * SSM_HG + j]
                lm = lam[j]
                hb = h_in.astype(BF16)
                lb = lm.astype(BF16)
                a_l = alast[:, j:j + 1]
                ea = jnp.exp(a_col)
                eb = jnp.exp(a_l - a_col)
                el = jnp.exp(a_l)
                du_off = eb * _dot_nt(bb, lb)
                du = _dot(mt.astype(BF16), dyb) + du_off
                dm = _dot_nt(dyb, ub)
                dmt = _dot_nt(ub, dyb)
                dcb = dcb + dm * decay
                dcbt = dcbt + dmt * decay_t
                y_off = ea * _dot_nt(cb, hb)
                d_a_j = (jnp.sum(dm * m, axis=-1, keepdims=True) - jnp.sum(dmt * mt, axis=-1, keepdims=True)
                         + jnp.sum(dy_j * y_off, axis=-1, keepdims=True))
                f_a_j = jnp.sum(du_off * u, axis=-1, keepdims=True)
                c_a = c_a + jnp.where(lane == j, el * jnp.sum(lm * h_in), 0.0)
                dc_acc = dc_acc + ea * _dot(dyb, hb)
                db_acc = db_acc + eb * _dot(ub, lb)
                lam[j] = el * lm + _dot_tn((ea * dy_j).astype(BF16), cb)
                d_a = jnp.where(lane == j, d_a_j, d_a)
                f_a = jnp.where(lane == j, f_a_j, f_a)
                dtu = jnp.where(lane == j, jnp.sum(du * x_j, axis=-1, keepdims=True), dtu)
                dxs_ref[rows, sl] = du * dt_j + dsk[:, j:j + 1] * dy_j
                dd_vec = dd_vec + jnp.where(lane == j, jnp.sum(dy_j * x_j), 0.0)
            dc_ref[rows, :] = dc_acc + _dot(dcb.astype(BF16), bb)
            db_ref[rows, :] = db_acc + _dot(dcbt.astype(BF16), cb)
            dda = _split_dot_r(triu, d_a, 2) + _split_dot_r(tril_strict, f_a, 2) + c_a
            ddt_s[rows, :] = dda * a + dtu
            da_vec = da_vec + jnp.sum(dda * dtc, axis=0, keepdims=True)
            return dd_vec, da_vec

        zv = jnp.zeros((1, LANE), F32)
        dd_vec, da_vec = lax.fori_loop(0, nc, bwd_chunk, (zv, zv))
        ddt_raw = ddt_s[...] * _sigmoid(dt_ref[...] + hpv[0:1, :])
        ddt_ref[...] = ddt_raw.astype(BF16)
        dhp_ref[0, 0:1, :] += jnp.sum(ddt_raw, axis=0, keepdims=True)
        dhp_ref[0, 1:2, :] += da_vec * a
        dhp_ref[0, 2:3, :] += dd_vec

    sp = _ssd_specs(seq)
    t = nb * seq
    return pl.pallas_call(
        body, name=name, grid=(SSM_GROUPS, nb),
        in_specs=[sp["xs"], sp["bm"], sp["cm"], sp["dt"], sp["hp"], sp["head"]],
        out_specs=[sp["head"], sp["grp"], sp["grp"], sp["grp"], sp["hp"]],
        out_shape=[jax.ShapeDtypeStruct((t, SSM_W), F32), jax.ShapeDtypeStruct((t, SSM_GROUPS * SSM_N), F32),
                   jax.ShapeDtypeStruct((t, SSM_GROUPS * SSM_N), F32),
                   jax.ShapeDtypeStruct((t, SSM_GROUPS * LANE), BF16),
                   jax.ShapeDtypeStruct((SSM_GROUPS, 8, LANE), F32)],
        scratch_shapes=[pltpu.VMEM((seq, LANE), F32)] * 3
        + [pltpu.VMEM((nc * SSM_HG, SSM_P, SSM_N), F32), pltpu.VMEM((SSM_HG, SSM_P, SSM_N), F32)],
        compiler_params=_params(("parallel", "arbitrary")),
    )(act, act, act, proj, hp, dy)


ANY = pl.BlockSpec(memory_space=pl.ANY)


def _block_index(p):
    return 4 * p[0] + 2 * p[1] + p[2]


def _all_gather(shards, name):
    n = len(shards)

    def body(*refs):
        ins, outs = refs[:n], refs[n:2 * n]
        send_sems, recv_sems, local_sems = refs[2 * n:]
        x, y, c = lax.axis_index("x"), lax.axis_index("y"), lax.axis_index("c")
        me, sibling = (x, y, c), (x, y, 1 - c)
        chips = [(1 - x, y), (x, 1 - y), (1 - x, 1 - y)]

        def copy(i, k, block, to, src=None):
            dst = outs[i].at[_block_index(block)]
            return pltpu.make_async_remote_copy(
                src_ref=dst if src is None else src, dst_ref=dst,
                send_sem=send_sems.at[i, k], recv_sem=recv_sems.at[i, k],
                device_id=to, device_id_type=MESH)

        mine = [pltpu.make_async_copy(ins[i], outs[i].at[_block_index(me)], local_sems.at[i]) for i in range(n)]
        for cp in mine:
            cp.start()
        first = []
        for i in range(n):
            first.append(copy(i, 0, me, sibling, src=ins[i]))
            first += [copy(i, 1 + j, me, (*chip, c), src=ins[i]) for j, chip in enumerate(chips)]
        for cp in first:
            cp.start()
        passed = []
        for j, chip in enumerate(chips):
            for i in range(n):
                copy(i, 1 + j, (*chip, c), me).wait_recv()
                fwd = copy(i, 4 + j, (*chip, c), sibling)
                fwd.start()
                passed.append(fwd)
        for i in range(n):
            copy(i, 0, sibling, me).wait_recv()
            for j, chip in enumerate(chips):
                copy(i, 4 + j, (*chip, 1 - c), me).wait_recv()
        for cp in first + passed:
            cp.wait_send()
        for cp in mine:
            cp.wait()

    return pl.pallas_call(
        body, name=name,
        in_specs=[ANY] * n, out_specs=[ANY] * n,
        out_shape=[jax.ShapeDtypeStruct((N_DEV,) + s.shape, s.dtype) for s in shards],
        scratch_shapes=[pltpu.SemaphoreType.DMA((n, 7)), pltpu.SemaphoreType.DMA((n, 7)),
                        pltpu.SemaphoreType.DMA((n,))],
    )(*shards)


def _exchange(sends, name):
    n = len(sends)

    def body(*refs):
        ins, outs = refs[:n], refs[n:2 * n]
        send_sems, recv_sems, local_sems = refs[2 * n:]
        x, y, c = lax.axis_index("x"), lax.axis_index("y"), lax.axis_index("c")
        me = _block_index((x, y, c))

        def peer(k):
            return (1 - x if k & 4 else x, 1 - y if k & 2 else y, 1 - c if k & 1 else c)

        def copy(i, k, src_block, dst_block):
            return pltpu.make_async_remote_copy(
                src_ref=ins[i].at[src_block], dst_ref=outs[i].at[dst_block],
                send_sem=send_sems.at[i, k - 1], recv_sem=recv_sems.at[i, k - 1],
                device_id=peer(k), device_id_type=MESH)

        mine = [pltpu.make_async_copy(ins[i].at[me], outs[i].at[me], local_sems.at[i]) for i in range(n)]
        for cp in mine:
            cp.start()
        sent = [copy(i, k, _block_index(peer(k)), me) for i in range(n) for k in range(1, N_DEV)]
        for cp in sent:
            cp.start()
        for i in range(n):
            for k in range(1, N_DEV):
                there = _block_index(peer(k))
                copy(i, k, there, there).wait_recv()
        for cp in sent:
            cp.wait_send()
        for cp in mine:
            cp.wait()

    return pl.pallas_call(
        body, name=name,
        in_specs=[ANY] * n, out_specs=[ANY] * n,
        out_shape=[jax.ShapeDtypeStruct(s.shape, s.dtype) for s in sends],
        scratch_shapes=[pltpu.SemaphoreType.DMA((n, N_DEV - 1)), pltpu.SemaphoreType.DMA((n, N_DEV - 1)),
                        pltpu.SemaphoreType.DMA((n,))],
    )(*sends)


def _adamw_math(w, g, m, v):
    m = ADAM_B1 * m + (1.0 - ADAM_B1) * g
    v = ADAM_B2 * v + (1.0 - ADAM_B2) * (g * g)
    m_hat = m / (1.0 - ADAM_B1 ** ADAM_STEP)
    v_hat = v / (1.0 - ADAM_B2 ** ADAM_STEP)
    delta = -ADAM_LR * (m_hat / (jnp.sqrt(v_hat) + ADAM_EPS) + ADAM_WD * w)
    return delta, m, v


def _adamw(parts, w, m, v, name, rows=256):
    r, c = w.shape
    tr = rows if r % rows == 0 else r

    def body(p_ref, w_ref, m_ref, v_ref, g_ref, d_ref, mo_ref, vo_ref):
        g = p_ref[0].astype(F32)
        for j in range(1, N_DEV):
            g = g + p_ref[j].astype(F32)
        d, mn, vn = _adamw_math(w_ref[...], g, m_ref[...], v_ref[...])
        g_ref[...] = g
        d_ref[...] = d
        mo_ref[...] = mn
        vo_ref[...] = vn

    blk = pl.BlockSpec((tr, c), lambda i: (i, 0))
    out = jax.ShapeDtypeStruct((r, c), F32)
    return pl.pallas_call(
        body, name=name, grid=(r // tr,),
        in_specs=[pl.BlockSpec((N_DEV, tr, c), lambda i: (0, i, 0)), blk, blk, blk],
        out_specs=[blk] * 4, out_shape=[out] * 4,
        compiler_params=_params(("parallel",)),
    )(parts, w, m, v)


def _sum_parts(parts, name):
    _, r, c = parts.shape

    def body(p_ref, o_ref):
        g = p_ref[0]
        for j in range(1, N_DEV):
            g = g + p_ref[j]
        o_ref[...] = g

    return pl.pallas_call(
        body, name=name, out_shape=jax.ShapeDtypeStruct((r, c), F32),
        compiler_params=_params(),
    )(parts)


def _adamw_small(g, w, m, v, name):
    def body(g_ref, w_ref, m_ref, v_ref, d_ref, mo_ref, vo_ref):
        d, mn, vn = _adamw_math(w_ref[...], g_ref[...], m_ref[...], v_ref[...])
        d_ref[...] = d
        mo_ref[...] = mn
        vo_ref[...] = vn

    out = jax.ShapeDtypeStruct(w.shape, F32)
    return pl.pallas_call(body, name=name, out_shape=[out] * 3, compiler_params=_params())(g, w, m, v)


SMALL = ("norm_mix", "q_gain", "k_gain", "conv_b", "dt_bias", "a_log", "d_skip", "attn_out_gain",
         "ssm_out_gain", "norm_ffn")


def _pad_lanes(a):
    n = a.shape[-1]
    return jnp.pad(a, ((0, 0), (0, -n % LANE)))


def _pack_small(d):
    return jnp.concatenate([_pad_lanes(d[k]) for k in SMALL], axis=1)


def _unpack_small(packed, like):
    out, off = {}, 0
    for k in SMALL:
        n = like[k].shape[-1]
        out[k] = packed[:, off:off + n]
        off += n + (-n % LANE)
    return out


def _full_cols(gathered):
    _, r, c = gathered.shape
    return gathered.transpose(1, 0, 2).reshape(r, N_DEV * c)


def _col_blocks(full):
    r, c8 = full.shape
    return full.reshape(r, N_DEV, c8 // N_DEV).transpose(1, 0, 2)


def _layer_forward(x, p, nb, seq, li):
    tag = f"l{li}_"
    h1 = _rms_fwd(x, p["norm_mix"], tag + "rms1")
    proj = _matmul(h1, p["w_in"], "nn", F32, tag + "mm_in")
    o = _attn_fwd(proj, p["q_gain"], p["k_gain"], nb, seq, tag + "attn")
    act = _conv_fwd(proj, p["cw8"], nb, seq, tag + "conv")
    y = _ssd_fwd(act, proj, p["hp"], nb, seq, tag + "ssd")
    cat = _mix_fwd(o, y, proj, p["attn_out_gain"], p["ssm_out_gain"], tag + "mix")
    x1 = _matmul(cat, p["w_out"], "nn", F32, tag + "mm_out", residual=x)
    h2 = _rms_fwd(x1, p["norm_ffn"], tag + "rms2")
    gu = _matmul(h2, p["w_gu"], "nn", F32, tag + "mm_gu")
    a = _swiglu_fwd(gu, tag + "swiglu")
    x2 = _matmul(a, p["w_down"], "nn", F32, tag + "mm_down", residual=x1)
    saved = dict(x=x, h1=h1, proj=proj, o=o, act=act, y=y, cat=cat, x1=x1, h2=h2, gu=gu, a=a)
    return x2, saved


def _layer_backward(dx2, dx2b, p, s, nb, seq, li):
    tag = f"l{li}_b_"
    g = {}
    dact = _matmul(dx2b, p["w_down"], "nt", F32, tag + "mm_dact")
    g["w_down"] = _matmul(s["a"], dx2b, "tn", F32, tag + "mm_dwd")
    dgu = _swiglu_bwd(s["gu"], dact, tag + "swiglu")
    dh2 = _matmul(dgu, p["w_gu"], "nt", F32, tag + "mm_dh2")
    g["w_gu"] = _matmul(s["h2"], dgu, "tn", F32, tag + "mm_dwgu")
    dx1, dx1b, g["norm_ffn"] = _rms_bwd(s["x1"], p["norm_ffn"], dh2, dx2, tag + "rms2")
    dcat = _matmul(dx1b, p["w_out"], "nt", F32, tag + "mm_dcat")
    g["w_out"] = _matmul(s["cat"], dx1b, "tn", F32, tag + "mm_dwo")
    do, dy, dz, g["attn_out_gain"], g["ssm_out_gain"] = _mix_bwd(
        dcat, s["o"], s["y"], s["proj"], p["attn_out_gain"], p["ssm_out_gain"], tag + "mix")
    dq, dk, dv, g["q_gain"], g["k_gain"] = _attn_bwd(s["proj"], do, p["q_gain"], p["k_gain"], nb, seq, tag + "attn")
    dxa, dba, dca, ddt, dhp = _ssd_bwd(s["act"], s["proj"], dy, p["hp"], nb, seq, tag + "ssd")
    dxbc, dcw8 = _conv_bwd(s["proj"], jnp.concatenate([dxa, dba, dca], axis=1), p["cw8"], nb, seq, tag + "conv")
    g["conv_w"] = dcw8[0:CONV_K]
    g["conv_b"] = dcw8[CONV_K:CONV_K + 1]
    heads = dhp[:, 0:3, 0:SSM_HG].transpose(1, 0, 2).reshape(3, SSM_HEADS)
    g["dt_bias"], g["a_log"], g["d_skip"] = heads[0:1], heads[1:2], heads[2:3]
    tail = jnp.zeros((dq.shape[0], NPROJ - OFF_DT - SSM_GROUPS * LANE), BF16)
    dproj = jnp.concatenate([dq, dk, dv, dz, dxbc, ddt, tail], axis=1)
    dh1 = _matmul(dproj, p["w_in"], "nt", F32, tag + "mm_dh1")
    g["w_in"] = _matmul(s["h1"], dproj, "tn", F32, tag + "mm_dwin")
    dx, dxb, g["norm_mix"] = _rms_bwd(s["x"], p["norm_mix"], dh1, dx1, tag + "rms1")
    return dx, dxb, g


def _pad_w_in(full):
    d = full.shape[0]
    z = jnp.zeros((d, LANE - SSM_HG), full.dtype)
    tail = jnp.zeros((d, NPROJ - OFF_DT - SSM_GROUPS * LANE), full.dtype)
    return jnp.concatenate([full[:, :OFF_DT], full[:, OFF_DT:OFF_DT + SSM_HG], z,
                            full[:, OFF_DT + SSM_HG:IN_DIM], z, tail], axis=1)


def _unpad_w_in(padded):
    return jnp.concatenate([padded[:, :OFF_DT], padded[:, OFF_DT:OFF_DT + SSM_HG],
                            padded[:, OFF_DT + LANE:OFF_DT + LANE + SSM_HG]], axis=1)


def kernel(x, norm_mix, w_in, q_gain, k_gain, conv_w, conv_b, dt_bias, a_log, d_skip, attn_out_gain, ssm_out_gain, w_out, norm_ffn, w_gate, w_up, w_down, loss_target, m_norm_mix, m_w_in, m_q_gain, m_k_gain, m_conv_w, m_conv_b, m_dt_bias, m_a_log, m_d_skip, m_attn_out_gain, m_ssm_out_gain, m_w_out, m_norm_ffn, m_w_gate, m_w_up, m_w_down, v_norm_mix, v_w_in, v_q_gain, v_k_gain, v_conv_w, v_conv_b, v_dt_bias, v_a_log, v_d_skip, v_attn_out_gain, v_ssm_out_gain, v_w_out, v_norm_ffn, v_w_gate, v_w_up, v_w_down):
    nb, seq, d = x.shape
    depth = w_in.shape[0]
    t = nb * seq
    w = dict(norm_mix=norm_mix, w_in=w_in, q_gain=q_gain, k_gain=k_gain, conv_w=conv_w, conv_b=conv_b,
             dt_bias=dt_bias, a_log=a_log, d_skip=d_skip, attn_out_gain=attn_out_gain, ssm_out_gain=ssm_out_gain,
             w_out=w_out, norm_ffn=norm_ffn, w_gate=w_gate, w_up=w_up, w_down=w_down)
    mom = dict(norm_mix=m_norm_mix, w_in=m_w_in, q_gain=m_q_gain, k_gain=m_k_gain, conv_w=m_conv_w, conv_b=m_conv_b,
               dt_bias=m_dt_bias, a_log=m_a_log, d_skip=m_d_skip, attn_out_gain=m_attn_out_gain,
               ssm_out_gain=m_ssm_out_gain, w_out=m_w_out, norm_ffn=m_norm_ffn, w_gate=m_w_gate, w_up=m_w_up,
               w_down=m_w_down)
    var = dict(norm_mix=v_norm_mix, w_in=v_w_in, q_gain=v_q_gain, k_gain=v_k_gain, conv_w=v_conv_w, conv_b=v_conv_b,
               dt_bias=v_dt_bias, a_log=v_a_log, d_skip=v_d_skip, attn_out_gain=v_attn_out_gain,
               ssm_out_gain=v_ssm_out_gain, w_out=v_w_out, norm_ffn=v_norm_ffn, w_gate=v_w_gate, w_up=v_w_up,
               w_down=v_w_down)
    big = ("w_in", "w_out", "w_gate", "w_up", "w_down")

    gathered = _all_gather([w[k].astype(BF16) for k in big] + [conv_w], "gather_weights")
    gw = dict(zip(big + ("conv_w",), gathered))

    layers = []
    for li in range(depth):
        p = {k: w[k][li][None, :] for k in ("norm_mix", "q_gain", "k_gain", "attn_out_gain", "ssm_out_gain", "norm_ffn")}
        p["w_in"] = _pad_w_in(_full_cols(gw["w_in"][:, li]))
        p["w_out"] = gw["w_out"][:, li].reshape(d, d)
        p["w_gu"] = jnp.concatenate([_full_cols(gw["w_gate"][:, li]), _full_cols(gw["w_up"][:, li])], axis=1)
        p["w_down"] = gw["w_down"][:, li].reshape(-1, d)
        conv_full = gw["conv_w"][:, li].transpose(1, 0, 2).reshape(CONV_K, CONV_DIM)
        p["cw8"] = _pack_conv(conv_full, conv_b[li])
        p["hp"] = _pack_heads(dt_bias[li], a_log[li], d_skip[li])
        layers.append(p)

    xc = x.reshape(t, d)
    saved = []
    for li in range(depth):
        xc, s = _layer_forward(xc, layers[li], nb, seq, li)
        saved.append(s)

    loss_blk, dx, dxb = _loss_fwd_bwd(xc, loss_target.reshape(t, d), "loss")
    loss = lax.psum(loss_blk[0, 0], ("x", "y", "c"))

    grads = [None] * depth
    for li in reversed(range(depth)):
        dx, dxb, grads[li] = _layer_backward(dx, dxb, layers[li], saved[li], nb, seq, li)
    grad_x = dx.reshape(nb, seq, d)

    def stack(f):
        return jnp.stack([f(grads[li]) for li in range(depth)], axis=1)

    sends = [
        stack(lambda g: _col_blocks(_unpad_w_in(g["w_in"])).astype(BF16)),
        stack(lambda g: g["w_out"].reshape(N_DEV, d // N_DEV, d).astype(BF16)),
        stack(lambda g: _col_blocks(g["w_gu"][:, :g["w_gu"].shape[1] // 2]).astype(BF16)),
        stack(lambda g: _col_blocks(g["w_gu"][:, g["w_gu"].shape[1] // 2:]).astype(BF16)),
        stack(lambda g: g["w_down"].reshape(N_DEV, -1, d).astype(BF16)),
    ]
    recvd = _exchange(sends, "exchange_grads")

    out_g, out_d, out_m, out_v = {}, {}, {}, {}
    for k, parts in zip(big, recvd):
        shp = w[k].shape
        r2 = shp[0] * shp[1]
        flat = lambda a: a.reshape(r2, shp[2])
        res = _adamw(parts.reshape(N_DEV, r2, shp[2]), flat(w[k]), flat(mom[k]), flat(var[k]), "adamw_" + k)
        out_g[k], out_d[k], out_m[k], out_v[k] = [a.reshape(shp) for a in res]

    small_g = {k: jnp.concatenate([grads[li][k] for li in range(depth)], axis=0) for k in SMALL}
    conv_g = jnp.stack([grads[li]["conv_w"] for li in range(depth)]).reshape(depth, CONV_K * CONV_DIM)
    packed_g = jnp.concatenate([_pack_small(small_g), conv_g], axis=1)
    (parts_small,) = _all_gather([packed_g], "gather_small_grads")
    g_small = _sum_parts(parts_small, "sum_small_grads")
    n_small = g_small.shape[1] - CONV_K * CONV_DIM
    conv_total = g_small[:, n_small:].reshape(depth, CONV_K, CONV_DIM)
    me = _block_index((lax.axis_index("x"), lax.axis_index("y"), lax.axis_index("c")))
    cshard = conv_w.shape[2]
    conv_mine = lax.dynamic_slice_in_dim(conv_total, me * cshard, cshard, axis=2)

    def with_conv(small_packed, conv_part):
        return jnp.concatenate([small_packed, conv_part.reshape(depth, CONV_K * cshard)], axis=1)

    res = _adamw_small(with_conv(g_small[:, :n_small], conv_mine),
                       with_conv(_pack_small(w), conv_w), with_conv(_pack_small(mom), m_conv_w),
                       with_conv(_pack_small(var), v_conv_w), "adamw_small")
    g_all = with_conv(g_small[:, :n_small], conv_mine)
    for dst, packed in zip((out_g, out_d, out_m, out_v), (g_all,) + tuple(res)):
        dst.update(_unpack_small(packed[:, :n_small], w))
        dst["conv_w"] = packed[:, n_small:].reshape(depth, CONV_K, cshard)

    names = ("norm_mix", "w_in", "q_gain", "k_gain", "conv_w", "conv_b", "dt_bias", "a_log", "d_skip",
             "attn_out_gain", "ssm_out_gain", "w_out", "norm_ffn", "w_gate", "w_up", "w_down")
    return (loss, grad_x, *[out_g[k] for k in names], *[out_d[k] for k in names],
            *[out_m[k] for k in names], *[out_v[k] for k in names])
```

```python
import functools
import math

import jax
import jax.numpy as jnp
from jax import lax
from jax.experimental import pallas as pl
from jax.experimental.pallas import tpu as pltpu

F32 = jnp.float32
BF16 = jnp.bfloat16
MESH = pl.DeviceIdType.MESH

N_DEV = 8
EPS = 1e-6
ATT_HEADS = 8
ATT_DH = 128
ATT_W = ATT_HEADS * ATT_DH
SSM_W = 1024
SSM_P = 64
SSM_N = 128
SSM_GROUPS = 2
SSM_HG = 8
SSM_HEADS = SSM_GROUPS * SSM_HG
CHUNK = 128
CONV_K = 4
CONV_DIM = SSM_W + 2 * SSM_GROUPS * SSM_N
LANE = 128
OFF_Q, OFF_K, OFF_V, OFF_Z, OFF_XS = 0, ATT_W, 2 * ATT_W, 3 * ATT_W, 4 * ATT_W
OFF_B = OFF_XS + SSM_W
OFF_C = OFF_B + SSM_GROUPS * SSM_N
OFF_DT = OFF_C + SSM_GROUPS * SSM_N
NPROJ = 6144
IN_DIM = OFF_DT + SSM_HEADS

ADAM_LR = 0.001
ADAM_B1 = 0.9
ADAM_B2 = 0.999
ADAM_EPS = 1e-08
ADAM_WD = 0.01
ADAM_STEP = 10

VMEM_LIMIT = 56 * 1024 * 1024


def _params(sem=None):
    return pltpu.CompilerParams(dimension_semantics=sem, vmem_limit_bytes=VMEM_LIMIT)


def _pick(dim, target):
    if dim <= target:
        return dim
    best = None
    for t in range(LANE, target + 1, LANE):
        if dim % t == 0:
            best = t
    assert best is not None, (dim, target)
    return best


def _dot(a, b, dims=((1,), (0,))):
    return lax.dot_general(a, b, (dims, ((), ())), preferred_element_type=F32)


def _dot_nt(a, b):
    return _dot(a, b, ((1,), (1,)))


def _dot_tn(a, b):
    return _dot(a, b, ((0,), (0,)))


def _split_dot(x, m, parts=2):
    acc = None
    rem = x
    for _ in range(parts):
        hi = rem.astype(BF16)
        d = _dot(hi, m)
        acc = d if acc is None else acc + d
        rem = rem - hi.astype(F32)
    return acc


def _sigmoid(x):
    return 1.0 / (1.0 + jnp.exp(-x))


def _softplus(x):
    return jnp.maximum(x, 0.0) + jnp.log(1.0 + jnp.exp(-jnp.abs(x)))


def _rstd(x):
    return lax.rsqrt(jnp.mean(x * x, axis=-1, keepdims=True) + EPS)


def _matmul(a, b, mode, out_dtype, name, residual=None, after=None, tm=512, tn=1024, tk=2048):
    if mode == "nn":
        (m, k), (k2, n) = a.shape, b.shape
    elif mode == "nt":
        (m, k), (n, k2) = a.shape, b.shape
    else:
        (k, m), (k2, n) = a.shape, b.shape
    assert k == k2, (a.shape, b.shape, mode)
    tm, tn, tk = _pick(m, tm), _pick(n, tn), _pick(k, tk)
    nk = k // tk
    dims = {"nn": ((1,), (0,)), "nt": ((1,), (1,)), "tn": ((0,), (0,))}[mode]
    has_res = residual is not None

    has_tok = after is not None

    def body(*refs):
        a_ref, b_ref = refs[:2]
        r_ref = refs[2] if has_res else None
        o_ref = refs[2 + has_res + has_tok]
        prod = _dot(a_ref[...], b_ref[...], dims)

        def finish(r):
            if r_ref is not None:
                r = r + r_ref[...]
            o_ref[...] = r.astype(o_ref.dtype)

        if nk == 1:
            finish(prod)
        else:
            acc = refs[-1]
            kk = pl.program_id(2)

            @pl.when(kk == 0)
            def _():
                acc[...] = prod

            @pl.when(kk > 0)
            def _():
                acc[...] += prod

            @pl.when(kk == nk - 1)
            def _():
                finish(acc[...])

    if mode == "tn":
        a_spec = pl.BlockSpec((tk, tm), lambda i, j, kk: (kk, i))
    else:
        a_spec = pl.BlockSpec((tm, tk), lambda i, j, kk: (i, kk))
    if mode == "nt":
        b_spec = pl.BlockSpec((tn, tk), lambda i, j, kk: (j, kk))
    else:
        b_spec = pl.BlockSpec((tk, tn), lambda i, j, kk: (kk, j))
    o_spec = pl.BlockSpec((tm, tn), lambda i, j, kk: (i, j))
    tok_spec = pl.BlockSpec((8, LANE), lambda i, j, kk: (0, 0))
    in_specs = [a_spec, b_spec] + ([o_spec] if has_res else []) + ([tok_spec] if has_tok else [])
    args = (a, b) + ((residual,) if has_res else ()) + ((after,) if has_tok else ())
    return pl.pallas_call(
        body,
        name=name,
        grid=(m // tm, n // tn, nk),
        in_specs=in_specs,
        out_specs=o_spec,
        out_shape=jax.ShapeDtypeStruct((m, n), out_dtype),
        scratch_shapes=[pltpu.VMEM((tm, tn), F32)] if nk > 1 else [],
        compiler_params=_params(("parallel", "parallel", "arbitrary")),
    )(*args)


ROWS = 512


def _rms_fwd(x, g, name, after=None):
    t, d = x.shape
    has_tok = after is not None

    def body(*refs):
        x_ref, g_ref, o_ref = refs[0], refs[1], refs[-1]
        xv = x_ref[...]
        o_ref[...] = (xv * _rstd(xv) * g_ref[...]).astype(BF16)

    row = pl.BlockSpec((ROWS, d), lambda i: (i, 0))
    tok = [pl.BlockSpec((8, LANE), lambda i: (0, 0))] if has_tok else []
    return pl.pallas_call(
        body, name=name, grid=(t // ROWS,),
        in_specs=[row, pl.BlockSpec((1, d), lambda i: (0, 0))] + tok,
        out_specs=row, out_shape=jax.ShapeDtypeStruct((t, d), BF16),
        compiler_params=_params(("parallel",)),
    )(x, g, *((after,) if has_tok else ()))


def _rms_bwd(x, g, dh, dres, name):
    t, d = x.shape

    def body(x_ref, g_ref, dh_ref, dr_ref, dx_ref, dxb_ref, dg_ref):
        xv = x_ref[...]
        r = _rstd(xv)
        xh = xv * r
        dhv = dh_ref[...]

        @pl.when(pl.program_id(0) == 0)
        def _():
            dg_ref[...] = jnp.zeros_like(dg_ref)

        dg_ref[...] += jnp.sum(dhv * xh, axis=0, keepdims=True)
        dxh = dhv * g_ref[...]
        dx = r * (dxh - xh * jnp.mean(dxh * xh, axis=-1, keepdims=True)) + dr_ref[...]
        dx_ref[...] = dx
        dxb_ref[...] = dx.astype(BF16)

    row = pl.BlockSpec((ROWS // 2, d), lambda i: (i, 0))
    vec = pl.BlockSpec((1, d), lambda i: (0, 0))
    return pl.pallas_call(
        body, name=name, grid=(t // (ROWS // 2),),
        in_specs=[row, vec, row, row],
        out_specs=[row, row, vec],
        out_shape=[jax.ShapeDtypeStruct((t, d), F32), jax.ShapeDtypeStruct((t, d), BF16),
                   jax.ShapeDtypeStruct((1, d), F32)],
        compiler_params=_params(("arbitrary",)),
    )(x, g, dh, dres)


def _loss_fwd_bwd(y, target, name):
    t, d = y.shape
    inv = 1.0 / d

    def body(y_ref, t_ref, l_ref, dy_ref, dyb_ref):
        e = y_ref[...] - t_ref[...]

        @pl.when(pl.program_id(0) == 0)
        def _():
            l_ref[...] = jnp.zeros_like(l_ref)

        l_ref[...] += 0.5 * inv * jnp.sum(e * e)
        dy = e * inv
        dy_ref[...] = dy
        dyb_ref[...] = dy.astype(BF16)

    row = pl.BlockSpec((ROWS, d), lambda i: (i, 0))
    return pl.pallas_call(
        body, name=name, grid=(t // ROWS,),
        in_specs=[row, row],
        out_specs=[pl.BlockSpec((8, LANE), lambda i: (0, 0)), row, row],
        out_shape=[jax.ShapeDtypeStruct((8, LANE), F32), jax.ShapeDtypeStruct((t, d), F32),
                   jax.ShapeDtypeStruct((t, d), BF16)],
        compiler_params=_params(("arbitrary",)),
    )(y, target)


def _swiglu_fwd(gu, name):
    t, f2 = gu.shape
    f = f2 // 2
    tf = _pick(f, 1024)
    nf = f // tf

    def body(g_ref, u_ref, o_ref):
        gv = g_ref[...]
        o_ref[...] = (gv * _sigmoid(gv) * u_ref[...]).astype(BF16)

    return pl.pallas_call(
        body, name=name, grid=(t // ROWS, nf),
        in_specs=[pl.BlockSpec((ROWS, tf), lambda i, j: (i, j)), pl.BlockSpec((ROWS, tf), lambda i, j: (i, j + nf))],
        out_specs=pl.BlockSpec((ROWS, tf), lambda i, j: (i, j)),
        out_shape=jax.ShapeDtypeStruct((t, f), BF16),
        compiler_params=_params(("parallel", "parallel")),
    )(gu, gu)


def _swiglu_bwd(gu, dact, name):
    t, f2 = gu.shape
    f = f2 // 2
    tf = _pick(f, 1024)
    nf = f // tf

    def body(g_ref, u_ref, d_ref, o_ref):
        gv = g_ref[...]
        sg = _sigmoid(gv)
        dv = d_ref[...]

        @pl.when(pl.program_id(2) == 0)
        def _():
            o_ref[...] = (dv * u_ref[...] * sg * (1.0 + gv * (1.0 - sg))).astype(BF16)

        @pl.when(pl.program_id(2) == 1)
        def _():
            o_ref[...] = (dv * gv * sg).astype(BF16)

    return pl.pallas_call(
        body, name=name, grid=(t // ROWS, nf, 2),
        in_specs=[pl.BlockSpec((ROWS, tf), lambda i, j, h: (i, j)),
                  pl.BlockSpec((ROWS, tf), lambda i, j, h: (i, j + nf)),
                  pl.BlockSpec((ROWS, tf), lambda i, j, h: (i, j))],
        out_specs=pl.BlockSpec((ROWS, tf), lambda i, j, h: (i, j + h * nf)),
        out_shape=jax.ShapeDtypeStruct((t, f2), BF16),
        compiler_params=_params(("parallel", "parallel", "arbitrary")),
    )(gu, gu, dact)


MIX_ROWS = 256


def _mix_fwd(o, y, proj, ga, gs, name):
    t = o.shape[0]
    gw = SSM_W // SSM_GROUPS

    def body(o_ref, y_ref, z_ref, ga_ref, gs_ref, c_ref):
        ov = o_ref[...]
        c_ref[:, 0:ATT_W] = (ov * _rstd(ov) * ga_ref[...]).astype(BF16)
        zv = z_ref[...]
        yz = y_ref[...] * (zv * _sigmoid(zv))
        for gi in range(SSM_GROUPS):
            seg = yz[:, gi * gw:(gi + 1) * gw]
            c_ref[:, ATT_W + gi * gw:ATT_W + (gi + 1) * gw] = (
                seg * _rstd(seg) * gs_ref[:, gi * gw:(gi + 1) * gw]).astype(BF16)

    half = pl.BlockSpec((MIX_ROWS, ATT_W), lambda i: (i, 0))
    vec = pl.BlockSpec((1, ATT_W), lambda i: (0, 0))
    return pl.pallas_call(
        body, name=name, grid=(t // MIX_ROWS,),
        in_specs=[half, half, pl.BlockSpec((MIX_ROWS, ATT_W), lambda i: (i, OFF_Z // ATT_W)), vec, vec],
        out_specs=pl.BlockSpec((MIX_ROWS, 2 * ATT_W), lambda i: (i, 0)),
        out_shape=jax.ShapeDtypeStruct((t, 2 * ATT_W), BF16),
        compiler_params=_params(("parallel",)),
    )(o, y, proj, ga, gs)


def _mix_bwd(dcat, o, y, proj, ga, gs, name):
    t = o.shape[0]
    gw = SSM_W // SSM_GROUPS

    def body(dc_ref, o_ref, y_ref, z_ref, ga_ref, gs_ref, do_ref, dy_ref, dz_ref, dga_ref, dgs_ref):
        @pl.when(pl.program_id(0) == 0)
        def _():
            dga_ref[...] = jnp.zeros_like(dga_ref)
            dgs_ref[...] = jnp.zeros_like(dgs_ref)

        ov = o_ref[...]
        r = _rstd(ov)
        oh = ov * r
        d_on = dc_ref[:, 0:ATT_W]
        dga_ref[...] += jnp.sum(d_on * oh, axis=0, keepdims=True)
        doh = d_on * ga_ref[...]
        do_ref[...] = r * (doh - oh * jnp.mean(doh * oh, axis=-1, keepdims=True))

        zv = z_ref[...]
        yv = y_ref[...]
        sz = _sigmoid(zv)
        silu = zv * sz
        yz = yv * silu
        for gi in range(SSM_GROUPS):
            sl = slice(gi * gw, (gi + 1) * gw)
            seg = yz[:, sl]
            rg = _rstd(seg)
            yh = seg * rg
            dyn = dc_ref[:, ATT_W + gi * gw:ATT_W + (gi + 1) * gw]
            dgs_ref[:, sl] += jnp.sum(dyn * yh, axis=0, keepdims=True)
            dyh = dyn * gs_ref[:, sl]
            dyz = rg * (dyh - yh * jnp.mean(dyh * yh, axis=-1, keepdims=True))
            dy_ref[:, sl] = dyz * silu[:, sl]
            dz_ref[:, sl] = (dyz * yv[:, sl] * (sz[:, sl] * (1.0 + zv[:, sl] * (1.0 - sz[:, sl])))).astype(BF16)

    half = pl.BlockSpec((MIX_ROWS, ATT_W), lambda i: (i, 0))
    vec = pl.BlockSpec((1, ATT_W), lambda i: (0, 0))
    return pl.pallas_call(
        body, name=name, grid=(t // MIX_ROWS,),
        in_specs=[pl.BlockSpec((MIX_ROWS, 2 * ATT_W), lambda i: (i, 0)), half, half,
                  pl.BlockSpec((MIX_ROWS, ATT_W), lambda i: (i, OFF_Z // ATT_W)), vec, vec],
        out_specs=[half, half, half, vec, vec],
        out_shape=[jax.ShapeDtypeStruct((t, ATT_W), F32), jax.ShapeDtypeStruct((t, SSM_W), F32),
                   jax.ShapeDtypeStruct((t, SSM_W), BF16), jax.ShapeDtypeStruct((1, ATT_W), F32),
                   jax.ShapeDtypeStruct((1, SSM_W), F32)],
        compiler_params=_params(("arbitrary",)),
    )(dcat, o, y, proj, ga, gs)


ATT_QB = 256
ATT_KB = 256


def _att_tile(q_t, k_t, qi, kj, row, col, m_strict, carry):
    z = _dot_nt(q_t, k_t)
    lse = jnp.log(1.0 + jnp.exp(-jnp.abs(z)))
    lb = jnp.minimum(z, 0.0) - lse
    lr = -jnp.maximum(z, 0.0) - lse
    mask = (kj * ATT_KB + col) < (qi * ATT_QB + row)
    lrm = jnp.where(mask, lr, 0.0)
    later = _split_dot(lrm, m_strict) + carry
    w = jnp.where(mask, jnp.exp(lb + later), 0.0)
    return mask, lb, lrm, w


def _attn_fwd(proj, gq, gk, nb, seq, name):
    nq = seq // ATT_QB
    scale = ATT_DH ** -0.5

    def body(q_ref, k_ref, v_ref, gq_ref, gk_ref, o_ref, qs, kn, vb):
        qv = q_ref[...]
        kv = k_ref[...]
        qs[...] = (qv * _rstd(qv) * gq_ref[...] * scale).astype(BF16)
        kn[...] = (kv * _rstd(kv) * gk_ref[...]).astype(BF16)
        vb[...] = v_ref[...].astype(BF16)
        row = lax.broadcasted_iota(jnp.int32, (ATT_QB, ATT_KB), 0)
        col = lax.broadcasted_iota(jnp.int32, (ATT_QB, ATT_KB), 1)
        m_strict = (row > col).astype(BF16)

        def q_loop(qi, _):
            q0 = pl.multiple_of(qi * ATT_QB, ATT_QB)
            q_t = qs[pl.ds(q0, ATT_QB), :]

            def k_loop(i, c):
                acc, carry = c
                kj = qi - i
                k0 = pl.multiple_of(kj * ATT_KB, ATT_KB)
                _, _, lrm, w = _att_tile(q_t, kn[pl.ds(k0, ATT_KB), :], qi, kj, row, col, m_strict, carry)
                acc = acc + _dot(w.astype(BF16), vb[pl.ds(k0, ATT_KB), :])
                return acc, carry + jnp.sum(lrm, axis=-1, keepdims=True)

            acc, _ = lax.fori_loop(0, qi + 1, k_loop,
                                   (jnp.zeros((ATT_QB, ATT_DH), F32), jnp.zeros((ATT_QB, 1), F32)))
            o_ref[pl.ds(q0, ATT_QB), :] = acc
            return 0

        lax.fori_loop(0, nq, q_loop, 0)

    def head(off):
        return pl.BlockSpec((seq, ATT_DH), lambda s: (s // ATT_HEADS, off // ATT_DH + s % ATT_HEADS))

    vec = pl.BlockSpec((1, ATT_DH), lambda s: (0, 0))
    return pl.pallas_call(
        body, name=name, grid=(nb * ATT_HEADS,),
        in_specs=[head(OFF_Q), head(OFF_K), head(OFF_V), vec, vec],
        out_specs=head(0),
        out_shape=jax.ShapeDtypeStruct((nb * seq, ATT_W), F32),
        scratch_shapes=[pltpu.VMEM((seq, ATT_DH), BF16)] * 3,
        compiler_params=_params(("parallel",)),
    )(proj, proj, proj, gq, gk)


def _attn_bwd(proj, do, gq, gk, nb, seq, name):
    nq = seq // ATT_QB
    scale = ATT_DH ** -0.5

    def body(q_ref, k_ref, v_ref, do_ref, gq_ref, gk_ref, dq_ref, dk_ref, dv_ref, dgq_ref, dgk_ref,
             qs, kn, vb, dob, dq_acc, dk_acc, dv_acc, gbuf, bbuf):
        @pl.when(pl.program_id(0) == 0)
        def _():
            dgq_ref[...] = jnp.zeros_like(dgq_ref)
            dgk_ref[...] = jnp.zeros_like(dgk_ref)

        qv = q_ref[...]
        kv = k_ref[...]
        rq = _rstd(qv)
        rk = _rstd(kv)
        qs[...] = (qv * rq * gq_ref[...] * scale).astype(BF16)
        kn[...] = (kv * rk * gk_ref[...]).astype(BF16)
        vb[...] = v_ref[...].astype(BF16)
        dob[...] = do_ref[...].astype(BF16)
        dk_acc[...] = jnp.zeros_like(dk_acc)
        dv_acc[...] = jnp.zeros_like(dv_acc)
        row = lax.broadcasted_iota(jnp.int32, (ATT_QB, ATT_KB), 0)
        col = lax.broadcasted_iota(jnp.int32, (ATT_QB, ATT_KB), 1)
        m_strict = (row > col).astype(BF16)
        m_prefix = (row < col).astype(BF16)

        def q_loop(qi, _):
            q0 = pl.multiple_of(qi * ATT_QB, ATT_QB)
            q_t = qs[pl.ds(q0, ATT_QB), :]
            do_t = dob[pl.ds(q0, ATT_QB), :]

            def down(i, carry):
                kj = qi - i
                k0 = pl.multiple_of(kj * ATT_KB, ATT_KB)
                _, lb, lrm, w = _att_tile(q_t, kn[pl.ds(k0, ATT_KB), :], qi, kj, row, col, m_strict, carry)
                dw = _dot_nt(do_t, vb[pl.ds(k0, ATT_KB), :])
                gbuf[kj] = w * dw
                bbuf[kj] = jnp.exp(lb)
                dv_acc[pl.ds(k0, ATT_KB), :] += _dot_tn(w.astype(BF16), do_t)
                return carry + jnp.sum(lrm, axis=-1, keepdims=True)

            lax.fori_loop(0, qi + 1, down, jnp.zeros((ATT_QB, 1), F32))

            def up(kj, c):
                acc, carry = c
                k0 = pl.multiple_of(kj * ATT_KB, ATT_KB)
                g = gbuf[kj]
                beta = bbuf[kj]
                big_g = _split_dot(g, m_prefix) + carry
                mask = (kj * ATT_KB + col) < (qi * ATT_QB + row)
                dz = jnp.where(mask, g * (1.0 - beta) - big_g * beta, 0.0).astype(BF16)
                acc = acc + _dot(dz, kn[pl.ds(k0, ATT_KB), :])
                dk_acc[pl.ds(k0, ATT_KB), :] += _dot_tn(dz, q_t)
                return acc, carry + jnp.sum(g, axis=-1, keepdims=True)

            acc, _ = lax.fori_loop(0, qi + 1, up,
                                   (jnp.zeros((ATT_QB, ATT_DH), F32), jnp.zeros((ATT_QB, 1), F32)))
            dq_acc[pl.ds(q0, ATT_QB), :] = acc
            return 0

        lax.fori_loop(0, nq, q_loop, 0)

        def norm_bwd(xv, r, gain, dyn):
            xh = xv * r
            dgain = jnp.sum(dyn * xh, axis=0, keepdims=True)
            dxh = dyn * gain
            return r * (dxh - xh * jnp.mean(dxh * xh, axis=-1, keepdims=True)), dgain

        dq, dgq = norm_bwd(qv, rq, gq_ref[...], dq_acc[...] * scale)
        dk, dgk = norm_bwd(kv, rk, gk_ref[...], dk_acc[...])
        dq_ref[...] = dq.astype(BF16)
        dk_ref[...] = dk.astype(BF16)
        dv_ref[...] = dv_acc[...].astype(BF16)
        dgq_ref[...] += dgq
        dgk_ref[...] += dgk

    def head(off):
        return pl.BlockSpec((seq, ATT_DH), lambda s: (s // ATT_HEADS, off // ATT_DH + s % ATT_HEADS))

    vec = pl.BlockSpec((1, ATT_DH), lambda s: (0, 0))
    big = jax.ShapeDtypeStruct((nb * seq, ATT_W), BF16)
    small = jax.ShapeDtypeStruct((1, ATT_DH), F32)
    return pl.pallas_call(
        body, name=name, grid=(nb * ATT_HEADS,),
        in_specs=[head(OFF_Q), head(OFF_K), head(OFF_V), head(0), vec, vec],
        out_specs=[head(0), head(0), head(0), vec, vec],
        out_shape=[big, big, big, small, small],
        scratch_shapes=[pltpu.VMEM((seq, ATT_DH), BF16)] * 4 + [pltpu.VMEM((seq, ATT_DH), F32)] * 3
        + [pltpu.VMEM((seq // ATT_KB, ATT_QB, ATT_KB), F32)] * 2,
        compiler_params=_params(("arbitrary",)),
    )(proj, proj, proj, do, gq, gk)


CONV_COLS = 256


def _pack_conv(conv_w, conv_b):
    return jnp.concatenate([conv_w, conv_b[None, :], jnp.zeros((3, CONV_DIM), F32)], axis=0)


def _conv_pre(raw, w8, rowi):
    pre = w8[CONV_K:CONV_K + 1, :] + raw * w8[CONV_K - 1:CONV_K, :]
    for k in range(1, CONV_K):
        sh = jnp.where(rowi >= k, pltpu.roll(raw, k, 0), 0.0)
        pre = pre + sh * w8[CONV_K - 1 - k:CONV_K - k, :]
    return pre


def _conv_fwd(proj, cw8, nb, seq, name):
    ncol = CONV_DIM // CONV_COLS

    def body(x_ref, w_ref, o_ref):
        rowi = lax.broadcasted_iota(jnp.int32, (seq, 1), 0)
        pre = _conv_pre(x_ref[...], w_ref[...], rowi)
        o_ref[...] = pre * _sigmoid(pre)

    return pl.pallas_call(
        body, name=name, grid=(nb, ncol),
        in_specs=[pl.BlockSpec((seq, CONV_COLS), lambda b, j: (b, OFF_XS // CONV_COLS + j)),
                  pl.BlockSpec((8, CONV_COLS), lambda b, j: (0, j))],
        out_specs=pl.BlockSpec((seq, CONV_COLS), lambda b, j: (b, j)),
        out_shape=jax.ShapeDtypeStruct((nb * seq, CONV_DIM), F32),
        compiler_params=_params(("parallel", "parallel")),
    )(proj, cw8)


def _conv_bwd(proj, dact, cw8, nb, seq, name):
    ncol = CONV_DIM // CONV_COLS

    def body(x_ref, d_ref, w_ref, dx_ref, dw_ref):
        @pl.when(pl.program_id(1) == 0)
        def _():
            dw_ref[...] = jnp.zeros_like(dw_ref)

        rowi = lax.broadcasted_iota(jnp.int32, (seq, 1), 0)
        raw = x_ref[...]
        w8 = w_ref[...]
        pre = _conv_pre(raw, w8, rowi)
        sg = _sigmoid(pre)
        dpre = d_ref[...] * (sg * (1.0 + pre * (1.0 - sg)))
        dw_ref[CONV_K:CONV_K + 1, :] += jnp.sum(dpre, axis=0, keepdims=True)
        dw_ref[CONV_K - 1:CONV_K, :] += jnp.sum(dpre * raw, axis=0, keepdims=True)
        draw = dpre * w8[CONV_K - 1:CONV_K, :]
        for k in range(1, CONV_K):
            sh = jnp.where(rowi >= k, pltpu.roll(raw, k, 0), 0.0)
            dw_ref[CONV_K - 1 - k:CONV_K - k, :] += jnp.sum(dpre * sh, axis=0, keepdims=True)
            up = jnp.where(rowi < seq - k, pltpu.roll(dpre, seq - k, 0), 0.0)
            draw = draw + up * w8[CONV_K - 1 - k:CONV_K - k, :]
        dx_ref[...] = draw.astype(BF16)

    return pl.pallas_call(
        body, name=name, grid=(ncol, nb),
        in_specs=[pl.BlockSpec((seq, CONV_COLS), lambda j, b: (b, OFF_XS // CONV_COLS + j)),
                  pl.BlockSpec((seq, CONV_COLS), lambda j, b: (b, j)),
                  pl.BlockSpec((8, CONV_COLS), lambda j, b: (0, j))],
        out_specs=[pl.BlockSpec((seq, CONV_COLS), lambda j, b: (b, j)),
                   pl.BlockSpec((8, CONV_COLS), lambda j, b: (0, j))],
        out_shape=[jax.ShapeDtypeStruct((nb * seq, CONV_DIM), BF16), jax.ShapeDtypeStruct((8, CONV_DIM), F32)],
        compiler_params=_params(("parallel", "arbitrary")),
    )(proj, dact, cw8)


def _pack_heads(dt_bias, a_log, d_skip):
    rows = jnp.stack([dt_bias, a_log, d_skip]).reshape(3, SSM_GROUPS, SSM_HG).transpose(1, 0, 2)
    return jnp.pad(rows, ((0, 0), (0, 8 - 3), (0, LANE - SSM_HG)))


def _split_dot_r(m, x, parts):
    acc = None
    rem = x
    for _ in range(parts):
        hi = rem.astype(BF16)
        d = _dot(m, hi)
        acc = d if acc is None else acc + d
        rem = rem - hi.astype(F32)
    return acc


def _ssd_specs(seq):
    gx = SSM_HG * SSM_P
    return dict(
        xs=pl.BlockSpec((seq, gx), lambda g, b: (b, g)),
        bm=pl.BlockSpec((seq, SSM_N), lambda g, b: (b, SSM_W // SSM_N + g)),
        cm=pl.BlockSpec((seq, SSM_N), lambda g, b: (b, SSM_W // SSM_N + SSM_GROUPS + g)),
        dt=pl.BlockSpec((seq, LANE), lambda g, b: (b, OFF_DT // LANE + g)),
        hp=pl.BlockSpec((1, 8, LANE), lambda g, b: (g, 0, 0)),
        head=pl.BlockSpec((seq, gx), lambda g, b: (b, g)),
        grp=pl.BlockSpec((seq, SSM_N), lambda g, b: (b, g)),
    )


def _ssd_fwd(act, proj, hp, nb, seq, name):
    nc = seq // CHUNK

    def body(xs_ref, b_ref, c_ref, dt_ref, hp_ref, y_ref, dt_s, da_s, hst):
        hpv = hp_ref[0]
        dt = _softplus(dt_ref[...] + hpv[0:1, :])
        a = -jnp.exp(hpv[1:2, :])
        dsk = hpv[2:3, :]
        dt_s[...] = dt
        da_s[...] = dt * a
        hst[...] = jnp.zeros_like(hst)
        li = lax.broadcasted_iota(jnp.int32, (CHUNK, CHUNK), 0)
        si = lax.broadcasted_iota(jnp.int32, (CHUNK, CHUNK), 1)
        tril = (si <= li).astype(BF16)
        causal = li >= si

        def chunk(c, _):
            rows = pl.ds(pl.multiple_of(c * CHUNK, CHUNK), CHUNK)
            acol = _split_dot_r(tril, da_s[rows, :], 3)
            arow = acol.T
            alast = acol[CHUNK - 1:CHUNK, :]
            bb = b_ref[rows, :].astype(BF16)
            cb = c_ref[rows, :].astype(BF16)
            cbm = _dot_nt(cb, bb)
            xc = xs_ref[rows, :]
            dtc = dt_s[rows, :]
            for j in range(SSM_HG):
                a_col = acol[:, j:j + 1]
                decay = jnp.where(causal, jnp.exp(jnp.minimum(a_col - arow[j:j + 1, :], 0.0)), 0.0)
                x_j = xc[:, j * SSM_P:(j + 1) * SSM_P]
                u = x_j * dtc[:, j:j + 1]
                h_in = hst[j]
                y = (_dot((cbm * decay).astype(BF16), u.astype(BF16))
                     + jnp.exp(a_col) * _dot_nt(cb, h_in.astype(BF16)) + dsk[:, j:j + 1] * x_j)
                y_ref[rows, j * SSM_P:(j + 1) * SSM_P] = y
                a_l = alast[:, j:j + 1]
                hst[j] = jnp.exp(a_l) * h_in + _dot_tn((u * jnp.exp(a_l - a_col)).astype(BF16), bb)
            return 0

        lax.fori_loop(0, nc, chunk, 0)

    sp = _ssd_specs(seq)
    return pl.pallas_call(
        body, name=name, grid=(SSM_GROUPS, nb),
        in_specs=[sp["xs"], sp["bm"], sp["cm"], sp["dt"], sp["hp"]],
        out_specs=sp["head"],
        out_shape=jax.ShapeDtypeStruct((nb * seq, SSM_W), F32),
        scratch_shapes=[pltpu.VMEM((seq, LANE), F32)] * 2 + [pltpu.VMEM((SSM_HG, SSM_P, SSM_N), F32)],
        compiler_params=_params(("parallel", "parallel")),
    )(act, act, act, proj, hp)


def _ssd_bwd(act, proj, dy, hp, nb, seq, name):
    nc = seq // CHUNK

    def body(xs_ref, b_ref, c_ref, dt_ref, hp_ref, dy_ref, dxs_ref, db_ref, dc_ref, ddt_ref, dhp_ref,
             dt_s, da_s, ddt_s, hs, lam):
        @pl.when(pl.program_id(1) == 0)
        def _():
            dhp_ref[...] = jnp.zeros_like(dhp_ref)

        hpv = hp_ref[0]
        a = -jnp.exp(hpv[1:2, :])
        dsk = hpv[2:3, :]
        dt_s[...] = _softplus(dt_ref[...] + hpv[0:1, :])
        da_s[...] = dt_s[...] * a
        li = lax.broadcasted_iota(jnp.int32, (CHUNK, CHUNK), 0)
        si = lax.broadcasted_iota(jnp.int32, (CHUNK, CHUNK), 1)
        tril = (si <= li).astype(BF16)
        triu = (si >= li).astype(BF16)
        causal = li >= si
        causal_t = si >= li
        lane = lax.broadcasted_iota(jnp.int32, (1, LANE), 1)
        tril_strict = (si < li).astype(BF16)

        def chunk_rows(c):
            return pl.ds(pl.multiple_of(c * CHUNK, CHUNK), CHUNK)

        hs[0:SSM_HG] = jnp.zeros((SSM_HG, SSM_P, SSM_N), F32)

        def fwd_chunk(c, _):
            rows = chunk_rows(c)
            acol = _split_dot_r(tril, da_s[rows, :], 3)
            alast = acol[CHUNK - 1:CHUNK, :]
            bb = b_ref[rows, :].astype(BF16)
            xc = xs_ref[rows, :]
            dtc = dt_s[rows, :]
            for j in range(SSM_HG):
                a_col = acol[:, j:j + 1]
                a_l = alast[:, j:j + 1]
                u = xc[:, j * SSM_P:(j + 1) * SSM_P] * dtc[:, j:j + 1]
                hs[(c + 1) * SSM_HG + j] = (jnp.exp(a_l) * hs[c * SSM_HG + j]
                                            + _dot_tn((u * jnp.exp(a_l - a_col)).astype(BF16), bb))
            return 0

        lax.fori_loop(0, nc - 1, fwd_chunk, 0)
        lam[...] = jnp.zeros_like(lam)

        def bwd_chunk(i, carry):
            dd_vec, da_vec = carry
            c = nc - 1 - i
            rows = chunk_rows(c)
            acol = _split_dot_r(tril, da_s[rows, :], 3)
            arow = acol.T
            alast = acol[CHUNK - 1:CHUNK, :]
            bb = b_ref[rows, :].astype(BF16)
            cb = c_ref[rows, :].astype(BF16)
            cbm = _dot_nt(cb, bb)
            cbt = _dot_nt(bb, cb)
            xc = xs_ref[rows, :]
            dtc = dt_s[rows, :]
            dyc = dy_ref[rows, :]
            zero = jnp.zeros((CHUNK, CHUNK), F32)
            dcb, dcbt, dc_acc, db_acc, d_a, f_a, dtu = zero, zero, zero, zero, zero, zero, zero
            c_a = jnp.zeros((1, LANE), F32)
            for j in range(SSM_HG):
                a_col = acol[:, j:j + 1]
                seg = a_col - arow[j:j + 1, :]
                decay = jnp.where(causal, jnp.exp(jnp.minimum(seg, 0.0)), 0.0)
                decay_t = jnp.where(causal_t, jnp.exp(jnp.minimum(-seg, 0.0)), 0.0)
                m = cbm * decay
                mt = cbt * decay_t
                sl = slice(j * SSM_P, (j + 1) * SSM_P)
                x_j = xc[:, sl]
                dt_j = dtc[:, j:j + 1]
                dy_j = dyc[:, sl]
                u = x_j * dt_j
                ub = u.astype(BF16)
                dyb = dy_j.astype(BF16)
                h_in = hs[c * SSM_HG + j]
                lm = lam[j]
                hb = h_in.astype(BF16)
                lb = lm.astype(BF16)
                a_l = alast[:, j:j + 1]
                ea = jnp.exp(a_col)
                eb = jnp.exp(a_l - a_col)
                el = jnp.exp(a_l)
                du_off = eb * _dot_nt(bb, lb)
                du = _dot(mt.astype(BF16), dyb) + du_off
                dm = _dot_nt(dyb, ub)
                dmt = _dot_nt(ub, dyb)
                dcb = dcb + dm * decay
                dcbt = dcbt + dmt * decay_t
                y_off = ea * _dot_nt(cb, hb)
                d_a_j = (jnp.sum(dm * m, axis=-1, keepdims=True) - jnp.sum(dmt * mt, axis=-1, keepdims=True)
                         + jnp.sum(dy_j * y_off, axis=-1, keepdims=True))
                f_a_j = jnp.sum(du_off * u, axis=-1, keepdims=True)
                c_a = c_a + jnp.where(lane == j, el * jnp.sum(lm * h_in), 0.0)
                dc_acc = dc_acc + ea * _dot(dyb, hb)
                db_acc = db_acc + eb * _dot(ub, lb)
                lam[j] = el * lm + _dot_tn((ea * dy_j).astype(BF16), cb)
                d_a = jnp.where(lane == j, d_a_j, d_a)
                f_a = jnp.where(lane == j, f_a_j, f_a)
                dtu = jnp.where(lane == j, jnp.sum(du * x_j, axis=-1, keepdims=True), dtu)
                dxs_ref[rows, sl] = du * dt_j + dsk[:, j:j + 1] * dy_j
                dd_vec = dd_vec + jnp.where(lane == j, jnp.sum(dy_j * x_j), 0.0)
            dc_ref[rows, :] = dc_acc + _dot(dcb.astype(BF16), bb)
            db_ref[rows, :] = db_acc + _dot(dcbt.astype(BF16), cb)
            dda = _split_dot_r(triu, d_a, 2) + _split_dot_r(tril_strict, f_a, 2) + c_a
            ddt_s[rows, :] = dda * a + dtu
            da_vec = da_vec + jnp.sum(dda * dtc, axis=0, keepdims=True)
            return dd_vec, da_vec

        zv = jnp.zeros((1, LANE), F32)
        dd_vec, da_vec = lax.fori_loop(0, nc, bwd_chunk, (zv, zv))
        ddt_raw = ddt_s[...] * _sigmoid(dt_ref[...] + hpv[0:1, :])
        ddt_ref[...] = ddt_raw.astype(BF16)
        dhp_ref[0, 0:1, :] += jnp.sum(ddt_raw, axis=0, keepdims=True)
        dhp_ref[0, 1:2, :] += da_vec * a
        dhp_ref[0, 2:3, :] += dd_vec

    sp = _ssd_specs(seq)
    t = nb * seq
    return pl.pallas_call(
        body, name=name, grid=(SSM_GROUPS, nb),
        in_specs=[sp["xs"], sp["bm"], sp["cm"], sp["dt"], sp["hp"], sp["head"]],
        out_specs=[sp["head"], sp["grp"], sp["grp"], sp["grp"], sp["hp"]],
        out_shape=[jax.ShapeDtypeStruct((t, SSM_W), F32), jax.ShapeDtypeStruct((t, SSM_GROUPS * SSM_N), F32),
                   jax.ShapeDtypeStruct((t, SSM_GROUPS * SSM_N), F32),
                   jax.ShapeDtypeStruct((t, SSM_GROUPS * LANE), BF16),
                   jax.ShapeDtypeStruct((SSM_GROUPS, 8, LANE), F32)],
        scratch_shapes=[pltpu.VMEM((seq, LANE), F32)] * 3
        + [pltpu.VMEM((nc * SSM_HG, SSM_P, SSM_N), F32), pltpu.VMEM((SSM_HG, SSM_P, SSM_N), F32)],
        compiler_params=_params(("parallel", "arbitrary")),
    )(act, act, act, proj, hp, dy)


ANY = pl.BlockSpec(memory_space=pl.ANY)


def _block_index(p):
    return 4 * p[0] + 2 * p[1] + p[2]


def _all_gather(shards, name):
    n = len(shards)

    def body(*refs):
        ins, outs = refs[:n], refs[n:2 * n]
        send_sems, recv_sems, local_sems = refs[2 * n:]
        x, y, c = lax.axis_index("x"), lax.axis_index("y"), lax.axis_index("c")
        me, sibling = (x, y, c), (x, y, 1 - c)
        chips = [(1 - x, y), (x, 1 - y), (1 - x, 1 - y)]

        def copy(i, k, block, to, src=None):
            dst = outs[i].at[_block_index(block)]
            return pltpu.make_async_remote_copy(
                src_ref=dst if src is None else src, dst_ref=dst,
                send_sem=send_sems.at[i, k], recv_sem=recv_sems.at[i, k],
                device_id=to, device_id_type=MESH)

        mine = [pltpu.make_async_copy(ins[i], outs[i].at[_block_index(me)], local_sems.at[i]) for i in range(n)]
        for cp in mine:
            cp.start()
        first = []
        for i in range(n):
            first.append(copy(i, 0, me, sibling, src=ins[i]))
            first += [copy(i, 1 + j, me, (*chip, c), src=ins[i]) for j, chip in enumerate(chips)]
        for cp in first:
            cp.start()
        passed = []
        for j, chip in enumerate(chips):
            for i in range(n):
                copy(i, 1 + j, (*chip, c), me).wait_recv()
                fwd = copy(i, 4 + j, (*chip, c), sibling)
                fwd.start()
                passed.append(fwd)
        for i in range(n):
            copy(i, 0, sibling, me).wait_recv()
            for j, chip in enumerate(chips):
                copy(i, 4 + j, (*chip, 1 - c), me).wait_recv()
        for cp in first + passed:
            cp.wait_send()
        for cp in mine:
            cp.wait()

    return pl.pallas_call(
        body, name=name,
        in_specs=[ANY] * n, out_specs=[ANY] * n,
        out_shape=[jax.ShapeDtypeStruct((N_DEV,) + s.shape, s.dtype) for s in shards],
        scratch_shapes=[pltpu.SemaphoreType.DMA((n, 7)), pltpu.SemaphoreType.DMA((n, 7)),
                        pltpu.SemaphoreType.DMA((n,))],
    )(*shards)


HBM = pl.BlockSpec(memory_space=pltpu.HBM)
SEM = pl.BlockSpec(memory_space=pltpu.SEMAPHORE)
EFFECT = pltpu.SideEffectType.DATAFLOW_SIDE_EFFECTING


def _my_block():
    return _block_index((lax.axis_index("x"), lax.axis_index("y"), lax.axis_index("c")))


def _peer(k):
    x, y, c = lax.axis_index("x"), lax.axis_index("y"), lax.axis_index("c")
    return (1 - x if k & 4 else x, 1 - y if k & 2 else y, 1 - c if k & 1 else c)


def _plan_copies(plan, src_refs, land_refs, send_sems, recv_sems):
    me = _my_block()
    copies = []
    for e, (si, di, src_view, dst_view, _) in enumerate(plan):
        for k in range(1, N_DEV):
            copies.append(pltpu.make_async_remote_copy(
                src_ref=src_view(src_refs[si], _block_index(_peer(k))),
                dst_ref=dst_view(land_refs[di], me),
                send_sem=send_sems[e], recv_sem=recv_sems[e],
                device_id=_peer(k), device_id_type=MESH))
    return copies


def _plan_waits(plan, land_refs, send_sems, recv_sems):
    waits = []
    for e, (_, di, _, _, seven) in enumerate(plan):
        view = seven(land_refs[di])
        waits.append(pltpu.make_async_remote_copy(
            src_ref=view, dst_ref=view, send_sem=send_sems[e], recv_sem=recv_sems[e],
            device_id=_peer(1), device_id_type=MESH))
    return waits


def _copies_start(srcs, lands, plan, name, after=None):
    ns, nl, ne = len(srcs), len(lands), len(plan)
    extra = [] if after is None else [after]

    nin = ns + nl + len(extra)

    def body(*refs):
        src_refs, land_refs = refs[:ns], refs[ns:ns + nl]
        send_sems, recv_sems = refs[nin:nin + ne], refs[nin + ne:nin + 2 * ne]
        token = refs[-1]
        for cp in _plan_copies(plan, src_refs, land_refs, send_sems, recv_sems):
            cp.start()
        token[...] = jnp.zeros_like(token)

    thru = [pltpu.HBM(a.shape, a.dtype) for a in list(srcs) + list(lands)]
    res = pl.pallas_call(
        body, name=name,
        in_specs=[HBM] * (ns + nl) + [ANY] * len(extra),
        out_specs=[SEM] * (2 * ne) + [HBM] * (ns + nl) + [pl.BlockSpec(memory_space=pltpu.VMEM)],
        out_shape=[pltpu.SemaphoreType.DMA(())] * (2 * ne) + thru + [jax.ShapeDtypeStruct((8, LANE), F32)],
        input_output_aliases={i: 2 * ne + i for i in range(ns + nl)},
        compiler_params=pltpu.CompilerParams(has_side_effects=EFFECT),
    )(*[pltpu.with_memory_space_constraint(a, pltpu.HBM) for a in list(srcs) + list(lands)], *extra)
    return dict(sems=res[:2 * ne], srcs=res[2 * ne:2 * ne + ns], lands=res[2 * ne + ns:2 * ne + ns + nl],
                token=res[-1], plan=plan)


def _copies_wait(flight, after, name):
    srcs, lands, plan = flight["srcs"], flight["lands"], flight["plan"]
    ns, nl, ne = len(srcs), len(lands), len(plan)

    def body(*refs):
        land_refs = refs[ns:ns + nl]
        send_sems, recv_sems = refs[ns + nl:ns + nl + ne], refs[ns + nl + ne:ns + nl + 2 * ne]
        for cp in _plan_waits(plan, land_refs, send_sems, recv_sems):
            cp.wait_send()
            cp.wait_recv()

    thru = [pltpu.HBM(a.shape, a.dtype) for a in list(srcs) + list(lands)]
    res = pl.pallas_call(
        body, name=name,
        in_specs=[HBM] * (ns + nl) + [SEM] * (2 * ne) + [ANY],
        out_specs=[HBM] * (ns + nl),
        out_shape=thru,
        input_output_aliases={i: i for i in range(ns + nl)},
        compiler_params=pltpu.CompilerParams(has_side_effects=EFFECT),
    )(*srcs, *lands, *flight["sems"], after)
    return list(res[ns:])


def _place_own(land, own, start):
    return lax.dynamic_update_slice(land, own, start)


def _adamw_math(w, g, m, v):
    m = ADAM_B1 * m + (1.0 - ADAM_B1) * g
    v = ADAM_B2 * v + (1.0 - ADAM_B2) * (g * g)
    m_hat = m / (1.0 - ADAM_B1 ** ADAM_STEP)
    v_hat = v / (1.0 - ADAM_B2 ** ADAM_STEP)
    delta = -ADAM_LR * (m_hat / (jnp.sqrt(v_hat) + ADAM_EPS) + ADAM_WD * w)
    return delta, m, v


def _adamw(parts, w, m, v, name, rows):
    depth, r, c = w.shape
    cp = parts[0].shape[2]
    assert r % rows == 0 and len(parts) == depth

    def body(*refs):
        p_refs = refs[:depth]
        w_ref, m_ref, v_ref, g_ref, d_ref, mo_ref, vo_ref = refs[depth:]
        for li in range(depth):
            @pl.when(pl.program_id(0) == li)
            def _(li=li):
                g = p_refs[li][0][:, :c].astype(F32)
                for j in range(1, N_DEV):
                    g = g + p_refs[li][j][:, :c].astype(F32)
                d, mn, vn = _adamw_math(w_ref[...], g, m_ref[...], v_ref[...])
                g_ref[...] = g
                d_ref[...] = d
                mo_ref[...] = mn
                vo_ref[...] = vn

    def part_spec(li):
        return pl.BlockSpec((N_DEV, rows, cp), lambda l, i: (0, jnp.where(l == li, i, 0), 0))

    blk = pl.BlockSpec((None, rows, c), lambda l, i: (l, i, 0))
    out = jax.ShapeDtypeStruct((depth, r, c), F32)
    return pl.pallas_call(
        body, name=name, grid=(depth, r // rows),
        in_specs=[part_spec(li) for li in range(depth)] + [blk, blk, blk],
        out_specs=[blk] * 4, out_shape=[out] * 4,
        compiler_params=_params(("arbitrary", "arbitrary")),
    )(*parts, w, m, v)


def _sum_parts(parts, name):
    _, r, c = parts.shape

    def body(p_ref, o_ref):
        g = p_ref[0]
        for j in range(1, N_DEV):
            g = g + p_ref[j]
        o_ref[...] = g

    return pl.pallas_call(
        body, name=name, out_shape=jax.ShapeDtypeStruct((r, c), F32),
        compiler_params=_params(),
    )(parts)


def _adamw_small(g, w, m, v, name):
    def body(g_ref, w_ref, m_ref, v_ref, d_ref, mo_ref, vo_ref):
        d, mn, vn = _adamw_math(w_ref[...], g_ref[...], m_ref[...], v_ref[...])
        d_ref[...] = d
        mo_ref[...] = mn
        vo_ref[...] = vn

    out = jax.ShapeDtypeStruct(w.shape, F32)
    return pl.pallas_call(body, name=name, out_shape=[out] * 3, compiler_params=_params())(g, w, m, v)


SMALL = ("norm_mix", "q_gain", "k_gain", "conv_b", "dt_bias", "a_log", "d_skip", "attn_out_gain",
         "ssm_out_gain", "norm_ffn")


def _pad_lanes(a):
    n = a.shape[-1]
    return jnp.pad(a, ((0, 0), (0, -n % LANE)))


def _pack_small(d):
    return jnp.concatenate([_pad_lanes(d[k]) for k in SMALL], axis=1)


def _unpack_small(packed, like):
    out, off = {}, 0
    for k in SMALL:
        n = like[k].shape[-1]
        out[k] = packed[:, off:off + n]
        off += n + (-n % LANE)
    return out


def _full_cols(gathered):
    _, r, c = gathered.shape
    return gathered.transpose(1, 0, 2).reshape(r, N_DEV * c)


def _col_blocks(full):
    r, c8 = full.shape
    return full.reshape(r, N_DEV, c8 // N_DEV).transpose(1, 0, 2)


FF_BLK = 768
FF_PAD = N_DEV * FF_BLK


def _whole(ref, block):
    return ref


def _rows_of(size):
    return lambda ref, block: ref.at[pl.ds(pl.multiple_of(block * size, size), size), :]


def _cols_of(size, base=0):
    return lambda ref, block: ref.at[:, pl.ds(pl.multiple_of(base + block * size, LANE), size)]


def _slot(ref, block):
    return ref.at[block]


def _seven_slots(ref):
    return ref.at[pl.ds(0, N_DEV - 1)]


def _seven_rows(size):
    return lambda ref: ref.at[pl.ds(0, (N_DEV - 1) * size), :]


def _seven_cols(size):
    return lambda ref: ref.at[:, pl.ds(0, (N_DEV - 1) * size)]


GATHER_A = [(0, 0, _whole, _slot, _seven_slots), (1, 1, _whole, _rows_of(256), _seven_rows(256))]
GATHER_B = [(0, 0, _whole, _cols_of(FF_BLK), _seven_cols(FF_BLK)),
            (1, 0, _whole, _cols_of(FF_BLK, FF_PAD), _seven_cols(FF_BLK)),
            (2, 1, _whole, _rows_of(FF_BLK), _seven_rows(FF_BLK))]
SCATTER_A = [(0, 0, _slot, _slot, _seven_slots), (1, 1, _rows_of(256), _slot, _seven_slots)]
SCATTER_B = [(0, 0, _cols_of(FF_BLK), _slot, _seven_slots), (0, 1, _cols_of(FF_BLK, FF_PAD), _slot, _seven_slots),
             (1, 2, _rows_of(FF_BLK), _slot, _seven_slots)]


def _gather_lands(which, shards, me, d):
    if which == "a":
        w_in, w_out = shards
        return [_place_own(lax.empty((N_DEV,) + w_in.shape, BF16), w_in[None], (me, 0, 0)),
                _place_own(lax.empty((d, d), BF16), w_out, (me * w_out.shape[0], 0))]
    w_gate, w_up, w_down = shards
    gu = _place_own(lax.empty((d, 2 * FF_PAD), BF16), w_gate, (0, me * FF_BLK))
    gu = _place_own(gu, w_up, (0, FF_PAD + me * FF_BLK))
    return [gu, _place_own(lax.empty((FF_PAD, d), BF16), w_down, (me * FF_BLK, 0))]


def _scatter_lands(which, grads, me):
    def own(block):
        return _place_own(lax.empty((N_DEV,) + block.shape, BF16), block[None], (me, 0, 0))

    if which == "a":
        g_in, g_out = grads
        rows = g_out.shape[0] // N_DEV
        return [own(lax.dynamic_index_in_dim(g_in, me, 0, keepdims=False)),
                own(lax.dynamic_slice_in_dim(g_out, me * rows, rows, 0))]
    g_gu, g_down = grads
    return [own(lax.dynamic_slice_in_dim(g_gu, me * FF_BLK, FF_BLK, 1)),
            own(lax.dynamic_slice_in_dim(g_gu, FF_PAD + me * FF_BLK, FF_BLK, 1)),
            own(lax.dynamic_slice_in_dim(g_down, me * FF_BLK, FF_BLK, 0))]


def _pad_w_in(full):
    d = full.shape[0]
    z = jnp.zeros((d, LANE - SSM_HG), full.dtype)
    tail = jnp.zeros((d, NPROJ - OFF_DT - SSM_GROUPS * LANE), full.dtype)
    return jnp.concatenate([full[:, :OFF_DT], full[:, OFF_DT:OFF_DT + SSM_HG], z,
                            full[:, OFF_DT + SSM_HG:IN_DIM], z, tail], axis=1)


def _unpad_w_in(padded):
    return jnp.concatenate([padded[:, :OFF_DT], padded[:, OFF_DT:OFF_DT + SSM_HG],
                            padded[:, OFF_DT + LANE:OFF_DT + LANE + SSM_HG]], axis=1)


def kernel(x, norm_mix, w_in, q_gain, k_gain, conv_w, conv_b, dt_bias, a_log, d_skip, attn_out_gain, ssm_out_gain, w_out, norm_ffn, w_gate, w_up, w_down, loss_target, m_norm_mix, m_w_in, m_q_gain, m_k_gain, m_conv_w, m_conv_b, m_dt_bias, m_a_log, m_d_skip, m_attn_out_gain, m_ssm_out_gain, m_w_out, m_norm_ffn, m_w_gate, m_w_up, m_w_down, v_norm_mix, v_w_in, v_q_gain, v_k_gain, v_conv_w, v_conv_b, v_dt_bias, v_a_log, v_d_skip, v_attn_out_gain, v_ssm_out_gain, v_w_out, v_norm_ffn, v_w_gate, v_w_up, v_w_down):
    nb, seq, d = x.shape
    depth = w_in.shape[0]
    t = nb * seq
    w = dict(norm_mix=norm_mix, w_in=w_in, q_gain=q_gain, k_gain=k_gain, conv_w=conv_w, conv_b=conv_b,
             dt_bias=dt_bias, a_log=a_log, d_skip=d_skip, attn_out_gain=attn_out_gain, ssm_out_gain=ssm_out_gain,
             w_out=w_out, norm_ffn=norm_ffn, w_gate=w_gate, w_up=w_up, w_down=w_down)
    mom = dict(norm_mix=m_norm_mix, w_in=m_w_in, q_gain=m_q_gain, k_gain=m_k_gain, conv_w=m_conv_w, conv_b=m_conv_b,
               dt_bias=m_dt_bias, a_log=m_a_log, d_skip=m_d_skip, attn_out_gain=m_attn_out_gain,
               ssm_out_gain=m_ssm_out_gain, w_out=m_w_out, norm_ffn=m_norm_ffn, w_gate=m_w_gate, w_up=m_w_up,
               w_down=m_w_down)
    var = dict(norm_mix=v_norm_mix, w_in=v_w_in, q_gain=v_q_gain, k_gain=v_k_gain, conv_w=v_conv_w, conv_b=v_conv_b,
               dt_bias=v_dt_bias, a_log=v_a_log, d_skip=v_d_skip, attn_out_gain=v_attn_out_gain,
               ssm_out_gain=v_ssm_out_gain, w_out=v_w_out, norm_ffn=v_norm_ffn, w_gate=v_w_gate, w_up=v_w_up,
               w_down=v_w_down)
    me = _my_block()
    ff = w_gate.shape[2]

    (conv_all,) = _all_gather([conv_w], "gather_conv")

    def shards_a(li):
        return [w_in[li].astype(BF16), w_out[li].astype(BF16)]

    def shards_b(li):
        return [jnp.pad(w_gate[li].astype(BF16), ((0, 0), (0, FF_BLK - ff))),
                jnp.pad(w_up[li].astype(BF16), ((0, 0), (0, FF_BLK - ff))),
                jnp.pad(w_down[li].astype(BF16), ((0, FF_BLK - ff), (0, 0)))]

    def small_params(li):
        p = {k: w[k][li][None, :] for k in ("norm_mix", "q_gain", "k_gain", "attn_out_gain", "ssm_out_gain", "norm_ffn")}
        conv_full = conv_all[:, li].transpose(1, 0, 2).reshape(CONV_K, CONV_DIM)
        p["cw8"] = _pack_conv(conv_full, conv_b[li])
        p["hp"] = _pack_heads(dt_bias[li], a_log[li], d_skip[li])
        return p

    xc = x.reshape(t, d)
    flight = _copies_start(shards_a(0), _gather_lands("a", shards_a(0), me, d), GATHER_A, "gather_a0")
    lands_a = _copies_wait(flight, xc, "gather_a0_wait")
    layers, saved = [], []
    for li in range(depth):
        tag = f"l{li}_"
        p = small_params(li)
        p["w_in"] = _pad_w_in(_full_cols(lands_a[0]))
        p["w_out"] = lands_a[1]
        cur = shards_b(li)
        flight = _copies_start(cur, _gather_lands("b", cur, me, d), GATHER_B, tag + "gather_b", after=lands_a[1])
        h1 = _rms_fwd(xc, p["norm_mix"], tag + "rms1", after=flight["token"])
        proj = _matmul(h1, p["w_in"], "nn", F32, tag + "mm_in")
        o = _attn_fwd(proj, p["q_gain"], p["k_gain"], nb, seq, tag + "attn")
        act = _conv_fwd(proj, p["cw8"], nb, seq, tag + "conv")
        y = _ssd_fwd(act, proj, p["hp"], nb, seq, tag + "ssd")
        cat = _mix_fwd(o, y, proj, p["attn_out_gain"], p["ssm_out_gain"], tag + "mix")
        x1 = _matmul(cat, p["w_out"], "nn", F32, tag + "mm_out", residual=xc)
        p["w_gu"], p["w_down"] = _copies_wait(flight, x1, tag + "gather_b_wait")
        token = None
        if li + 1 < depth:
            nxt = shards_a(li + 1)
            flight = _copies_start(nxt, _gather_lands("a", nxt, me, d), GATHER_A, f"gather_a{li + 1}",
                                   after=p["w_down"])
            token = flight["token"]
        h2 = _rms_fwd(x1, p["norm_ffn"], tag + "rms2", after=token)
        gu = _matmul(h2, p["w_gu"], "nn", F32, tag + "mm_gu")
        a = _swiglu_fwd(gu, tag + "swiglu")
        x2 = _matmul(a, p["w_down"], "nn", F32, tag + "mm_down", residual=x1)
        if li + 1 < depth:
            lands_a = _copies_wait(flight, x2, f"gather_a{li + 1}_wait")
        saved.append(dict(x=xc, h1=h1, proj=proj, o=o, act=act, y=y, cat=cat, x1=x1, h2=h2, gu=gu, a=a))
        layers.append(p)
        xc = x2

    loss_blk, dx, dxb = _loss_fwd_bwd(xc, loss_target.reshape(t, d), "loss")
    loss = lax.psum(loss_blk[0, 0], ("x", "y", "c"))

    grads = [dict() for _ in range(depth)]
    recv = [dict() for _ in range(depth)]
    flight_a, token = None, None
    for li in reversed(range(depth)):
        tag = f"l{li}_b_"
        p, s, g = layers[li], saved[li], grads[li]
        dact = _matmul(dxb, p["w_down"], "nt", F32, tag + "mm_dact", after=token)
        g_down = _matmul(s["a"], dxb, "tn", BF16, tag + "mm_dwd")
        dgu = _swiglu_bwd(s["gu"], dact, tag + "swiglu")
        dh2 = _matmul(dgu, p["w_gu"], "nt", F32, tag + "mm_dh2")
        g_gu = _matmul(s["h2"], dgu, "tn", BF16, tag + "mm_dwgu")
        if flight_a is not None:
            recv[li + 1]["w_in"], recv[li + 1]["w_out"] = _copies_wait(flight_a, g_gu, f"l{li + 1}_b_scatter_a_wait")
        flight_b = _copies_start([g_gu, g_down], _scatter_lands("b", [g_gu, g_down], me), SCATTER_B,
                                 tag + "scatter_b", after=recv[li + 1]["w_out"] if li + 1 < depth else None)
        dx1, dx1b, g["norm_ffn"] = _rms_bwd(s["x1"], p["norm_ffn"], dh2, dx, tag + "rms2")
        dcat = _matmul(dx1b, p["w_out"], "nt", F32, tag + "mm_dcat", after=flight_b["token"])
        g_out = _matmul(s["cat"], dx1b, "tn", BF16, tag + "mm_dwo")
        do, dy, dz, g["attn_out_gain"], g["ssm_out_gain"] = _mix_bwd(
            dcat, s["o"], s["y"], s["proj"], p["attn_out_gain"], p["ssm_out_gain"], tag + "mix")
        dq, dk, dv, g["q_gain"], g["k_gain"] = _attn_bwd(s["proj"], do, p["q_gain"], p["k_gain"], nb, seq, tag + "attn")
        dxa, dba, dca, ddt, dhp = _ssd_bwd(s["act"], s["proj"], dy, p["hp"], nb, seq, tag + "ssd")
        dxbc, dcw8 = _conv_bwd(s["proj"], jnp.concatenate([dxa, dba, dca], axis=1), p["cw8"], nb, seq, tag + "conv")
        g["conv_w"] = dcw8[0:CONV_K]
        g["conv_b"] = dcw8[CONV_K:CONV_K + 1]
        heads = dhp[:, 0:3, 0:SSM_HG].transpose(1, 0, 2).reshape(3, SSM_HEADS)
        g["dt_bias"], g["a_log"], g["d_skip"] = heads[0:1], heads[1:2], heads[2:3]
        tail = jnp.zeros((t, NPROJ - OFF_DT - SSM_GROUPS * LANE), BF16)
        dproj = jnp.concatenate([dq, dk, dv, dz, dxbc, ddt, tail], axis=1)
        recv[li]["w_gate"], recv[li]["w_up"], recv[li]["w_down"] = _copies_wait(flight_b, dproj, tag + "scatter_b_wait")
        dh1 = _matmul(dproj, p["w_in"], "nt", F32, tag + "mm_dh1")
        g_in = _col_blocks(_unpad_w_in(_matmul(s["h1"], dproj, "tn", BF16, tag + "mm_dwin")))
        flight_a = _copies_start([g_in, g_out], _scatter_lands("a", [g_in, g_out], me), SCATTER_A, tag + "scatter_a")
        token = flight_a["token"]
        dx, dxb, g["norm_mix"] = _rms_bwd(s["x"], p["norm_mix"], dh1, dx1, tag + "rms1")
    grad_x = dx.reshape(nb, seq, d)

    out_g, out_d, out_m, out_v = {}, {}, {}, {}

    def update(k, rows):
        parts = [recv[li][k] for li in range(depth)]
        out_g[k], out_d[k], out_m[k], out_v[k] = _adamw(parts, w[k], mom[k], var[k], "adamw_" + k, rows)

    update("w_gate", 128)
    update("w_up", 128)
    update("w_down", 64)
    recv[0]["w_in"], recv[0]["w_out"] = _copies_wait(flight_a, out_g["w_down"], "l0_b_scatter_a_wait")
    update("w_in", 128)
    update("w_out", 128)

    small_g = {k: jnp.concatenate([grads[li][k] for li in range(depth)], axis=0) for k in SMALL}
    conv_g = jnp.stack([grads[li]["conv_w"] for li in range(depth)]).reshape(depth, CONV_K * CONV_DIM)
    packed_g = jnp.concatenate([_pack_small(small_g), conv_g], axis=1)
    (parts_small,) = _all_gather([packed_g], "gather_small_grads")
    g_small = _sum_parts(parts_small, "sum_small_grads")
    n_small = g_small.shape[1] - CONV_K * CONV_DIM
    conv_total = g_small[:, n_small:].reshape(depth, CONV_K, CONV_DIM)
    me = _block_index((lax.axis_index("x"), lax.axis_index("y"), lax.axis_index("c")))
    cshard = conv_w.shape[2]
    conv_mine = lax.dynamic_slice_in_dim(conv_total, me * cshard, cshard, axis=2)

    def with_conv(small_packed, conv_part):
        return jnp.concatenate([small_packed, conv_part.reshape(depth, CONV_K * cshard)], axis=1)

    res = _adamw_small(with_conv(g_small[:, :n_small], conv_mine),
                       with_conv(_pack_small(w), conv_w), with_conv(_pack_small(mom), m_conv_w),
                       with_conv(_pack_small(var), v_conv_w), "adamw_small")
    g_all = with_conv(g_small[:, :n_small], conv_mine)
    for dst, packed in zip((out_g, out_d, out_m, out_v), (g_all,) + tuple(res)):
        dst.update(_unpack_small(packed[:, :n_small], w))
        dst["conv_w"] = packed[:, n_small:].reshape(depth, CONV_K, cshard)

    names = ("norm_mix", "w_in", "q_gain", "k_gain", "conv_w", "conv_b", "dt_bias", "a_log", "d_skip",
             "attn_out_gain", "ssm_out_gain", "w_out", "norm_ffn", "w_gate", "w_up", "w_down")
    return (loss, grad_x, *[out_g[k] for k in names], *[out_d[k] for k in names],
            *[out_m[k] for k in names], *[out_v[k] for k in names])
```

```python
import functools
import math

import jax
import jax.numpy as jnp
from jax import lax
from jax.experimental import pallas as pl
from jax.experimental.pallas import tpu as pltpu

F32 = jnp.float32
BF16 = jnp.bfloat16
MESH = pl.DeviceIdType.MESH

N_DEV = 8
EPS = 1e-6
ATT_HEADS = 8
ATT_DH = 128
ATT_W = ATT_HEADS * ATT_DH
SSM_W = 1024
SSM_P = 64
SSM_N = 128
SSM_GROUPS = 2
SSM_HG = 8
SSM_HEADS = SSM_GROUPS * SSM_HG
CHUNK = 128
CONV_K = 4
CONV_DIM = SSM_W + 2 * SSM_GROUPS * SSM_N
LANE = 128
OFF_Q, OFF_K, OFF_V, OFF_Z, OFF_XS = 0, ATT_W, 2 * ATT_W, 3 * ATT_W, 4 * ATT_W
OFF_B = OFF_XS + SSM_W
OFF_C = OFF_B + SSM_GROUPS * SSM_N
OFF_DT = OFF_C + SSM_GROUPS * SSM_N
NPROJ = 6144
IN_DIM = OFF_DT + SSM_HEADS

ADAM_LR = 0.001
ADAM_B1 = 0.9
ADAM_B2 = 0.999
ADAM_EPS = 1e-08
ADAM_WD = 0.01
ADAM_STEP = 10

VMEM_LIMIT = 56 * 1024 * 1024


def _params(sem=None):
    return pltpu.CompilerParams(dimension_semantics=sem, vmem_limit_bytes=VMEM_LIMIT)


def _pick(dim, target):
    if dim <= target:
        return dim
    best = None
    for t in range(LANE, target + 1, LANE):
        if dim % t == 0:
            best = t
    assert best is not None, (dim, target)
    return best


def _dot(a, b, dims=((1,), (0,))):
    return lax.dot_general(a, b, (dims, ((), ())), preferred_element_type=F32)


def _dot_nt(a, b):
    return _dot(a, b, ((1,), (1,)))


def _dot_tn(a, b):
    return _dot(a, b, ((0,), (0,)))


def _sigmoid(x):
    return 1.0 / (1.0 + jnp.exp(-x))


def _softplus(x):
    return jnp.maximum(x, 0.0) + jnp.log(1.0 + jnp.exp(-jnp.abs(x)))


def _rstd(x):
    return lax.rsqrt(jnp.mean(x * x, axis=-1, keepdims=True) + EPS)


def _matmul(a, b, mode, out_dtype, name, residual=None, after=None, tm=512, tn=1024, tk=2048):
    if mode == "nn":
        (m, k), (k2, n) = a.shape, b.shape
    elif mode == "nt":
        (m, k), (n, k2) = a.shape, b.shape
    else:
        (k, m), (k2, n) = a.shape, b.shape
    assert k == k2, (a.shape, b.shape, mode)
    tm, tn, tk = _pick(m, tm), _pick(n, tn), _pick(k, tk)
    nk = k // tk
    dims = {"nn": ((1,), (0,)), "nt": ((1,), (1,)), "tn": ((0,), (0,))}[mode]
    has_res = residual is not None

    has_tok = after is not None

    def body(*refs):
        a_ref, b_ref = refs[:2]
        r_ref = refs[2] if has_res else None
        o_ref = refs[2 + has_res + has_tok]
        prod = _dot(a_ref[...], b_ref[...], dims)

        def finish(r):
            if r_ref is not None:
                r = r + r_ref[...]
            o_ref[...] = r.astype(o_ref.dtype)

        if nk == 1:
            finish(prod)
        else:
            acc = refs[-1]
            kk = pl.program_id(2)

            @pl.when(kk == 0)
            def _():
                acc[...] = prod

            @pl.when(kk > 0)
            def _():
                acc[...] += prod

            @pl.when(kk == nk - 1)
            def _():
                finish(acc[...])

    if mode == "tn":
        a_spec = pl.BlockSpec((tk, tm), lambda i, j, kk: (kk, i))
    else:
        a_spec = pl.BlockSpec((tm, tk), lambda i, j, kk: (i, kk))
    if mode == "nt":
        b_spec = pl.BlockSpec((tn, tk), lambda i, j, kk: (j, kk))
    else:
        b_spec = pl.BlockSpec((tk, tn), lambda i, j, kk: (kk, j))
    o_spec = pl.BlockSpec((tm, tn), lambda i, j, kk: (i, j))
    tok_spec = pl.BlockSpec((8, LANE), lambda i, j, kk: (0, 0))
    in_specs = [a_spec, b_spec] + ([o_spec] if has_res else []) + ([tok_spec] if has_tok else [])
    args = (a, b) + ((residual,) if has_res else ()) + ((after,) if has_tok else ())
    return pl.pallas_call(
        body,
        name=name,
        grid=(m // tm, n // tn, nk),
        in_specs=in_specs,
        out_specs=o_spec,
        out_shape=jax.ShapeDtypeStruct((m, n), out_dtype),
        scratch_shapes=[pltpu.VMEM((tm, tn), F32)] if nk > 1 else [],
        compiler_params=_params(("parallel", "parallel", "arbitrary")),
    )(*args)


def _mm_swiglu(h, wg, wu, name, tm=512, tn=1024):
    m, k = h.shape
    n = wg.shape[1]
    tm, tn = _pick(m, tm), _pick(n, tn)

    def body(h_ref, g_ref, u_ref, a_ref, gs_ref, us_ref):
        hv = h_ref[...]
        g = _dot(hv, g_ref[...])
        u = _dot(hv, u_ref[...])
        a_ref[...] = (g * _sigmoid(g) * u).astype(BF16)
        gs_ref[...] = g.astype(BF16)
        us_ref[...] = u.astype(BF16)

    w_spec = pl.BlockSpec((k, tn), lambda i, j: (0, j))
    o_spec = pl.BlockSpec((tm, tn), lambda i, j: (i, j))
    out = jax.ShapeDtypeStruct((m, n), BF16)
    return pl.pallas_call(
        body, name=name, grid=(m // tm, n // tn),
        in_specs=[pl.BlockSpec((tm, k), lambda i, j: (i, 0)), w_spec, w_spec],
        out_specs=[o_spec] * 3, out_shape=[out] * 3,
        compiler_params=_params(("parallel", "parallel")),
    )(h, wg, wu)


def _mm_dact_swiglu(dx, wd, gs, us, name, after=None, tm=512, tn=1024):
    m, k = dx.shape
    n = wd.shape[0]
    tm, tn = _pick(m, tm), _pick(n, tn)
    has_tok = after is not None

    def body(*refs):
        dx_ref, wd_ref, g_ref, u_ref = refs[:4]
        dg_ref, du_ref = refs[-2:]
        dact = _dot_nt(dx_ref[...], wd_ref[...])
        g = g_ref[...].astype(F32)
        sg = _sigmoid(g)
        dg_ref[...] = (dact * u_ref[...].astype(F32) * sg * (1.0 + g * (1.0 - sg))).astype(BF16)
        du_ref[...] = (dact * g * sg).astype(BF16)

    o_spec = pl.BlockSpec((tm, tn), lambda i, j: (i, j))
    tok = [pl.BlockSpec((8, LANE), lambda i, j: (0, 0))] if has_tok else []
    out = jax.ShapeDtypeStruct((m, n), BF16)
    return pl.pallas_call(
        body, name=name, grid=(m // tm, n // tn),
        in_specs=[pl.BlockSpec((tm, k), lambda i, j: (i, 0)), pl.BlockSpec((tn, k), lambda i, j: (j, 0)),
                  o_spec, o_spec] + tok,
        out_specs=[o_spec] * 2, out_shape=[out] * 2,
        compiler_params=_params(("parallel", "parallel")),
    )(dx, wd, gs, us, *((after,) if has_tok else ()))


ROWS = 512


def _rms_fwd(x, g, name, after=None):
    t, d = x.shape
    has_tok = after is not None

    def body(*refs):
        x_ref, g_ref, o_ref = refs[0], refs[1], refs[-1]
        xv = x_ref[...]
        o_ref[...] = (xv * _rstd(xv) * g_ref[...]).astype(BF16)

    row = pl.BlockSpec((ROWS, d), lambda i: (i, 0))
    tok = [pl.BlockSpec((8, LANE), lambda i: (0, 0))] if has_tok else []
    return pl.pallas_call(
        body, name=name, grid=(t // ROWS,),
        in_specs=[row, pl.BlockSpec((1, d), lambda i: (0, 0))] + tok,
        out_specs=row, out_shape=jax.ShapeDtypeStruct((t, d), BF16),
        compiler_params=_params(("parallel",)),
    )(x, g, *((after,) if has_tok else ()))


def _rms_bwd(x, g, dh, dres, name):
    t, d = x.shape

    def body(x_ref, g_ref, dh_ref, dr_ref, dx_ref, dxb_ref, dg_ref):
        xv = x_ref[...]
        r = _rstd(xv)
        xh = xv * r
        dhv = dh_ref[...]

        @pl.when(pl.program_id(0) == 0)
        def _():
            dg_ref[...] = jnp.zeros_like(dg_ref)

        dg_ref[...] += jnp.sum(dhv * xh, axis=0, keepdims=True)
        dxh = dhv * g_ref[...]
        dx = r * (dxh - xh * jnp.mean(dxh * xh, axis=-1, keepdims=True)) + dr_ref[...]
        dx_ref[...] = dx
        dxb_ref[...] = dx.astype(BF16)

    row = pl.BlockSpec((ROWS // 2, d), lambda i: (i, 0))
    vec = pl.BlockSpec((1, d), lambda i: (0, 0))
    return pl.pallas_call(
        body, name=name, grid=(t // (ROWS // 2),),
        in_specs=[row, vec, row, row],
        out_specs=[row, row, vec],
        out_shape=[jax.ShapeDtypeStruct((t, d), F32), jax.ShapeDtypeStruct((t, d), BF16),
                   jax.ShapeDtypeStruct((1, d), F32)],
        compiler_params=_params(("arbitrary",)),
    )(x, g, dh, dres)


def _loss_fwd_bwd(y, target, name):
    t, d = y.shape
    inv = 1.0 / d

    def body(y_ref, t_ref, l_ref, dy_ref, dyb_ref):
        e = y_ref[...] - t_ref[...]

        @pl.when(pl.program_id(0) == 0)
        def _():
            l_ref[...] = jnp.zeros_like(l_ref)

        l_ref[...] += 0.5 * inv * jnp.sum(e * e)
        dy = e * inv
        dy_ref[...] = dy
        dyb_ref[...] = dy.astype(BF16)

    row = pl.BlockSpec((ROWS, d), lambda i: (i, 0))
    return pl.pallas_call(
        body, name=name, grid=(t // ROWS,),
        in_specs=[row, row],
        out_specs=[pl.BlockSpec((8, LANE), lambda i: (0, 0)), row, row],
        out_shape=[jax.ShapeDtypeStruct((8, LANE), F32), jax.ShapeDtypeStruct((t, d), F32),
                   jax.ShapeDtypeStruct((t, d), BF16)],
        compiler_params=_params(("arbitrary",)),
    )(y, target)


MIX_ROWS = 256


def _mix_fwd(o, y, proj, ga, gs, name):
    t = o.shape[0]
    gw = SSM_W // SSM_GROUPS

    def body(o_ref, y_ref, z_ref, ga_ref, gs_ref, c_ref):
        ov = o_ref[...]
        c_ref[:, 0:ATT_W] = (ov * _rstd(ov) * ga_ref[...]).astype(BF16)
        zv = z_ref[...]
        yz = y_ref[...] * (zv * _sigmoid(zv))
        for gi in range(SSM_GROUPS):
            seg = yz[:, gi * gw:(gi + 1) * gw]
            c_ref[:, ATT_W + gi * gw:ATT_W + (gi + 1) * gw] = (
                seg * _rstd(seg) * gs_ref[:, gi * gw:(gi + 1) * gw]).astype(BF16)

    half = pl.BlockSpec((MIX_ROWS, ATT_W), lambda i: (i, 0))
    vec = pl.BlockSpec((1, ATT_W), lambda i: (0, 0))
    return pl.pallas_call(
        body, name=name, grid=(t // MIX_ROWS,),
        in_specs=[half, half, pl.BlockSpec((MIX_ROWS, ATT_W), lambda i: (i, OFF_Z // ATT_W)), vec, vec],
        out_specs=pl.BlockSpec((MIX_ROWS, 2 * ATT_W), lambda i: (i, 0)),
        out_shape=jax.ShapeDtypeStruct((t, 2 * ATT_W), BF16),
        compiler_params=_params(("parallel",)),
    )(o, y, proj, ga, gs)


def _mix_bwd(dcat, o, y, proj, ga, gs, name):
    t = o.shape[0]
    gw = SSM_W // SSM_GROUPS

    def body(dc_ref, o_ref, y_ref, z_ref, ga_ref, gs_ref, do_ref, dy_ref, dz_ref, dga_ref, dgs_ref):
        @pl.when(pl.program_id(0) == 0)
        def _():
            dga_ref[...] = jnp.zeros_like(dga_ref)
            dgs_ref[...] = jnp.zeros_like(dgs_ref)

        ov = o_ref[...]
        r = _rstd(ov)
        oh = ov * r
        d_on = dc_ref[:, 0:ATT_W]
        dga_ref[...] += jnp.sum(d_on * oh, axis=0, keepdims=True)
        doh = d_on * ga_ref[...]
        do_ref[...] = r * (doh - oh * jnp.mean(doh * oh, axis=-1, keepdims=True))

        zv = z_ref[...]
        yv = y_ref[...]
        sz = _sigmoid(zv)
        silu = zv * sz
        yz = yv * silu
        for gi in range(SSM_GROUPS):
            sl = slice(gi * gw, (gi + 1) * gw)
            seg = yz[:, sl]
            rg = _rstd(seg)
            yh = seg * rg
            dyn = dc_ref[:, ATT_W + gi * gw:ATT_W + (gi + 1) * gw]
            dgs_ref[:, sl] += jnp.sum(dyn * yh, axis=0, keepdims=True)
            dyh = dyn * gs_ref[:, sl]
            dyz = rg * (dyh - yh * jnp.mean(dyh * yh, axis=-1, keepdims=True))
            dy_ref[:, sl] = dyz * silu[:, sl]
            dz_ref[:, sl] = (dyz * yv[:, sl] * (sz[:, sl] * (1.0 + zv[:, sl] * (1.0 - sz[:, sl])))).astype(BF16)

    half = pl.BlockSpec((MIX_ROWS, ATT_W), lambda i: (i, 0))
    vec = pl.BlockSpec((1, ATT_W), lambda i: (0, 0))
    return pl.pallas_call(
        body, name=name, grid=(t // MIX_ROWS,),
        in_specs=[pl.BlockSpec((MIX_ROWS, 2 * ATT_W), lambda i: (i, 0)), half, half,
                  pl.BlockSpec((MIX_ROWS, ATT_W), lambda i: (i, OFF_Z // ATT_W)), vec, vec],
        out_specs=[half, half, half, vec, vec],
        out_shape=[jax.ShapeDtypeStruct((t, ATT_W), F32), jax.ShapeDtypeStruct((t, SSM_W), F32),
                   jax.ShapeDtypeStruct((t, SSM_W), BF16), jax.ShapeDtypeStruct((1, ATT_W), F32),
                   jax.ShapeDtypeStruct((1, SSM_W), F32)],
        compiler_params=_params(("arbitrary",)),
    )(dcat, o, y, proj, ga, gs)


ATT_QB = 256
ATT_KB = 256


def _stacked(m):
    return jnp.concatenate([m, m], axis=0)


def _split_sum(x, m2):
    hi = x.astype(BF16)
    lo = (x - hi.astype(F32)).astype(BF16)
    return _dot(jnp.concatenate([hi, lo], axis=1), m2)


def _att_tile(z, qi, kj, row, col, m_strict2, carry):
    lse = jnp.log(1.0 + jnp.exp(-jnp.abs(z)))
    lb = jnp.minimum(z, 0.0) - lse
    lr = -jnp.maximum(z, 0.0) - lse
    mask = (kj * ATT_KB + col) < (qi * ATT_QB + row)
    lrm = jnp.where(mask, lr, 0.0)
    later = _split_sum(lrm, m_strict2) + carry
    w = jnp.where(mask, jnp.exp(lb + later), 0.0)
    return mask, lb, lrm, w


ATT_HP = 2


def _head_spec(seq, off):
    width = ATT_HP * ATT_DH
    per = ATT_HEADS // ATT_HP
    return pl.BlockSpec((seq, width), lambda s: (s // per, off // width + s % per))


def _head_lanes(h):
    return slice(h * ATT_DH, (h + 1) * ATT_DH)


def _attn_fwd(proj, gq, gk, nb, seq, name):
    nq = seq // ATT_QB
    scale = ATT_DH ** -0.5
    heads = range(ATT_HP)

    def body(q_ref, k_ref, v_ref, gq_ref, gk_ref, o_ref, qs, kn, vb):
        for h in heads:
            sl = _head_lanes(h)
            qv = q_ref[:, sl]
            kv = k_ref[:, sl]
            qs[:, sl] = (qv * _rstd(qv) * gq_ref[...] * scale).astype(BF16)
            kn[:, sl] = (kv * _rstd(kv) * gk_ref[...]).astype(BF16)
            vb[:, sl] = v_ref[:, sl].astype(BF16)
        row = lax.broadcasted_iota(jnp.int32, (ATT_QB, ATT_KB), 0)
        col = lax.broadcasted_iota(jnp.int32, (ATT_QB, ATT_KB), 1)
        m_strict2 = _stacked((row > col).astype(BF16))

        def key_rows(kj):
            return pl.ds(pl.multiple_of(kj * ATT_KB, ATT_KB), ATT_KB)

        def q_loop(qi, _):
            q0 = pl.multiple_of(qi * ATT_QB, ATT_QB)
            q_ts = [qs[pl.ds(q0, ATT_QB), _head_lanes(h)] for h in heads]

            def scores(h, kj):
                return _dot_nt(q_ts[h], kn[key_rows(kj), _head_lanes(h)])

            def k_loop(i, c):
                kj = qi - i
                rows = key_rows(kj)
                out = []
                for h in heads:
                    acc, carry, z = c[h]
                    z_next = scores(h, jnp.maximum(kj - 1, 0))
                    _, _, lrm, w = _att_tile(z, qi, kj, row, col, m_strict2, carry)
                    acc = acc + _dot(w.astype(BF16), vb[rows, _head_lanes(h)])
                    out.append((acc, carry + jnp.sum(lrm, axis=-1, keepdims=True), z_next))
                return tuple(out)

            init = tuple((jnp.zeros((ATT_QB, ATT_DH), F32), jnp.zeros((ATT_QB, 1), F32), scores(h, qi)) for h in heads)
            res = lax.fori_loop(0, qi + 1, k_loop, init)
            for h in heads:
                o_ref[pl.ds(q0, ATT_QB), _head_lanes(h)] = res[h][0]
            return 0

        lax.fori_loop(0, nq, q_loop, 0)

    vec = pl.BlockSpec((1, ATT_DH), lambda s: (0, 0))
    return pl.pallas_call(
        body, name=name, grid=(nb * ATT_HEADS // ATT_HP,),
        in_specs=[_head_spec(seq, OFF_Q), _head_spec(seq, OFF_K), _head_spec(seq, OFF_V), vec, vec],
        out_specs=_head_spec(seq, 0),
        out_shape=jax.ShapeDtypeStruct((nb * seq, ATT_W), F32),
        scratch_shapes=[pltpu.VMEM((seq, ATT_HP * ATT_DH), BF16)] * 3,
        compiler_params=_params(("parallel",)),
    )(proj, proj, proj, gq, gk)


def _attn_bwd(proj, do, gq, gk, nb, seq, name):
    nq = seq // ATT_QB
    nk = seq // ATT_KB
    scale = ATT_DH ** -0.5
    heads = range(ATT_HP)

    def body(q_ref, k_ref, v_ref, do_ref, gq_ref, gk_ref, dq_ref, dk_ref, dv_ref, dgq_ref, dgk_ref,
             qs, kn, vb, dob, dq_acc, dk_acc, dv_acc, gbuf, bbuf):
        @pl.when(pl.program_id(0) == 0)
        def _():
            dgq_ref[...] = jnp.zeros_like(dgq_ref)
            dgk_ref[...] = jnp.zeros_like(dgk_ref)

        for h in heads:
            sl = _head_lanes(h)
            qv = q_ref[:, sl]
            kv = k_ref[:, sl]
            qs[:, sl] = (qv * _rstd(qv) * gq_ref[...] * scale).astype(BF16)
            kn[:, sl] = (kv * _rstd(kv) * gk_ref[...]).astype(BF16)
            vb[:, sl] = v_ref[:, sl].astype(BF16)
            dob[:, sl] = do_ref[:, sl].astype(BF16)
        dk_acc[...] = jnp.zeros_like(dk_acc)
        dv_acc[...] = jnp.zeros_like(dv_acc)
        row = lax.broadcasted_iota(jnp.int32, (ATT_QB, ATT_KB), 0)
        col = lax.broadcasted_iota(jnp.int32, (ATT_QB, ATT_KB), 1)
        m_strict2 = _stacked((row > col).astype(BF16))
        m_prefix2 = _stacked((row < col).astype(BF16))

        def key_rows(kj):
            return pl.ds(pl.multiple_of(kj * ATT_KB, ATT_KB), ATT_KB)

        def q_loop(qi, _):
            q0 = pl.multiple_of(qi * ATT_QB, ATT_QB)
            q_ts = [qs[pl.ds(q0, ATT_QB), _head_lanes(h)] for h in heads]
            do_ts = [dob[pl.ds(q0, ATT_QB), _head_lanes(h)] for h in heads]

            def scores(h, kj):
                return _dot_nt(q_ts[h], kn[key_rows(kj), _head_lanes(h)])

            def down(i, c):
                kj = qi - i
                rows = key_rows(kj)
                out = []
                for h in heads:
                    carry, z = c[h]
                    sl = _head_lanes(h)
                    z_next = scores(h, jnp.maximum(kj - 1, 0))
                    _, lb, lrm, w = _att_tile(z, qi, kj, row, col, m_strict2, carry)
                    dw = _dot_nt(do_ts[h], vb[rows, sl])
                    gbuf[h * nk + kj] = w * dw
                    bbuf[h * nk + kj] = jnp.exp(lb)
                    dv_acc[rows, sl] += _dot_tn(w.astype(BF16), do_ts[h])
                    out.append((carry + jnp.sum(lrm, axis=-1, keepdims=True), z_next))
                return tuple(out)

            lax.fori_loop(0, qi + 1, down, tuple((jnp.zeros((ATT_QB, 1), F32), scores(h, qi)) for h in heads))

            def up(kj, c):
                rows = key_rows(kj)
                mask = (kj * ATT_KB + col) < (qi * ATT_QB + row)
                out = []
                for h in heads:
                    acc, carry, within = c[h]
                    sl = _head_lanes(h)
                    g = gbuf[h * nk + kj]
                    beta = bbuf[h * nk + kj]
                    within_next = _split_sum(gbuf[h * nk + jnp.minimum(kj + 1, qi)], m_prefix2)
                    big_g = within + carry
                    dz = jnp.where(mask, g * (1.0 - beta) - big_g * beta, 0.0).astype(BF16)
                    acc = acc + _dot(dz, kn[rows, sl])
                    dk_acc[rows, sl] += _dot_tn(dz, q_ts[h])
                    out.append((acc, carry + jnp.sum(g, axis=-1, keepdims=True), within_next))
                return tuple(out)

            init = tuple((jnp.zeros((ATT_QB, ATT_DH), F32), jnp.zeros((ATT_QB, 1), F32),
                          _split_sum(gbuf[h * nk], m_prefix2)) for h in heads)
            res = lax.fori_loop(0, qi + 1, up, init)
            for h in heads:
                dq_acc[pl.ds(q0, ATT_QB), _head_lanes(h)] = res[h][0]
            return 0

        lax.fori_loop(0, nq, q_loop, 0)

        def norm_bwd(xv, gain, dyn):
            r = _rstd(xv)
            xh = xv * r
            dgain = jnp.sum(dyn * xh, axis=0, keepdims=True)
            dxh = dyn * gain
            return r * (dxh - xh * jnp.mean(dxh * xh, axis=-1, keepdims=True)), dgain

        for h in heads:
            sl = _head_lanes(h)
            dq, dgq = norm_bwd(q_ref[:, sl], gq_ref[...], dq_acc[:, sl] * scale)
            dk, dgk = norm_bwd(k_ref[:, sl], gk_ref[...], dk_acc[:, sl])
            dq_ref[:, sl] = dq.astype(BF16)
            dk_ref[:, sl] = dk.astype(BF16)
            dv_ref[:, sl] = dv_acc[:, sl].astype(BF16)
            dgq_ref[...] += dgq
            dgk_ref[...] += dgk

    vec = pl.BlockSpec((1, ATT_DH), lambda s: (0, 0))
    big = jax.ShapeDtypeStruct((nb * seq, ATT_W), BF16)
    small = jax.ShapeDtypeStruct((1, ATT_DH), F32)
    width = ATT_HP * ATT_DH
    return pl.pallas_call(
        body, name=name, grid=(nb * ATT_HEADS // ATT_HP,),
        in_specs=[_head_spec(seq, OFF_Q), _head_spec(seq, OFF_K), _head_spec(seq, OFF_V), _head_spec(seq, 0), vec, vec],
        out_specs=[_head_spec(seq, 0)] * 3 + [vec, vec],
        out_shape=[big, big, big, small, small],
        scratch_shapes=[pltpu.VMEM((seq, width), BF16)] * 4 + [pltpu.VMEM((seq, width), F32)] * 3
        + [pltpu.VMEM((ATT_HP * nk, ATT_QB, ATT_KB), F32)] * 2,
        compiler_params=_params(("arbitrary",)),
    )(proj, proj, proj, do, gq, gk)


CONV_COLS = 256


def _pack_conv(conv_w, conv_b):
    return jnp.concatenate([conv_w, conv_b[None, :], jnp.zeros((3, CONV_DIM), F32)], axis=0)


def _conv_pre(raw, w8, rowi):
    pre = w8[CONV_K:CONV_K + 1, :] + raw * w8[CONV_K - 1:CONV_K, :]
    for k in range(1, CONV_K):
        sh = jnp.where(rowi >= k, pltpu.roll(raw, k, 0), 0.0)
        pre = pre + sh * w8[CONV_K - 1 - k:CONV_K - k, :]
    return pre


def _conv_fwd(proj, cw8, nb, seq, name):
    ncol = CONV_DIM // CONV_COLS

    def body(x_ref, w_ref, o_ref):
        rowi = lax.broadcasted_iota(jnp.int32, (seq, 1), 0)
        pre = _conv_pre(x_ref[...], w_ref[...], rowi)
        o_ref[...] = pre * _sigmoid(pre)

    return pl.pallas_call(
        body, name=name, grid=(nb, ncol),
        in_specs=[pl.BlockSpec((seq, CONV_COLS), lambda b, j: (b, OFF_XS // CONV_COLS + j)),
                  pl.BlockSpec((8, CONV_COLS), lambda b, j: (0, j))],
        out_specs=pl.BlockSpec((seq, CONV_COLS), lambda b, j: (b, j)),
        out_shape=jax.ShapeDtypeStruct((nb * seq, CONV_DIM), F32),
        compiler_params=_params(("parallel", "parallel")),
    )(proj, cw8)


def _conv_bwd(proj, dact, cw8, nb, seq, name):
    ncol = CONV_DIM // CONV_COLS

    def body(x_ref, d_ref, w_ref, dx_ref, dw_ref):
        @pl.when(pl.program_id(1) == 0)
        def _():
            dw_ref[...] = jnp.zeros_like(dw_ref)

        rowi = lax.broadcasted_iota(jnp.int32, (seq, 1), 0)
        raw = x_ref[...]
        w8 = w_ref[...]
        pre = _conv_pre(raw, w8, rowi)
        sg = _sigmoid(pre)
        dpre = d_ref[...] * (sg * (1.0 + pre * (1.0 - sg)))
        dw_ref[CONV_K:CONV_K + 1, :] += jnp.sum(dpre, axis=0, keepdims=True)
        dw_ref[CONV_K - 1:CONV_K, :] += jnp.sum(dpre * raw, axis=0, keepdims=True)
        draw = dpre * w8[CONV_K - 1:CONV_K, :]
        for k in range(1, CONV_K):
            sh = jnp.where(rowi >= k, pltpu.roll(raw, k, 0), 0.0)
            dw_ref[CONV_K - 1 - k:CONV_K - k, :] += jnp.sum(dpre * sh, axis=0, keepdims=True)
            up = jnp.where(rowi < seq - k, pltpu.roll(dpre, seq - k, 0), 0.0)
            draw = draw + up * w8[CONV_K - 1 - k:CONV_K - k, :]
        dx_ref[...] = draw.astype(BF16)

    return pl.pallas_call(
        body, name=name, grid=(ncol, nb),
        in_specs=[pl.BlockSpec((seq, CONV_COLS), lambda j, b: (b, OFF_XS // CONV_COLS + j)),
                  pl.BlockSpec((seq, CONV_COLS), lambda j, b: (b, j)),
                  pl.BlockSpec((8, CONV_COLS), lambda j, b: (0, j))],
        out_specs=[pl.BlockSpec((seq, CONV_COLS), lambda j, b: (b, j)),
                   pl.BlockSpec((8, CONV_COLS), lambda j, b: (0, j))],
        out_shape=[jax.ShapeDtypeStruct((nb * seq, CONV_DIM), BF16), jax.ShapeDtypeStruct((8, CONV_DIM), F32)],
        compiler_params=_params(("parallel", "arbitrary")),
    )(proj, dact, cw8)


def _pack_heads(dt_bias, a_log, d_skip):
    rows = jnp.stack([dt_bias, a_log, d_skip]).reshape(3, SSM_GROUPS, SSM_HG).transpose(1, 0, 2)
    return jnp.pad(rows, ((0, 0), (0, 8 - 3), (0, LANE - SSM_HG)))


def _split_dot_r(m, x, parts):
    acc = None
    rem = x
    for _ in range(parts):
        hi = rem.astype(BF16)
        d = _dot(m, hi)
        acc = d if acc is None else acc + d
        rem = rem - hi.astype(F32)
    return acc


def _ssd_specs(seq):
    gx = SSM_HG * SSM_P
    return dict(
        xs=pl.BlockSpec((seq, gx), lambda g, b: (b, g)),
        bm=pl.BlockSpec((seq, SSM_N), lambda g, b: (b, SSM_W // SSM_N + g)),
        cm=pl.BlockSpec((seq, SSM_N), lambda g, b: (b, SSM_W // SSM_N + SSM_GROUPS + g)),
        dt=pl.BlockSpec((seq, LANE), lambda g, b: (b, OFF_DT // LANE + g)),
        hp=pl.BlockSpec((1, 8, LANE), lambda g, b: (g, 0, 0)),
        head=pl.BlockSpec((seq, gx), lambda g, b: (b, g)),
        grp=pl.BlockSpec((seq, SSM_N), lambda g, b: (b, g)),
    )


def _ssd_fwd(act, proj, hp, nb, seq, name):
    nc = seq // CHUNK

    def body(xs_ref, b_ref, c_ref, dt_ref, hp_ref, y_ref, dt_s, da_s, hst):
        hpv = hp_ref[0]
        dt = _softplus(dt_ref[...] + hpv[0:1, :])
        a = -jnp.exp(hpv[1:2, :])
        dsk = hpv[2:3, :]
        dt_s[...] = dt
        da_s[...] = dt * a
        hst[...] = jnp.zeros_like(hst)
        li = lax.broadcasted_iota(jnp.int32, (CHUNK, CHUNK), 0)
        si = lax.broadcasted_iota(jnp.int32, (CHUNK, CHUNK), 1)
        tril = (si <= li).astype(BF16)
        causal = li >= si

        def chunk(c, _):
            rows = pl.ds(pl.multiple_of(c * CHUNK, CHUNK), CHUNK)
            acol = _split_dot_r(tril, da_s[rows, :], 3)
            arow = acol.T
            alast = acol[CHUNK - 1:CHUNK, :]
            bb = b_ref[rows, :].astype(BF16)
            cb = c_ref[rows, :].astype(BF16)
            cbm = _dot_nt(cb, bb)
            xc = xs_ref[rows, :]
            dtc = dt_s[rows, :]
            for j in range(SSM_HG):
                a_col = acol[:, j:j + 1]
                decay = jnp.where(causal, jnp.exp(jnp.minimum(a_col - arow[j:j + 1, :], 0.0)), 0.0)
                x_j = xc[:, j * SSM_P:(j + 1) * SSM_P]
                u = x_j * dtc[:, j:j + 1]
                h_in = hst[j]
                y = (_dot((cbm * decay).astype(BF16), u.astype(BF16))
                     + jnp.exp(a_col) * _dot_nt(cb, h_in.astype(BF16)) + dsk[:, j:j + 1] * x_j)
                y_ref[rows, j * SSM_P:(j + 1) * SSM_P] = y
                a_l = alast[:, j:j + 1]
                hst[j] = jnp.exp(a_l) * h_in + _dot_tn((u * jnp.exp(a_l - a_col)).astype(BF16), bb)
            return 0

        lax.fori_loop(0, nc, chunk, 0)

    sp = _ssd_specs(seq)
    return pl.pallas_call(
        body, name=name, grid=(SSM_GROUPS, nb),
        in_specs=[sp["xs"], sp["bm"], sp["cm"], sp["dt"], sp["hp"]],
        out_specs=sp["head"],
        out_shape=jax.ShapeDtypeStruct((nb * seq, SSM_W), F32),
        scratch_shapes=[pltpu.VMEM((seq, LANE), F32)] * 2 + [pltpu.VMEM((SSM_HG, SSM_P, SSM_N), F32)],
        compiler_params=_params(("parallel", "parallel")),
    )(act, act, act, proj, hp)


def _ssd_bwd(act, proj, dy, hp, nb, seq, name):
    nc = seq // CHUNK

    def body(xs_ref, b_ref, c_ref, dt_ref, hp_ref, dy_ref, dxs_ref, db_ref, dc_ref, ddt_ref, dhp_ref,
             dt_s, da_s, ddt_s, hs, lam):
        @pl.when(pl.program_id(1) == 0)
        def _():
            dhp_ref[...] = jnp.zeros_like(dhp_ref)

        hpv = hp_ref[0]
        a = -jnp.exp(hpv[1:2, :])
        dsk = hpv[2:3, :]
        dt_s[...] = _softplus(dt_ref[...] + hpv[0:1, :])
        da_s[...] = dt_s[...] * a
        li = lax.broadcasted_iota(jnp.int32, (CHUNK, CHUNK), 0)
        si = lax.broadcasted_iota(jnp.int32, (CHUNK, CHUNK), 1)
        tril = (si <= li).astype(BF16)
        triu = (si >= li).astype(BF16)
        causal = li >= si
        causal_t = si >= li
        lane = lax.broadcasted_iota(jnp.int32, (1, LANE), 1)
        tril_strict = (si < li).astype(BF16)

        def chunk_rows(c):
            return pl.ds(pl.multiple_of(c * CHUNK, CHUNK), CHUNK)

        hs[0:SSM_HG] = jnp.zeros((SSM_HG, SSM_P, SSM_N), F32)

        def fwd_chunk(c, _):
            rows = chunk_rows(c)
            acol = _split_dot_r(tril, da_s[rows, :], 3)
            alast = acol[CHUNK - 1:CHUNK, :]
            bb = b_ref[rows, :].astype(BF16)
            xc = xs_ref[rows, :]
            dtc = dt_s[rows, :]
            for j in range(SSM_HG):
                a_col = acol[:, j:j + 1]
                a_l = alast[:, j:j + 1]
                u = xc[:, j * SSM_P:(j + 1) * SSM_P] * dtc[:, j:j + 1]
                hs[(c + 1) * SSM_HG + j] = (jnp.exp(a_l) * hs[c * SSM_HG + j]
                                            + _dot_tn((u * jnp.exp(a_l - a_col)).astype(BF16), bb))
            return 0

        lax.fori_loop(0, nc - 1, fwd_chunk, 0)
        lam[...] = jnp.zeros_like(lam)

        def bwd_chunk(i, carry):
            dd_vec, da_vec = carry
            c = nc - 1 - i
            rows = chunk_rows(c)
            acol = _split_dot_r(tril, da_s[rows, :], 3)
            arow = acol.T
            alast = acol[CHUNK - 1:CHUNK, :]
            bb = b_ref[rows, :].astype(BF16)
            cb = c_ref[rows, :].astype(BF16)
            cbm = _dot_nt(cb, bb)
            cbt = _dot_nt(bb, cb)
            xc = xs_ref[rows, :]
            dtc = dt_s[rows, :]
            dyc = dy_ref[rows, :]
            zero = jnp.zeros((CHUNK, CHUNK), F32)
            dcb, dcbt, dc_acc, db_acc, d_a, f_a, dtu = zero, zero, zero, zero, zero, zero, zero
            c_a = jnp.zeros((1, LANE), F32)
            for j in range(SSM_HG):
                a_col = acol[:, j:j + 1]
                seg = a_col - arow[j:j + 1, :]
                decay = jnp.where(causal, jnp.exp(jnp.minimum(seg, 0.0)), 0.0)
                decay_t = jnp.where(causal_t, jnp.exp(jnp.minimum(-seg, 0.0)), 0.0)
                m = cbm * decay
                mt = cbt * decay_t
                sl = slice(j * SSM_P, (j + 1) * SSM_P)
                x_j = xc[:, sl]
                dt_j = dtc[:, j:j + 1]
                dy_j = dyc[:, sl]
                u = x_j * dt_j
                ub = u.astype(BF16)
                dyb = dy_j.astype(BF16)
                h_in = hs[c * SSM_HG + j]
                lm = lam[j]
                hb = h_in.astype(BF16)
                lb = lm.astype(BF16)
                a_l = alast[:, j:j + 1]
                ea = jnp.exp(a_col)
                eb = jnp.exp(a_l - a_col)
                el = jnp.exp(a_l)
                du_off = eb * _dot_nt(bb, lb)
                du = _dot(mt.astype(BF16), dyb) + du_off
                dm = _dot_nt(dyb, ub)
                dmt = _dot_nt(ub, dyb)
                dcb = dcb + dm * decay
                dcbt = dcbt + dmt * decay_t
                y_off = ea * _dot_nt(cb, hb)
                d_a_j = (jnp.sum(dm * m, axis=-1, keepdims=True) - jnp.sum(dmt * mt, axis=-1, keepdims=True)
                         + jnp.sum(dy_j * y_off, axis=-1, keepdims=True))
                f_a_j = jnp.sum(du_off * u, axis=-1, keepdims=True)
                c_a = c_a + jnp.where(lane == j, el * jnp.sum(lm * h_in), 0.0)
                dc_acc = dc_acc + ea * _dot(dyb, hb)
                db_acc = db_acc + eb * _dot(ub, lb)
                lam[j] = el * lm + _dot_tn((ea * dy_j).astype(BF16), cb)
                d_a = jnp.where(lane == j, d_a_j, d_a)
                f_a = jnp.where(lane == j, f_a_j, f_a)
                dtu = jnp.where(lane == j, jnp.sum(du * x_j, axis=-1, keepdims=True), dtu)
                dxs_ref[rows, sl] = du * dt_j + dsk[:, j:j + 1] * dy_j
                dd_vec = dd_vec + jnp.where(lane == j, jnp.sum(dy_j * x_j), 0.0)
            dc_ref[rows, :] = dc_acc + _dot(dcb.astype(BF16), bb)
            db_ref[rows, :] = db_acc + _dot(dcbt.astype(BF16), cb)
            dda = _split_dot_r(triu, d_a, 2) + _split_dot_r(tril_strict, f_a, 2) + c_a
            ddt_s[rows, :] = dda * a + dtu
            da_vec = da_vec + jnp.sum(dda * dtc, axis=0, keepdims=True)
            return dd_vec, da_vec

        zv = jnp.zeros((1, LANE), F32)
        dd_vec, da_vec = lax.fori_loop(0, nc, bwd_chunk, (zv, zv))
        ddt_raw = ddt_s[...] * _sigmoid(dt_ref[...] + hpv[0:1, :])
        ddt_ref[...] = ddt_raw.astype(BF16)
        dhp_ref[0, 0:1, :] += jnp.sum(ddt_raw, axis=0, keepdims=True)
        dhp_ref[0, 1:2, :] += da_vec * a
        dhp_ref[0, 2:3, :] += dd_vec

    sp = _ssd_specs(seq)
    t = nb * seq
    return pl.pallas_call(
        body, name=name, grid=(SSM_GROUPS, nb),
        in_specs=[sp["xs"], sp["bm"], sp["cm"], sp["dt"], sp["hp"], sp["head"]],
        out_specs=[sp["head"], sp["grp"], sp["grp"], sp["grp"], sp["hp"]],
        out_shape=[jax.ShapeDtypeStruct((t, SSM_W), F32), jax.ShapeDtypeStruct((t, SSM_GROUPS * SSM_N), F32),
                   jax.ShapeDtypeStruct((t, SSM_GROUPS * SSM_N), F32),
                   jax.ShapeDtypeStruct((t, SSM_GROUPS * LANE), BF16),
                   jax.ShapeDtypeStruct((SSM_GROUPS, 8, LANE), F32)],
        scratch_shapes=[pltpu.VMEM((seq, LANE), F32)] * 3
        + [pltpu.VMEM((nc * SSM_HG, SSM_P, SSM_N), F32), pltpu.VMEM((SSM_HG, SSM_P, SSM_N), F32)],
        compiler_params=_params(("parallel", "arbitrary")),
    )(act, act, act, proj, hp, dy)


ANY = pl.BlockSpec(memory_space=pl.ANY)


def _block_index(p):
    return 4 * p[0] + 2 * p[1] + p[2]


def _all_gather(shards, name):
    n = len(shards)

    def body(*refs):
        ins, outs = refs[:n], refs[n:2 * n]
        send_sems, recv_sems, local_sems = refs[2 * n:]
        x, y, c = lax.axis_index("x"), lax.axis_index("y"), lax.axis_index("c")
        me, sibling = (x, y, c), (x, y, 1 - c)
        chips = [(1 - x, y), (x, 1 - y), (1 - x, 1 - y)]

        def copy(i, k, block, to, src=None):
            dst = outs[i].at[_block_index(block)]
            return pltpu.make_async_remote_copy(
                src_ref=dst if src is None else src, dst_ref=dst,
                send_sem=send_sems.at[i, k], recv_sem=recv_sems.at[i, k],
                device_id=to, device_id_type=MESH)

        mine = [pltpu.make_async_copy(ins[i], outs[i].at[_block_index(me)], local_sems.at[i]) for i in range(n)]
        for cp in mine:
            cp.start()
        first = []
        for i in range(n):
            first.append(copy(i, 0, me, sibling, src=ins[i]))
            first += [copy(i, 1 + j, me, (*chip, c), src=ins[i]) for j, chip in enumerate(chips)]
        for cp in first:
            cp.start()
        passed = []
        for j, chip in enumerate(chips):
            for i in range(n):
                copy(i, 1 + j, (*chip, c), me).wait_recv()
                fwd = copy(i, 4 + j, (*chip, c), sibling)
                fwd.start()
                passed.append(fwd)
        for i in range(n):
            copy(i, 0, sibling, me).wait_recv()
            for j, chip in enumerate(chips):
                copy(i, 4 + j, (*chip, 1 - c), me).wait_recv()
        for cp in first + passed:
            cp.wait_send()
        for cp in mine:
            cp.wait()

    return pl.pallas_call(
        body, name=name,
        in_specs=[ANY] * n, out_specs=[ANY] * n,
        out_shape=[jax.ShapeDtypeStruct((N_DEV,) + s.shape, s.dtype) for s in shards],
        scratch_shapes=[pltpu.SemaphoreType.DMA((n, 7)), pltpu.SemaphoreType.DMA((n, 7)),
                        pltpu.SemaphoreType.DMA((n,))],
    )(*shards)


HBM = pl.BlockSpec(memory_space=pltpu.HBM)
SEM = pl.BlockSpec(memory_space=pltpu.SEMAPHORE)
EFFECT = pltpu.SideEffectType.DATAFLOW_SIDE_EFFECTING


def _my_block():
    return _block_index((lax.axis_index("x"), lax.axis_index("y"), lax.axis_index("c")))


def _peer(k):
    x, y, c = lax.axis_index("x"), lax.axis_index("y"), lax.axis_index("c")
    return (1 - x if k & 4 else x, 1 - y if k & 2 else y, 1 - c if k & 1 else c)


def _plan_copies(plan, src_refs, land_refs, send_sems, recv_sems):
    me = _my_block()
    copies = []
    for e, (si, di, src_view, dst_view, _) in enumerate(plan):
        for k in range(1, N_DEV):
            copies.append(pltpu.make_async_remote_copy(
                src_ref=src_view(src_refs[si], _block_index(_peer(k))),
                dst_ref=dst_view(land_refs[di], me),
                send_sem=send_sems[e], recv_sem=recv_sems[e],
                device_id=_peer(k), device_id_type=MESH))
    return copies


def _plan_waits(plan, land_refs, send_sems, recv_sems):
    waits = []
    for e, (_, di, _, _, seven) in enumerate(plan):
        view = seven(land_refs[di])
        waits.append(pltpu.make_async_remote_copy(
            src_ref=view, dst_ref=view, send_sem=send_sems[e], recv_sem=recv_sems[e],
            device_id=_peer(1), device_id_type=MESH))
    return waits


def _plan_own(plan, src_refs, land_refs, own_sems):
    me = _my_block()
    return [pltpu.make_async_copy(src_view(src_refs[si], me), dst_view(land_refs[di], me), own_sems[e])
            for e, (si, di, src_view, dst_view, _) in enumerate(plan)]


def _copies_start(srcs, lands, plan, name, after=None):
    ns, nl, ne = len(srcs), len(lands), len(plan)
    extra = [] if after is None else [after]

    nin = ns + nl + len(extra)

    def body(*refs):
        src_refs, land_refs = refs[:ns], refs[ns:ns + nl]
        send_sems, recv_sems = refs[nin:nin + ne], refs[nin + ne:nin + 2 * ne]
        own_sems = refs[nin + 2 * ne:nin + 3 * ne]
        token = refs[-1]
        for cp in _plan_copies(plan, src_refs, land_refs, send_sems, recv_sems):
            cp.start()
        for cp in _plan_own(plan, src_refs, land_refs, own_sems):
            cp.start()
        token[...] = jnp.zeros_like(token)

    thru = [pltpu.HBM(a.shape, a.dtype) for a in list(srcs) + list(lands)]
    res = pl.pallas_call(
        body, name=name,
        in_specs=[HBM] * (ns + nl) + [ANY] * len(extra),
        out_specs=[SEM] * (3 * ne) + [HBM] * (ns + nl) + [pl.BlockSpec(memory_space=pltpu.VMEM)],
        out_shape=[pltpu.SemaphoreType.DMA(())] * (3 * ne) + thru + [jax.ShapeDtypeStruct((8, LANE), F32)],
        input_output_aliases={i: 3 * ne + i for i in range(ns + nl)},
        compiler_params=pltpu.CompilerParams(has_side_effects=EFFECT),
    )(*[pltpu.with_memory_space_constraint(a, pltpu.HBM) for a in list(srcs) + list(lands)], *extra)
    return dict(sems=res[:3 * ne], srcs=res[3 * ne:3 * ne + ns], lands=res[3 * ne + ns:3 * ne + ns + nl],
                token=res[-1], plan=plan)


def _copies_wait(flight, after, name):
    srcs, lands, plan = flight["srcs"], flight["lands"], flight["plan"]
    ns, nl, ne = len(srcs), len(lands), len(plan)

    def body(*refs):
        src_refs, land_refs = refs[:ns], refs[ns:ns + nl]
        send_sems, recv_sems = refs[ns + nl:ns + nl + ne], refs[ns + nl + ne:ns + nl + 2 * ne]
        own_sems = refs[ns + nl + 2 * ne:ns + nl + 3 * ne]
        for cp in _plan_waits(plan, land_refs, send_sems, recv_sems):
            cp.wait_send()
            cp.wait_recv()
        for cp in _plan_own(plan, src_refs, land_refs, own_sems):
            cp.wait()

    thru = [pltpu.HBM(a.shape, a.dtype) for a in list(srcs) + list(lands)]
    res = pl.pallas_call(
        body, name=name,
        in_specs=[HBM] * (ns + nl) + [SEM] * (3 * ne) + [ANY],
        out_specs=[HBM] * (ns + nl),
        out_shape=thru,
        input_output_aliases={i: i for i in range(ns + nl)},
        compiler_params=pltpu.CompilerParams(has_side_effects=EFFECT),
    )(*srcs, *lands, *flight["sems"], after)
    return list(res[ns:])


def _adamw_math(w, g, m, v):
    m = ADAM_B1 * m + (1.0 - ADAM_B1) * g
    v = ADAM_B2 * v + (1.0 - ADAM_B2) * (g * g)
    m_hat = m / (1.0 - ADAM_B1 ** ADAM_STEP)
    v_hat = v / (1.0 - ADAM_B2 ** ADAM_STEP)
    delta = -ADAM_LR * (m_hat / (jnp.sqrt(v_hat) + ADAM_EPS) + ADAM_WD * w)
    return delta, m, v


def _adamw(parts, w, m, v, name, rows):
    depth, r, c = w.shape
    cp = parts[0].shape[2]
    assert r % rows == 0 and len(parts) == depth

    def body(*refs):
        p_refs = refs[:depth]
        w_ref, m_ref, v_ref, g_ref, d_ref, mo_ref, vo_ref = refs[depth:]
        for li in range(depth):
            @pl.when(pl.program_id(0) == li)
            def _(li=li):
                g = p_refs[li][0][:, :c].astype(F32)
                for j in range(1, N_DEV):
                    g = g + p_refs[li][j][:, :c].astype(F32)
                d, mn, vn = _adamw_math(w_ref[...], g, m_ref[...], v_ref[...])
                g_ref[...] = g
                d_ref[...] = d
                mo_ref[...] = mn
                vo_ref[...] = vn

    def part_spec(li):
        return pl.BlockSpec((N_DEV, rows, cp), lambda l, i: (0, jnp.where(l == li, i, 0), 0))

    blk = pl.BlockSpec((None, rows, c), lambda l, i: (l, i, 0))
    out = jax.ShapeDtypeStruct((depth, r, c), F32)
    return pl.pallas_call(
        body, name=name, grid=(depth, r // rows),
        in_specs=[part_spec(li) for li in range(depth)] + [blk, blk, blk],
        out_specs=[blk] * 4, out_shape=[out] * 4,
        compiler_params=_params(("arbitrary", "arbitrary")),
    )(*parts, w, m, v)


def _sum_parts(parts, name):
    _, r, c = parts.shape

    def body(p_ref, o_ref):
        g = p_ref[0]
        for j in range(1, N_DEV):
            g = g + p_ref[j]
        o_ref[...] = g

    return pl.pallas_call(
        body, name=name, out_shape=jax.ShapeDtypeStruct((r, c), F32),
        compiler_params=_params(),
    )(parts)


def _adamw_small(g, w, m, v, name):
    def body(g_ref, w_ref, m_ref, v_ref, d_ref, mo_ref, vo_ref):
        d, mn, vn = _adamw_math(w_ref[...], g_ref[...], m_ref[...], v_ref[...])
        d_ref[...] = d
        mo_ref[...] = mn
        vo_ref[...] = vn

    out = jax.ShapeDtypeStruct(w.shape, F32)
    return pl.pallas_call(body, name=name, out_shape=[out] * 3, compiler_params=_params())(g, w, m, v)


SMALL = ("norm_mix", "q_gain", "k_gain", "conv_b", "dt_bias", "a_log", "d_skip", "attn_out_gain",
         "ssm_out_gain", "norm_ffn")


def _pad_lanes(a):
    n = a.shape[-1]
    return jnp.pad(a, ((0, 0), (0, -n % LANE)))


def _pack_small(d):
    return jnp.concatenate([_pad_lanes(d[k]) for k in SMALL], axis=1)


def _unpack_small(packed, like):
    out, off = {}, 0
    for k in SMALL:
        n = like[k].shape[-1]
        out[k] = packed[:, off:off + n]
        off += n + (-n % LANE)
    return out


def _full_cols(gathered):
    _, r, c = gathered.shape
    return gathered.transpose(1, 0, 2).reshape(r, N_DEV * c)


def _col_blocks(full):
    r, c8 = full.shape
    return full.reshape(r, N_DEV, c8 // N_DEV).transpose(1, 0, 2)


FF_BLK = 768
FF_PAD = N_DEV * FF_BLK


def _whole(ref, block):
    return ref


def _rows_of(size):
    return lambda ref, block: ref.at[pl.ds(pl.multiple_of(block * size, size), size), :]


def _cols_of(size, base=0):
    return lambda ref, block: ref.at[:, pl.ds(pl.multiple_of(base + block * size, LANE), size)]


def _slot(ref, block):
    return ref.at[block]


def _seven_slots(ref):
    return ref.at[pl.ds(0, N_DEV - 1)]


def _seven_rows(size):
    return lambda ref: ref.at[pl.ds(0, (N_DEV - 1) * size), :]


def _seven_cols(size):
    return lambda ref: ref.at[:, pl.ds(0, (N_DEV - 1) * size)]


GATHER_A = [(0, 0, _whole, _slot, _seven_slots), (1, 1, _whole, _rows_of(256), _seven_rows(256))]
GATHER_B = [(0, 0, _whole, _cols_of(FF_BLK), _seven_cols(FF_BLK)),
            (1, 1, _whole, _cols_of(FF_BLK), _seven_cols(FF_BLK)),
            (2, 2, _whole, _rows_of(FF_BLK), _seven_rows(FF_BLK))]
SCATTER_A = [(0, 0, _slot, _slot, _seven_slots), (1, 1, _rows_of(256), _slot, _seven_slots)]
SCATTER_B = [(0, 0, _cols_of(FF_BLK), _slot, _seven_slots), (1, 1, _cols_of(FF_BLK), _slot, _seven_slots),
             (2, 2, _rows_of(FF_BLK), _slot, _seven_slots)]


def _gather_lands(which, shards, d):
    if which == "a":
        return [lax.empty((N_DEV,) + shards[0].shape, BF16), lax.empty((d, d), BF16)]
    return [lax.empty((d, FF_PAD), BF16), lax.empty((d, FF_PAD), BF16), lax.empty((FF_PAD, d), BF16)]


def _scatter_lands(which, grads):
    if which == "a":
        g_in, g_out = grads
        return [lax.empty(g_in.shape, BF16), lax.empty((N_DEV, g_out.shape[0] // N_DEV, g_out.shape[1]), BF16)]
    g_gate, _, g_down = grads
    return [lax.empty((N_DEV, g_gate.shape[0], FF_BLK), BF16), lax.empty((N_DEV, g_gate.shape[0], FF_BLK), BF16),
            lax.empty((N_DEV, FF_BLK, g_down.shape[1]), BF16)]


def _pad_w_in(full):
    d = full.shape[0]
    z = jnp.zeros((d, LANE - SSM_HG), full.dtype)
    tail = jnp.zeros((d, NPROJ - OFF_DT - SSM_GROUPS * LANE), full.dtype)
    return jnp.concatenate([full[:, :OFF_DT], full[:, OFF_DT:OFF_DT + SSM_HG], z,
                            full[:, OFF_DT + SSM_HG:IN_DIM], z, tail], axis=1)


def _unpad_w_in(padded):
    return jnp.concatenate([padded[:, :OFF_DT], padded[:, OFF_DT:OFF_DT + SSM_HG],
                            padded[:, OFF_DT + LANE:OFF_DT + LANE + SSM_HG]], axis=1)


def kernel(x, norm_mix, w_in, q_gain, k_gain, conv_w, conv_b, dt_bias, a_log, d_skip, attn_out_gain, ssm_out_gain, w_out, norm_ffn, w_gate, w_up, w_down, loss_target, m_norm_mix, m_w_in, m_q_gain, m_k_gain, m_conv_w, m_conv_b, m_dt_bias, m_a_log, m_d_skip, m_attn_out_gain, m_ssm_out_gain, m_w_out, m_norm_ffn, m_w_gate, m_w_up, m_w_down, v_norm_mix, v_w_in, v_q_gain, v_k_gain, v_conv_w, v_conv_b, v_dt_bias, v_a_log, v_d_skip, v_attn_out_gain, v_ssm_out_gain, v_w_out, v_norm_ffn, v_w_gate, v_w_up, v_w_down):
    nb, seq, d = x.shape
    depth = w_in.shape[0]
    t = nb * seq
    w = dict(norm_mix=norm_mix, w_in=w_in, q_gain=q_gain, k_gain=k_gain, conv_w=conv_w, conv_b=conv_b,
             dt_bias=dt_bias, a_log=a_log, d_skip=d_skip, attn_out_gain=attn_out_gain, ssm_out_gain=ssm_out_gain,
             w_out=w_out, norm_ffn=norm_ffn, w_gate=w_gate, w_up=w_up, w_down=w_down)
    mom = dict(norm_mix=m_norm_mix, w_in=m_w_in, q_gain=m_q_gain, k_gain=m_k_gain, conv_w=m_conv_w, conv_b=m_conv_b,
               dt_bias=m_dt_bias, a_log=m_a_log, d_skip=m_d_skip, attn_out_gain=m_attn_out_gain,
               ssm_out_gain=m_ssm_out_gain, w_out=m_w_out, norm_ffn=m_norm_ffn, w_gate=m_w_gate, w_up=m_w_up,
               w_down=m_w_down)
    var = dict(norm_mix=v_norm_mix, w_in=v_w_in, q_gain=v_q_gain, k_gain=v_k_gain, conv_w=v_conv_w, conv_b=v_conv_b,
               dt_bias=v_dt_bias, a_log=v_a_log, d_skip=v_d_skip, attn_out_gain=v_attn_out_gain,
               ssm_out_gain=v_ssm_out_gain, w_out=v_w_out, norm_ffn=v_norm_ffn, w_gate=v_w_gate, w_up=v_w_up,
               w_down=v_w_down)
    ff = w_gate.shape[2]

    (conv_all,) = _all_gather([conv_w], "gather_conv")

    def shards_a(li):
        return [w_in[li].astype(BF16), w_out[li].astype(BF16)]

    def shards_b(li):
        return [jnp.pad(w_gate[li].astype(BF16), ((0, 0), (0, FF_BLK - ff))),
                jnp.pad(w_up[li].astype(BF16), ((0, 0), (0, FF_BLK - ff))),
                jnp.pad(w_down[li].astype(BF16), ((0, FF_BLK - ff), (0, 0)))]

    def small_params(li):
        p = {k: w[k][li][None, :] for k in ("norm_mix", "q_gain", "k_gain", "attn_out_gain", "ssm_out_gain", "norm_ffn")}
        conv_full = conv_all[:, li].transpose(1, 0, 2).reshape(CONV_K, CONV_DIM)
        p["cw8"] = _pack_conv(conv_full, conv_b[li])
        p["hp"] = _pack_heads(dt_bias[li], a_log[li], d_skip[li])
        return p

    xc = x.reshape(t, d)
    cur = shards_a(0)
    flight = _copies_start(cur, _gather_lands("a", cur, d), GATHER_A, "gather_a0")
    lands_a = _copies_wait(flight, xc, "gather_a0_wait")
    layers, saved = [], []
    for li in range(depth):
        tag = f"l{li}_"
        p = small_params(li)
        p["w_in"] = _pad_w_in(_full_cols(lands_a[0]))
        p["w_out"] = lands_a[1]
        cur = shards_b(li)
        flight = _copies_start(cur, _gather_lands("b", cur, d), GATHER_B, tag + "gather_b", after=lands_a[1])
        h1 = _rms_fwd(xc, p["norm_mix"], tag + "rms1", after=flight["token"])
        proj = _matmul(h1, p["w_in"], "nn", F32, tag + "mm_in")
        o = _attn_fwd(proj, p["q_gain"], p["k_gain"], nb, seq, tag + "attn")
        act = _conv_fwd(proj, p["cw8"], nb, seq, tag + "conv")
        y = _ssd_fwd(act, proj, p["hp"], nb, seq, tag + "ssd")
        cat = _mix_fwd(o, y, proj, p["attn_out_gain"], p["ssm_out_gain"], tag + "mix")
        x1 = _matmul(cat, p["w_out"], "nn", F32, tag + "mm_out", residual=xc)
        p["w_gate"], p["w_up"], p["w_down"] = _copies_wait(flight, x1, tag + "gather_b_wait")
        token = None
        if li + 1 < depth:
            nxt = shards_a(li + 1)
            flight = _copies_start(nxt, _gather_lands("a", nxt, d), GATHER_A, f"gather_a{li + 1}",
                                   after=p["w_down"])
            token = flight["token"]
        h2 = _rms_fwd(x1, p["norm_ffn"], tag + "rms2", after=token)
        a, gate, up = _mm_swiglu(h2, p["w_gate"], p["w_up"], tag + "mm_gu")
        x2 = _matmul(a, p["w_down"], "nn", F32, tag + "mm_down", residual=x1)
        if li + 1 < depth:
            lands_a = _copies_wait(flight, x2, f"gather_a{li + 1}_wait")
        saved.append(dict(x=xc, h1=h1, proj=proj, o=o, act=act, y=y, cat=cat, x1=x1, h2=h2, gate=gate, up=up, a=a))
        layers.append(p)
        xc = x2

    loss_blk, dx, dxb = _loss_fwd_bwd(xc, loss_target.reshape(t, d), "loss")
    loss = lax.psum(loss_blk[0, 0], ("x", "y", "c"))

    grads = [dict() for _ in range(depth)]
    recv = [dict() for _ in range(depth)]
    flight_a, token = None, None
    for li in reversed(range(depth)):
        tag = f"l{li}_b_"
        p, s, g = layers[li], saved[li], grads[li]
        dgate, dup = _mm_dact_swiglu(dxb, p["w_down"], s["gate"], s["up"], tag + "mm_dact", after=token)
        g_down = _matmul(s["a"], dxb, "tn", BF16, tag + "mm_dwd")
        dh2 = _matmul(dgate, p["w_gate"], "nt", F32, tag + "mm_dh2g")
        dh2 = _matmul(dup, p["w_up"], "nt", F32, tag + "mm_dh2u", residual=dh2)
        g_gate = _matmul(s["h2"], dgate, "tn", BF16, tag + "mm_dwg")
        g_up = _matmul(s["h2"], dup, "tn", BF16, tag + "mm_dwu")
        if flight_a is not None:
            recv[li + 1]["w_in"], recv[li + 1]["w_out"] = _copies_wait(flight_a, g_up, f"l{li + 1}_b_scatter_a_wait")
        grads_b = [g_gate, g_up, g_down]
        flight_b = _copies_start(grads_b, _scatter_lands("b", grads_b), SCATTER_B,
                                 tag + "scatter_b", after=recv[li + 1]["w_out"] if li + 1 < depth else None)
        dx1, dx1b, g["norm_ffn"] = _rms_bwd(s["x1"], p["norm_ffn"], dh2, dx, tag + "rms2")
        dcat = _matmul(dx1b, p["w_out"], "nt", F32, tag + "mm_dcat", after=flight_b["token"])
        g_out = _matmul(s["cat"], dx1b, "tn", BF16, tag + "mm_dwo")
        do, dy, dz, g["attn_out_gain"], g["ssm_out_gain"] = _mix_bwd(
            dcat, s["o"], s["y"], s["proj"], p["attn_out_gain"], p["ssm_out_gain"], tag + "mix")
        dq, dk, dv, g["q_gain"], g["k_gain"] = _attn_bwd(s["proj"], do, p["q_gain"], p["k_gain"], nb, seq, tag + "attn")
        dxa, dba, dca, ddt, dhp = _ssd_bwd(s["act"], s["proj"], dy, p["hp"], nb, seq, tag + "ssd")
        dxbc, dcw8 = _conv_bwd(s["proj"], jnp.concatenate([dxa, dba, dca], axis=1), p["cw8"], nb, seq, tag + "conv")
        g["conv_w"] = dcw8[0:CONV_K]
        g["conv_b"] = dcw8[CONV_K:CONV_K + 1]
        heads = dhp[:, 0:3, 0:SSM_HG].transpose(1, 0, 2).reshape(3, SSM_HEADS)
        g["dt_bias"], g["a_log"], g["d_skip"] = heads[0:1], heads[1:2], heads[2:3]
        tail = jnp.zeros((t, NPROJ - OFF_DT - SSM_GROUPS * LANE), BF16)
        dproj = jnp.concatenate([dq, dk, dv, dz, dxbc, ddt, tail], axis=1)
        recv[li]["w_gate"], recv[li]["w_up"], recv[li]["w_down"] = _copies_wait(flight_b, dproj, tag + "scatter_b_wait")
        dh1 = _matmul(dproj, p["w_in"], "nt", F32, tag + "mm_dh1")
        g_in = _col_blocks(_unpad_w_in(_matmul(s["h1"], dproj, "tn", BF16, tag + "mm_dwin")))
        flight_a = _copies_start([g_in, g_out], _scatter_lands("a", [g_in, g_out]), SCATTER_A, tag + "scatter_a")
        token = flight_a["token"]
        dx, dxb, g["norm_mix"] = _rms_bwd(s["x"], p["norm_mix"], dh1, dx1, tag + "rms1")
    grad_x = dx.reshape(nb, seq, d)

    out_g, out_d, out_m, out_v = {}, {}, {}, {}

    def update(k, rows):
        parts = [recv[li][k] for li in range(depth)]
        out_g[k], out_d[k], out_m[k], out_v[k] = _adamw(parts, w[k], mom[k], var[k], "adamw_" + k, rows)

    update("w_gate", 128)
    update("w_up", 128)
    update("w_down", 64)
    recv[0]["w_in"], recv[0]["w_out"] = _copies_wait(flight_a, out_g["w_down"], "l0_b_scatter_a_wait")
    update("w_in", 128)
    update("w_out", 128)

    small_g = {k: jnp.concatenate([grads[li][k] for li in range(depth)], axis=0) for k in SMALL}
    conv_g = jnp.stack([grads[li]["conv_w"] for li in range(depth)]).reshape(depth, CONV_K * CONV_DIM)
    packed_g = jnp.concatenate([_pack_small(small_g), conv_g], axis=1)
    (parts_small,) = _all_gather([packed_g], "gather_small_grads")
    g_small = _sum_parts(parts_small, "sum_small_grads")
    n_small = g_small.shape[1] - CONV_K * CONV_DIM
    conv_total = g_small[:, n_small:].reshape(depth, CONV_K, CONV_DIM)
    me = _block_index((lax.axis_index("x"), lax.axis_index("y"), lax.axis_index("c")))
    cshard = conv_w.shape[2]
    conv_mine = lax.dynamic_slice_in_dim(conv_total, me * cshard, cshard, axis=2)

    def with_conv(small_packed, conv_part):
        return jnp.concatenate([small_packed, conv_part.reshape(depth, CONV_K * cshard)], axis=1)

    res = _adamw_small(with_conv(g_small[:, :n_small], conv_mine),
                       with_conv(_pack_small(w), conv_w), with_conv(_pack_small(mom), m_conv_w),
                       with_conv(_pack_small(var), v_conv_w), "adamw_small")
    g_all = with_conv(g_small[:, :n_small], conv_mine)
    for dst, packed in zip((out_g, out_d, out_m, out_v), (g_all,) + tuple(res)):
        dst.update(_unpack_small(packed[:, :n_small], w))
        dst["conv_w"] = packed[:, n_small:].reshape(depth, CONV_K, cshard)

    names = ("norm_mix", "w_in", "q_gain", "k_gain", "conv_w", "conv_b", "dt_bias", "a_log", "d_skip",
             "attn_out_gain", "ssm_out_gain", "w_out", "norm_ffn", "w_gate", "w_up", "w_down")
    return (loss, grad_x, *[out_g[k] for k in names], *[out_d[k] for k in names],
            *[out_m[k] for k in names], *[out_v[k] for k in names])
```

```python
import functools
import math

import jax
import jax.numpy as jnp
from jax import lax
from jax.experimental import pallas as pl
from jax.experimental.pallas import tpu as pltpu

F32 = jnp.float32
BF16 = jnp.bfloat16
MESH = pl.DeviceIdType.MESH

N_DEV = 8
EPS = 1e-6
ATT_HEADS = 8
ATT_DH = 128
ATT_W = ATT_HEADS * ATT_DH
SSM_W = 1024
SSM_P = 64
SSM_N = 128
SSM_GROUPS = 2
SSM_HG = 8
SSM_HEADS = SSM_GROUPS * SSM_HG
CHUNK = 128
CONV_K = 4
CONV_DIM = SSM_W + 2 * SSM_GROUPS * SSM_N
LANE = 128
OFF_Q, OFF_K, OFF_V, OFF_Z, OFF_XS = 0, ATT_W, 2 * ATT_W, 3 * ATT_W, 4 * ATT_W
OFF_B = OFF_XS + SSM_W
OFF_C = OFF_B + SSM_GROUPS * SSM_N
OFF_DT = OFF_C + SSM_GROUPS * SSM_N
NPROJ = 6144
IN_DIM = OFF_DT + SSM_HEADS

ADAM_LR = 0.001
ADAM_B1 = 0.9
ADAM_B2 = 0.999
ADAM_EPS = 1e-08
ADAM_WD = 0.01
ADAM_STEP = 10

VMEM_LIMIT = 56 * 1024 * 1024
MATMUL_OPERAND_BYTES = 26 * 1024 * 1024


def _params(sem=None):
    return pltpu.CompilerParams(dimension_semantics=sem, vmem_limit_bytes=VMEM_LIMIT)


def _pick(dim, target):
    if dim <= target:
        return dim
    best = None
    for t in range(LANE, target + 1, LANE):
        if dim % t == 0:
            best = t
    assert best is not None, (dim, target)
    return best


def _dot(a, b, dims=((1,), (0,))):
    return lax.dot_general(a, b, (dims, ((), ())), preferred_element_type=F32)


def _dot_nt(a, b):
    return _dot(a, b, ((1,), (1,)))


def _dot_tn(a, b):
    return _dot(a, b, ((0,), (0,)))


def _sigmoid(x):
    return 1.0 / (1.0 + jnp.exp(-x))


def _softplus(x):
    return jnp.maximum(x, 0.0) + jnp.log(1.0 + jnp.exp(-jnp.abs(x)))


def _rstd(x):
    return lax.rsqrt(jnp.mean(x * x, axis=-1, keepdims=True) + EPS)


def _matmul(a, b, mode, out_dtype, name, residual=None, after=None, tm=512, tn=1024, tk=2048):
    if mode == "nn":
        (m, k), (k2, n) = a.shape, b.shape
    elif mode == "nt":
        (m, k), (n, k2) = a.shape, b.shape
    else:
        (k, m), (k2, n) = a.shape, b.shape
    assert k == k2, (a.shape, b.shape, mode)
    tm, tn, tk = _pick(m, tm), _pick(n, tn), _pick(k, tk)
    for cand_tn in (tn, _pick(n, tn // 2)):
        if 2 * 2 * (tm * k + k * cand_tn) <= MATMUL_OPERAND_BYTES:
            tn, tk = cand_tn, k
            break
    nk = k // tk
    dims = {"nn": ((1,), (0,)), "nt": ((1,), (1,)), "tn": ((0,), (0,))}[mode]
    has_res = residual is not None

    has_tok = after is not None

    def body(*refs):
        a_ref, b_ref = refs[:2]
        r_ref = refs[2] if has_res else None
        o_ref = refs[2 + has_res + has_tok]
        prod = _dot(a_ref[...], b_ref[...], dims)

        def finish(r):
            if r_ref is not None:
                r = r + r_ref[...]
            o_ref[...] = r.astype(o_ref.dtype)

        if nk == 1:
            finish(prod)
        else:
            acc = refs[-1]
            kk = pl.program_id(2)

            @pl.when(kk == 0)
            def _():
                acc[...] = prod

            @pl.when(kk > 0)
            def _():
                acc[...] += prod

            @pl.when(kk == nk - 1)
            def _():
                finish(acc[...])

    if mode == "tn":
        a_spec = pl.BlockSpec((tk, tm), lambda i, j, kk: (kk, i))
    else:
        a_spec = pl.BlockSpec((tm, tk), lambda i, j, kk: (i, kk))
    if mode == "nt":
        b_spec = pl.BlockSpec((tn, tk), lambda i, j, kk: (j, kk))
    else:
        b_spec = pl.BlockSpec((tk, tn), lambda i, j, kk: (kk, j))
    o_spec = pl.BlockSpec((tm, tn), lambda i, j, kk: (i, j))
    tok_spec = pl.BlockSpec((8, LANE), lambda i, j, kk: (0, 0))
    in_specs = [a_spec, b_spec] + ([o_spec] if has_res else []) + ([tok_spec] if has_tok else [])
    args = (a, b) + ((residual,) if has_res else ()) + ((after,) if has_tok else ())
    return pl.pallas_call(
        body,
        name=name,
        grid=(m // tm, n // tn, nk),
        in_specs=in_specs,
        out_specs=o_spec,
        out_shape=jax.ShapeDtypeStruct((m, n), out_dtype),
        scratch_shapes=[pltpu.VMEM((tm, tn), F32)] if nk > 1 else [],
        compiler_params=_params(("parallel", "parallel", "arbitrary")),
    )(*args)


def _mm_swiglu(h, wg, wu, name, tm=512, tn=1024):
    m, k = h.shape
    n = wg.shape[1]
    tm, tn = _pick(m, tm), _pick(n, tn)

    def body(h_ref, g_ref, u_ref, a_ref, gs_ref, us_ref):
        hv = h_ref[...]
        g = _dot(hv, g_ref[...])
        u = _dot(hv, u_ref[...])
        a_ref[...] = (g * _sigmoid(g) * u).astype(BF16)
        gs_ref[...] = g.astype(BF16)
        us_ref[...] = u.astype(BF16)

    w_spec = pl.BlockSpec((k, tn), lambda i, j: (0, j))
    o_spec = pl.BlockSpec((tm, tn), lambda i, j: (i, j))
    out = jax.ShapeDtypeStruct((m, n), BF16)
    return pl.pallas_call(
        body, name=name, grid=(m // tm, n // tn),
        in_specs=[pl.BlockSpec((tm, k), lambda i, j: (i, 0)), w_spec, w_spec],
        out_specs=[o_spec] * 3, out_shape=[out] * 3,
        compiler_params=_params(("parallel", "parallel")),
    )(h, wg, wu)


def _mm_dact_swiglu(dx, wd, gs, us, name, after=None, tm=512, tn=1024):
    m, k = dx.shape
    n = wd.shape[0]
    tm, tn = _pick(m, tm), _pick(n, tn)
    has_tok = after is not None

    def body(*refs):
        dx_ref, wd_ref, g_ref, u_ref = refs[:4]
        dg_ref, du_ref = refs[-2:]
        dact = _dot_nt(dx_ref[...], wd_ref[...])
        g = g_ref[...].astype(F32)
        sg = _sigmoid(g)
        dg_ref[...] = (dact * u_ref[...].astype(F32) * sg * (1.0 + g * (1.0 - sg))).astype(BF16)
        du_ref[...] = (dact * g * sg).astype(BF16)

    o_spec = pl.BlockSpec((tm, tn), lambda i, j: (i, j))
    tok = [pl.BlockSpec((8, LANE), lambda i, j: (0, 0))] if has_tok else []
    out = jax.ShapeDtypeStruct((m, n), BF16)
    return pl.pallas_call(
        body, name=name, grid=(m // tm, n // tn),
        in_specs=[pl.BlockSpec((tm, k), lambda i, j: (i, 0)), pl.BlockSpec((tn, k), lambda i, j: (j, 0)),
                  o_spec, o_spec] + tok,
        out_specs=[o_spec] * 2, out_shape=[out] * 2,
        compiler_params=_params(("parallel", "parallel")),
    )(dx, wd, gs, us, *((after,) if has_tok else ()))


ROWS = 512


def _rms_fwd(x, g, name, after=None):
    t, d = x.shape
    has_tok = after is not None

    def body(*refs):
        x_ref, g_ref, o_ref = refs[0], refs[1], refs[-1]
        xv = x_ref[...]
        o_ref[...] = (xv * _rstd(xv) * g_ref[...]).astype(BF16)

    row = pl.BlockSpec((ROWS, d), lambda i: (i, 0))
    tok = [pl.BlockSpec((8, LANE), lambda i: (0, 0))] if has_tok else []
    return pl.pallas_call(
        body, name=name, grid=(t // ROWS,),
        in_specs=[row, pl.BlockSpec((1, d), lambda i: (0, 0))] + tok,
        out_specs=row, out_shape=jax.ShapeDtypeStruct((t, d), BF16),
        compiler_params=_params(("parallel",)),
    )(x, g, *((after,) if has_tok else ()))


def _rms_bwd(x, g, dh, dres, name):
    t, d = x.shape

    def body(x_ref, g_ref, dh_ref, dr_ref, dx_ref, dxb_ref, dg_ref):
        xv = x_ref[...]
        r = _rstd(xv)
        xh = xv * r
        dhv = dh_ref[...]

        @pl.when(pl.program_id(0) == 0)
        def _():
            dg_ref[...] = jnp.zeros_like(dg_ref)

        dg_ref[...] += jnp.sum(dhv * xh, axis=0, keepdims=True)
        dxh = dhv * g_ref[...]
        dx = r * (dxh - xh * jnp.mean(dxh * xh, axis=-1, keepdims=True)) + dr_ref[...]
        dx_ref[...] = dx
        dxb_ref[...] = dx.astype(BF16)

    row = pl.BlockSpec((ROWS // 2, d), lambda i: (i, 0))
    vec = pl.BlockSpec((1, d), lambda i: (0, 0))
    return pl.pallas_call(
        body, name=name, grid=(t // (ROWS // 2),),
        in_specs=[row, vec, row, row],
        out_specs=[row, row, vec],
        out_shape=[jax.ShapeDtypeStruct((t, d), F32), jax.ShapeDtypeStruct((t, d), BF16),
                   jax.ShapeDtypeStruct((1, d), F32)],
        compiler_params=_params(("arbitrary",)),
    )(x, g, dh, dres)


def _loss_fwd_bwd(y, target, name):
    t, d = y.shape
    inv = 1.0 / d

    def body(y_ref, t_ref, l_ref, dy_ref, dyb_ref):
        e = y_ref[...] - t_ref[...]

        @pl.when(pl.program_id(0) == 0)
        def _():
            l_ref[...] = jnp.zeros_like(l_ref)

        l_ref[...] += 0.5 * inv * jnp.sum(e * e)
        dy = e * inv
        dy_ref[...] = dy
        dyb_ref[...] = dy.astype(BF16)

    row = pl.BlockSpec((ROWS, d), lambda i: (i, 0))
    return pl.pallas_call(
        body, name=name, grid=(t // ROWS,),
        in_specs=[row, row],
        out_specs=[pl.BlockSpec((8, LANE), lambda i: (0, 0)), row, row],
        out_shape=[jax.ShapeDtypeStruct((8, LANE), F32), jax.ShapeDtypeStruct((t, d), F32),
                   jax.ShapeDtypeStruct((t, d), BF16)],
        compiler_params=_params(("arbitrary",)),
    )(y, target)


MIX_ROWS = 256


def _mix_fwd(o, y, proj, ga, gs, name):
    t = o.shape[0]
    gw = SSM_W // SSM_GROUPS

    def body(o_ref, y_ref, z_ref, ga_ref, gs_ref, c_ref):
        ov = o_ref[...]
        c_ref[:, 0:ATT_W] = (ov * _rstd(ov) * ga_ref[...]).astype(BF16)
        zv = z_ref[...]
        yz = y_ref[...] * (zv * _sigmoid(zv))
        for gi in range(SSM_GROUPS):
            seg = yz[:, gi * gw:(gi + 1) * gw]
            c_ref[:, ATT_W + gi * gw:ATT_W + (gi + 1) * gw] = (
                seg * _rstd(seg) * gs_ref[:, gi * gw:(gi + 1) * gw]).astype(BF16)

    half = pl.BlockSpec((MIX_ROWS, ATT_W), lambda i: (i, 0))
    vec = pl.BlockSpec((1, ATT_W), lambda i: (0, 0))
    return pl.pallas_call(
        body, name=name, grid=(t // MIX_ROWS,),
        in_specs=[half, half, pl.BlockSpec((MIX_ROWS, ATT_W), lambda i: (i, OFF_Z // ATT_W)), vec, vec],
        out_specs=pl.BlockSpec((MIX_ROWS, 2 * ATT_W), lambda i: (i, 0)),
        out_shape=jax.ShapeDtypeStruct((t, 2 * ATT_W), BF16),
        compiler_params=_params(("parallel",)),
    )(o, y, proj, ga, gs)


def _mix_bwd(dcat, o, y, proj, ga, gs, name):
    t = o.shape[0]
    gw = SSM_W // SSM_GROUPS

    def body(dc_ref, o_ref, y_ref, z_ref, ga_ref, gs_ref, do_ref, dy_ref, dz_ref, dga_ref, dgs_ref):
        @pl.when(pl.program_id(0) == 0)
        def _():
            dga_ref[...] = jnp.zeros_like(dga_ref)
            dgs_ref[...] = jnp.zeros_like(dgs_ref)

        ov = o_ref[...]
        r = _rstd(ov)
        oh = ov * r
        d_on = dc_ref[:, 0:ATT_W]
        dga_ref[...] += jnp.sum(d_on * oh, axis=0, keepdims=True)
        doh = d_on * ga_ref[...]
        do_ref[...] = r * (doh - oh * jnp.mean(doh * oh, axis=-1, keepdims=True))

        zv = z_ref[...]
        yv = y_ref[...]
        sz = _sigmoid(zv)
        silu = zv * sz
        yz = yv * silu
        for gi in range(SSM_GROUPS):
            sl = slice(gi * gw, (gi + 1) * gw)
            seg = yz[:, sl]
            rg = _rstd(seg)
            yh = seg * rg
            dyn = dc_ref[:, ATT_W + gi * gw:ATT_W + (gi + 1) * gw]
            dgs_ref[:, sl] += jnp.sum(dyn * yh, axis=0, keepdims=True)
            dyh = dyn * gs_ref[:, sl]
            dyz = rg * (dyh - yh * jnp.mean(dyh * yh, axis=-1, keepdims=True))
            dy_ref[:, sl] = dyz * silu[:, sl]
            dz_ref[:, sl] = (dyz * yv[:, sl] * (sz[:, sl] * (1.0 + zv[:, sl] * (1.0 - sz[:, sl])))).astype(BF16)

    half = pl.BlockSpec((MIX_ROWS, ATT_W), lambda i: (i, 0))
    vec = pl.BlockSpec((1, ATT_W), lambda i: (0, 0))
    return pl.pallas_call(
        body, name=name, grid=(t // MIX_ROWS,),
        in_specs=[pl.BlockSpec((MIX_ROWS, 2 * ATT_W), lambda i: (i, 0)), half, half,
                  pl.BlockSpec((MIX_ROWS, ATT_W), lambda i: (i, OFF_Z // ATT_W)), vec, vec],
        out_specs=[half, half, half, vec, vec],
        out_shape=[jax.ShapeDtypeStruct((t, ATT_W), F32), jax.ShapeDtypeStruct((t, SSM_W), F32),
                   jax.ShapeDtypeStruct((t, SSM_W), BF16), jax.ShapeDtypeStruct((1, ATT_W), F32),
                   jax.ShapeDtypeStruct((1, SSM_W), F32)],
        compiler_params=_params(("arbitrary",)),
    )(dcat, o, y, proj, ga, gs)


ATT_QB = 256
ATT_KB = 256


def _stacked(m):
    return jnp.concatenate([m, m], axis=0)


def _split_sum(x, m2):
    hi = x.astype(BF16)
    lo = (x - hi.astype(F32)).astype(BF16)
    return _dot(jnp.concatenate([hi, lo], axis=1), m2)


def _att_tile(z, qi, kj, row, col, m_strict2, carry):
    lse = jnp.log(1.0 + jnp.exp(-jnp.abs(z)))
    lb = jnp.minimum(z, 0.0) - lse
    lr = -jnp.maximum(z, 0.0) - lse
    mask = (kj * ATT_KB + col) < (qi * ATT_QB + row)
    lrm = jnp.where(mask, lr, 0.0)
    later = _split_sum(lrm, m_strict2) + carry
    w = jnp.where(mask, jnp.exp(lb + later), 0.0)
    return mask, lb, lrm, w


ATT_HP = 2


def _head_spec(seq, off):
    width = ATT_HP * ATT_DH
    per = ATT_HEADS // ATT_HP
    return pl.BlockSpec((seq, width), lambda s: (s // per, off // width + s % per))


def _head_lanes(h):
    return slice(h * ATT_DH, (h + 1) * ATT_DH)


def _attn_fwd(proj, gq, gk, nb, seq, name):
    nq = seq // ATT_QB
    scale = ATT_DH ** -0.5
    heads = range(ATT_HP)

    def body(q_ref, k_ref, v_ref, gq_ref, gk_ref, o_ref, qs, kn, vb):
        for h in heads:
            sl = _head_lanes(h)
            qv = q_ref[:, sl]
            kv = k_ref[:, sl]
            qs[:, sl] = (qv * _rstd(qv) * gq_ref[...] * scale).astype(BF16)
            kn[:, sl] = (kv * _rstd(kv) * gk_ref[...]).astype(BF16)
            vb[:, sl] = v_ref[:, sl].astype(BF16)
        row = lax.broadcasted_iota(jnp.int32, (ATT_QB, ATT_KB), 0)
        col = lax.broadcasted_iota(jnp.int32, (ATT_QB, ATT_KB), 1)
        m_strict2 = _stacked((row > col).astype(BF16))

        def key_rows(kj):
            return pl.ds(pl.multiple_of(kj * ATT_KB, ATT_KB), ATT_KB)

        def q_loop(qi, _):
            q0 = pl.multiple_of(qi * ATT_QB, ATT_QB)
            q_ts = [qs[pl.ds(q0, ATT_QB), _head_lanes(h)] for h in heads]

            def scores(h, kj):
                return _dot_nt(q_ts[h], kn[key_rows(kj), _head_lanes(h)])

            def k_loop(i, c):
                kj = qi - i
                rows = key_rows(kj)
                out = []
                for h in heads:
                    acc, carry, z = c[h]
                    z_next = scores(h, jnp.maximum(kj - 1, 0))
                    _, _, lrm, w = _att_tile(z, qi, kj, row, col, m_strict2, carry)
                    acc = acc + _dot(w.astype(BF16), vb[rows, _head_lanes(h)])
                    out.append((acc, carry + jnp.sum(lrm, axis=-1, keepdims=True), z_next))
                return tuple(out)

            init = tuple((jnp.zeros((ATT_QB, ATT_DH), F32), jnp.zeros((ATT_QB, 1), F32), scores(h, qi)) for h in heads)
            res = lax.fori_loop(0, qi + 1, k_loop, init)
            for h in heads:
                o_ref[pl.ds(q0, ATT_QB), _head_lanes(h)] = res[h][0]
            return 0

        lax.fori_loop(0, nq, q_loop, 0)

    vec = pl.BlockSpec((1, ATT_DH), lambda s: (0, 0))
    return pl.pallas_call(
        body, name=name, grid=(nb * ATT_HEADS // ATT_HP,),
        in_specs=[_head_spec(seq, OFF_Q), _head_spec(seq, OFF_K), _head_spec(seq, OFF_V), vec, vec],
        out_specs=_head_spec(seq, 0),
        out_shape=jax.ShapeDtypeStruct((nb * seq, ATT_W), F32),
        scratch_shapes=[pltpu.VMEM((seq, ATT_HP * ATT_DH), BF16)] * 3,
        compiler_params=_params(("parallel",)),
    )(proj, proj, proj, gq, gk)


def _attn_bwd(proj, do, gq, gk, nb, seq, name):
    nq = seq // ATT_QB
    nk = seq // ATT_KB
    scale = ATT_DH ** -0.5
    heads = range(ATT_HP)

    def body(q_ref, k_ref, v_ref, do_ref, gq_ref, gk_ref, dq_ref, dk_ref, dv_ref, dgq_ref, dgk_ref,
             qs, kn, vb, dob, dq_acc, dk_acc, dv_acc, gbuf, bbuf):
        @pl.when(pl.program_id(0) == 0)
        def _():
            dgq_ref[...] = jnp.zeros_like(dgq_ref)
            dgk_ref[...] = jnp.zeros_like(dgk_ref)

        for h in heads:
            sl = _head_lanes(h)
            qv = q_ref[:, sl]
            kv = k_ref[:, sl]
            qs[:, sl] = (qv * _rstd(qv) * gq_ref[...] * scale).astype(BF16)
            kn[:, sl] = (kv * _rstd(kv) * gk_ref[...]).astype(BF16)
            vb[:, sl] = v_ref[:, sl].astype(BF16)
            dob[:, sl] = do_ref[:, sl].astype(BF16)
        dk_acc[...] = jnp.zeros_like(dk_acc)
        dv_acc[...] = jnp.zeros_like(dv_acc)
        row = lax.broadcasted_iota(jnp.int32, (ATT_QB, ATT_KB), 0)
        col = lax.broadcasted_iota(jnp.int32, (ATT_QB, ATT_KB), 1)
        m_strict2 = _stacked((row > col).astype(BF16))
        m_prefix2 = _stacked((row < col).astype(BF16))

        def key_rows(kj):
            return pl.ds(pl.multiple_of(kj * ATT_KB, ATT_KB), ATT_KB)

        def q_loop(qi, _):
            q0 = pl.multiple_of(qi * ATT_QB, ATT_QB)
            q_ts = [qs[pl.ds(q0, ATT_QB), _head_lanes(h)] for h in heads]
            do_ts = [dob[pl.ds(q0, ATT_QB), _head_lanes(h)] for h in heads]

            def scores(h, kj):
                return _dot_nt(q_ts[h], kn[key_rows(kj), _head_lanes(h)])

            def down(i, c):
                kj = qi - i
                rows = key_rows(kj)
                out = []
                for h in heads:
                    carry, z = c[h]
                    sl = _head_lanes(h)
                    z_next = scores(h, jnp.maximum(kj - 1, 0))
                    _, lb, lrm, w = _att_tile(z, qi, kj, row, col, m_strict2, carry)
                    dw = _dot_nt(do_ts[h], vb[rows, sl])
                    gbuf[h * nk + kj] = w * dw
                    bbuf[h * nk + kj] = jnp.exp(lb)
                    dv_acc[rows, sl] += _dot_tn(w.astype(BF16), do_ts[h])
                    out.append((carry + jnp.sum(lrm, axis=-1, keepdims=True), z_next))
                return tuple(out)

            lax.fori_loop(0, qi + 1, down, tuple((jnp.zeros((ATT_QB, 1), F32), scores(h, qi)) for h in heads))

            def up(kj, c):
                rows = key_rows(kj)
                mask = (kj * ATT_KB + col) < (qi * ATT_QB + row)
                out = []
                for h in heads:
                    acc, carry, within = c[h]
                    sl = _head_lanes(h)
                    g = gbuf[h * nk + kj]
                    beta = bbuf[h * nk + kj]
                    within_next = _split_sum(gbuf[h * nk + jnp.minimum(kj + 1, qi)], m_prefix2)
                    big_g = within + carry
                    dz = jnp.where(mask, g * (1.0 - beta) - big_g * beta, 0.0).astype(BF16)
                    acc = acc + _dot(dz, kn[rows, sl])
                    dk_acc[rows, sl] += _dot_tn(dz, q_ts[h])
                    out.append((acc, carry + jnp.sum(g, axis=-1, keepdims=True), within_next))
                return tuple(out)

            init = tuple((jnp.zeros((ATT_QB, ATT_DH), F32), jnp.zeros((ATT_QB, 1), F32),
                          _split_sum(gbuf[h * nk], m_prefix2)) for h in heads)
            res = lax.fori_loop(0, qi + 1, up, init)
            for h in heads:
                dq_acc[pl.ds(q0, ATT_QB), _head_lanes(h)] = res[h][0]
            return 0

        lax.fori_loop(0, nq, q_loop, 0)

        def norm_bwd(xv, gain, dyn):
            r = _rstd(xv)
            xh = xv * r
            dgain = jnp.sum(dyn * xh, axis=0, keepdims=True)
            dxh = dyn * gain
            return r * (dxh - xh * jnp.mean(dxh * xh, axis=-1, keepdims=True)), dgain

        for h in heads:
            sl = _head_lanes(h)
            dq, dgq = norm_bwd(q_ref[:, sl], gq_ref[...], dq_acc[:, sl] * scale)
            dk, dgk = norm_bwd(k_ref[:, sl], gk_ref[...], dk_acc[:, sl])
            dq_ref[:, sl] = dq.astype(BF16)
            dk_ref[:, sl] = dk.astype(BF16)
            dv_ref[:, sl] = dv_acc[:, sl].astype(BF16)
            dgq_ref[...] += dgq
            dgk_ref[...] += dgk

    vec = pl.BlockSpec((1, ATT_DH), lambda s: (0, 0))
    big = jax.ShapeDtypeStruct((nb * seq, ATT_W), BF16)
    small = jax.ShapeDtypeStruct((1, ATT_DH), F32)
    width = ATT_HP * ATT_DH
    return pl.pallas_call(
        body, name=name, grid=(nb * ATT_HEADS // ATT_HP,),
        in_specs=[_head_spec(seq, OFF_Q), _head_spec(seq, OFF_K), _head_spec(seq, OFF_V), _head_spec(seq, 0), vec, vec],
        out_specs=[_head_spec(seq, 0)] * 3 + [vec, vec],
        out_shape=[big, big, big, small, small],
        scratch_shapes=[pltpu.VMEM((seq, width), BF16)] * 4 + [pltpu.VMEM((seq, width), F32)] * 3
        + [pltpu.VMEM((ATT_HP * nk, ATT_QB, ATT_KB), F32)] * 2,
        compiler_params=_params(("arbitrary",)),
    )(proj, proj, proj, do, gq, gk)


CONV_COLS = 256


def _pack_conv(conv_w, conv_b):
    return jnp.concatenate([conv_w, conv_b[None, :], jnp.zeros((3, CONV_DIM), F32)], axis=0)


def _conv_pre(raw, w8, rowi):
    pre = w8[CONV_K:CONV_K + 1, :] + raw * w8[CONV_K - 1:CONV_K, :]
    for k in range(1, CONV_K):
        sh = jnp.where(rowi >= k, pltpu.roll(raw, k, 0), 0.0)
        pre = pre + sh * w8[CONV_K - 1 - k:CONV_K - k, :]
    return pre


def _conv_fwd(proj, cw8, nb, seq, name):
    ncol = CONV_DIM // CONV_COLS

    def body(x_ref, w_ref, o_ref):
        rowi = lax.broadcasted_iota(jnp.int32, (seq, 1), 0)
        pre = _conv_pre(x_ref[...], w_ref[...], rowi)
        o_ref[...] = pre * _sigmoid(pre)

    return pl.pallas_call(
        body, name=name, grid=(nb, ncol),
        in_specs=[pl.BlockSpec((seq, CONV_COLS), lambda b, j: (b, OFF_XS // CONV_COLS + j)),
                  pl.BlockSpec((8, CONV_COLS), lambda b, j: (0, j))],
        out_specs=pl.BlockSpec((seq, CONV_COLS), lambda b, j: (b, j)),
        out_shape=jax.ShapeDtypeStruct((nb * seq, CONV_DIM), F32),
        compiler_params=_params(("parallel", "parallel")),
    )(proj, cw8)


def _conv_bwd(proj, dact, cw8, nb, seq, name):
    ncol = CONV_DIM // CONV_COLS

    def body(x_ref, d_ref, w_ref, dx_ref, dw_ref):
        @pl.when(pl.program_id(1) == 0)
        def _():
            dw_ref[...] = jnp.zeros_like(dw_ref)

        rowi = lax.broadcasted_iota(jnp.int32, (seq, 1), 0)
        raw = x_ref[...]
        w8 = w_ref[...]
        pre = _conv_pre(raw, w8, rowi)
        sg = _sigmoid(pre)
        dpre = d_ref[...] * (sg * (1.0 + pre * (1.0 - sg)))
        dw_ref[CONV_K:CONV_K + 1, :] += jnp.sum(dpre, axis=0, keepdims=True)
        dw_ref[CONV_K - 1:CONV_K, :] += jnp.sum(dpre * raw, axis=0, keepdims=True)
        draw = dpre * w8[CONV_K - 1:CONV_K, :]
        for k in range(1, CONV_K):
            sh = jnp.where(rowi >= k, pltpu.roll(raw, k, 0), 0.0)
            dw_ref[CONV_K - 1 - k:CONV_K - k, :] += jnp.sum(dpre * sh, axis=0, keepdims=True)
            up = jnp.where(rowi < seq - k, pltpu.roll(dpre, seq - k, 0), 0.0)
            draw = draw + up * w8[CONV_K - 1 - k:CONV_K - k, :]
        dx_ref[...] = draw.astype(BF16)

    return pl.pallas_call(
        body, name=name, grid=(ncol, nb),
        in_specs=[pl.BlockSpec((seq, CONV_COLS), lambda j, b: (b, OFF_XS // CONV_COLS + j)),
                  pl.BlockSpec((seq, CONV_COLS), lambda j, b: (b, j)),
                  pl.BlockSpec((8, CONV_COLS), lambda j, b: (0, j))],
        out_specs=[pl.BlockSpec((seq, CONV_COLS), lambda j, b: (b, j)),
                   pl.BlockSpec((8, CONV_COLS), lambda j, b: (0, j))],
        out_shape=[jax.ShapeDtypeStruct((nb * seq, CONV_DIM), BF16), jax.ShapeDtypeStruct((8, CONV_DIM), F32)],
        compiler_params=_params(("parallel", "arbitrary")),
    )(proj, dact, cw8)


def _pack_heads(dt_bias, a_log, d_skip):
    rows = jnp.stack([dt_bias, a_log, d_skip]).reshape(3, SSM_GROUPS, SSM_HG).transpose(1, 0, 2)
    return jnp.pad(rows, ((0, 0), (0, 8 - 3), (0, LANE - SSM_HG)))


def _split3_rows(x):
    hi = x.astype(BF16)
    r1 = x - hi.astype(F32)
    mid = r1.astype(BF16)
    lo = (r1 - mid.astype(F32)).astype(BF16)
    return jnp.concatenate([hi, mid, lo], axis=0)


def _split2_rows(x):
    hi = x.astype(BF16)
    return jnp.concatenate([hi, (x - hi.astype(F32)).astype(BF16)], axis=0)


def _ssd_specs(seq):
    gx = SSM_HG * SSM_P
    return dict(
        xs=pl.BlockSpec((seq, gx), lambda g, b: (b, g)),
        bm=pl.BlockSpec((seq, SSM_N), lambda g, b: (b, SSM_W // SSM_N + g)),
        cm=pl.BlockSpec((seq, SSM_N), lambda g, b: (b, SSM_W // SSM_N + SSM_GROUPS + g)),
        dt=pl.BlockSpec((seq, LANE), lambda g, b: (b, OFF_DT // LANE + g)),
        hp=pl.BlockSpec((1, 8, LANE), lambda g, b: (g, 0, 0)),
        head=pl.BlockSpec((seq, gx), lambda g, b: (b, g)),
        grp=pl.BlockSpec((seq, SSM_N), lambda g, b: (b, g)),
    )


SSM_GX = SSM_HG * SSM_P


def _ssd_masks():
    li = lax.broadcasted_iota(jnp.int32, (CHUNK, CHUNK), 0)
    si = lax.broadcasted_iota(jnp.int32, (CHUNK, CHUNK), 1)
    head = lax.broadcasted_iota(jnp.int32, (LANE, SSM_GX), 0)
    lane = lax.broadcasted_iota(jnp.int32, (LANE, SSM_GX), 1)
    expand = (lane // SSM_P == head).astype(BF16)
    head_t = lax.broadcasted_iota(jnp.int32, (SSM_GX, LANE), 1)
    lane_t = lax.broadcasted_iota(jnp.int32, (SSM_GX, LANE), 0)
    gather = (lane_t // SSM_P == head_t).astype(BF16)
    return dict(
        causal=li >= si, causal_t=si >= li,
        tril3=jnp.concatenate([(si <= li).astype(BF16)] * 3, axis=1),
        triu2=jnp.concatenate([(si >= li).astype(BF16)] * 2, axis=1),
        below2=jnp.concatenate([(si < li).astype(BF16)] * 2, axis=1),
        expand2=_stacked(expand), gather2=_stacked(gather))


def _per_head(x, mk):
    return _split_sum(x, mk["expand2"])


def _head_sums(x, mk):
    return _split_sum(x, mk["gather2"])


def _row8(v):
    return jnp.broadcast_to(v, (8, v.shape[1]))


def _ssd_fwd(act, proj, hp, nb, seq, name):
    nc = seq // CHUNK

    def body(xs_ref, b_ref, c_ref, dt_ref, hp_ref, y_ref, dt_s, da_s, hst):
        mk = _ssd_masks()
        hpv = hp_ref[0]
        dt = _softplus(dt_ref[...] + hpv[0:1, :])
        a = -jnp.exp(hpv[1:2, :])
        dsk_row = _per_head(_row8(hpv[2:3, :]), mk)[0:1]
        dt_s[...] = dt
        da_s[...] = dt * a
        hst[...] = jnp.zeros_like(hst)

        def chunk(c, _):
            rows = pl.ds(pl.multiple_of(c * CHUNK, CHUNK), CHUNK)
            acol = _dot(mk["tril3"], _split3_rows(da_s[rows, :]))
            arow = acol.T
            alast = acol[CHUNK - 1:CHUNK, :]
            ea = _per_head(jnp.exp(acol), mk)
            eb = _per_head(jnp.exp(alast - acol), mk)
            el = _per_head(_row8(jnp.exp(alast)), mk)[0:1]
            bb = b_ref[rows, :].astype(BF16)
            cb = c_ref[rows, :].astype(BF16)
            cbm = _dot_nt(cb, bb)
            xc = xs_ref[rows, :]
            u = xc * _per_head(dt_s[rows, :], mk)
            ub = u.astype(BF16)
            ht = hst[...]
            y_ref[rows, :] = ea * _dot(cb, ht.astype(BF16)) + dsk_row * xc
            for j in range(SSM_HG):
                sl = slice(j * SSM_P, (j + 1) * SSM_P)
                decay = jnp.where(mk["causal"], jnp.exp(jnp.minimum(acol[:, j:j + 1] - arow[j:j + 1, :], 0.0)), 0.0)
                y_ref[rows, sl] += _dot((cbm * decay).astype(BF16), ub[:, sl])
            hst[...] = el * ht + _dot_tn(bb, (u * eb).astype(BF16))
            return 0

        lax.fori_loop(0, nc, chunk, 0)

    sp = _ssd_specs(seq)
    return pl.pallas_call(
        body, name=name, grid=(SSM_GROUPS, nb),
        in_specs=[sp["xs"], sp["bm"], sp["cm"], sp["dt"], sp["hp"]],
        out_specs=sp["head"],
        out_shape=jax.ShapeDtypeStruct((nb * seq, SSM_W), F32),
        scratch_shapes=[pltpu.VMEM((seq, LANE), F32)] * 2 + [pltpu.VMEM((SSM_N, SSM_GX), F32)],
        compiler_params=_params(("parallel", "parallel")),
    )(act, act, act, proj, hp)


def _ssd_bwd(act, proj, dy, hp, nb, seq, name):
    nc = seq // CHUNK

    def body(xs_ref, b_ref, c_ref, dt_ref, hp_ref, dy_ref, dxs_ref, db_ref, dc_ref, ddt_ref, dhp_ref,
             dt_s, da_s, ddt_s, hs, lam, du_s):
        @pl.when(pl.program_id(1) == 0)
        def _():
            dhp_ref[...] = jnp.zeros_like(dhp_ref)

        mk = _ssd_masks()
        hpv = hp_ref[0]
        a = -jnp.exp(hpv[1:2, :])
        dsk_row = _per_head(_row8(hpv[2:3, :]), mk)[0:1]
        dt_s[...] = _softplus(dt_ref[...] + hpv[0:1, :])
        da_s[...] = dt_s[...] * a
        lane = lax.broadcasted_iota(jnp.int32, (1, LANE), 1)

        def chunk_rows(c):
            return pl.ds(pl.multiple_of(c * CHUNK, CHUNK), CHUNK)

        def decays(c):
            acol = _dot(mk["tril3"], _split3_rows(da_s[chunk_rows(c), :]))
            alast = acol[CHUNK - 1:CHUNK, :]
            return acol, alast

        hs[0] = jnp.zeros((SSM_N, SSM_GX), F32)

        def fwd_chunk(c, _):
            rows = chunk_rows(c)
            acol, alast = decays(c)
            eb = _per_head(jnp.exp(alast - acol), mk)
            el = _per_head(_row8(jnp.exp(alast)), mk)[0:1]
            u = xs_ref[rows, :] * _per_head(dt_s[rows, :], mk)
            hs[c + 1] = el * hs[c] + _dot_tn(b_ref[rows, :].astype(BF16), (u * eb).astype(BF16))
            return 0

        lax.fori_loop(0, nc - 1, fwd_chunk, 0)
        lam[...] = jnp.zeros_like(lam)

        def bwd_chunk(i, carry):
            dd_row, da_vec = carry
            c = nc - 1 - i
            rows = chunk_rows(c)
            acol, alast = decays(c)
            arow = acol.T
            ea = _per_head(jnp.exp(acol), mk)
            eb = _per_head(jnp.exp(alast - acol), mk)
            el = _per_head(_row8(jnp.exp(alast)), mk)[0:1]
            dt_all = _per_head(dt_s[rows, :], mk)
            bb = b_ref[rows, :].astype(BF16)
            cb = c_ref[rows, :].astype(BF16)
            cbm = _dot_nt(cb, bb)
            cbt = _dot_nt(bb, cb)
            xc = xs_ref[rows, :]
            dyc = dy_ref[rows, :]
            u = xc * dt_all
            ub = u.astype(BF16)
            dyb = dyc.astype(BF16)
            h_in = hs[c]
            lm = lam[...]
            hb = h_in.astype(BF16)
            lb = lm.astype(BF16)
            y_off = ea * _dot(cb, hb)
            du_off = eb * _dot(bb, lb)
            dye = (ea * dyc).astype(BF16)
            zero = jnp.zeros((CHUNK, CHUNK), F32)
            dcb, dcbt, d_a = zero, zero, zero
            for j in range(SSM_HG):
                sl = slice(j * SSM_P, (j + 1) * SSM_P)
                seg = acol[:, j:j + 1] - arow[j:j + 1, :]
                decay = jnp.where(mk["causal"], jnp.exp(jnp.minimum(seg, 0.0)), 0.0)
                decay_t = jnp.where(mk["causal_t"], jnp.exp(jnp.minimum(-seg, 0.0)), 0.0)
                m = cbm * decay
                mt = cbt * decay_t
                dm = _dot_nt(dyb[:, sl], ub[:, sl])
                dmt = _dot_nt(ub[:, sl], dyb[:, sl])
                dcb = dcb + dm * decay
                dcbt = dcbt + dmt * decay_t
                du_s[:, sl] = _dot(mt.astype(BF16), dyb[:, sl])
                d_a_j = jnp.sum(dm * m, axis=-1, keepdims=True) - jnp.sum(dmt * mt, axis=-1, keepdims=True)
                d_a = jnp.where(lane == j, d_a_j, d_a)
            du = du_s[...] + du_off
            dxs_ref[rows, :] = du * dt_all + dsk_row * dyc
            dc_ref[rows, :] = _dot_nt(dye, hb) + _dot(dcb.astype(BF16), bb)
            db_ref[rows, :] = _dot_nt((eb * u).astype(BF16), lb) + _dot(dcbt.astype(BF16), cb)
            lam[...] = el * lm + _dot_tn(cb, dye)
            d_a = d_a + _head_sums(dyc * y_off, mk)
            f_a = _head_sums(du_off * u, mk)
            c_a = jnp.exp(alast) * _head_sums(_row8(jnp.sum(lm * h_in, axis=0, keepdims=True)), mk)[0:1]
            dda = _dot(mk["triu2"], _split2_rows(d_a)) + _dot(mk["below2"], _split2_rows(f_a)) + c_a
            ddt_s[rows, :] = dda * a + _head_sums(du * xc, mk)
            da_vec = da_vec + jnp.sum(dda * dt_s[rows, :], axis=0, keepdims=True)
            dd_row = dd_row + jnp.sum(dyc * xc, axis=0, keepdims=True)
            return dd_row, da_vec

        init = (jnp.zeros((1, SSM_GX), F32), jnp.zeros((1, LANE), F32))
        dd_row, da_vec = lax.fori_loop(0, nc, bwd_chunk, init)
        ddt_raw = ddt_s[...] * _sigmoid(dt_ref[...] + hpv[0:1, :])
        ddt_ref[...] = ddt_raw.astype(BF16)
        dhp_ref[0, 0:1, :] += jnp.sum(ddt_raw, axis=0, keepdims=True)
        dhp_ref[0, 1:2, :] += da_vec * a
        dhp_ref[0, 2:3, :] += _head_sums(_row8(dd_row), mk)[0:1]

    sp = _ssd_specs(seq)
    t = nb * seq
    return pl.pallas_call(
        body, name=name, grid=(SSM_GROUPS, nb),
        in_specs=[sp["xs"], sp["bm"], sp["cm"], sp["dt"], sp["hp"], sp["head"]],
        out_specs=[sp["head"], sp["grp"], sp["grp"], sp["grp"], sp["hp"]],
        out_shape=[jax.ShapeDtypeStruct((t, SSM_W), F32), jax.ShapeDtypeStruct((t, SSM_GROUPS * SSM_N), F32),
                   jax.ShapeDtypeStruct((t, SSM_GROUPS * SSM_N), F32),
                   jax.ShapeDtypeStruct((t, SSM_GROUPS * LANE), BF16),
                   jax.ShapeDtypeStruct((SSM_GROUPS, 8, LANE), F32)],
        scratch_shapes=[pltpu.VMEM((seq, LANE), F32)] * 3
        + [pltpu.VMEM((nc, SSM_N, SSM_GX), F32), pltpu.VMEM((SSM_N, SSM_GX), F32), pltpu.VMEM((CHUNK, SSM_GX), F32)],
        compiler_params=_params(("parallel", "arbitrary")),
    )(act, act, act, proj, hp, dy)


ANY = pl.BlockSpec(memory_space=pl.ANY)


def _block_index(p):
    return 4 * p[0] + 2 * p[1] + p[2]


def _all_gather(shards, name):
    n = len(shards)

    def body(*refs):
        ins, outs = refs[:n], refs[n:2 * n]
        send_sems, recv_sems, local_sems = refs[2 * n:]
        x, y, c = lax.axis_index("x"), lax.axis_index("y"), lax.axis_index("c")
        me, sibling = (x, y, c), (x, y, 1 - c)
        chips = [(1 - x, y), (x, 1 - y), (1 - x, 1 - y)]

        def copy(i, k, block, to, src=None):
            dst = outs[i].at[_block_index(block)]
            return pltpu.make_async_remote_copy(
                src_ref=dst if src is None else src, dst_ref=dst,
                send_sem=send_sems.at[i, k], recv_sem=recv_sems.at[i, k],
                device_id=to, device_id_type=MESH)

        mine = [pltpu.make_async_copy(ins[i], outs[i].at[_block_index(me)], local_sems.at[i]) for i in range(n)]
        for cp in mine:
            cp.start()
        first = []
        for i in range(n):
            first.append(copy(i, 0, me, sibling, src=ins[i]))
            first += [copy(i, 1 + j, me, (*chip, c), src=ins[i]) for j, chip in enumerate(chips)]
        for cp in first:
            cp.start()
        passed = []
        for j, chip in enumerate(chips):
            for i in range(n):
                copy(i, 1 + j, (*chip, c), me).wait_recv()
                fwd = copy(i, 4 + j, (*chip, c), sibling)
                fwd.start()
                passed.append(fwd)
        for i in range(n):
            copy(i, 0, sibling, me).wait_recv()
            for j, chip in enumerate(chips):
                copy(i, 4 + j, (*chip, 1 - c), me).wait_recv()
        for cp in first + passed:
            cp.wait_send()
        for cp in mine:
            cp.wait()

    return pl.pallas_call(
        body, name=name,
        in_specs=[ANY] * n, out_specs=[ANY] * n,
        out_shape=[jax.ShapeDtypeStruct((N_DEV,) + s.shape, s.dtype) for s in shards],
        scratch_shapes=[pltpu.SemaphoreType.DMA((n, 7)), pltpu.SemaphoreType.DMA((n, 7)),
                        pltpu.SemaphoreType.DMA((n,))],
    )(*shards)


HBM = pl.BlockSpec(memory_space=pltpu.HBM)
SEM = pl.BlockSpec(memory_space=pltpu.SEMAPHORE)
EFFECT = pltpu.SideEffectType.DATAFLOW_SIDE_EFFECTING


def _my_block():
    return _block_index((lax.axis_index("x"), lax.axis_index("y"), lax.axis_index("c")))


def _peer(k):
    x, y, c = lax.axis_index("x"), lax.axis_index("y"), lax.axis_index("c")
    return (1 - x if k & 4 else x, 1 - y if k & 2 else y, 1 - c if k & 1 else c)


def _plan_copies(plan, src_refs, land_refs, send_sems, recv_sems):
    me = _my_block()
    copies = []
    for e, (si, di, src_view, dst_view, _) in enumerate(plan):
        for k in range(1, N_DEV):
            copies.append(pltpu.make_async_remote_copy(
                src_ref=src_view(src_refs[si], _block_index(_peer(k))),
                dst_ref=dst_view(land_refs[di], me),
                send_sem=send_sems[e], recv_sem=recv_sems[e],
                device_id=_peer(k), device_id_type=MESH))
    return copies


def _plan_waits(plan, land_refs, send_sems, recv_sems):
    waits = []
    for e, (_, di, _, _, seven) in enumerate(plan):
        view = seven(land_refs[di])
        waits.append(pltpu.make_async_remote_copy(
            src_ref=view, dst_ref=view, send_sem=send_sems[e], recv_sem=recv_sems[e],
            device_id=_peer(1), device_id_type=MESH))
    return waits


def _plan_own(plan, src_refs, land_refs, own_sems):
    me = _my_block()
    return [pltpu.make_async_copy(src_view(src_refs[si], me), dst_view(land_refs[di], me), own_sems[e])
            for e, (si, di, src_view, dst_view, _) in enumerate(plan)]


def _copies_start(srcs, lands, plan, name, after=None):
    ns, nl, ne = len(srcs), len(lands), len(plan)
    extra = [] if after is None else [after]

    nin = ns + nl + len(extra)

    def body(*refs):
        src_refs, land_refs = refs[:ns], refs[ns:ns + nl]
        send_sems, recv_sems = refs[nin:nin + ne], refs[nin + ne:nin + 2 * ne]
        own_sems = refs[nin + 2 * ne:nin + 3 * ne]
        token = refs[-1]
        for cp in _plan_copies(plan, src_refs, land_refs, send_sems, recv_sems):
            cp.start()
        for cp in _plan_own(plan, src_refs, land_refs, own_sems):
            cp.start()
        token[...] = jnp.zeros_like(token)

    thru = [pltpu.HBM(a.shape, a.dtype) for a in list(srcs) + list(lands)]
    res = pl.pallas_call(
        body, name=name,
        in_specs=[HBM] * (ns + nl) + [ANY] * len(extra),
        out_specs=[SEM] * (3 * ne) + [HBM] * (ns + nl) + [pl.BlockSpec(memory_space=pltpu.VMEM)],
        out_shape=[pltpu.SemaphoreType.DMA(())] * (3 * ne) + thru + [jax.ShapeDtypeStruct((8, LANE), F32)],
        input_output_aliases={i: 3 * ne + i for i in range(ns + nl)},
        compiler_params=pltpu.CompilerParams(has_side_effects=EFFECT),
    )(*[pltpu.with_memory_space_constraint(a, pltpu.HBM) for a in list(srcs) + list(lands)], *extra)
    return dict(sems=res[:3 * ne], srcs=res[3 * ne:3 * ne + ns], lands=res[3 * ne + ns:3 * ne + ns + nl],
                token=res[-1], plan=plan)


def _copies_wait(flight, after, name):
    srcs, lands, plan = flight["srcs"], flight["lands"], flight["plan"]
    ns, nl, ne = len(srcs), len(lands), len(plan)

    def body(*refs):
        src_refs, land_refs = refs[:ns], refs[ns:ns + nl]
        send_sems, recv_sems = refs[ns + nl:ns + nl + ne], refs[ns + nl + ne:ns + nl + 2 * ne]
        own_sems = refs[ns + nl + 2 * ne:ns + nl + 3 * ne]
        for cp in _plan_waits(plan, land_refs, send_sems, recv_sems):
            cp.wait_send()
            cp.wait_recv()
        for cp in _plan_own(plan, src_refs, land_refs, own_sems):
            cp.wait()

    thru = [pltpu.HBM(a.shape, a.dtype) for a in list(srcs) + list(lands)]
    res = pl.pallas_call(
        body, name=name,
        in_specs=[HBM] * (ns + nl) + [SEM] * (3 * ne) + [ANY],
        out_specs=[HBM] * (ns + nl),
        out_shape=thru,
        input_output_aliases={i: i for i in range(ns + nl)},
        compiler_params=pltpu.CompilerParams(has_side_effects=EFFECT),
    )(*srcs, *lands, *flight["sems"], after)
    return list(res[ns:])


def _adamw_math(w, g, m, v):
    m = ADAM_B1 * m + (1.0 - ADAM_B1) * g
    v = ADAM_B2 * v + (1.0 - ADAM_B2) * (g * g)
    m_hat = m / (1.0 - ADAM_B1 ** ADAM_STEP)
    v_hat = v / (1.0 - ADAM_B2 ** ADAM_STEP)
    delta = -ADAM_LR * (m_hat / (jnp.sqrt(v_hat) + ADAM_EPS) + ADAM_WD * w)
    return delta, m, v


def _adamw(parts, w, m, v, name, rows):
    depth, r, c = w.shape
    cp = parts[0].shape[2]
    assert r % rows == 0 and len(parts) == depth

    def body(*refs):
        p_refs = refs[:depth]
        w_ref, m_ref, v_ref, g_ref, d_ref, mo_ref, vo_ref = refs[depth:]
        for li in range(depth):
            @pl.when(pl.program_id(0) == li)
            def _(li=li):
                g = p_refs[li][0][:, :c].astype(F32)
                for j in range(1, N_DEV):
                    g = g + p_refs[li][j][:, :c].astype(F32)
                d, mn, vn = _adamw_math(w_ref[...], g, m_ref[...], v_ref[...])
                g_ref[...] = g
                d_ref[...] = d
                mo_ref[...] = mn
                vo_ref[...] = vn

    def part_spec(li):
        return pl.BlockSpec((N_DEV, rows, cp), lambda l, i: (0, jnp.where(l == li, i, 0), 0))

    blk = pl.BlockSpec((None, rows, c), lambda l, i: (l, i, 0))
    out = jax.ShapeDtypeStruct((depth, r, c), F32)
    return pl.pallas_call(
        body, name=name, grid=(depth, r // rows),
        in_specs=[part_spec(li) for li in range(depth)] + [blk, blk, blk],
        out_specs=[blk] * 4, out_shape=[out] * 4,
        compiler_params=_params(("arbitrary", "arbitrary")),
    )(*parts, w, m, v)


def _sum_parts(parts, name):
    _, r, c = parts.shape

    def body(p_ref, o_ref):
        g = p_ref[0]
        for j in range(1, N_DEV):
            g = g + p_ref[j]
        o_ref[...] = g

    return pl.pallas_call(
        body, name=name, out_shape=jax.ShapeDtypeStruct((r, c), F32),
        compiler_params=_params(),
    )(parts)


def _adamw_small(g, w, m, v, name):
    def body(g_ref, w_ref, m_ref, v_ref, d_ref, mo_ref, vo_ref):
        d, mn, vn = _adamw_math(w_ref[...], g_ref[...], m_ref[...], v_ref[...])
        d_ref[...] = d
        mo_ref[...] = mn
        vo_ref[...] = vn

    out = jax.ShapeDtypeStruct(w.shape, F32)
    return pl.pallas_call(body, name=name, out_shape=[out] * 3, compiler_params=_params())(g, w, m, v)


SMALL = ("norm_mix", "q_gain", "k_gain", "conv_b", "dt_bias", "a_log", "d_skip", "attn_out_gain",
         "ssm_out_gain", "norm_ffn")


def _pad_lanes(a):
    n = a.shape[-1]
    return jnp.pad(a, ((0, 0), (0, -n % LANE)))


def _pack_small(d):
    return jnp.concatenate([_pad_lanes(d[k]) for k in SMALL], axis=1)


def _unpack_small(packed, like):
    out, off = {}, 0
    for k in SMALL:
        n = like[k].shape[-1]
        out[k] = packed[:, off:off + n]
        off += n + (-n % LANE)
    return out


def _full_cols(gathered):
    _, r, c = gathered.shape
    return gathered.transpose(1, 0, 2).reshape(r, N_DEV * c)


def _col_blocks(full):
    r, c8 = full.shape
    return full.reshape(r, N_DEV, c8 // N_DEV).transpose(1, 0, 2)


FF_BLK = 768
FF_PAD = N_DEV * FF_BLK


def _whole(ref, block):
    return ref


def _rows_of(size):
    return lambda ref, block: ref.at[pl.ds(pl.multiple_of(block * size, size), size), :]


def _cols_of(size, base=0):
    return lambda ref, block: ref.at[:, pl.ds(pl.multiple_of(base + block * size, LANE), size)]


def _slot(ref, block):
    return ref.at[block]


def _seven_slots(ref):
    return ref.at[pl.ds(0, N_DEV - 1)]


def _seven_rows(size):
    return lambda ref: ref.at[pl.ds(0, (N_DEV - 1) * size), :]


def _seven_cols(size):
    return lambda ref: ref.at[:, pl.ds(0, (N_DEV - 1) * size)]


GATHER_A = [(0, 0, _whole, _slot, _seven_slots), (1, 1, _whole, _rows_of(256), _seven_rows(256))]
GATHER_B = [(0, 0, _whole, _cols_of(FF_BLK), _seven_cols(FF_BLK)),
            (1, 1, _whole, _cols_of(FF_BLK), _seven_cols(FF_BLK)),
            (2, 2, _whole, _rows_of(FF_BLK), _seven_rows(FF_BLK))]
SCATTER_A = [(0, 0, _slot, _slot, _seven_slots), (1, 1, _rows_of(256), _slot, _seven_slots)]
SCATTER_B = [(0, 0, _cols_of(FF_BLK), _slot, _seven_slots), (1, 1, _cols_of(FF_BLK), _slot, _seven_slots),
             (2, 2, _rows_of(FF_BLK), _slot, _seven_slots)]


def _gather_lands(which, shards, d):
    if which == "a":
        return [lax.empty((N_DEV,) + shards[0].shape, BF16), lax.empty((d, d), BF16)]
    return [lax.empty((d, FF_PAD), BF16), lax.empty((d, FF_PAD), BF16), lax.empty((FF_PAD, d), BF16)]


def _scatter_lands(which, grads):
    if which == "a":
        g_in, g_out = grads
        return [lax.empty(g_in.shape, BF16), lax.empty((N_DEV, g_out.shape[0] // N_DEV, g_out.shape[1]), BF16)]
    g_gate, _, g_down = grads
    return [lax.empty((N_DEV, g_gate.shape[0], FF_BLK), BF16), lax.empty((N_DEV, g_gate.shape[0], FF_BLK), BF16),
            lax.empty((N_DEV, FF_BLK, g_down.shape[1]), BF16)]


def _pad_w_in(full):
    d = full.shape[0]
    z = jnp.zeros((d, LANE - SSM_HG), full.dtype)
    tail = jnp.zeros((d, NPROJ - OFF_DT - SSM_GROUPS * LANE), full.dtype)
    return jnp.concatenate([full[:, :OFF_DT], full[:, OFF_DT:OFF_DT + SSM_HG], z,
                            full[:, OFF_DT + SSM_HG:IN_DIM], z, tail], axis=1)


def _unpad_w_in(padded):
    return jnp.concatenate([padded[:, :OFF_DT], padded[:, OFF_DT:OFF_DT + SSM_HG],
                            padded[:, OFF_DT + LANE:OFF_DT + LANE + SSM_HG]], axis=1)


def kernel(x, norm_mix, w_in, q_gain, k_gain, conv_w, conv_b, dt_bias, a_log, d_skip, attn_out_gain, ssm_out_gain, w_out, norm_ffn, w_gate, w_up, w_down, loss_target, m_norm_mix, m_w_in, m_q_gain, m_k_gain, m_conv_w, m_conv_b, m_dt_bias, m_a_log, m_d_skip, m_attn_out_gain, m_ssm_out_gain, m_w_out, m_norm_ffn, m_w_gate, m_w_up, m_w_down, v_norm_mix, v_w_in, v_q_gain, v_k_gain, v_conv_w, v_conv_b, v_dt_bias, v_a_log, v_d_skip, v_attn_out_gain, v_ssm_out_gain, v_w_out, v_norm_ffn, v_w_gate, v_w_up, v_w_down):
    nb, seq, d = x.shape
    depth = w_in.shape[0]
    t = nb * seq
    w = dict(norm_mix=norm_mix, w_in=w_in, q_gain=q_gain, k_gain=k_gain, conv_w=conv_w, conv_b=conv_b,
             dt_bias=dt_bias, a_log=a_log, d_skip=d_skip, attn_out_gain=attn_out_gain, ssm_out_gain=ssm_out_gain,
             w_out=w_out, norm_ffn=norm_ffn, w_gate=w_gate, w_up=w_up, w_down=w_down)
    mom = dict(norm_mix=m_norm_mix, w_in=m_w_in, q_gain=m_q_gain, k_gain=m_k_gain, conv_w=m_conv_w, conv_b=m_conv_b,
               dt_bias=m_dt_bias, a_log=m_a_log, d_skip=m_d_skip, attn_out_gain=m_attn_out_gain,
               ssm_out_gain=m_ssm_out_gain, w_out=m_w_out, norm_ffn=m_norm_ffn, w_gate=m_w_gate, w_up=m_w_up,
               w_down=m_w_down)
    var = dict(norm_mix=v_norm_mix, w_in=v_w_in, q_gain=v_q_gain, k_gain=v_k_gain, conv_w=v_conv_w, conv_b=v_conv_b,
               dt_bias=v_dt_bias, a_log=v_a_log, d_skip=v_d_skip, attn_out_gain=v_attn_out_gain,
               ssm_out_gain=v_ssm_out_gain, w_out=v_w_out, norm_ffn=v_norm_ffn, w_gate=v_w_gate, w_up=v_w_up,
               w_down=v_w_down)
    ff = w_gate.shape[2]

    (conv_all,) = _all_gather([conv_w], "gather_conv")

    def shards_a(li):
        return [w_in[li].astype(BF16), w_out[li].astype(BF16)]

    def shards_b(li):
        return [jnp.pad(w_gate[li].astype(BF16), ((0, 0), (0, FF_BLK - ff))),
                jnp.pad(w_up[li].astype(BF16), ((0, 0), (0, FF_BLK - ff))),
                jnp.pad(w_down[li].astype(BF16), ((0, FF_BLK - ff), (0, 0)))]

    def small_params(li):
        p = {k: w[k][li][None, :] for k in ("norm_mix", "q_gain", "k_gain", "attn_out_gain", "ssm_out_gain", "norm_ffn")}
        conv_full = conv_all[:, li].transpose(1, 0, 2).reshape(CONV_K, CONV_DIM)
        p["cw8"] = _pack_conv(conv_full, conv_b[li])
        p["hp"] = _pack_heads(dt_bias[li], a_log[li], d_skip[li])
        return p

    xc = x.reshape(t, d)
    cur = shards_a(0)
    flight = _copies_start(cur, _gather_lands("a", cur, d), GATHER_A, "gather_a0")
    lands_a = _copies_wait(flight, xc, "gather_a0_wait")
    layers, saved = [], []
    for li in range(depth):
        tag = f"l{li}_"
        p = small_params(li)
        p["w_in"] = _pad_w_in(_full_cols(lands_a[0]))
        p["w_out"] = lands_a[1]
        cur = shards_b(li)
        flight = _copies_start(cur, _gather_lands("b", cur, d), GATHER_B, tag + "gather_b", after=lands_a[1])
        h1 = _rms_fwd(xc, p["norm_mix"], tag + "rms1", after=flight["token"])
        proj = _matmul(h1, p["w_in"], "nn", F32, tag + "mm_in")
        o = _attn_fwd(proj, p["q_gain"], p["k_gain"], nb, seq, tag + "attn")
        act = _conv_fwd(proj, p["cw8"], nb, seq, tag + "conv")
        y = _ssd_fwd(act, proj, p["hp"], nb, seq, tag + "ssd")
        cat = _mix_fwd(o, y, proj, p["attn_out_gain"], p["ssm_out_gain"], tag + "mix")
        x1 = _matmul(cat, p["w_out"], "nn", F32, tag + "mm_out", residual=xc)
        p["w_gate"], p["w_up"], p["w_down"] = _copies_wait(flight, x1, tag + "gather_b_wait")
        token = None
        if li + 1 < depth:
            nxt = shards_a(li + 1)
            flight = _copies_start(nxt, _gather_lands("a", nxt, d), GATHER_A, f"gather_a{li + 1}",
                                   after=p["w_down"])
            token = flight["token"]
        h2 = _rms_fwd(x1, p["norm_ffn"], tag + "rms2", after=token)
        a, gate, up = _mm_swiglu(h2, p["w_gate"], p["w_up"], tag + "mm_gu")
        x2 = _matmul(a, p["w_down"], "nn", F32, tag + "mm_down", residual=x1)
        if li + 1 < depth:
            lands_a = _copies_wait(flight, x2, f"gather_a{li + 1}_wait")
        saved.append(dict(x=xc, h1=h1, proj=proj, o=o, act=act, y=y, cat=cat, x1=x1, h2=h2, gate=gate, up=up, a=a))
        layers.append(p)
        xc = x2

    loss_blk, dx, dxb = _loss_fwd_bwd(xc, loss_target.reshape(t, d), "loss")
    loss = lax.psum(loss_blk[0, 0], ("x", "y", "c"))

    grads = [dict() for _ in range(depth)]
    recv = [dict() for _ in range(depth)]
    flight_a, token = None, None
    for li in reversed(range(depth)):
        tag = f"l{li}_b_"
        p, s, g = layers[li], saved[li], grads[li]
        dgate, dup = _mm_dact_swiglu(dxb, p["w_down"], s["gate"], s["up"], tag + "mm_dact", after=token)
        g_down = _matmul(s["a"], dxb, "tn", BF16, tag + "mm_dwd")
        dh2 = _matmul(dgate, p["w_gate"], "nt", F32, tag + "mm_dh2g")
        dh2 = _matmul(dup, p["w_up"], "nt", F32, tag + "mm_dh2u", residual=dh2)
        g_gate = _matmul(s["h2"], dgate, "tn", BF16, tag + "mm_dwg")
        g_up = _matmul(s["h2"], dup, "tn", BF16, tag + "mm_dwu")
        if flight_a is not None:
            recv[li + 1]["w_in"], recv[li + 1]["w_out"] = _copies_wait(flight_a, g_up, f"l{li + 1}_b_scatter_a_wait")
        grads_b = [g_gate, g_up, g_down]
        flight_b = _copies_start(grads_b, _scatter_lands("b", grads_b), SCATTER_B,
                                 tag + "scatter_b", after=recv[li + 1]["w_out"] if li + 1 < depth else None)
        dx1, dx1b, g["norm_ffn"] = _rms_bwd(s["x1"], p["norm_ffn"], dh2, dx, tag + "rms2")
        dcat = _matmul(dx1b, p["w_out"], "nt", F32, tag + "mm_dcat", after=flight_b["token"])
        g_out = _matmul(s["cat"], dx1b, "tn", BF16, tag + "mm_dwo")
        do, dy, dz, g["attn_out_gain"], g["ssm_out_gain"] = _mix_bwd(
            dcat, s["o"], s["y"], s["proj"], p["attn_out_gain"], p["ssm_out_gain"], tag + "mix")
        dq, dk, dv, g["q_gain"], g["k_gain"] = _attn_bwd(s["proj"], do, p["q_gain"], p["k_gain"], nb, seq, tag + "attn")
        dxa, dba, dca, ddt, dhp = _ssd_bwd(s["act"], s["proj"], dy, p["hp"], nb, seq, tag + "ssd")
        dxbc, dcw8 = _conv_bwd(s["proj"], jnp.concatenate([dxa, dba, dca], axis=1), p["cw8"], nb, seq, tag + "conv")
        g["conv_w"] = dcw8[0:CONV_K]
        g["conv_b"] = dcw8[CONV_K:CONV_K + 1]
        heads = dhp[:, 0:3, 0:SSM_HG].transpose(1, 0, 2).reshape(3, SSM_HEADS)
        g["dt_bias"], g["a_log"], g["d_skip"] = heads[0:1], heads[1:2], heads[2:3]
        tail = jnp.zeros((t, NPROJ - OFF_DT - SSM_GROUPS * LANE), BF16)
        dproj = jnp.concatenate([dq, dk, dv, dz, dxbc, ddt, tail], axis=1)
        recv[li]["w_gate"], recv[li]["w_up"], recv[li]["w_down"] = _copies_wait(flight_b, dproj, tag + "scatter_b_wait")
        dh1 = _matmul(dproj, p["w_in"], "nt", F32, tag + "mm_dh1")
        g_in = _col_blocks(_unpad_w_in(_matmul(s["h1"], dproj, "tn", BF16, tag + "mm_dwin")))
        flight_a = _copies_start([g_in, g_out], _scatter_lands("a", [g_in, g_out]), SCATTER_A, tag + "scatter_a")
        token = flight_a["token"]
        dx, dxb, g["norm_mix"] = _rms_bwd(s["x"], p["norm_mix"], dh1, dx1, tag + "rms1")
    grad_x = dx.reshape(nb, seq, d)

    out_g, out_d, out_m, out_v = {}, {}, {}, {}

    def update(k, rows):
        parts = [recv[li][k] for li in range(depth)]
        out_g[k], out_d[k], out_m[k], out_v[k] = _adamw(parts, w[k], mom[k], var[k], "adamw_" + k, rows)

    update("w_gate", 128)
    update("w_up", 128)
    update("w_down", 64)
    recv[0]["w_in"], recv[0]["w_out"] = _copies_wait(flight_a, out_g["w_down"], "l0_b_scatter_a_wait")
    update("w_in", 128)
    update("w_out", 128)

    small_g = {k: jnp.concatenate([grads[li][k] for li in range(depth)], axis=0) for k in SMALL}
    conv_g = jnp.stack([grads[li]["conv_w"] for li in range(depth)]).reshape(depth, CONV_K * CONV_DIM)
    packed_g = jnp.concatenate([_pack_small(small_g), conv_g], axis=1)
    (parts_small,) = _all_gather([packed_g], "gather_small_grads")
    g_small = _sum_parts(parts_small, "sum_small_grads")
    n_small = g_small.shape[1] - CONV_K * CONV_DIM
    conv_total = g_small[:, n_small:].reshape(depth, CONV_K, CONV_DIM)
    me = _block_index((lax.axis_index("x"), lax.axis_index("y"), lax.axis_index("c")))
    cshard = conv_w.shape[2]
    conv_mine = lax.dynamic_slice_in_dim(conv_total, me * cshard, cshard, axis=2)

    def with_conv(small_packed, conv_part):
        return jnp.concatenate([small_packed, conv_part.reshape(depth, CONV_K * cshard)], axis=1)

    res = _adamw_small(with_conv(g_small[:, :n_small], conv_mine),
                       with_conv(_pack_small(w), conv_w), with_conv(_pack_small(mom), m_conv_w),
                       with_conv(_pack_small(var), v_conv_w), "adamw_small")
    g_all = with_conv(g_small[:, :n_small], conv_mine)
    for dst, packed in zip((out_g, out_d, out_m, out_v), (g_all,) + tuple(res)):
        dst.update(_unpack_small(packed[:, :n_small], w))
        dst["conv_w"] = packed[:, n_small:].reshape(depth, CONV_K, cshard)

    names = ("norm_mix", "w_in", "q_gain", "k_gain", "conv_w", "conv_b", "dt_bias", "a_log", "d_skip",
             "attn_out_gain", "ssm_out_gain", "w_out", "norm_ffn", "w_gate", "w_up", "w_down")
    return (loss, grad_x, *[out_g[k] for k in names], *[out_d[k] for k in names],
            *[out_m[k] for k in names], *[out_v[k] for k in names])
```

```python
import functools
import math

import jax
import jax.numpy as jnp
from jax import lax
from jax.experimental import pallas as pl
from jax.experimental.pallas import tpu as pltpu

F32 = jnp.float32
BF16 = jnp.bfloat16
MESH = pl.DeviceIdType.MESH

N_DEV = 8
EPS = 1e-6
ATT_HEADS = 8
ATT_DH = 128
ATT_W = ATT_HEADS * ATT_DH
SSM_W = 1024
SSM_P = 64
SSM_N = 128
SSM_GROUPS = 2
SSM_HG = 8
SSM_HEADS = SSM_GROUPS * SSM_HG
CHUNK = 128
CONV_K = 4
CONV_DIM = SSM_W + 2 * SSM_GROUPS * SSM_N
LANE = 128
OFF_Q, OFF_K, OFF_V, OFF_Z, OFF_XS = 0, ATT_W, 2 * ATT_W, 3 * ATT_W, 4 * ATT_W
OFF_B = OFF_XS + SSM_W
OFF_C = OFF_B + SSM_GROUPS * SSM_N
OFF_DT = OFF_C + SSM_GROUPS * SSM_N
NPROJ = 6144
IN_DIM = OFF_DT + SSM_HEADS

ADAM_LR = 0.001
ADAM_B1 = 0.9
ADAM_B2 = 0.999
ADAM_EPS = 1e-08
ADAM_WD = 0.01
ADAM_STEP = 10

VMEM_LIMIT = 56 * 1024 * 1024
MATMUL_OPERAND_BYTES = 26 * 1024 * 1024


def _params(sem=None):
    return pltpu.CompilerParams(dimension_semantics=sem, vmem_limit_bytes=VMEM_LIMIT)


def _pick(dim, target):
    if dim <= target:
        return dim
    best = None
    for t in range(LANE, target + 1, LANE):
        if dim % t == 0:
            best = t
    assert best is not None, (dim, target)
    return best


def _dot(a, b, dims=((1,), (0,))):
    return lax.dot_general(a, b, (dims, ((), ())), preferred_element_type=F32)


def _dot_nt(a, b):
    return _dot(a, b, ((1,), (1,)))


def _dot_tn(a, b):
    return _dot(a, b, ((0,), (0,)))


def _sigmoid(x):
    return 1.0 / (1.0 + jnp.exp(-x))


def _softplus(x):
    return jnp.maximum(x, 0.0) + jnp.log(1.0 + jnp.exp(-jnp.abs(x)))


def _rstd(x):
    return lax.rsqrt(jnp.mean(x * x, axis=-1, keepdims=True) + EPS)


def _matmul(a, b, mode, out_dtype, name, residual=None, after=None, tm=512, tn=1024, tk=2048):
    if mode == "nn":
        (m, k), (k2, n) = a.shape, b.shape
    elif mode == "nt":
        (m, k), (n, k2) = a.shape, b.shape
    else:
        (k, m), (k2, n) = a.shape, b.shape
    assert k == k2, (a.shape, b.shape, mode)
    tm, tn, tk = _pick(m, tm), _pick(n, tn), _pick(k, tk)
    for cand_tn in (tn, _pick(n, tn // 2)):
        if 2 * 2 * (tm * k + k * cand_tn) <= MATMUL_OPERAND_BYTES:
            tn, tk = cand_tn, k
            break
    nk = k // tk
    dims = {"nn": ((1,), (0,)), "nt": ((1,), (1,)), "tn": ((0,), (0,))}[mode]
    has_res = residual is not None

    has_tok = after is not None

    def body(*refs):
        a_ref, b_ref = refs[:2]
        r_ref = refs[2] if has_res else None
        o_ref = refs[2 + has_res + has_tok]
        prod = _dot(a_ref[...], b_ref[...], dims)

        def finish(r):
            if r_ref is not None:
                r = r + r_ref[...]
            o_ref[...] = r.astype(o_ref.dtype)

        if nk == 1:
            finish(prod)
        else:
            acc = refs[-1]
            kk = pl.program_id(2)

            @pl.when(kk == 0)
            def _():
                acc[...] = prod

            @pl.when(kk > 0)
            def _():
                acc[...] += prod

            @pl.when(kk == nk - 1)
            def _():
                finish(acc[...])

    if mode == "tn":
        a_spec = pl.BlockSpec((tk, tm), lambda i, j, kk: (kk, i))
    else:
        a_spec = pl.BlockSpec((tm, tk), lambda i, j, kk: (i, kk))
    if mode == "nt":
        b_spec = pl.BlockSpec((tn, tk), lambda i, j, kk: (j, kk))
    else:
        b_spec = pl.BlockSpec((tk, tn), lambda i, j, kk: (kk, j))
    o_spec = pl.BlockSpec((tm, tn), lambda i, j, kk: (i, j))
    tok_spec = pl.BlockSpec((8, LANE), lambda i, j, kk: (0, 0))
    in_specs = [a_spec, b_spec] + ([o_spec] if has_res else []) + ([tok_spec] if has_tok else [])
    args = (a, b) + ((residual,) if has_res else ()) + ((after,) if has_tok else ())
    return pl.pallas_call(
        body,
        name=name,
        grid=(m // tm, n // tn, nk),
        in_specs=in_specs,
        out_specs=o_spec,
        out_shape=jax.ShapeDtypeStruct((m, n), out_dtype),
        scratch_shapes=[pltpu.VMEM((tm, tn), F32)] if nk > 1 else [],
        compiler_params=_params(("parallel", "parallel", "arbitrary")),
    )(*args)


def _mm_swiglu(h, wg_t, wu_t, name, tm=512, tn=1024):
    m, k = h.shape
    n = wg_t.shape[0]
    tm, tn = _pick(m, tm), _pick(n, tn)

    def body(h_ref, g_ref, u_ref, a_ref, gs_ref, us_ref):
        hv = h_ref[...]
        g = _dot_nt(hv, g_ref[...])
        u = _dot_nt(hv, u_ref[...])
        a_ref[...] = (g * _sigmoid(g) * u).astype(BF16)
        gs_ref[...] = g.astype(BF16)
        us_ref[...] = u.astype(BF16)

    w_spec = pl.BlockSpec((tn, k), lambda i, j: (j, 0))
    o_spec = pl.BlockSpec((tm, tn), lambda i, j: (i, j))
    out = jax.ShapeDtypeStruct((m, n), BF16)
    return pl.pallas_call(
        body, name=name, grid=(m // tm, n // tn),
        in_specs=[pl.BlockSpec((tm, k), lambda i, j: (i, 0)), w_spec, w_spec],
        out_specs=[o_spec] * 3, out_shape=[out] * 3,
        compiler_params=_params(("parallel", "parallel")),
    )(h, wg_t, wu_t)


def _mm_dact_swiglu(dx, wd, gs, us, name, after=None, tm=512, tn=1024):
    m, k = dx.shape
    n = wd.shape[0]
    tm, tn = _pick(m, tm), _pick(n, tn)
    has_tok = after is not None

    def body(*refs):
        dx_ref, wd_ref, g_ref, u_ref = refs[:4]
        dg_ref, du_ref = refs[-2:]
        dact = _dot_nt(dx_ref[...], wd_ref[...])
        g = g_ref[...].astype(F32)
        sg = _sigmoid(g)
        dg_ref[...] = (dact * u_ref[...].astype(F32) * sg * (1.0 + g * (1.0 - sg))).astype(BF16)
        du_ref[...] = (dact * g * sg).astype(BF16)

    o_spec = pl.BlockSpec((tm, tn), lambda i, j: (i, j))
    tok = [pl.BlockSpec((8, LANE), lambda i, j: (0, 0))] if has_tok else []
    out = jax.ShapeDtypeStruct((m, n), BF16)
    return pl.pallas_call(
        body, name=name, grid=(m // tm, n // tn),
        in_specs=[pl.BlockSpec((tm, k), lambda i, j: (i, 0)), pl.BlockSpec((tn, k), lambda i, j: (j, 0)),
                  o_spec, o_spec] + tok,
        out_specs=[o_spec] * 2, out_shape=[out] * 2,
        compiler_params=_params(("parallel", "parallel")),
    )(dx, wd, gs, us, *((after,) if has_tok else ()))


ROWS = 512


def _rms_fwd(x, g, name, after=None):
    t, d = x.shape
    has_tok = after is not None

    def body(*refs):
        x_ref, g_ref, o_ref = refs[0], refs[1], refs[-1]
        xv = x_ref[...]
        o_ref[...] = (xv * _rstd(xv) * g_ref[...]).astype(BF16)

    row = pl.BlockSpec((ROWS, d), lambda i: (i, 0))
    tok = [pl.BlockSpec((8, LANE), lambda i: (0, 0))] if has_tok else []
    return pl.pallas_call(
        body, name=name, grid=(t // ROWS,),
        in_specs=[row, pl.BlockSpec((1, d), lambda i: (0, 0))] + tok,
        out_specs=row, out_shape=jax.ShapeDtypeStruct((t, d), BF16),
        compiler_params=_params(("parallel",)),
    )(x, g, *((after,) if has_tok else ()))


def _rms_bwd(x, g, dh, dres, name):
    t, d = x.shape

    def body(x_ref, g_ref, dh_ref, dr_ref, dx_ref, dxb_ref, dg_ref):
        xv = x_ref[...]
        r = _rstd(xv)
        xh = xv * r
        dhv = dh_ref[...]

        @pl.when(pl.program_id(0) == 0)
        def _():
            dg_ref[...] = jnp.zeros_like(dg_ref)

        dg_ref[...] += jnp.sum(dhv * xh, axis=0, keepdims=True)
        dxh = dhv * g_ref[...]
        dx = r * (dxh - xh * jnp.mean(dxh * xh, axis=-1, keepdims=True)) + dr_ref[...]
        dx_ref[...] = dx
        dxb_ref[...] = dx.astype(BF16)

    row = pl.BlockSpec((ROWS // 2, d), lambda i: (i, 0))
    vec = pl.BlockSpec((1, d), lambda i: (0, 0))
    return pl.pallas_call(
        body, name=name, grid=(t // (ROWS // 2),),
        in_specs=[row, vec, row, row],
        out_specs=[row, row, vec],
        out_shape=[jax.ShapeDtypeStruct((t, d), F32), jax.ShapeDtypeStruct((t, d), BF16),
                   jax.ShapeDtypeStruct((1, d), F32)],
        compiler_params=_params(("arbitrary",)),
    )(x, g, dh, dres)


def _loss_fwd_bwd(y, target, name):
    t, d = y.shape
    inv = 1.0 / d

    def body(y_ref, t_ref, l_ref, dy_ref, dyb_ref):
        e = y_ref[...] - t_ref[...]

        @pl.when(pl.program_id(0) == 0)
        def _():
            l_ref[...] = jnp.zeros_like(l_ref)

        l_ref[...] += 0.5 * inv * jnp.sum(e * e)
        dy = e * inv
        dy_ref[...] = dy
        dyb_ref[...] = dy.astype(BF16)

    row = pl.BlockSpec((ROWS, d), lambda i: (i, 0))
    return pl.pallas_call(
        body, name=name, grid=(t // ROWS,),
        in_specs=[row, row],
        out_specs=[pl.BlockSpec((8, LANE), lambda i: (0, 0)), row, row],
        out_shape=[jax.ShapeDtypeStruct((8, LANE), F32), jax.ShapeDtypeStruct((t, d), F32),
                   jax.ShapeDtypeStruct((t, d), BF16)],
        compiler_params=_params(("arbitrary",)),
    )(y, target)


MIX_ROWS = 256


def _mix_fwd(o, y, proj, ga, gs, name):
    t = o.shape[0]
    gw = SSM_W // SSM_GROUPS

    def body(o_ref, y_ref, z_ref, ga_ref, gs_ref, c_ref):
        ov = o_ref[...]
        c_ref[:, 0:ATT_W] = (ov * _rstd(ov) * ga_ref[...]).astype(BF16)
        zv = z_ref[...]
        yz = y_ref[...] * (zv * _sigmoid(zv))
        for gi in range(SSM_GROUPS):
            seg = yz[:, gi * gw:(gi + 1) * gw]
            c_ref[:, ATT_W + gi * gw:ATT_W + (gi + 1) * gw] = (
                seg * _rstd(seg) * gs_ref[:, gi * gw:(gi + 1) * gw]).astype(BF16)

    half = pl.BlockSpec((MIX_ROWS, ATT_W), lambda i: (i, 0))
    vec = pl.BlockSpec((1, ATT_W), lambda i: (0, 0))
    return pl.pallas_call(
        body, name=name, grid=(t // MIX_ROWS,),
        in_specs=[half, half, pl.BlockSpec((MIX_ROWS, ATT_W), lambda i: (i, OFF_Z // ATT_W)), vec, vec],
        out_specs=pl.BlockSpec((MIX_ROWS, 2 * ATT_W), lambda i: (i, 0)),
        out_shape=jax.ShapeDtypeStruct((t, 2 * ATT_W), BF16),
        compiler_params=_params(("parallel",)),
    )(o, y, proj, ga, gs)


def _mix_bwd(dcat, o, y, proj, ga, gs, name):
    t = o.shape[0]
    gw = SSM_W // SSM_GROUPS

    def body(dc_ref, o_ref, y_ref, z_ref, ga_ref, gs_ref, do_ref, dy_ref, dz_ref, dga_ref, dgs_ref):
        @pl.when(pl.program_id(0) == 0)
        def _():
            dga_ref[...] = jnp.zeros_like(dga_ref)
            dgs_ref[...] = jnp.zeros_like(dgs_ref)

        ov = o_ref[...]
        r = _rstd(ov)
        oh = ov * r
        d_on = dc_ref[:, 0:ATT_W]
        dga_ref[...] += jnp.sum(d_on * oh, axis=0, keepdims=True)
        doh = d_on * ga_ref[...]
        do_ref[...] = r * (doh - oh * jnp.mean(doh * oh, axis=-1, keepdims=True))

        zv = z_ref[...]
        yv = y_ref[...]
        sz = _sigmoid(zv)
        silu = zv * sz
        yz = yv * silu
        for gi in range(SSM_GROUPS):
            sl = slice(gi * gw, (gi + 1) * gw)
            seg = yz[:, sl]
            rg = _rstd(seg)
            yh = seg * rg
            dyn = dc_ref[:, ATT_W + gi * gw:ATT_W + (gi + 1) * gw]
            dgs_ref[:, sl] += jnp.sum(dyn * yh, axis=0, keepdims=True)
            dyh = dyn * gs_ref[:, sl]
            dyz = rg * (dyh - yh * jnp.mean(dyh * yh, axis=-1, keepdims=True))
            dy_ref[:, sl] = dyz * silu[:, sl]
            dz_ref[:, sl] = (dyz * yv[:, sl] * (sz[:, sl] * (1.0 + zv[:, sl] * (1.0 - sz[:, sl])))).astype(BF16)

    half = pl.BlockSpec((MIX_ROWS, ATT_W), lambda i: (i, 0))
    vec = pl.BlockSpec((1, ATT_W), lambda i: (0, 0))
    return pl.pallas_call(
        body, name=name, grid=(t // MIX_ROWS,),
        in_specs=[pl.BlockSpec((MIX_ROWS, 2 * ATT_W), lambda i: (i, 0)), half, half,
                  pl.BlockSpec((MIX_ROWS, ATT_W), lambda i: (i, OFF_Z // ATT_W)), vec, vec],
        out_specs=[half, half, half, vec, vec],
        out_shape=[jax.ShapeDtypeStruct((t, ATT_W), F32), jax.ShapeDtypeStruct((t, SSM_W), F32),
                   jax.ShapeDtypeStruct((t, SSM_W), BF16), jax.ShapeDtypeStruct((1, ATT_W), F32),
                   jax.ShapeDtypeStruct((1, SSM_W), F32)],
        compiler_params=_params(("arbitrary",)),
    )(dcat, o, y, proj, ga, gs)


ATT_QB = 256
ATT_KB = 256


def _stacked(m):
    return jnp.concatenate([m, m], axis=0)


def _split_sum(x, m2):
    hi = x.astype(BF16)
    lo = (x - hi.astype(F32)).astype(BF16)
    return _dot(jnp.concatenate([hi, lo], axis=1), m2)


def _att_tile(z, qi, kj, row, col, m_strict2, carry):
    lse = jnp.log(1.0 + jnp.exp(-jnp.abs(z)))
    lb = jnp.minimum(z, 0.0) - lse
    lr = -jnp.maximum(z, 0.0) - lse
    mask = (kj * ATT_KB + col) < (qi * ATT_QB + row)
    lrm = jnp.where(mask, lr, 0.0)
    later = _split_sum(lrm, m_strict2) + carry
    w = jnp.where(mask, jnp.exp(lb + later), 0.0)
    return mask, lb, lrm, w


ATT_HP = 2


def _head_spec(seq, off):
    width = ATT_HP * ATT_DH
    per = ATT_HEADS // ATT_HP
    return pl.BlockSpec((seq, width), lambda s: (s // per, off // width + s % per))


def _head_lanes(h):
    return slice(h * ATT_DH, (h + 1) * ATT_DH)


def _attn_fwd(proj, gq, gk, nb, seq, name):
    nq = seq // ATT_QB
    scale = ATT_DH ** -0.5
    heads = range(ATT_HP)

    def body(q_ref, k_ref, v_ref, gq_ref, gk_ref, o_ref, qs, kn, vb):
        for h in heads:
            sl = _head_lanes(h)
            qv = q_ref[:, sl]
            kv = k_ref[:, sl]
            qs[:, sl] = (qv * _rstd(qv) * gq_ref[...] * scale).astype(BF16)
            kn[:, sl] = (kv * _rstd(kv) * gk_ref[...]).astype(BF16)
            vb[:, sl] = v_ref[:, sl].astype(BF16)
        row = lax.broadcasted_iota(jnp.int32, (ATT_QB, ATT_KB), 0)
        col = lax.broadcasted_iota(jnp.int32, (ATT_QB, ATT_KB), 1)
        m_strict2 = _stacked((row > col).astype(BF16))

        def key_rows(kj):
            return pl.ds(pl.multiple_of(kj * ATT_KB, ATT_KB), ATT_KB)

        def q_loop(qi, _):
            q0 = pl.multiple_of(qi * ATT_QB, ATT_QB)
            q_ts = [qs[pl.ds(q0, ATT_QB), _head_lanes(h)] for h in heads]

            def scores(h, kj):
                return _dot_nt(q_ts[h], kn[key_rows(kj), _head_lanes(h)])

            def k_loop(i, c):
                kj = qi - i
                rows = key_rows(kj)
                out = []
                for h in heads:
                    acc, carry, z = c[h]
                    z_next = scores(h, jnp.maximum(kj - 1, 0))
                    _, _, lrm, w = _att_tile(z, qi, kj, row, col, m_strict2, carry)
                    acc = acc + _dot(w.astype(BF16), vb[rows, _head_lanes(h)])
                    out.append((acc, carry + jnp.sum(lrm, axis=-1, keepdims=True), z_next))
                return tuple(out)

            init = tuple((jnp.zeros((ATT_QB, ATT_DH), F32), jnp.zeros((ATT_QB, 1), F32), scores(h, qi)) for h in heads)
            res = lax.fori_loop(0, qi + 1, k_loop, init)
            for h in heads:
                o_ref[pl.ds(q0, ATT_QB), _head_lanes(h)] = res[h][0]
            return 0

        lax.fori_loop(0, nq, q_loop, 0)

    vec = pl.BlockSpec((1, ATT_DH), lambda s: (0, 0))
    return pl.pallas_call(
        body, name=name, grid=(nb * ATT_HEADS // ATT_HP,),
        in_specs=[_head_spec(seq, OFF_Q), _head_spec(seq, OFF_K), _head_spec(seq, OFF_V), vec, vec],
        out_specs=_head_spec(seq, 0),
        out_shape=jax.ShapeDtypeStruct((nb * seq, ATT_W), F32),
        scratch_shapes=[pltpu.VMEM((seq, ATT_HP * ATT_DH), BF16)] * 3,
        compiler_params=_params(("parallel",)),
    )(proj, proj, proj, gq, gk)


def _attn_bwd(proj, do, gq, gk, nb, seq, name):
    nq = seq // ATT_QB
    nk = seq // ATT_KB
    scale = ATT_DH ** -0.5
    heads = range(ATT_HP)

    def body(q_ref, k_ref, v_ref, do_ref, gq_ref, gk_ref, dq_ref, dk_ref, dv_ref, dgq_ref, dgk_ref,
             qs, kn, vb, dob, dq_acc, dk_acc, dv_acc, gbuf, bbuf):
        @pl.when(pl.program_id(0) == 0)
        def _():
            dgq_ref[...] = jnp.zeros_like(dgq_ref)
            dgk_ref[...] = jnp.zeros_like(dgk_ref)

        for h in heads:
            sl = _head_lanes(h)
            qv = q_ref[:, sl]
            kv = k_ref[:, sl]
            qs[:, sl] = (qv * _rstd(qv) * gq_ref[...] * scale).astype(BF16)
            kn[:, sl] = (kv * _rstd(kv) * gk_ref[...]).astype(BF16)
            vb[:, sl] = v_ref[:, sl].astype(BF16)
            dob[:, sl] = do_ref[:, sl].astype(BF16)
        dk_acc[...] = jnp.zeros_like(dk_acc)
        dv_acc[...] = jnp.zeros_like(dv_acc)
        row = lax.broadcasted_iota(jnp.int32, (ATT_QB, ATT_KB), 0)
        col = lax.broadcasted_iota(jnp.int32, (ATT_QB, ATT_KB), 1)
        m_strict2 = _stacked((row > col).astype(BF16))
        m_prefix2 = _stacked((row < col).astype(BF16))

        def key_rows(kj):
            return pl.ds(pl.multiple_of(kj * ATT_KB, ATT_KB), ATT_KB)

        def q_loop(qi, _):
            q0 = pl.multiple_of(qi * ATT_QB, ATT_QB)
            q_ts = [qs[pl.ds(q0, ATT_QB), _head_lanes(h)] for h in heads]
            do_ts = [dob[pl.ds(q0, ATT_QB), _head_lanes(h)] for h in heads]

            def scores(h, kj):
                return _dot_nt(q_ts[h], kn[key_rows(kj), _head_lanes(h)])

            def down(i, c):
                kj = qi - i
                rows = key_rows(kj)
                out = []
                for h in heads:
                    carry, z = c[h]
                    sl = _head_lanes(h)
                    z_next = scores(h, jnp.maximum(kj - 1, 0))
                    _, lb, lrm, w = _att_tile(z, qi, kj, row, col, m_strict2, carry)
                    dw = _dot_nt(do_ts[h], vb[rows, sl])
                    gbuf[h * nk + kj] = w * dw
                    bbuf[h * nk + kj] = jnp.exp(lb)
                    dv_acc[rows, sl] += _dot_tn(w.astype(BF16), do_ts[h])
                    out.append((carry + jnp.sum(lrm, axis=-1, keepdims=True), z_next))
                return tuple(out)

            lax.fori_loop(0, qi + 1, down, tuple((jnp.zeros((ATT_QB, 1), F32), scores(h, qi)) for h in heads))

            def up(kj, c):
                rows = key_rows(kj)
                mask = (kj * ATT_KB + col) < (qi * ATT_QB + row)
                out = []
                for h in heads:
                    acc, carry, within = c[h]
                    sl = _head_lanes(h)
                    g = gbuf[h * nk + kj]
                    beta = bbuf[h * nk + kj]
                    within_next = _split_sum(gbuf[h * nk + jnp.minimum(kj + 1, qi)], m_prefix2)
                    big_g = within + carry
                    dz = jnp.where(mask, g * (1.0 - beta) - big_g * beta, 0.0).astype(BF16)
                    acc = acc + _dot(dz, kn[rows, sl])
                    dk_acc[rows, sl] += _dot_tn(dz, q_ts[h])
                    out.append((acc, carry + jnp.sum(g, axis=-1, keepdims=True), within_next))
                return tuple(out)

            init = tuple((jnp.zeros((ATT_QB, ATT_DH), F32), jnp.zeros((ATT_QB, 1), F32),
                          _split_sum(gbuf[h * nk], m_prefix2)) for h in heads)
            res = lax.fori_loop(0, qi + 1, up, init)
            for h in heads:
                dq_acc[pl.ds(q0, ATT_QB), _head_lanes(h)] = res[h][0]
            return 0

        lax.fori_loop(0, nq, q_loop, 0)

        def norm_bwd(xv, gain, dyn):
            r = _rstd(xv)
            xh = xv * r
            dgain = jnp.sum(dyn * xh, axis=0, keepdims=True)
            dxh = dyn * gain
            return r * (dxh - xh * jnp.mean(dxh * xh, axis=-1, keepdims=True)), dgain

        for h in heads:
            sl = _head_lanes(h)
            dq, dgq = norm_bwd(q_ref[:, sl], gq_ref[...], dq_acc[:, sl] * scale)
            dk, dgk = norm_bwd(k_ref[:, sl], gk_ref[...], dk_acc[:, sl])
            dq_ref[:, sl] = dq.astype(BF16)
            dk_ref[:, sl] = dk.astype(BF16)
            dv_ref[:, sl] = dv_acc[:, sl].astype(BF16)
            dgq_ref[...] += dgq
            dgk_ref[...] += dgk

    vec = pl.BlockSpec((1, ATT_DH), lambda s: (0, 0))
    big = jax.ShapeDtypeStruct((nb * seq, ATT_W), BF16)
    small = jax.ShapeDtypeStruct((1, ATT_DH), F32)
    width = ATT_HP * ATT_DH
    return pl.pallas_call(
        body, name=name, grid=(nb * ATT_HEADS // ATT_HP,),
        in_specs=[_head_spec(seq, OFF_Q), _head_spec(seq, OFF_K), _head_spec(seq, OFF_V), _head_spec(seq, 0), vec, vec],
        out_specs=[_head_spec(seq, 0)] * 3 + [vec, vec],
        out_shape=[big, big, big, small, small],
        scratch_shapes=[pltpu.VMEM((seq, width), BF16)] * 4 + [pltpu.VMEM((seq, width), F32)] * 3
        + [pltpu.VMEM((ATT_HP * nk, ATT_QB, ATT_KB), F32)] * 2,
        compiler_params=_params(("arbitrary",)),
    )(proj, proj, proj, do, gq, gk)


CONV_COLS = 256


def _pack_conv(conv_w, conv_b):
    return jnp.concatenate([conv_w, conv_b[None, :], jnp.zeros((3, CONV_DIM), F32)], axis=0)


def _conv_pre(raw, w8, rowi):
    pre = w8[CONV_K:CONV_K + 1, :] + raw * w8[CONV_K - 1:CONV_K, :]
    for k in range(1, CONV_K):
        sh = jnp.where(rowi >= k, pltpu.roll(raw, k, 0), 0.0)
        pre = pre + sh * w8[CONV_K - 1 - k:CONV_K - k, :]
    return pre


def _conv_fwd(proj, cw8, nb, seq, name):
    ncol = CONV_DIM // CONV_COLS

    def body(x_ref, w_ref, o_ref):
        rowi = lax.broadcasted_iota(jnp.int32, (seq, 1), 0)
        pre = _conv_pre(x_ref[...], w_ref[...], rowi)
        o_ref[...] = pre * _sigmoid(pre)

    return pl.pallas_call(
        body, name=name, grid=(nb, ncol),
        in_specs=[pl.BlockSpec((seq, CONV_COLS), lambda b, j: (b, OFF_XS // CONV_COLS + j)),
                  pl.BlockSpec((8, CONV_COLS), lambda b, j: (0, j))],
        out_specs=pl.BlockSpec((seq, CONV_COLS), lambda b, j: (b, j)),
        out_shape=jax.ShapeDtypeStruct((nb * seq, CONV_DIM), F32),
        compiler_params=_params(("parallel", "parallel")),
    )(proj, cw8)


def _conv_bwd(proj, dact, cw8, nb, seq, name):
    ncol = CONV_DIM // CONV_COLS

    def body(x_ref, d_ref, w_ref, dx_ref, dw_ref):
        @pl.when(pl.program_id(1) == 0)
        def _():
            dw_ref[...] = jnp.zeros_like(dw_ref)

        rowi = lax.broadcasted_iota(jnp.int32, (seq, 1), 0)
        raw = x_ref[...]
        w8 = w_ref[...]
        pre = _conv_pre(raw, w8, rowi)
        sg = _sigmoid(pre)
        dpre = d_ref[...] * (sg * (1.0 + pre * (1.0 - sg)))
        dw_ref[CONV_K:CONV_K + 1, :] += jnp.sum(dpre, axis=0, keepdims=True)
        dw_ref[CONV_K - 1:CONV_K, :] += jnp.sum(dpre * raw, axis=0, keepdims=True)
        draw = dpre * w8[CONV_K - 1:CONV_K, :]
        for k in range(1, CONV_K):
            sh = jnp.where(rowi >= k, pltpu.roll(raw, k, 0), 0.0)
            dw_ref[CONV_K - 1 - k:CONV_K - k, :] += jnp.sum(dpre * sh, axis=0, keepdims=True)
            up = jnp.where(rowi < seq - k, pltpu.roll(dpre, seq - k, 0), 0.0)
            draw = draw + up * w8[CONV_K - 1 - k:CONV_K - k, :]
        dx_ref[...] = draw.astype(BF16)

    return pl.pallas_call(
        body, name=name, grid=(ncol, nb),
        in_specs=[pl.BlockSpec((seq, CONV_COLS), lambda j, b: (b, OFF_XS // CONV_COLS + j)),
                  pl.BlockSpec((seq, CONV_COLS), lambda j, b: (b, j)),
                  pl.BlockSpec((8, CONV_COLS), lambda j, b: (0, j))],
        out_specs=[pl.BlockSpec((seq, CONV_COLS), lambda j, b: (b, j)),
                   pl.BlockSpec((8, CONV_COLS), lambda j, b: (0, j))],
        out_shape=[jax.ShapeDtypeStruct((nb * seq, CONV_DIM), BF16), jax.ShapeDtypeStruct((8, CONV_DIM), F32)],
        compiler_params=_params(("parallel", "arbitrary")),
    )(proj, dact, cw8)


def _pack_heads(dt_bias, a_log, d_skip):
    rows = jnp.stack([dt_bias, a_log, d_skip]).reshape(3, SSM_GROUPS, SSM_HG).transpose(1, 0, 2)
    return jnp.pad(rows, ((0, 0), (0, 8 - 3), (0, LANE - SSM_HG)))


def _split3_rows(x):
    hi = x.astype(BF16)
    r1 = x - hi.astype(F32)
    mid = r1.astype(BF16)
    lo = (r1 - mid.astype(F32)).astype(BF16)
    return jnp.concatenate([hi, mid, lo], axis=0)


def _split2_rows(x):
    hi = x.astype(BF16)
    return jnp.concatenate([hi, (x - hi.astype(F32)).astype(BF16)], axis=0)


def _ssd_specs(seq):
    gx = SSM_HG * SSM_P
    return dict(
        xs=pl.BlockSpec((seq, gx), lambda g, b: (b, g)),
        bm=pl.BlockSpec((seq, SSM_N), lambda g, b: (b, SSM_W // SSM_N + g)),
        cm=pl.BlockSpec((seq, SSM_N), lambda g, b: (b, SSM_W // SSM_N + SSM_GROUPS + g)),
        dt=pl.BlockSpec((seq, LANE), lambda g, b: (b, OFF_DT // LANE + g)),
        hp=pl.BlockSpec((1, 8, LANE), lambda g, b: (g, 0, 0)),
        head=pl.BlockSpec((seq, gx), lambda g, b: (b, g)),
        grp=pl.BlockSpec((seq, SSM_N), lambda g, b: (b, g)),
    )


SSM_GX = SSM_HG * SSM_P


def _ssd_masks():
    li = lax.broadcasted_iota(jnp.int32, (CHUNK, CHUNK), 0)
    si = lax.broadcasted_iota(jnp.int32, (CHUNK, CHUNK), 1)
    head = lax.broadcasted_iota(jnp.int32, (LANE, SSM_GX), 0)
    lane = lax.broadcasted_iota(jnp.int32, (LANE, SSM_GX), 1)
    expand = (lane // SSM_P == head).astype(BF16)
    head_t = lax.broadcasted_iota(jnp.int32, (SSM_GX, LANE), 1)
    lane_t = lax.broadcasted_iota(jnp.int32, (SSM_GX, LANE), 0)
    gather = (lane_t // SSM_P == head_t).astype(BF16)
    return dict(
        causal=li >= si, causal_t=si >= li,
        tril3=jnp.concatenate([(si <= li).astype(BF16)] * 3, axis=1),
        triu2=jnp.concatenate([(si >= li).astype(BF16)] * 2, axis=1),
        below2=jnp.concatenate([(si < li).astype(BF16)] * 2, axis=1),
        expand2=_stacked(expand), gather2=_stacked(gather))


def _per_head(x, mk):
    return _split_sum(x, mk["expand2"])


def _head_sums(x, mk):
    return _split_sum(x, mk["gather2"])


def _row8(v):
    return jnp.broadcast_to(v, (8, v.shape[1]))


def _ssd_fwd(act, proj, hp, nb, seq, name):
    nc = seq // CHUNK

    def body(xs_ref, b_ref, c_ref, dt_ref, hp_ref, y_ref, dt_s, da_s, hst):
        mk = _ssd_masks()
        hpv = hp_ref[0]
        dt = _softplus(dt_ref[...] + hpv[0:1, :])
        a = -jnp.exp(hpv[1:2, :])
        dsk_row = _per_head(_row8(hpv[2:3, :]), mk)[0:1]
        dt_s[...] = dt
        da_s[...] = dt * a
        hst[...] = jnp.zeros_like(hst)

        def chunk(c, _):
            rows = pl.ds(pl.multiple_of(c * CHUNK, CHUNK), CHUNK)
            acol = _dot(mk["tril3"], _split3_rows(da_s[rows, :]))
            arow = acol.T
            alast = acol[CHUNK - 1:CHUNK, :]
            ea = _per_head(jnp.exp(acol), mk)
            eb = _per_head(jnp.exp(alast - acol), mk)
            el = _per_head(_row8(jnp.exp(alast)), mk)[0:1]
            bb = b_ref[rows, :].astype(BF16)
            cb = c_ref[rows, :].astype(BF16)
            cbm = _dot_nt(cb, bb)
            xc = xs_ref[rows, :]
            u = xc * _per_head(dt_s[rows, :], mk)
            ub = u.astype(BF16)
            ht = hst[...]
            y_ref[rows, :] = ea * _dot(cb, ht.astype(BF16)) + dsk_row * xc
            for j in range(SSM_HG):
                sl = slice(j * SSM_P, (j + 1) * SSM_P)
                decay = jnp.where(mk["causal"], jnp.exp(jnp.minimum(acol[:, j:j + 1] - arow[j:j + 1, :], 0.0)), 0.0)
                y_ref[rows, sl] += _dot((cbm * decay).astype(BF16), ub[:, sl])
            hst[...] = el * ht + _dot_tn(bb, (u * eb).astype(BF16))
            return 0

        lax.fori_loop(0, nc, chunk, 0)

    sp = _ssd_specs(seq)
    return pl.pallas_call(
        body, name=name, grid=(SSM_GROUPS, nb),
        in_specs=[sp["xs"], sp["bm"], sp["cm"], sp["dt"], sp["hp"]],
        out_specs=sp["head"],
        out_shape=jax.ShapeDtypeStruct((nb * seq, SSM_W), F32),
        scratch_shapes=[pltpu.VMEM((seq, LANE), F32)] * 2 + [pltpu.VMEM((SSM_N, SSM_GX), F32)],
        compiler_params=_params(("parallel", "parallel")),
    )(act, act, act, proj, hp)


def _ssd_bwd(act, proj, dy, hp, nb, seq, name):
    nc = seq // CHUNK

    def body(xs_ref, b_ref, c_ref, dt_ref, hp_ref, dy_ref, dxs_ref, db_ref, dc_ref, ddt_ref, dhp_ref,
             dt_s, da_s, ddt_s, hs, lam, du_s):
        @pl.when(pl.program_id(1) == 0)
        def _():
            dhp_ref[...] = jnp.zeros_like(dhp_ref)

        mk = _ssd_masks()
        hpv = hp_ref[0]
        a = -jnp.exp(hpv[1:2, :])
        dsk_row = _per_head(_row8(hpv[2:3, :]), mk)[0:1]
        dt_s[...] = _softplus(dt_ref[...] + hpv[0:1, :])
        da_s[...] = dt_s[...] * a
        lane = lax.broadcasted_iota(jnp.int32, (1, LANE), 1)

        def chunk_rows(c):
            return pl.ds(pl.multiple_of(c * CHUNK, CHUNK), CHUNK)

        def decays(c):
            acol = _dot(mk["tril3"], _split3_rows(da_s[chunk_rows(c), :]))
            alast = acol[CHUNK - 1:CHUNK, :]
            return acol, alast

        hs[0] = jnp.zeros((SSM_N, SSM_GX), F32)

        def fwd_chunk(c, _):
            rows = chunk_rows(c)
            acol, alast = decays(c)
            eb = _per_head(jnp.exp(alast - acol), mk)
            el = _per_head(_row8(jnp.exp(alast)), mk)[0:1]
            u = xs_ref[rows, :] * _per_head(dt_s[rows, :], mk)
            hs[c + 1] = el * hs[c] + _dot_tn(b_ref[rows, :].astype(BF16), (u * eb).astype(BF16))
            return 0

        lax.fori_loop(0, nc - 1, fwd_chunk, 0)
        lam[...] = jnp.zeros_like(lam)

        def bwd_chunk(i, carry):
            dd_row, da_vec = carry
            c = nc - 1 - i
            rows = chunk_rows(c)
            acol, alast = decays(c)
            arow = acol.T
            ea = _per_head(jnp.exp(acol), mk)
            eb = _per_head(jnp.exp(alast - acol), mk)
            el = _per_head(_row8(jnp.exp(alast)), mk)[0:1]
            dt_all = _per_head(dt_s[rows, :], mk)
            bb = b_ref[rows, :].astype(BF16)
            cb = c_ref[rows, :].astype(BF16)
            cbm = _dot_nt(cb, bb)
            cbt = _dot_nt(bb, cb)
            xc = xs_ref[rows, :]
            dyc = dy_ref[rows, :]
            u = xc * dt_all
            ub = u.astype(BF16)
            dyb = dyc.astype(BF16)
            h_in = hs[c]
            lm = lam[...]
            hb = h_in.astype(BF16)
            lb = lm.astype(BF16)
            y_off = ea * _dot(cb, hb)
            du_off = eb * _dot(bb, lb)
            dye = (ea * dyc).astype(BF16)
            zero = jnp.zeros((CHUNK, CHUNK), F32)
            dcb, dcbt, d_a = zero, zero, zero
            for j in range(SSM_HG):
                sl = slice(j * SSM_P, (j + 1) * SSM_P)
                seg = acol[:, j:j + 1] - arow[j:j + 1, :]
                decay = jnp.where(mk["causal"], jnp.exp(jnp.minimum(seg, 0.0)), 0.0)
                decay_t = jnp.where(mk["causal_t"], jnp.exp(jnp.minimum(-seg, 0.0)), 0.0)
                m = cbm * decay
                mt = cbt * decay_t
                dm = _dot_nt(dyb[:, sl], ub[:, sl])
                dmt = _dot_nt(ub[:, sl], dyb[:, sl])
                dcb = dcb + dm * decay
                dcbt = dcbt + dmt * decay_t
                du_s[:, sl] = _dot(mt.astype(BF16), dyb[:, sl])
                d_a_j = jnp.sum(dm * m, axis=-1, keepdims=True) - jnp.sum(dmt * mt, axis=-1, keepdims=True)
                d_a = jnp.where(lane == j, d_a_j, d_a)
            du = du_s[...] + du_off
            dxs_ref[rows, :] = du * dt_all + dsk_row * dyc
            dc_ref[rows, :] = _dot_nt(dye, hb) + _dot(dcb.astype(BF16), bb)
            db_ref[rows, :] = _dot_nt((eb * u).astype(BF16), lb) + _dot(dcbt.astype(BF16), cb)
            lam[...] = el * lm + _dot_tn(cb, dye)
            d_a = d_a + _head_sums(dyc * y_off, mk)
            f_a = _head_sums(du_off * u, mk)
            c_a = jnp.exp(alast) * _head_sums(_row8(jnp.sum(lm * h_in, axis=0, keepdims=True)), mk)[0:1]
            dda = _dot(mk["triu2"], _split2_rows(d_a)) + _dot(mk["below2"], _split2_rows(f_a)) + c_a
            ddt_s[rows, :] = dda * a + _head_sums(du * xc, mk)
            da_vec = da_vec + jnp.sum(dda * dt_s[rows, :], axis=0, keepdims=True)
            dd_row = dd_row + jnp.sum(dyc * xc, axis=0, keepdims=True)
            return dd_row, da_vec

        init = (jnp.zeros((1, SSM_GX), F32), jnp.zeros((1, LANE), F32))
        dd_row, da_vec = lax.fori_loop(0, nc, bwd_chunk, init)
        ddt_raw = ddt_s[...] * _sigmoid(dt_ref[...] + hpv[0:1, :])
        ddt_ref[...] = ddt_raw.astype(BF16)
        dhp_ref[0, 0:1, :] += jnp.sum(ddt_raw, axis=0, keepdims=True)
        dhp_ref[0, 1:2, :] += da_vec * a
        dhp_ref[0, 2:3, :] += _head_sums(_row8(dd_row), mk)[0:1]

    sp = _ssd_specs(seq)
    t = nb * seq
    return pl.pallas_call(
        body, name=name, grid=(SSM_GROUPS, nb),
        in_specs=[sp["xs"], sp["bm"], sp["cm"], sp["dt"], sp["hp"], sp["head"]],
        out_specs=[sp["head"], sp["grp"], sp["grp"], sp["grp"], sp["hp"]],
        out_shape=[jax.ShapeDtypeStruct((t, SSM_W), F32), jax.ShapeDtypeStruct((t, SSM_GROUPS * SSM_N), F32),
                   jax.ShapeDtypeStruct((t, SSM_GROUPS * SSM_N), F32),
                   jax.ShapeDtypeStruct((t, SSM_GROUPS * LANE), BF16),
                   jax.ShapeDtypeStruct((SSM_GROUPS, 8, LANE), F32)],
        scratch_shapes=[pltpu.VMEM((seq, LANE), F32)] * 3
        + [pltpu.VMEM((nc, SSM_N, SSM_GX), F32), pltpu.VMEM((SSM_N, SSM_GX), F32), pltpu.VMEM((CHUNK, SSM_GX), F32)],
        compiler_params=_params(("parallel", "arbitrary")),
    )(act, act, act, proj, hp, dy)


ANY = pl.BlockSpec(memory_space=pl.ANY)


def _block_index(p):
    return 4 * p[0] + 2 * p[1] + p[2]


def _all_gather(shards, name):
    n = len(shards)

    def body(*refs):
        ins, outs = refs[:n], refs[n:2 * n]
        send_sems, recv_sems, local_sems = refs[2 * n:]
        x, y, c = lax.axis_index("x"), lax.axis_index("y"), lax.axis_index("c")
        me, sibling = (x, y, c), (x, y, 1 - c)
        chips = [(1 - x, y), (x, 1 - y), (1 - x, 1 - y)]

        def copy(i, k, block, to, src=None):
            dst = outs[i].at[_block_index(block)]
            return pltpu.make_async_remote_copy(
                src_ref=dst if src is None else src, dst_ref=dst,
                send_sem=send_sems.at[i, k], recv_sem=recv_sems.at[i, k],
                device_id=to, device_id_type=MESH)

        mine = [pltpu.make_async_copy(ins[i], outs[i].at[_block_index(me)], local_sems.at[i]) for i in range(n)]
        for cp in mine:
            cp.start()
        first = []
        for i in range(n):
            first.append(copy(i, 0, me, sibling, src=ins[i]))
            first += [copy(i, 1 + j, me, (*chip, c), src=ins[i]) for j, chip in enumerate(chips)]
        for cp in first:
            cp.start()
        passed = []
        for j, chip in enumerate(chips):
            for i in range(n):
                copy(i, 1 + j, (*chip, c), me).wait_recv()
                fwd = copy(i, 4 + j, (*chip, c), sibling)
                fwd.start()
                passed.append(fwd)
        for i in range(n):
            copy(i, 0, sibling, me).wait_recv()
            for j, chip in enumerate(chips):
                copy(i, 4 + j, (*chip, 1 - c), me).wait_recv()
        for cp in first + passed:
            cp.wait_send()
        for cp in mine:
            cp.wait()

    return pl.pallas_call(
        body, name=name,
        in_specs=[ANY] * n, out_specs=[ANY] * n,
        out_shape=[jax.ShapeDtypeStruct((N_DEV,) + s.shape, s.dtype) for s in shards],
        scratch_shapes=[pltpu.SemaphoreType.DMA((n, 7)), pltpu.SemaphoreType.DMA((n, 7)),
                        pltpu.SemaphoreType.DMA((n,))],
    )(*shards)


HBM = pl.BlockSpec(memory_space=pltpu.HBM)
SEM = pl.BlockSpec(memory_space=pltpu.SEMAPHORE)
EFFECT = pltpu.SideEffectType.DATAFLOW_SIDE_EFFECTING


def _my_block():
    return _block_index((lax.axis_index("x"), lax.axis_index("y"), lax.axis_index("c")))


def _peer(k):
    x, y, c = lax.axis_index("x"), lax.axis_index("y"), lax.axis_index("c")
    return (1 - x if k & 4 else x, 1 - y if k & 2 else y, 1 - c if k & 1 else c)


def _plan_copies(plan, src_refs, land_refs, send_sems, recv_sems):
    me = _my_block()
    copies = []
    for e, (si, di, src_view, dst_view, _) in enumerate(plan):
        for k in range(1, N_DEV):
            copies.append(pltpu.make_async_remote_copy(
                src_ref=src_view(src_refs[si], _block_index(_peer(k))),
                dst_ref=dst_view(land_refs[di], me),
                send_sem=send_sems[e], recv_sem=recv_sems[e],
                device_id=_peer(k), device_id_type=MESH))
    return copies


def _plan_waits(plan, land_refs, send_sems, recv_sems):
    waits = []
    for e, (_, di, _, _, seven) in enumerate(plan):
        view = seven(land_refs[di])
        waits.append(pltpu.make_async_remote_copy(
            src_ref=view, dst_ref=view, send_sem=send_sems[e], recv_sem=recv_sems[e],
            device_id=_peer(1), device_id_type=MESH))
    return waits


def _plan_own(plan, src_refs, land_refs, own_sems):
    me = _my_block()
    return [pltpu.make_async_copy(src_view(src_refs[si], me), dst_view(land_refs[di], me), own_sems[e])
            for e, (si, di, src_view, dst_view, _) in enumerate(plan)]


def _copies_start(srcs, lands, plan, name, after=None):
    ns, nl, ne = len(srcs), len(lands), len(plan)
    extra = [] if after is None else [after]

    nin = ns + nl + len(extra)

    def body(*refs):
        src_refs, land_refs = refs[:ns], refs[ns:ns + nl]
        send_sems, recv_sems = refs[nin:nin + ne], refs[nin + ne:nin + 2 * ne]
        own_sems = refs[nin + 2 * ne:nin + 3 * ne]
        token = refs[-1]
        for cp in _plan_copies(plan, src_refs, land_refs, send_sems, recv_sems):
            cp.start()
        for cp in _plan_own(plan, src_refs, land_refs, own_sems):
            cp.start()
        token[...] = jnp.zeros_like(token)

    thru = [pltpu.HBM(a.shape, a.dtype) for a in list(srcs) + list(lands)]
    res = pl.pallas_call(
        body, name=name,
        in_specs=[HBM] * (ns + nl) + [ANY] * len(extra),
        out_specs=[SEM] * (3 * ne) + [HBM] * (ns + nl) + [pl.BlockSpec(memory_space=pltpu.VMEM)],
        out_shape=[pltpu.SemaphoreType.DMA(())] * (3 * ne) + thru + [jax.ShapeDtypeStruct((8, LANE), F32)],
        input_output_aliases={i: 3 * ne + i for i in range(ns + nl)},
        compiler_params=pltpu.CompilerParams(has_side_effects=EFFECT),
    )(*[pltpu.with_memory_space_constraint(a, pltpu.HBM) for a in list(srcs) + list(lands)], *extra)
    return dict(sems=res[:3 * ne], srcs=res[3 * ne:3 * ne + ns], lands=res[3 * ne + ns:3 * ne + ns + nl],
                token=res[-1], plan=plan)


def _copies_wait(flight, after, name):
    srcs, lands, plan = flight["srcs"], flight["lands"], flight["plan"]
    ns, nl, ne = len(srcs), len(lands), len(plan)

    def body(*refs):
        src_refs, land_refs = refs[:ns], refs[ns:ns + nl]
        send_sems, recv_sems = refs[ns + nl:ns + nl + ne], refs[ns + nl + ne:ns + nl + 2 * ne]
        own_sems = refs[ns + nl + 2 * ne:ns + nl + 3 * ne]
        for cp in _plan_waits(plan, land_refs, send_sems, recv_sems):
            cp.wait_send()
            cp.wait_recv()
        for cp in _plan_own(plan, src_refs, land_refs, own_sems):
            cp.wait()

    thru = [pltpu.HBM(a.shape, a.dtype) for a in list(srcs) + list(lands)]
    res = pl.pallas_call(
        body, name=name,
        in_specs=[HBM] * (ns + nl) + [SEM] * (3 * ne) + [ANY],
        out_specs=[HBM] * (ns + nl),
        out_shape=thru,
        input_output_aliases={i: i for i in range(ns + nl)},
        compiler_params=pltpu.CompilerParams(has_side_effects=EFFECT),
    )(*srcs, *lands, *flight["sems"], after)
    return list(res[ns:])


def _adamw_math(w, g, m, v):
    m = ADAM_B1 * m + (1.0 - ADAM_B1) * g
    v = ADAM_B2 * v + (1.0 - ADAM_B2) * (g * g)
    m_hat = m / (1.0 - ADAM_B1 ** ADAM_STEP)
    v_hat = v / (1.0 - ADAM_B2 ** ADAM_STEP)
    delta = -ADAM_LR * (m_hat / (jnp.sqrt(v_hat) + ADAM_EPS) + ADAM_WD * w)
    return delta, m, v


def _adamw(parts, w, m, v, name, rows):
    depth, r, c = w.shape
    cp = parts[0].shape[2]
    assert r % rows == 0 and len(parts) == depth

    def body(*refs):
        p_refs = refs[:depth]
        w_ref, m_ref, v_ref, g_ref, d_ref, mo_ref, vo_ref = refs[depth:]
        for li in range(depth):
            @pl.when(pl.program_id(0) == li)
            def _(li=li):
                g = p_refs[li][0][:, :c].astype(F32)
                for j in range(1, N_DEV):
                    g = g + p_refs[li][j][:, :c].astype(F32)
                d, mn, vn = _adamw_math(w_ref[...], g, m_ref[...], v_ref[...])
                g_ref[...] = g
                d_ref[...] = d
                mo_ref[...] = mn
                vo_ref[...] = vn

    def part_spec(li):
        return pl.BlockSpec((N_DEV, rows, cp), lambda l, i: (0, jnp.where(l == li, i, 0), 0))

    blk = pl.BlockSpec((None, rows, c), lambda l, i: (l, i, 0))
    out = jax.ShapeDtypeStruct((depth, r, c), F32)
    return pl.pallas_call(
        body, name=name, grid=(depth, r // rows),
        in_specs=[part_spec(li) for li in range(depth)] + [blk, blk, blk],
        out_specs=[blk] * 4, out_shape=[out] * 4,
        compiler_params=_params(("arbitrary", "arbitrary")),
    )(*parts, w, m, v)


def _sum_parts(parts, name):
    _, r, c = parts.shape

    def body(p_ref, o_ref):
        g = p_ref[0]
        for j in range(1, N_DEV):
            g = g + p_ref[j]
        o_ref[...] = g

    return pl.pallas_call(
        body, name=name, out_shape=jax.ShapeDtypeStruct((r, c), F32),
        compiler_params=_params(),
    )(parts)


def _adamw_small(g, w, m, v, name):
    def body(g_ref, w_ref, m_ref, v_ref, d_ref, mo_ref, vo_ref):
        d, mn, vn = _adamw_math(w_ref[...], g_ref[...], m_ref[...], v_ref[...])
        d_ref[...] = d
        mo_ref[...] = mn
        vo_ref[...] = vn

    out = jax.ShapeDtypeStruct(w.shape, F32)
    return pl.pallas_call(body, name=name, out_shape=[out] * 3, compiler_params=_params())(g, w, m, v)


SMALL = ("norm_mix", "q_gain", "k_gain", "conv_b", "dt_bias", "a_log", "d_skip", "attn_out_gain",
         "ssm_out_gain", "norm_ffn")


def _pad_lanes(a):
    n = a.shape[-1]
    return jnp.pad(a, ((0, 0), (0, -n % LANE)))


def _pack_small(d):
    return jnp.concatenate([_pad_lanes(d[k]) for k in SMALL], axis=1)


def _unpack_small(packed, like):
    out, off = {}, 0
    for k in SMALL:
        n = like[k].shape[-1]
        out[k] = packed[:, off:off + n]
        off += n + (-n % LANE)
    return out


def _full_cols(gathered):
    _, r, c = gathered.shape
    return gathered.transpose(1, 0, 2).reshape(r, N_DEV * c)


def _col_blocks(full):
    r, c8 = full.shape
    return full.reshape(r, N_DEV, c8 // N_DEV).transpose(1, 0, 2)


FF_BLK = 768
FF_PAD = N_DEV * FF_BLK


def _whole(ref, block):
    return ref


def _rows_of(size):
    return lambda ref, block: ref.at[pl.ds(pl.multiple_of(block * size, size), size), :]


def _slot(ref, block):
    return ref.at[block]


def _seven_slots(ref):
    return ref.at[pl.ds(0, N_DEV - 1)]


def _seven_rows(size):
    return lambda ref: ref.at[pl.ds(0, (N_DEV - 1) * size), :]


GATHER_A = [(0, 0, _whole, _slot, _seven_slots), (1, 1, _whole, _rows_of(256), _seven_rows(256))]
GATHER_B = [(i, i, _whole, _rows_of(FF_BLK), _seven_rows(FF_BLK)) for i in range(3)]
SCATTER_A = [(0, 0, _slot, _slot, _seven_slots), (1, 1, _rows_of(256), _slot, _seven_slots)]
SCATTER_B = [(i, i, _rows_of(FF_BLK), _slot, _seven_slots) for i in range(3)]


def _gather_lands(which, shards, d):
    if which == "a":
        return [lax.empty((N_DEV,) + shards[0].shape, BF16), lax.empty((d, d), BF16)]
    return [lax.empty((FF_PAD, d), BF16) for _ in range(3)]


def _scatter_lands(which, grads):
    if which == "a":
        g_in, g_out = grads
        return [lax.empty(g_in.shape, BF16), lax.empty((N_DEV, g_out.shape[0] // N_DEV, g_out.shape[1]), BF16)]
    return [lax.empty((N_DEV, FF_BLK, g.shape[1]), BF16) for g in grads]


def _pad_w_in(full):
    d = full.shape[0]
    z = jnp.zeros((d, LANE - SSM_HG), full.dtype)
    tail = jnp.zeros((d, NPROJ - OFF_DT - SSM_GROUPS * LANE), full.dtype)
    return jnp.concatenate([full[:, :OFF_DT], full[:, OFF_DT:OFF_DT + SSM_HG], z,
                            full[:, OFF_DT + SSM_HG:IN_DIM], z, tail], axis=1)


def _unpad_w_in(padded):
    return jnp.concatenate([padded[:, :OFF_DT], padded[:, OFF_DT:OFF_DT + SSM_HG],
                            padded[:, OFF_DT + LANE:OFF_DT + LANE + SSM_HG]], axis=1)


def kernel(x, norm_mix, w_in, q_gain, k_gain, conv_w, conv_b, dt_bias, a_log, d_skip, attn_out_gain, ssm_out_gain, w_out, norm_ffn, w_gate, w_up, w_down, loss_target, m_norm_mix, m_w_in, m_q_gain, m_k_gain, m_conv_w, m_conv_b, m_dt_bias, m_a_log, m_d_skip, m_attn_out_gain, m_ssm_out_gain, m_w_out, m_norm_ffn, m_w_gate, m_w_up, m_w_down, v_norm_mix, v_w_in, v_q_gain, v_k_gain, v_conv_w, v_conv_b, v_dt_bias, v_a_log, v_d_skip, v_attn_out_gain, v_ssm_out_gain, v_w_out, v_norm_ffn, v_w_gate, v_w_up, v_w_down):
    nb, seq, d = x.shape
    depth = w_in.shape[0]
    t = nb * seq
    w = dict(norm_mix=norm_mix, w_in=w_in, q_gain=q_gain, k_gain=k_gain, conv_w=conv_w, conv_b=conv_b,
             dt_bias=dt_bias, a_log=a_log, d_skip=d_skip, attn_out_gain=attn_out_gain, ssm_out_gain=ssm_out_gain,
             w_out=w_out, norm_ffn=norm_ffn, w_gate=w_gate, w_up=w_up, w_down=w_down)
    mom = dict(norm_mix=m_norm_mix, w_in=m_w_in, q_gain=m_q_gain, k_gain=m_k_gain, conv_w=m_conv_w, conv_b=m_conv_b,
               dt_bias=m_dt_bias, a_log=m_a_log, d_skip=m_d_skip, attn_out_gain=m_attn_out_gain,
               ssm_out_gain=m_ssm_out_gain, w_out=m_w_out, norm_ffn=m_norm_ffn, w_gate=m_w_gate, w_up=m_w_up,
               w_down=m_w_down)
    var = dict(norm_mix=v_norm_mix, w_in=v_w_in, q_gain=v_q_gain, k_gain=v_k_gain, conv_w=v_conv_w, conv_b=v_conv_b,
               dt_bias=v_dt_bias, a_log=v_a_log, d_skip=v_d_skip, attn_out_gain=v_attn_out_gain,
               ssm_out_gain=v_ssm_out_gain, w_out=v_w_out, norm_ffn=v_norm_ffn, w_gate=v_w_gate, w_up=v_w_up,
               w_down=v_w_down)
    ff = w_gate.shape[2]

    (conv_all,) = _all_gather([conv_w], "gather_conv")

    def shards_a(li):
        return [w_in[li].astype(BF16), w_out[li].astype(BF16)]

    w["w_gate"], mom["w_gate"], var["w_gate"] = (jnp.swapaxes(a, 1, 2) for a in (w_gate, m_w_gate, v_w_gate))
    w["w_up"], mom["w_up"], var["w_up"] = (jnp.swapaxes(a, 1, 2) for a in (w_up, m_w_up, v_w_up))

    def shards_b(li):
        return [jnp.pad(w[k][li].astype(BF16), ((0, FF_BLK - ff), (0, 0))) for k in ("w_gate", "w_up", "w_down")]

    def small_params(li):
        p = {k: w[k][li][None, :] for k in ("norm_mix", "q_gain", "k_gain", "attn_out_gain", "ssm_out_gain", "norm_ffn")}
        conv_full = conv_all[:, li].transpose(1, 0, 2).reshape(CONV_K, CONV_DIM)
        p["cw8"] = _pack_conv(conv_full, conv_b[li])
        p["hp"] = _pack_heads(dt_bias[li], a_log[li], d_skip[li])
        return p

    xc = x.reshape(t, d)
    cur = shards_a(0)
    flight = _copies_start(cur, _gather_lands("a", cur, d), GATHER_A, "gather_a0")
    lands_a = _copies_wait(flight, xc, "gather_a0_wait")
    layers, saved = [], []
    for li in range(depth):
        tag = f"l{li}_"
        p = small_params(li)
        p["w_in"] = _pad_w_in(_full_cols(lands_a[0]))
        p["w_out"] = lands_a[1]
        cur = shards_b(li)
        flight = _copies_start(cur, _gather_lands("b", cur, d), GATHER_B, tag + "gather_b", after=lands_a[1])
        h1 = _rms_fwd(xc, p["norm_mix"], tag + "rms1", after=flight["token"])
        proj = _matmul(h1, p["w_in"], "nn", F32, tag + "mm_in")
        o = _attn_fwd(proj, p["q_gain"], p["k_gain"], nb, seq, tag + "attn")
        act = _conv_fwd(proj, p["cw8"], nb, seq, tag + "conv")
        y = _ssd_fwd(act, proj, p["hp"], nb, seq, tag + "ssd")
        cat = _mix_fwd(o, y, proj, p["attn_out_gain"], p["ssm_out_gain"], tag + "mix")
        x1 = _matmul(cat, p["w_out"], "nn", F32, tag + "mm_out", residual=xc)
        p["w_gate"], p["w_up"], p["w_down"] = _copies_wait(flight, x1, tag + "gather_b_wait")
        token = None
        if li + 1 < depth:
            nxt = shards_a(li + 1)
            flight = _copies_start(nxt, _gather_lands("a", nxt, d), GATHER_A, f"gather_a{li + 1}",
                                   after=p["w_down"])
            token = flight["token"]
        h2 = _rms_fwd(x1, p["norm_ffn"], tag + "rms2", after=token)
        a, gate, up = _mm_swiglu(h2, p["w_gate"], p["w_up"], tag + "mm_gu")
        x2 = _matmul(a, p["w_down"], "nn", F32, tag + "mm_down", residual=x1)
        if li + 1 < depth:
            lands_a = _copies_wait(flight, x2, f"gather_a{li + 1}_wait")
        saved.append(dict(x=xc, h1=h1, proj=proj, o=o, act=act, y=y, cat=cat, x1=x1, h2=h2, gate=gate, up=up, a=a))
        layers.append(p)
        xc = x2

    loss_blk, dx, dxb = _loss_fwd_bwd(xc, loss_target.reshape(t, d), "loss")
    loss = lax.psum(loss_blk[0, 0], ("x", "y", "c"))

    grads = [dict() for _ in range(depth)]
    recv = [dict() for _ in range(depth)]
    flight_a, token = None, None
    for li in reversed(range(depth)):
        tag = f"l{li}_b_"
        p, s, g = layers[li], saved[li], grads[li]
        dgate, dup = _mm_dact_swiglu(dxb, p["w_down"], s["gate"], s["up"], tag + "mm_dact", after=token)
        g_down = _matmul(s["a"], dxb, "tn", BF16, tag + "mm_dwd")
        dh2 = _matmul(dgate, p["w_gate"], "nn", F32, tag + "mm_dh2g")
        dh2 = _matmul(dup, p["w_up"], "nn", F32, tag + "mm_dh2u", residual=dh2)
        g_gate = _matmul(dgate, s["h2"], "tn", BF16, tag + "mm_dwg")
        g_up = _matmul(dup, s["h2"], "tn", BF16, tag + "mm_dwu")
        if flight_a is not None:
            recv[li + 1]["w_in"], recv[li + 1]["w_out"] = _copies_wait(flight_a, g_up, f"l{li + 1}_b_scatter_a_wait")
        grads_b = [g_gate, g_up, g_down]
        flight_b = _copies_start(grads_b, _scatter_lands("b", grads_b), SCATTER_B,
                                 tag + "scatter_b", after=recv[li + 1]["w_out"] if li + 1 < depth else None)
        dx1, dx1b, g["norm_ffn"] = _rms_bwd(s["x1"], p["norm_ffn"], dh2, dx, tag + "rms2")
        dcat = _matmul(dx1b, p["w_out"], "nt", F32, tag + "mm_dcat", after=flight_b["token"])
        g_out = _matmul(s["cat"], dx1b, "tn", BF16, tag + "mm_dwo")
        do, dy, dz, g["attn_out_gain"], g["ssm_out_gain"] = _mix_bwd(
            dcat, s["o"], s["y"], s["proj"], p["attn_out_gain"], p["ssm_out_gain"], tag + "mix")
        dq, dk, dv, g["q_gain"], g["k_gain"] = _attn_bwd(s["proj"], do, p["q_gain"], p["k_gain"], nb, seq, tag + "attn")
        dxa, dba, dca, ddt, dhp = _ssd_bwd(s["act"], s["proj"], dy, p["hp"], nb, seq, tag + "ssd")
        dxbc, dcw8 = _conv_bwd(s["proj"], jnp.concatenate([dxa, dba, dca], axis=1), p["cw8"], nb, seq, tag + "conv")
        g["conv_w"] = dcw8[0:CONV_K]
        g["conv_b"] = dcw8[CONV_K:CONV_K + 1]
        heads = dhp[:, 0:3, 0:SSM_HG].transpose(1, 0, 2).reshape(3, SSM_HEADS)
        g["dt_bias"], g["a_log"], g["d_skip"] = heads[0:1], heads[1:2], heads[2:3]
        tail = jnp.zeros((t, NPROJ - OFF_DT - SSM_GROUPS * LANE), BF16)
        dproj = jnp.concatenate([dq, dk, dv, dz, dxbc, ddt, tail], axis=1)
        recv[li]["w_gate"], recv[li]["w_up"], recv[li]["w_down"] = _copies_wait(flight_b, dproj, tag + "scatter_b_wait")
        dh1 = _matmul(dproj, p["w_in"], "nt", F32, tag + "mm_dh1")
        g_in = _col_blocks(_unpad_w_in(_matmul(s["h1"], dproj, "tn", BF16, tag + "mm_dwin")))
        flight_a = _copies_start([g_in, g_out], _scatter_lands("a", [g_in, g_out]), SCATTER_A, tag + "scatter_a")
        token = flight_a["token"]
        dx, dxb, g["norm_mix"] = _rms_bwd(s["x"], p["norm_mix"], dh1, dx1, tag + "rms1")
    grad_x = dx.reshape(nb, seq, d)

    out_g, out_d, out_m, out_v = {}, {}, {}, {}

    def update(k, rows):
        parts = [recv[li][k] for li in range(depth)]
        out_g[k], out_d[k], out_m[k], out_v[k] = _adamw(parts, w[k], mom[k], var[k], "adamw_" + k, rows)

    update("w_gate", 64)
    update("w_up", 64)
    update("w_down", 64)
    recv[0]["w_in"], recv[0]["w_out"] = _copies_wait(flight_a, out_g["w_down"], "l0_b_scatter_a_wait")
    update("w_in", 128)
    update("w_out", 128)
    for res in (out_g, out_d, out_m, out_v):
        res["w_gate"], res["w_up"] = jnp.swapaxes(res["w_gate"], 1, 2), jnp.swapaxes(res["w_up"], 1, 2)

    small_g = {k: jnp.concatenate([grads[li][k] for li in range(depth)], axis=0) for k in SMALL}
    conv_g = jnp.stack([grads[li]["conv_w"] for li in range(depth)]).reshape(depth, CONV_K * CONV_DIM)
    packed_g = jnp.concatenate([_pack_small(small_g), conv_g], axis=1)
    (parts_small,) = _all_gather([packed_g], "gather_small_grads")
    g_small = _sum_parts(parts_small, "sum_small_grads")
    n_small = g_small.shape[1] - CONV_K * CONV_DIM
    conv_total = g_small[:, n_small:].reshape(depth, CONV_K, CONV_DIM)
    me = _block_index((lax.axis_index("x"), lax.axis_index("y"), lax.axis_index("c")))
    cshard = conv_w.shape[2]
    conv_mine = lax.dynamic_slice_in_dim(conv_total, me * cshard, cshard, axis=2)

    def with_conv(small_packed, conv_part):
        return jnp.concatenate([small_packed, conv_part.reshape(depth, CONV_K * cshard)], axis=1)

    res = _adamw_small(with_conv(g_small[:, :n_small], conv_mine),
                       with_conv(_pack_small(w), conv_w), with_conv(_pack_small(mom), m_conv_w),
                       with_conv(_pack_small(var), v_conv_w), "adamw_small")
    g_all = with_conv(g_small[:, :n_small], conv_mine)
    for dst, packed in zip((out_g, out_d, out_m, out_v), (g_all,) + tuple(res)):
        dst.update(_unpack_small(packed[:, :n_small], w))
        dst["conv_w"] = packed[:, n_small:].reshape(depth, CONV_K, cshard)

    names = ("norm_mix", "w_in", "q_gain", "k_gain", "conv_w", "conv_b", "dt_bias", "a_log", "d_skip",
             "attn_out_gain", "ssm_out_gain", "w_out", "norm_ffn", "w_gate", "w_up", "w_down")
    return (loss, grad_x, *[out_g[k] for k in names], *[out_d[k] for k in names],
            *[out_m[k] for k in names], *[out_v[k] for k in names])
```

```python
import functools
import math

import jax
import jax.numpy as jnp
from jax import lax
from jax.experimental import pallas as pl
from jax.experimental.pallas import tpu as pltpu

F32 = jnp.float32
BF16 = jnp.bfloat16
MESH = pl.DeviceIdType.MESH

N_DEV = 8
EPS = 1e-6
ATT_HEADS = 8
ATT_DH = 128
ATT_W = ATT_HEADS * ATT_DH
SSM_W = 1024
SSM_P = 64
SSM_N = 128
SSM_GROUPS = 2
SSM_HG = 8
SSM_HEADS = SSM_GROUPS * SSM_HG
CHUNK = 128
CONV_K = 4
CONV_DIM = SSM_W + 2 * SSM_GROUPS * SSM_N
LANE = 128
OFF_Q, OFF_K, OFF_V, OFF_Z, OFF_XS = 0, ATT_W, 2 * ATT_W, 3 * ATT_W, 4 * ATT_W
OFF_B = OFF_XS + SSM_W
OFF_C = OFF_B + SSM_GROUPS * SSM_N
OFF_DT = OFF_C + SSM_GROUPS * SSM_N
NPROJ = 6144
IN_DIM = OFF_DT + SSM_HEADS

ADAM_LR = 0.001
ADAM_B1 = 0.9
ADAM_B2 = 0.999
ADAM_EPS = 1e-08
ADAM_WD = 0.01
ADAM_STEP = 10

VMEM_LIMIT = 56 * 1024 * 1024
MATMUL_OPERAND_BYTES = 26 * 1024 * 1024


def _params(sem=None):
    return pltpu.CompilerParams(dimension_semantics=sem, vmem_limit_bytes=VMEM_LIMIT)


def _pick(dim, target):
    if dim <= target:
        return dim
    best = None
    for t in range(LANE, target + 1, LANE):
        if dim % t == 0:
            best = t
    assert best is not None, (dim, target)
    return best


def _dot(a, b, dims=((1,), (0,))):
    return lax.dot_general(a, b, (dims, ((), ())), preferred_element_type=F32)


def _dot_nt(a, b):
    return _dot(a, b, ((1,), (1,)))


def _dot_tn(a, b):
    return _dot(a, b, ((0,), (0,)))


def _sigmoid(x):
    return 1.0 / (1.0 + jnp.exp(-x))


def _softplus(x):
    return jnp.maximum(x, 0.0) + jnp.log(1.0 + jnp.exp(-jnp.abs(x)))


def _rstd(x):
    return lax.rsqrt(jnp.mean(x * x, axis=-1, keepdims=True) + EPS)


def _matmul(a, b, mode, out_dtype, name, residual=None, after=None, tm=512, tn=1024, tk=2048):
    if mode == "nn":
        (m, k), (k2, n) = a.shape, b.shape
    elif mode == "nt":
        (m, k), (n, k2) = a.shape, b.shape
    else:
        (k, m), (k2, n) = a.shape, b.shape
    assert k == k2, (a.shape, b.shape, mode)
    tm, tn, tk = _pick(m, tm), _pick(n, tn), _pick(k, tk)
    for cand_tn in (tn, _pick(n, tn // 2)):
        if 2 * 2 * (tm * k + k * cand_tn) <= MATMUL_OPERAND_BYTES:
            tn, tk = cand_tn, k
            break
    nk = k // tk
    dims = {"nn": ((1,), (0,)), "nt": ((1,), (1,)), "tn": ((0,), (0,))}[mode]
    has_res = residual is not None

    has_tok = after is not None

    def body(*refs):
        a_ref, b_ref = refs[:2]
        r_ref = refs[2] if has_res else None
        o_ref = refs[2 + has_res + has_tok]
        prod = _dot(a_ref[...], b_ref[...], dims)

        def finish(r):
            if r_ref is not None:
                r = r + r_ref[...]
            o_ref[...] = r.astype(o_ref.dtype)

        if nk == 1:
            finish(prod)
        else:
            acc = refs[-1]
            kk = pl.program_id(2)

            @pl.when(kk == 0)
            def _():
                acc[...] = prod

            @pl.when(kk > 0)
            def _():
                acc[...] += prod

            @pl.when(kk == nk - 1)
            def _():
                finish(acc[...])

    if mode == "tn":
        a_spec = pl.BlockSpec((tk, tm), lambda i, j, kk: (kk, i))
    else:
        a_spec = pl.BlockSpec((tm, tk), lambda i, j, kk: (i, kk))
    if mode == "nt":
        b_spec = pl.BlockSpec((tn, tk), lambda i, j, kk: (j, kk))
    else:
        b_spec = pl.BlockSpec((tk, tn), lambda i, j, kk: (kk, j))
    o_spec = pl.BlockSpec((tm, tn), lambda i, j, kk: (i, j))
    tok_spec = pl.BlockSpec((8, LANE), lambda i, j, kk: (0, 0))
    in_specs = [a_spec, b_spec] + ([o_spec] if has_res else []) + ([tok_spec] if has_tok else [])
    args = (a, b) + ((residual,) if has_res else ()) + ((after,) if has_tok else ())
    return pl.pallas_call(
        body,
        name=name,
        grid=(m // tm, n // tn, nk),
        in_specs=in_specs,
        out_specs=o_spec,
        out_shape=jax.ShapeDtypeStruct((m, n), out_dtype),
        scratch_shapes=[pltpu.VMEM((tm, tn), F32)] if nk > 1 else [],
        compiler_params=_params(("parallel", "parallel", "arbitrary")),
    )(*args)


def _mm_swiglu(h, wg_t, wu_t, name, tm=512, tn=1024):
    m, k = h.shape
    n = wg_t.shape[0]
    tm, tn = _pick(m, tm), _pick(n, tn)

    def body(h_ref, g_ref, u_ref, a_ref, gs_ref, us_ref):
        hv = h_ref[...]
        g = _dot_nt(hv, g_ref[...])
        u = _dot_nt(hv, u_ref[...])
        a_ref[...] = (g * _sigmoid(g) * u).astype(BF16)
        gs_ref[...] = g.astype(BF16)
        us_ref[...] = u.astype(BF16)

    w_spec = pl.BlockSpec((tn, k), lambda i, j: (j, 0))
    o_spec = pl.BlockSpec((tm, tn), lambda i, j: (i, j))
    out = jax.ShapeDtypeStruct((m, n), BF16)
    return pl.pallas_call(
        body, name=name, grid=(m // tm, n // tn),
        in_specs=[pl.BlockSpec((tm, k), lambda i, j: (i, 0)), w_spec, w_spec],
        out_specs=[o_spec] * 3, out_shape=[out] * 3,
        compiler_params=_params(("parallel", "parallel")),
    )(h, wg_t, wu_t)


def _mm_dact_swiglu(dx, wd, gs, us, name, after=None, tm=512, tn=1024):
    m, k = dx.shape
    n = wd.shape[0]
    tm, tn = _pick(m, tm), _pick(n, tn)
    has_tok = after is not None

    def body(*refs):
        dx_ref, wd_ref, g_ref, u_ref = refs[:4]
        dg_ref, du_ref = refs[-2:]
        dact = _dot_nt(dx_ref[...], wd_ref[...])
        g = g_ref[...].astype(F32)
        sg = _sigmoid(g)
        dg_ref[...] = (dact * u_ref[...].astype(F32) * sg * (1.0 + g * (1.0 - sg))).astype(BF16)
        du_ref[...] = (dact * g * sg).astype(BF16)

    o_spec = pl.BlockSpec((tm, tn), lambda i, j: (i, j))
    tok = [pl.BlockSpec((8, LANE), lambda i, j: (0, 0))] if has_tok else []
    out = jax.ShapeDtypeStruct((m, n), BF16)
    return pl.pallas_call(
        body, name=name, grid=(m // tm, n // tn),
        in_specs=[pl.BlockSpec((tm, k), lambda i, j: (i, 0)), pl.BlockSpec((tn, k), lambda i, j: (j, 0)),
                  o_spec, o_spec] + tok,
        out_specs=[o_spec] * 2, out_shape=[out] * 2,
        compiler_params=_params(("parallel", "parallel")),
    )(dx, wd, gs, us, *((after,) if has_tok else ()))


ROWS = 512


def _rms_fwd(x, g, name, after=None):
    t, d = x.shape
    has_tok = after is not None

    def body(*refs):
        x_ref, g_ref, o_ref = refs[0], refs[1], refs[-1]
        xv = x_ref[...]
        o_ref[...] = (xv * _rstd(xv) * g_ref[...]).astype(BF16)

    row = pl.BlockSpec((ROWS, d), lambda i: (i, 0))
    tok = [pl.BlockSpec((8, LANE), lambda i: (0, 0))] if has_tok else []
    return pl.pallas_call(
        body, name=name, grid=(t // ROWS,),
        in_specs=[row, pl.BlockSpec((1, d), lambda i: (0, 0))] + tok,
        out_specs=row, out_shape=jax.ShapeDtypeStruct((t, d), BF16),
        compiler_params=_params(("parallel",)),
    )(x, g, *((after,) if has_tok else ()))


def _rms_bwd(x, g, dh, dres, name):
    t, d = x.shape

    def body(x_ref, g_ref, dh_ref, dr_ref, dx_ref, dxb_ref, dg_ref):
        xv = x_ref[...]
        r = _rstd(xv)
        xh = xv * r
        dhv = dh_ref[...]

        @pl.when(pl.program_id(0) == 0)
        def _():
            dg_ref[...] = jnp.zeros_like(dg_ref)

        dg_ref[...] += jnp.sum(dhv * xh, axis=0, keepdims=True)
        dxh = dhv * g_ref[...]
        dx = r * (dxh - xh * jnp.mean(dxh * xh, axis=-1, keepdims=True)) + dr_ref[...]
        dx_ref[...] = dx
        dxb_ref[...] = dx.astype(BF16)

    row = pl.BlockSpec((ROWS // 2, d), lambda i: (i, 0))
    vec = pl.BlockSpec((1, d), lambda i: (0, 0))
    return pl.pallas_call(
        body, name=name, grid=(t // (ROWS // 2),),
        in_specs=[row, vec, row, row],
        out_specs=[row, row, vec],
        out_shape=[jax.ShapeDtypeStruct((t, d), F32), jax.ShapeDtypeStruct((t, d), BF16),
                   jax.ShapeDtypeStruct((1, d), F32)],
        compiler_params=_params(("arbitrary",)),
    )(x, g, dh, dres)


def _loss_fwd_bwd(y, target, name):
    t, d = y.shape
    inv = 1.0 / d

    def body(y_ref, t_ref, l_ref, dy_ref, dyb_ref):
        e = y_ref[...] - t_ref[...]

        @pl.when(pl.program_id(0) == 0)
        def _():
            l_ref[...] = jnp.zeros_like(l_ref)

        l_ref[...] += 0.5 * inv * jnp.sum(e * e)
        dy = e * inv
        dy_ref[...] = dy
        dyb_ref[...] = dy.astype(BF16)

    row = pl.BlockSpec((ROWS, d), lambda i: (i, 0))
    return pl.pallas_call(
        body, name=name, grid=(t // ROWS,),
        in_specs=[row, row],
        out_specs=[pl.BlockSpec((8, LANE), lambda i: (0, 0)), row, row],
        out_shape=[jax.ShapeDtypeStruct((8, LANE), F32), jax.ShapeDtypeStruct((t, d), F32),
                   jax.ShapeDtypeStruct((t, d), BF16)],
        compiler_params=_params(("arbitrary",)),
    )(y, target)


MIX_ROWS = 256


def _mix_fwd(o, y, proj, ga, gs, name):
    t = o.shape[0]
    gw = SSM_W // SSM_GROUPS

    def body(o_ref, y_ref, z_ref, ga_ref, gs_ref, c_ref):
        ov = o_ref[...]
        c_ref[:, 0:ATT_W] = (ov * _rstd(ov) * ga_ref[...]).astype(BF16)
        zv = z_ref[...]
        yz = y_ref[...] * (zv * _sigmoid(zv))
        for gi in range(SSM_GROUPS):
            seg = yz[:, gi * gw:(gi + 1) * gw]
            c_ref[:, ATT_W + gi * gw:ATT_W + (gi + 1) * gw] = (
                seg * _rstd(seg) * gs_ref[:, gi * gw:(gi + 1) * gw]).astype(BF16)

    half = pl.BlockSpec((MIX_ROWS, ATT_W), lambda i: (i, 0))
    vec = pl.BlockSpec((1, ATT_W), lambda i: (0, 0))
    return pl.pallas_call(
        body, name=name, grid=(t // MIX_ROWS,),
        in_specs=[half, half, pl.BlockSpec((MIX_ROWS, ATT_W), lambda i: (i, OFF_Z // ATT_W)), vec, vec],
        out_specs=pl.BlockSpec((MIX_ROWS, 2 * ATT_W), lambda i: (i, 0)),
        out_shape=jax.ShapeDtypeStruct((t, 2 * ATT_W), BF16),
        compiler_params=_params(("parallel",)),
    )(o, y, proj, ga, gs)


def _mix_bwd(dcat, o, y, proj, ga, gs, name):
    t = o.shape[0]
    gw = SSM_W // SSM_GROUPS

    def body(dc_ref, o_ref, y_ref, z_ref, ga_ref, gs_ref, do_ref, dy_ref, dz_ref, dga_ref, dgs_ref):
        @pl.when(pl.program_id(0) == 0)
        def _():
            dga_ref[...] = jnp.zeros_like(dga_ref)
            dgs_ref[...] = jnp.zeros_like(dgs_ref)

        ov = o_ref[...]
        r = _rstd(ov)
        oh = ov * r
        d_on = dc_ref[:, 0:ATT_W]
        dga_ref[...] += jnp.sum(d_on * oh, axis=0, keepdims=True)
        doh = d_on * ga_ref[...]
        do_ref[...] = r * (doh - oh * jnp.mean(doh * oh, axis=-1, keepdims=True))

        zv = z_ref[...]
        yv = y_ref[...]
        sz = _sigmoid(zv)
        silu = zv * sz
        yz = yv * silu
        for gi in range(SSM_GROUPS):
            sl = slice(gi * gw, (gi + 1) * gw)
            seg = yz[:, sl]
            rg = _rstd(seg)
            yh = seg * rg
            dyn = dc_ref[:, ATT_W + gi * gw:ATT_W + (gi + 1) * gw]
            dgs_ref[:, sl] += jnp.sum(dyn * yh, axis=0, keepdims=True)
            dyh = dyn * gs_ref[:, sl]
            dyz = rg * (dyh - yh * jnp.mean(dyh * yh, axis=-1, keepdims=True))
            dy_ref[:, sl] = dyz * silu[:, sl]
            dz_ref[:, sl] = (dyz * yv[:, sl] * (sz[:, sl] * (1.0 + zv[:, sl] * (1.0 - sz[:, sl])))).astype(BF16)

    half = pl.BlockSpec((MIX_ROWS, ATT_W), lambda i: (i, 0))
    vec = pl.BlockSpec((1, ATT_W), lambda i: (0, 0))
    return pl.pallas_call(
        body, name=name, grid=(t // MIX_ROWS,),
        in_specs=[pl.BlockSpec((MIX_ROWS, 2 * ATT_W), lambda i: (i, 0)), half, half,
                  pl.BlockSpec((MIX_ROWS, ATT_W), lambda i: (i, OFF_Z // ATT_W)), vec, vec],
        out_specs=[half, half, half, vec, vec],
        out_shape=[jax.ShapeDtypeStruct((t, ATT_W), F32), jax.ShapeDtypeStruct((t, SSM_W), F32),
                   jax.ShapeDtypeStruct((t, SSM_W), BF16), jax.ShapeDtypeStruct((1, ATT_W), F32),
                   jax.ShapeDtypeStruct((1, SSM_W), F32)],
        compiler_params=_params(("arbitrary",)),
    )(dcat, o, y, proj, ga, gs)


ATT_QB = 256
ATT_KB = 256
assert ATT_QB == ATT_KB


def _stacked(m):
    return jnp.concatenate([m, m], axis=0)


def _split_sum(x, m2):
    hi = x.astype(BF16)
    lo = (x - hi.astype(F32)).astype(BF16)
    return _dot(jnp.concatenate([hi, lo], axis=1), m2)


def _att_tile(z, mask, m_strict2, carry):
    lse = jnp.log(1.0 + jnp.exp(-jnp.abs(z)))
    lb = jnp.minimum(z, 0.0) - lse
    lrm = -jnp.maximum(z, 0.0) - lse
    if mask is not None:
        lrm = jnp.where(mask, lrm, 0.0)
    w = jnp.exp(lb + _split_sum(lrm, m_strict2) + carry)
    if mask is not None:
        w = jnp.where(mask, w, 0.0)
    return lb, lrm, w


ATT_HP = 2


def _head_spec(seq, off):
    width = ATT_HP * ATT_DH
    per = ATT_HEADS // ATT_HP
    return pl.BlockSpec((seq, width), lambda s: (s // per, off // width + s % per))


def _head_lanes(h):
    return slice(h * ATT_DH, (h + 1) * ATT_DH)


def _attn_fwd(proj, gq, gk, nb, seq, name):
    nq = seq // ATT_QB
    scale = ATT_DH ** -0.5
    heads = range(ATT_HP)

    def body(q_ref, k_ref, v_ref, gq_ref, gk_ref, o_ref, qs, kn, vb):
        for h in heads:
            sl = _head_lanes(h)
            qv = q_ref[:, sl]
            kv = k_ref[:, sl]
            qs[:, sl] = (qv * _rstd(qv) * gq_ref[...] * scale).astype(BF16)
            kn[:, sl] = (kv * _rstd(kv) * gk_ref[...]).astype(BF16)
            vb[:, sl] = v_ref[:, sl].astype(BF16)
        row = lax.broadcasted_iota(jnp.int32, (ATT_QB, ATT_KB), 0)
        col = lax.broadcasted_iota(jnp.int32, (ATT_QB, ATT_KB), 1)
        m_strict2 = _stacked((row > col).astype(BF16))
        diagonal = col < row

        def key_rows(kj):
            return pl.ds(pl.multiple_of(kj * ATT_KB, ATT_KB), ATT_KB)

        def q_loop(qi, _):
            q0 = pl.multiple_of(qi * ATT_QB, ATT_QB)
            q_ts = [qs[pl.ds(q0, ATT_QB), _head_lanes(h)] for h in heads]

            def scores(h, kj):
                return _dot_nt(q_ts[h], kn[key_rows(kj), _head_lanes(h)])

            def tile(c, kj, mask):
                rows = key_rows(kj)
                out = []
                for h in heads:
                    acc, carry, z = c[h]
                    z_next = scores(h, jnp.maximum(kj - 1, 0))
                    _, lrm, w = _att_tile(z, mask, m_strict2, carry)
                    acc = acc + _dot(w.astype(BF16), vb[rows, _head_lanes(h)])
                    out.append((acc, carry + jnp.sum(lrm, axis=-1, keepdims=True), z_next))
                return tuple(out)

            init = tuple((jnp.zeros((ATT_QB, ATT_DH), F32), jnp.zeros((ATT_QB, 1), F32), scores(h, qi)) for h in heads)
            res = lax.fori_loop(1, qi + 1, lambda i, c: tile(c, qi - i, None), tile(init, qi, diagonal))
            for h in heads:
                o_ref[pl.ds(q0, ATT_QB), _head_lanes(h)] = res[h][0]
            return 0

        lax.fori_loop(0, nq, q_loop, 0)

    vec = pl.BlockSpec((1, ATT_DH), lambda s: (0, 0))
    return pl.pallas_call(
        body, name=name, grid=(nb * ATT_HEADS // ATT_HP,),
        in_specs=[_head_spec(seq, OFF_Q), _head_spec(seq, OFF_K), _head_spec(seq, OFF_V), vec, vec],
        out_specs=_head_spec(seq, 0),
        out_shape=jax.ShapeDtypeStruct((nb * seq, ATT_W), F32),
        scratch_shapes=[pltpu.VMEM((seq, ATT_HP * ATT_DH), BF16)] * 3,
        compiler_params=_params(("parallel",)),
    )(proj, proj, proj, gq, gk)


def _attn_bwd(proj, do, gq, gk, nb, seq, name):
    nq = seq // ATT_QB
    nk = seq // ATT_KB
    scale = ATT_DH ** -0.5
    heads = range(ATT_HP)

    def body(q_ref, k_ref, v_ref, do_ref, gq_ref, gk_ref, dq_ref, dk_ref, dv_ref, dgq_ref, dgk_ref,
             qs, kn, vb, dob, dq_acc, dk_acc, dv_acc, gbuf, bbuf):
        @pl.when(pl.program_id(0) == 0)
        def _():
            dgq_ref[...] = jnp.zeros_like(dgq_ref)
            dgk_ref[...] = jnp.zeros_like(dgk_ref)

        for h in heads:
            sl = _head_lanes(h)
            qv = q_ref[:, sl]
            kv = k_ref[:, sl]
            qs[:, sl] = (qv * _rstd(qv) * gq_ref[...] * scale).astype(BF16)
            kn[:, sl] = (kv * _rstd(kv) * gk_ref[...]).astype(BF16)
            vb[:, sl] = v_ref[:, sl].astype(BF16)
            dob[:, sl] = do_ref[:, sl].astype(BF16)
        dk_acc[...] = jnp.zeros_like(dk_acc)
        dv_acc[...] = jnp.zeros_like(dv_acc)
        row = lax.broadcasted_iota(jnp.int32, (ATT_QB, ATT_KB), 0)
        col = lax.broadcasted_iota(jnp.int32, (ATT_QB, ATT_KB), 1)
        m_strict2 = _stacked((row > col).astype(BF16))
        m_prefix2 = _stacked((row < col).astype(BF16))
        diagonal = col < row

        def key_rows(kj):
            return pl.ds(pl.multiple_of(kj * ATT_KB, ATT_KB), ATT_KB)

        def q_loop(qi, _):
            q0 = pl.multiple_of(qi * ATT_QB, ATT_QB)
            q_ts = [qs[pl.ds(q0, ATT_QB), _head_lanes(h)] for h in heads]
            do_ts = [dob[pl.ds(q0, ATT_QB), _head_lanes(h)] for h in heads]

            def scores(h, kj):
                return _dot_nt(q_ts[h], kn[key_rows(kj), _head_lanes(h)])

            def down(c, kj, mask):
                rows = key_rows(kj)
                out = []
                for h in heads:
                    carry, z = c[h]
                    sl = _head_lanes(h)
                    z_next = scores(h, jnp.maximum(kj - 1, 0))
                    lb, lrm, w = _att_tile(z, mask, m_strict2, carry)
                    dw = _dot_nt(do_ts[h], vb[rows, sl])
                    gbuf[h * nk + kj] = w * dw
                    bbuf[h * nk + kj] = jnp.exp(lb)
                    dv_acc[rows, sl] += _dot_tn(w.astype(BF16), do_ts[h])
                    out.append((carry + jnp.sum(lrm, axis=-1, keepdims=True), z_next))
                return tuple(out)

            init = tuple((jnp.zeros((ATT_QB, 1), F32), scores(h, qi)) for h in heads)
            lax.fori_loop(1, qi + 1, lambda i, c: down(c, qi - i, None), down(init, qi, diagonal))

            def up(c, kj, mask):
                rows = key_rows(kj)
                out = []
                for h in heads:
                    acc, carry, within = c[h]
                    sl = _head_lanes(h)
                    g = gbuf[h * nk + kj]
                    beta = bbuf[h * nk + kj]
                    within_next = _split_sum(gbuf[h * nk + jnp.minimum(kj + 1, qi)], m_prefix2)
                    dz = g * (1.0 - beta) - (within + carry) * beta
                    if mask is not None:
                        dz = jnp.where(mask, dz, 0.0)
                    dz = dz.astype(BF16)
                    acc = acc + _dot(dz, kn[rows, sl])
                    dk_acc[rows, sl] += _dot_tn(dz, q_ts[h])
                    out.append((acc, carry + jnp.sum(g, axis=-1, keepdims=True), within_next))
                return tuple(out)

            init = tuple((jnp.zeros((ATT_QB, ATT_DH), F32), jnp.zeros((ATT_QB, 1), F32),
                          _split_sum(gbuf[h * nk], m_prefix2)) for h in heads)
            res = up(lax.fori_loop(0, qi, lambda kj, c: up(c, kj, None), init), qi, diagonal)
            for h in heads:
                dq_acc[pl.ds(q0, ATT_QB), _head_lanes(h)] = res[h][0]
            return 0

        lax.fori_loop(0, nq, q_loop, 0)

        def norm_bwd(xv, gain, dyn):
            r = _rstd(xv)
            xh = xv * r
            dgain = jnp.sum(dyn * xh, axis=0, keepdims=True)
            dxh = dyn * gain
            return r * (dxh - xh * jnp.mean(dxh * xh, axis=-1, keepdims=True)), dgain

        for h in heads:
            sl = _head_lanes(h)
            dq, dgq = norm_bwd(q_ref[:, sl], gq_ref[...], dq_acc[:, sl] * scale)
            dk, dgk = norm_bwd(k_ref[:, sl], gk_ref[...], dk_acc[:, sl])
            dq_ref[:, sl] = dq.astype(BF16)
            dk_ref[:, sl] = dk.astype(BF16)
            dv_ref[:, sl] = dv_acc[:, sl].astype(BF16)
            dgq_ref[...] += dgq
            dgk_ref[...] += dgk

    vec = pl.BlockSpec((1, ATT_DH), lambda s: (0, 0))
    big = jax.ShapeDtypeStruct((nb * seq, ATT_W), BF16)
    small = jax.ShapeDtypeStruct((1, ATT_DH), F32)
    width = ATT_HP * ATT_DH
    return pl.pallas_call(
        body, name=name, grid=(nb * ATT_HEADS // ATT_HP,),
        in_specs=[_head_spec(seq, OFF_Q), _head_spec(seq, OFF_K), _head_spec(seq, OFF_V), _head_spec(seq, 0), vec, vec],
        out_specs=[_head_spec(seq, 0)] * 3 + [vec, vec],
        out_shape=[big, big, big, small, small],
        scratch_shapes=[pltpu.VMEM((seq, width), BF16)] * 4 + [pltpu.VMEM((seq, width), F32)] * 3
        + [pltpu.VMEM((ATT_HP * nk, ATT_QB, ATT_KB), F32)] * 2,
        compiler_params=_params(("arbitrary",)),
    )(proj, proj, proj, do, gq, gk)


CONV_COLS = 256


def _pack_conv(conv_w, conv_b):
    return jnp.concatenate([conv_w, conv_b[None, :], jnp.zeros((3, CONV_DIM), F32)], axis=0)


def _conv_pre(raw, w8, rowi):
    pre = w8[CONV_K:CONV_K + 1, :] + raw * w8[CONV_K - 1:CONV_K, :]
    for k in range(1, CONV_K):
        sh = jnp.where(rowi >= k, pltpu.roll(raw, k, 0), 0.0)
        pre = pre + sh * w8[CONV_K - 1 - k:CONV_K - k, :]
    return pre


def _conv_fwd(proj, cw8, nb, seq, name):
    ncol = CONV_DIM // CONV_COLS

    def body(x_ref, w_ref, o_ref):
        rowi = lax.broadcasted_iota(jnp.int32, (seq, 1), 0)
        pre = _conv_pre(x_ref[...], w_ref[...], rowi)
        o_ref[...] = pre * _sigmoid(pre)

    return pl.pallas_call(
        body, name=name, grid=(nb, ncol),
        in_specs=[pl.BlockSpec((seq, CONV_COLS), lambda b, j: (b, OFF_XS // CONV_COLS + j)),
                  pl.BlockSpec((8, CONV_COLS), lambda b, j: (0, j))],
        out_specs=pl.BlockSpec((seq, CONV_COLS), lambda b, j: (b, j)),
        out_shape=jax.ShapeDtypeStruct((nb * seq, CONV_DIM), F32),
        compiler_params=_params(("parallel", "parallel")),
    )(proj, cw8)


def _conv_bwd(proj, dact, cw8, nb, seq, name):
    ncol = CONV_DIM // CONV_COLS

    def body(x_ref, d_ref, w_ref, dx_ref, dw_ref):
        @pl.when(pl.program_id(1) == 0)
        def _():
            dw_ref[...] = jnp.zeros_like(dw_ref)

        rowi = lax.broadcasted_iota(jnp.int32, (seq, 1), 0)
        raw = x_ref[...]
        w8 = w_ref[...]
        pre = _conv_pre(raw, w8, rowi)
        sg = _sigmoid(pre)
        dpre = d_ref[...] * (sg * (1.0 + pre * (1.0 - sg)))
        dw_ref[CONV_K:CONV_K + 1, :] += jnp.sum(dpre, axis=0, keepdims=True)
        dw_ref[CONV_K - 1:CONV_K, :] += jnp.sum(dpre * raw, axis=0, keepdims=True)
        draw = dpre * w8[CONV_K - 1:CONV_K, :]
        for k in range(1, CONV_K):
            sh = jnp.where(rowi >= k, pltpu.roll(raw, k, 0), 0.0)
            dw_ref[CONV_K - 1 - k:CONV_K - k, :] += jnp.sum(dpre * sh, axis=0, keepdims=True)
            up = jnp.where(rowi < seq - k, pltpu.roll(dpre, seq - k, 0), 0.0)
            draw = draw + up * w8[CONV_K - 1 - k:CONV_K - k, :]
        dx_ref[...] = draw.astype(BF16)

    return pl.pallas_call(
        body, name=name, grid=(ncol, nb),
        in_specs=[pl.BlockSpec((seq, CONV_COLS), lambda j, b: (b, OFF_XS // CONV_COLS + j)),
                  pl.BlockSpec((seq, CONV_COLS), lambda j, b: (b, j)),
                  pl.BlockSpec((8, CONV_COLS), lambda j, b: (0, j))],
        out_specs=[pl.BlockSpec((seq, CONV_COLS), lambda j, b: (b, j)),
                   pl.BlockSpec((8, CONV_COLS), lambda j, b: (0, j))],
        out_shape=[jax.ShapeDtypeStruct((nb * seq, CONV_DIM), BF16), jax.ShapeDtypeStruct((8, CONV_DIM), F32)],
        compiler_params=_params(("parallel", "arbitrary")),
    )(proj, dact, cw8)


def _pack_heads(dt_bias, a_log, d_skip):
    rows = jnp.stack([dt_bias, a_log, d_skip]).reshape(3, SSM_GROUPS, SSM_HG).transpose(1, 0, 2)
    return jnp.pad(rows, ((0, 0), (0, 8 - 3), (0, LANE - SSM_HG)))


def _split3_rows(x):
    hi = x.astype(BF16)
    r1 = x - hi.astype(F32)
    mid = r1.astype(BF16)
    lo = (r1 - mid.astype(F32)).astype(BF16)
    return jnp.concatenate([hi, mid, lo], axis=0)


def _split3_cols(x):
    hi = x.astype(BF16)
    r1 = x - hi.astype(F32)
    mid = r1.astype(BF16)
    lo = (r1 - mid.astype(F32)).astype(BF16)
    return jnp.concatenate([hi, mid, lo], axis=1)


def _split2_rows(x):
    hi = x.astype(BF16)
    return jnp.concatenate([hi, (x - hi.astype(F32)).astype(BF16)], axis=0)


def _ssd_specs(seq):
    gx = SSM_HG * SSM_P
    return dict(
        xs=pl.BlockSpec((seq, gx), lambda g, b: (b, g)),
        bm=pl.BlockSpec((seq, SSM_N), lambda g, b: (b, SSM_W // SSM_N + g)),
        cm=pl.BlockSpec((seq, SSM_N), lambda g, b: (b, SSM_W // SSM_N + SSM_GROUPS + g)),
        dt=pl.BlockSpec((seq, LANE), lambda g, b: (b, OFF_DT // LANE)),
        hp=pl.BlockSpec((1, 8, LANE), lambda g, b: (g, 0, 0)),
        head=pl.BlockSpec((seq, gx), lambda g, b: (b, g)),
        grp=pl.BlockSpec((seq, SSM_N), lambda g, b: (b, g)),
    )


SSM_GX = SSM_HG * SSM_P


def _ssd_masks():
    li = lax.broadcasted_iota(jnp.int32, (CHUNK, CHUNK), 0)
    si = lax.broadcasted_iota(jnp.int32, (CHUNK, CHUNK), 1)
    head = lax.broadcasted_iota(jnp.int32, (LANE, SSM_GX), 0)
    lane = lax.broadcasted_iota(jnp.int32, (LANE, SSM_GX), 1)
    expand = (lane // SSM_P == head).astype(BF16)
    head_t = lax.broadcasted_iota(jnp.int32, (SSM_GX, LANE), 1)
    lane_t = lax.broadcasted_iota(jnp.int32, (SSM_GX, LANE), 0)
    gather = (lane_t // SSM_P == head_t).astype(BF16)
    return dict(
        causal=li >= si, causal_t=si >= li,
        tril3=jnp.concatenate([(si <= li).astype(BF16)] * 3, axis=1),
        triu2=jnp.concatenate([(si >= li).astype(BF16)] * 2, axis=1),
        below2=jnp.concatenate([(si < li).astype(BF16)] * 2, axis=1),
        expand2=_stacked(expand), gather2=_stacked(gather))


def _per_head(x, mk):
    return _split_sum(x, mk["expand2"])


def _head_sums(x, mk):
    return _split_sum(x, mk["gather2"])


def _row8(v):
    return jnp.broadcast_to(v, (8, v.shape[1]))


def _group_dt(dt_ref):
    shift = (LANE - SSM_HG * pl.program_id(0)) % LANE
    return pltpu.roll(dt_ref[...], shift, 1)


def _ssd_fwd(act, proj, hp, nb, seq, name):
    nc = seq // CHUNK

    def body(xs_ref, b_ref, c_ref, dt_ref, hp_ref, y_ref, dt_s, da_s, hst):
        mk = _ssd_masks()
        hpv = hp_ref[0]
        dt = _softplus(_group_dt(dt_ref) + hpv[0:1, :])
        a = -jnp.exp(hpv[1:2, :])
        dsk_row = _per_head(_row8(hpv[2:3, :]), mk)[0:1]
        dt_s[...] = dt
        da_s[...] = dt * a
        hst[...] = jnp.zeros_like(hst)

        def chunk(c, _):
            rows = pl.ds(pl.multiple_of(c * CHUNK, CHUNK), CHUNK)
            acol = _dot(mk["tril3"], _split3_rows(da_s[rows, :]))
            arow = acol.T
            alast = acol[CHUNK - 1:CHUNK, :]
            ea = _per_head(jnp.exp(acol), mk)
            eb = _per_head(jnp.exp(alast - acol), mk)
            el = _per_head(_row8(jnp.exp(alast)), mk)[0:1]
            bb = b_ref[rows, :].astype(BF16)
            cb = c_ref[rows, :].astype(BF16)
            cbm = _dot_nt(cb, bb)
            xc = xs_ref[rows, :]
            u = xc * _per_head(dt_s[rows, :], mk)
            ub = u.astype(BF16)
            ht = hst[...]
            y_ref[rows, :] = ea * _dot(cb, ht.astype(BF16)) + dsk_row * xc
            for j in range(SSM_HG):
                sl = slice(j * SSM_P, (j + 1) * SSM_P)
                decay = jnp.where(mk["causal"], jnp.exp(jnp.minimum(acol[:, j:j + 1] - arow[j:j + 1, :], 0.0)), 0.0)
                y_ref[rows, sl] += _dot((cbm * decay).astype(BF16), ub[:, sl])
            hst[...] = el * ht + _dot_tn(bb, (u * eb).astype(BF16))
            return 0

        lax.fori_loop(0, nc, chunk, 0)

    sp = _ssd_specs(seq)
    return pl.pallas_call(
        body, name=name, grid=(SSM_GROUPS, nb),
        in_specs=[sp["xs"], sp["bm"], sp["cm"], sp["dt"], sp["hp"]],
        out_specs=sp["head"],
        out_shape=jax.ShapeDtypeStruct((nb * seq, SSM_W), F32),
        scratch_shapes=[pltpu.VMEM((seq, LANE), F32)] * 2 + [pltpu.VMEM((SSM_N, SSM_GX), F32)],
        compiler_params=_params(("parallel", "parallel")),
    )(act, act, act, proj, hp)


def _ssd_bwd(act, proj, dy, hp, nb, seq, name):
    nc = seq // CHUNK

    def body(xs_ref, b_ref, c_ref, dt_ref, hp_ref, dy_ref, dxs_ref, db_ref, dc_ref, ddt_ref, dhp_ref,
             dt_s, da_s, ddt_s, hs, lam, du_s):
        @pl.when(pl.program_id(1) == 0)
        def _():
            dhp_ref[...] = jnp.zeros_like(dhp_ref)

        mk = _ssd_masks()
        hpv = hp_ref[0]
        a = -jnp.exp(hpv[1:2, :])
        dsk_row = _per_head(_row8(hpv[2:3, :]), mk)[0:1]
        dt_s[...] = _softplus(_group_dt(dt_ref) + hpv[0:1, :])
        da_s[...] = dt_s[...] * a
        lane = lax.broadcasted_iota(jnp.int32, (1, LANE), 1)

        def chunk_rows(c):
            return pl.ds(pl.multiple_of(c * CHUNK, CHUNK), CHUNK)

        def decays(c):
            acol = _dot(mk["tril3"], _split3_rows(da_s[chunk_rows(c), :]))
            alast = acol[CHUNK - 1:CHUNK, :]
            return acol, alast

        hs[0] = jnp.zeros((SSM_N, SSM_GX), F32)

        def fwd_chunk(c, _):
            rows = chunk_rows(c)
            acol, alast = decays(c)
            eb = _per_head(jnp.exp(alast - acol), mk)
            el = _per_head(_row8(jnp.exp(alast)), mk)[0:1]
            u = xs_ref[rows, :] * _per_head(dt_s[rows, :], mk)
            hs[c + 1] = el * hs[c] + _dot_tn(b_ref[rows, :].astype(BF16), (u * eb).astype(BF16))
            return 0

        lax.fori_loop(0, nc - 1, fwd_chunk, 0)
        lam[...] = jnp.zeros_like(lam)

        def bwd_chunk(i, carry):
            dd_row, da_vec = carry
            c = nc - 1 - i
            rows = chunk_rows(c)
            acol, alast = decays(c)
            arow = acol.T
            ea = _per_head(jnp.exp(acol), mk)
            eb = _per_head(jnp.exp(alast - acol), mk)
            el = _per_head(_row8(jnp.exp(alast)), mk)[0:1]
            dt_all = _per_head(dt_s[rows, :], mk)
            bb = b_ref[rows, :].astype(BF16)
            cb = c_ref[rows, :].astype(BF16)
            cbm = _dot_nt(cb, bb)
            cbt = _dot_nt(bb, cb)
            xc = xs_ref[rows, :]
            dyc = dy_ref[rows, :]
            u = xc * dt_all
            ub = u.astype(BF16)
            dyb = dyc.astype(BF16)
            h_in = hs[c]
            lm = lam[...]
            hb = h_in.astype(BF16)
            lb = lm.astype(BF16)
            y_off = ea * _dot(cb, hb)
            du_off = eb * _dot(bb, lb)
            dye = (ea * dyc).astype(BF16)
            zero = jnp.zeros((CHUNK, CHUNK), F32)
            dcb, dcbt, d_a = zero, zero, zero
            for j in range(SSM_HG):
                sl = slice(j * SSM_P, (j + 1) * SSM_P)
                seg = acol[:, j:j + 1] - arow[j:j + 1, :]
                decay = jnp.where(mk["causal"], jnp.exp(jnp.minimum(seg, 0.0)), 0.0)
                decay_t = jnp.where(mk["causal_t"], jnp.exp(jnp.minimum(-seg, 0.0)), 0.0)
                m = cbm * decay
                mt = cbt * decay_t
                dm = _dot_nt(dyb[:, sl], ub[:, sl])
                dmt = _dot_nt(ub[:, sl], dyb[:, sl])
                dcb = dcb + dm * decay
                dcbt = dcbt + dmt * decay_t
                du_s[:, sl] = _dot(mt.astype(BF16), dyb[:, sl])
                d_a_j = jnp.sum(dm * m, axis=-1, keepdims=True) - jnp.sum(dmt * mt, axis=-1, keepdims=True)
                d_a = jnp.where(lane == j, d_a_j, d_a)
            du = du_s[...] + du_off
            dxs_ref[rows, :] = du * dt_all + dsk_row * dyc
            dc_ref[rows, :] = _dot_nt(dye, hb) + _dot(dcb.astype(BF16), bb)
            db_ref[rows, :] = _dot_nt((eb * u).astype(BF16), lb) + _dot(dcbt.astype(BF16), cb)
            lam[...] = el * lm + _dot_tn(cb, dye)
            d_a = d_a + _head_sums(dyc * y_off, mk)
            f_a = _head_sums(du_off * u, mk)
            c_a = jnp.exp(alast) * _head_sums(_row8(jnp.sum(lm * h_in, axis=0, keepdims=True)), mk)[0:1]
            dda = _dot(mk["triu2"], _split2_rows(d_a)) + _dot(mk["below2"], _split2_rows(f_a)) + c_a
            ddt_s[rows, :] = dda * a + _head_sums(du * xc, mk)
            da_vec = da_vec + jnp.sum(dda * dt_s[rows, :], axis=0, keepdims=True)
            dd_row = dd_row + jnp.sum(dyc * xc, axis=0, keepdims=True)
            return dd_row, da_vec

        init = (jnp.zeros((1, SSM_GX), F32), jnp.zeros((1, LANE), F32))
        dd_row, da_vec = lax.fori_loop(0, nc, bwd_chunk, init)
        ddt_raw = ddt_s[...] * _sigmoid(_group_dt(dt_ref) + hpv[0:1, :])
        ddt_ref[...] = ddt_raw.astype(BF16)
        dhp_ref[0, 0:1, :] += jnp.sum(ddt_raw, axis=0, keepdims=True)
        dhp_ref[0, 1:2, :] += da_vec * a
        dhp_ref[0, 2:3, :] += _head_sums(_row8(dd_row), mk)[0:1]

    sp = _ssd_specs(seq)
    t = nb * seq
    return pl.pallas_call(
        body, name=name, grid=(SSM_GROUPS, nb),
        in_specs=[sp["xs"], sp["bm"], sp["cm"], sp["dt"], sp["hp"], sp["head"]],
        out_specs=[sp["head"], sp["grp"], sp["grp"], sp["grp"], sp["hp"]],
        out_shape=[jax.ShapeDtypeStruct((t, SSM_W), F32), jax.ShapeDtypeStruct((t, SSM_GROUPS * SSM_N), F32),
                   jax.ShapeDtypeStruct((t, SSM_GROUPS * SSM_N), F32),
                   jax.ShapeDtypeStruct((t, SSM_GROUPS * LANE), BF16),
                   jax.ShapeDtypeStruct((SSM_GROUPS, 8, LANE), F32)],
        scratch_shapes=[pltpu.VMEM((seq, LANE), F32)] * 3
        + [pltpu.VMEM((nc, SSM_N, SSM_GX), F32), pltpu.VMEM((SSM_N, SSM_GX), F32), pltpu.VMEM((CHUNK, SSM_GX), F32)],
        compiler_params=_params(("parallel", "arbitrary")),
    )(act, act, act, proj, hp, dy)


ANY = pl.BlockSpec(memory_space=pl.ANY)


def _block_index(p):
    return 4 * p[0] + 2 * p[1] + p[2]


def _all_gather(shards, name):
    n = len(shards)

    def body(*refs):
        ins, outs = refs[:n], refs[n:2 * n]
        send_sems, recv_sems, local_sems = refs[2 * n:]
        x, y, c = lax.axis_index("x"), lax.axis_index("y"), lax.axis_index("c")
        me, sibling = (x, y, c), (x, y, 1 - c)
        chips = [(1 - x, y), (x, 1 - y), (1 - x, 1 - y)]

        def copy(i, k, block, to, src=None):
            dst = outs[i].at[_block_index(block)]
            return pltpu.make_async_remote_copy(
                src_ref=dst if src is None else src, dst_ref=dst,
                send_sem=send_sems.at[i, k], recv_sem=recv_sems.at[i, k],
                device_id=to, device_id_type=MESH)

        mine = [pltpu.make_async_copy(ins[i], outs[i].at[_block_index(me)], local_sems.at[i]) for i in range(n)]
        for cp in mine:
            cp.start()
        first = []
        for i in range(n):
            first.append(copy(i, 0, me, sibling, src=ins[i]))
            first += [copy(i, 1 + j, me, (*chip, c), src=ins[i]) for j, chip in enumerate(chips)]
        for cp in first:
            cp.start()
        passed = []
        for j, chip in enumerate(chips):
            for i in range(n):
                copy(i, 1 + j, (*chip, c), me).wait_recv()
                fwd = copy(i, 4 + j, (*chip, c), sibling)
                fwd.start()
                passed.append(fwd)
        for i in range(n):
            copy(i, 0, sibling, me).wait_recv()
            for j, chip in enumerate(chips):
                copy(i, 4 + j, (*chip, 1 - c), me).wait_recv()
        for cp in first + passed:
            cp.wait_send()
        for cp in mine:
            cp.wait()

    return pl.pallas_call(
        body, name=name,
        in_specs=[ANY] * n, out_specs=[ANY] * n,
        out_shape=[jax.ShapeDtypeStruct((N_DEV,) + s.shape, s.dtype) for s in shards],
        scratch_shapes=[pltpu.SemaphoreType.DMA((n, 7)), pltpu.SemaphoreType.DMA((n, 7)),
                        pltpu.SemaphoreType.DMA((n,))],
    )(*shards)


HBM = pl.BlockSpec(memory_space=pltpu.HBM)
SEM = pl.BlockSpec(memory_space=pltpu.SEMAPHORE)
EFFECT = pltpu.SideEffectType.DATAFLOW_SIDE_EFFECTING


def _my_block():
    return _block_index((lax.axis_index("x"), lax.axis_index("y"), lax.axis_index("c")))


def _peer(k):
    x, y, c = lax.axis_index("x"), lax.axis_index("y"), lax.axis_index("c")
    return (1 - x if k & 4 else x, 1 - y if k & 2 else y, 1 - c if k & 1 else c)


def _plan_copies(plan, src_refs, land_refs, send_sems, recv_sems):
    me = _my_block()
    copies = []
    for e, (si, di, src_view, dst_view, _) in enumerate(plan):
        for k in range(1, N_DEV):
            copies.append(pltpu.make_async_remote_copy(
                src_ref=src_view(src_refs[si], _block_index(_peer(k))),
                dst_ref=dst_view(land_refs[di], me),
                send_sem=send_sems[e], recv_sem=recv_sems[e],
                device_id=_peer(k), device_id_type=MESH))
    return copies


def _plan_waits(plan, land_refs, send_sems, recv_sems):
    waits = []
    for e, (_, di, _, _, seven) in enumerate(plan):
        view = seven(land_refs[di])
        waits.append(pltpu.make_async_remote_copy(
            src_ref=view, dst_ref=view, send_sem=send_sems[e], recv_sem=recv_sems[e],
            device_id=_peer(1), device_id_type=MESH))
    return waits


def _plan_own(plan, src_refs, land_refs, own_sems):
    me = _my_block()
    return [pltpu.make_async_copy(src_view(src_refs[si], me), dst_view(land_refs[di], me), own_sems[e])
            for e, (si, di, src_view, dst_view, _) in enumerate(plan)]


def _copies_start(srcs, lands, plan, name, after=None):
    ns, nl, ne = len(srcs), len(lands), len(plan)
    extra = [] if after is None else [after]

    nin = ns + nl + len(extra)

    def body(*refs):
        src_refs, land_refs = refs[:ns], refs[ns:ns + nl]
        send_sems, recv_sems = refs[nin:nin + ne], refs[nin + ne:nin + 2 * ne]
        own_sems = refs[nin + 2 * ne:nin + 3 * ne]
        token = refs[-1]
        for cp in _plan_copies(plan, src_refs, land_refs, send_sems, recv_sems):
            cp.start()
        for cp in _plan_own(plan, src_refs, land_refs, own_sems):
            cp.start()
        token[...] = jnp.zeros_like(token)

    thru = [pltpu.HBM(a.shape, a.dtype) for a in list(srcs) + list(lands)]
    res = pl.pallas_call(
        body, name=name,
        in_specs=[HBM] * (ns + nl) + [ANY] * len(extra),
        out_specs=[SEM] * (3 * ne) + [HBM] * (ns + nl) + [pl.BlockSpec(memory_space=pltpu.VMEM)],
        out_shape=[pltpu.SemaphoreType.DMA(())] * (3 * ne) + thru + [jax.ShapeDtypeStruct((8, LANE), F32)],
        input_output_aliases={i: 3 * ne + i for i in range(ns + nl)},
        compiler_params=pltpu.CompilerParams(has_side_effects=EFFECT),
    )(*[pltpu.with_memory_space_constraint(a, pltpu.HBM) for a in list(srcs) + list(lands)], *extra)
    return dict(sems=res[:3 * ne], srcs=res[3 * ne:3 * ne + ns], lands=res[3 * ne + ns:3 * ne + ns + nl],
                token=res[-1], plan=plan)


def _copies_wait(flight, after, name):
    srcs, lands, plan = flight["srcs"], flight["lands"], flight["plan"]
    ns, nl, ne = len(srcs), len(lands), len(plan)

    def body(*refs):
        src_refs, land_refs = refs[:ns], refs[ns:ns + nl]
        send_sems, recv_sems = refs[ns + nl:ns + nl + ne], refs[ns + nl + ne:ns + nl + 2 * ne]
        own_sems = refs[ns + nl + 2 * ne:ns + nl + 3 * ne]
        for cp in _plan_waits(plan, land_refs, send_sems, recv_sems):
            cp.wait_send()
            cp.wait_recv()
        for cp in _plan_own(plan, src_refs, land_refs, own_sems):
            cp.wait()

    thru = [pltpu.HBM(a.shape, a.dtype) for a in list(srcs) + list(lands)]
    res = pl.pallas_call(
        body, name=name,
        in_specs=[HBM] * (ns + nl) + [SEM] * (3 * ne) + [ANY],
        out_specs=[HBM] * (ns + nl),
        out_shape=thru,
        input_output_aliases={i: i for i in range(ns + nl)},
        compiler_params=pltpu.CompilerParams(has_side_effects=EFFECT),
    )(*srcs, *lands, *flight["sems"], after)
    return list(res[ns:])


def _adamw_math(w, g, m, v):
    m = ADAM_B1 * m + (1.0 - ADAM_B1) * g
    v = ADAM_B2 * v + (1.0 - ADAM_B2) * (g * g)
    m_hat = m / (1.0 - ADAM_B1 ** ADAM_STEP)
    v_hat = v / (1.0 - ADAM_B2 ** ADAM_STEP)
    delta = -ADAM_LR * (m_hat / (jnp.sqrt(v_hat) + ADAM_EPS) + ADAM_WD * w)
    return delta, m, v


def _adamw(parts, w, m, v, name, rows, lane_offset=None):
    depth, r, c = w.shape
    cp = parts[0].shape[2]
    assert r % rows == 0 and len(parts) == depth

    def body(*refs):
        p_refs = refs[:depth]
        w_ref, m_ref, v_ref, g_ref, d_ref, mo_ref, vo_ref = refs[depth:]
        if lane_offset is not None:
            cw = -(-c // LANE) * LANE
            src = lax.broadcasted_iota(jnp.int32, (cp, cw), 0)
            dst = lax.broadcasted_iota(jnp.int32, (cp, cw), 1)
            pick = (src == dst + lane_offset()).astype(BF16)
            pick3 = jnp.concatenate([pick] * 3, axis=0)
        for li in range(depth):
            @pl.when(pl.program_id(0) == li)
            def _(li=li):
                g = p_refs[li][0].astype(F32)
                for j in range(1, N_DEV):
                    g = g + p_refs[li][j].astype(F32)
                if lane_offset is not None:
                    g = _dot(_split3_cols(g), pick3)
                g = g[:, :c]
                d, mn, vn = _adamw_math(w_ref[...], g, m_ref[...], v_ref[...])
                g_ref[...] = g
                d_ref[...] = d
                mo_ref[...] = mn
                vo_ref[...] = vn

    def part_spec(li):
        return pl.BlockSpec((N_DEV, rows, cp), lambda l, i: (0, jnp.where(l == li, i, 0), 0))

    blk = pl.BlockSpec((None, rows, c), lambda l, i: (l, i, 0))
    out = jax.ShapeDtypeStruct((depth, r, c), F32)
    return pl.pallas_call(
        body, name=name, grid=(depth, r // rows),
        in_specs=[part_spec(li) for li in range(depth)] + [blk, blk, blk],
        out_specs=[blk] * 4, out_shape=[out] * 4,
        compiler_params=_params(("arbitrary", "arbitrary")),
    )(*parts, w, m, v)


def _sum_parts(parts, name):
    _, r, c = parts.shape

    def body(p_ref, o_ref):
        g = p_ref[0]
        for j in range(1, N_DEV):
            g = g + p_ref[j]
        o_ref[...] = g

    return pl.pallas_call(
        body, name=name, out_shape=jax.ShapeDtypeStruct((r, c), F32),
        compiler_params=_params(),
    )(parts)


def _adamw_small(parts, ws, ms, vs, name):
    n = len(ws)

    def body(*refs):
        ins, outs = refs[:4 * n], refs[4 * n:]
        for i in range(n):
            p_ref, w_ref, m_ref, v_ref = ins[i], ins[n + i], ins[2 * n + i], ins[3 * n + i]
            g = p_ref[0]
            for j in range(1, p_ref.shape[0]):
                g = g + p_ref[j]
            d, mn, vn = _adamw_math(w_ref[...], g, m_ref[...], v_ref[...])
            outs[i][...] = g
            outs[n + i][...] = d
            outs[2 * n + i][...] = mn
            outs[3 * n + i][...] = vn

    out = [jax.ShapeDtypeStruct(a.shape, F32) for a in ws] * 4
    res = pl.pallas_call(body, name=name, out_shape=out, compiler_params=_params())(*parts, *ws, *ms, *vs)
    return res[:n], res[n:2 * n], res[2 * n:3 * n], res[3 * n:]


SMALL = ("norm_mix", "q_gain", "k_gain", "conv_b", "dt_bias", "a_log", "d_skip", "attn_out_gain",
         "ssm_out_gain", "norm_ffn")


def _full_cols(gathered):
    _, r, c = gathered.shape
    return gathered.transpose(1, 0, 2).reshape(r, N_DEV * c)


FF_BLK = 768
FF_PAD = N_DEV * FF_BLK
W_IN_COLS = IN_DIM // N_DEV
W_IN_WINDOW = 896


def _w_in_window_start(block):
    return (block * W_IN_COLS // LANE) * LANE


def _w_in_window(ref, block):
    return ref.at[:, pl.ds(pl.multiple_of(_w_in_window_start(block), LANE), W_IN_WINDOW)]


def _whole(ref, block):
    return ref


def _rows_of(size):
    return lambda ref, block: ref.at[pl.ds(pl.multiple_of(block * size, size), size), :]


def _slot(ref, block):
    return ref.at[block]


def _seven_slots(ref):
    return ref.at[pl.ds(0, N_DEV - 1)]


def _seven_rows(size):
    return lambda ref: ref.at[pl.ds(0, (N_DEV - 1) * size), :]


GATHER_A = [(0, 0, _whole, _slot, _seven_slots), (1, 1, _whole, _rows_of(256), _seven_rows(256))]
GATHER_B = [(i, i, _whole, _rows_of(FF_BLK), _seven_rows(FF_BLK)) for i in range(3)]
SCATTER_A = [(0, 0, _w_in_window, _slot, _seven_slots), (1, 1, _rows_of(256), _slot, _seven_slots)]
SCATTER_B = [(i, i, _rows_of(FF_BLK), _slot, _seven_slots) for i in range(3)]


def _gather_lands(which, shards, d):
    if which == "a":
        return [lax.empty((N_DEV,) + shards[0].shape, BF16), lax.empty((d, d), BF16)]
    return [lax.empty((FF_PAD, d), BF16) for _ in range(3)]


def _scatter_lands(which, grads):
    if which == "a":
        g_in, g_out = grads
        return [lax.empty((N_DEV, g_in.shape[0], W_IN_WINDOW), BF16),
                lax.empty((N_DEV, g_out.shape[0] // N_DEV, g_out.shape[1]), BF16)]
    return [lax.empty((N_DEV, FF_BLK, g.shape[1]), BF16) for g in grads]


def _pad_w_in(full):
    return jnp.pad(full, ((0, 0), (0, NPROJ - IN_DIM)))


def kernel(x, norm_mix, w_in, q_gain, k_gain, conv_w, conv_b, dt_bias, a_log, d_skip, attn_out_gain, ssm_out_gain, w_out, norm_ffn, w_gate, w_up, w_down, loss_target, m_norm_mix, m_w_in, m_q_gain, m_k_gain, m_conv_w, m_conv_b, m_dt_bias, m_a_log, m_d_skip, m_attn_out_gain, m_ssm_out_gain, m_w_out, m_norm_ffn, m_w_gate, m_w_up, m_w_down, v_norm_mix, v_w_in, v_q_gain, v_k_gain, v_conv_w, v_conv_b, v_dt_bias, v_a_log, v_d_skip, v_attn_out_gain, v_ssm_out_gain, v_w_out, v_norm_ffn, v_w_gate, v_w_up, v_w_down):
    nb, seq, d = x.shape
    depth = w_in.shape[0]
    t = nb * seq
    w = dict(norm_mix=norm_mix, w_in=w_in, q_gain=q_gain, k_gain=k_gain, conv_w=conv_w, conv_b=conv_b,
             dt_bias=dt_bias, a_log=a_log, d_skip=d_skip, attn_out_gain=attn_out_gain, ssm_out_gain=ssm_out_gain,
             w_out=w_out, norm_ffn=norm_ffn, w_gate=w_gate, w_up=w_up, w_down=w_down)
    mom = dict(norm_mix=m_norm_mix, w_in=m_w_in, q_gain=m_q_gain, k_gain=m_k_gain, conv_w=m_conv_w, conv_b=m_conv_b,
               dt_bias=m_dt_bias, a_log=m_a_log, d_skip=m_d_skip, attn_out_gain=m_attn_out_gain,
               ssm_out_gain=m_ssm_out_gain, w_out=m_w_out, norm_ffn=m_norm_ffn, w_gate=m_w_gate, w_up=m_w_up,
               w_down=m_w_down)
    var = dict(norm_mix=v_norm_mix, w_in=v_w_in, q_gain=v_q_gain, k_gain=v_k_gain, conv_w=v_conv_w, conv_b=v_conv_b,
               dt_bias=v_dt_bias, a_log=v_a_log, d_skip=v_d_skip, attn_out_gain=v_attn_out_gain,
               ssm_out_gain=v_ssm_out_gain, w_out=v_w_out, norm_ffn=v_norm_ffn, w_gate=v_w_gate, w_up=v_w_up,
               w_down=v_w_down)
    ff = w_gate.shape[2]

    (conv_all,) = _all_gather([conv_w], "gather_conv")

    def shards_a(li):
        return [w_in[li].astype(BF16), w_out[li].astype(BF16)]

    w["w_gate"], mom["w_gate"], var["w_gate"] = (jnp.swapaxes(a, 1, 2) for a in (w_gate, m_w_gate, v_w_gate))
    w["w_up"], mom["w_up"], var["w_up"] = (jnp.swapaxes(a, 1, 2) for a in (w_up, m_w_up, v_w_up))

    def shards_b(li):
        return [jnp.pad(w[k][li].astype(BF16), ((0, FF_BLK - ff), (0, 0))) for k in ("w_gate", "w_up", "w_down")]

    def small_params(li):
        p = {k: w[k][li][None, :] for k in ("norm_mix", "q_gain", "k_gain", "attn_out_gain", "ssm_out_gain", "norm_ffn")}
        conv_full = conv_all[:, li].transpose(1, 0, 2).reshape(CONV_K, CONV_DIM)
        p["cw8"] = _pack_conv(conv_full, conv_b[li])
        p["hp"] = _pack_heads(dt_bias[li], a_log[li], d_skip[li])
        return p

    xc = x.reshape(t, d)
    cur = shards_a(0)
    flight = _copies_start(cur, _gather_lands("a", cur, d), GATHER_A, "gather_a0")
    lands_a = _copies_wait(flight, xc, "gather_a0_wait")
    layers, saved = [], []
    for li in range(depth):
        tag = f"l{li}_"
        p = small_params(li)
        p["w_in"] = _pad_w_in(_full_cols(lands_a[0]))
        p["w_out"] = lands_a[1]
        cur = shards_b(li)
        flight = _copies_start(cur, _gather_lands("b", cur, d), GATHER_B, tag + "gather_b", after=lands_a[1])
        h1 = _rms_fwd(xc, p["norm_mix"], tag + "rms1", after=flight["token"])
        proj = _matmul(h1, p["w_in"], "nn", F32, tag + "mm_in")
        o = _attn_fwd(proj, p["q_gain"], p["k_gain"], nb, seq, tag + "attn")
        act = _conv_fwd(proj, p["cw8"], nb, seq, tag + "conv")
        y = _ssd_fwd(act, proj, p["hp"], nb, seq, tag + "ssd")
        cat = _mix_fwd(o, y, proj, p["attn_out_gain"], p["ssm_out_gain"], tag + "mix")
        x1 = _matmul(cat, p["w_out"], "nn", F32, tag + "mm_out", residual=xc)
        p["w_gate"], p["w_up"], p["w_down"] = _copies_wait(flight, x1, tag + "gather_b_wait")
        token = None
        if li + 1 < depth:
            nxt = shards_a(li + 1)
            flight = _copies_start(nxt, _gather_lands("a", nxt, d), GATHER_A, f"gather_a{li + 1}",
                                   after=p["w_down"])
            token = flight["token"]
        h2 = _rms_fwd(x1, p["norm_ffn"], tag + "rms2", after=token)
        a, gate, up = _mm_swiglu(h2, p["w_gate"], p["w_up"], tag + "mm_gu")
        x2 = _matmul(a, p["w_down"], "nn", F32, tag + "mm_down", residual=x1)
        if li + 1 < depth:
            lands_a = _copies_wait(flight, x2, f"gather_a{li + 1}_wait")
        saved.append(dict(x=xc, h1=h1, proj=proj, o=o, act=act, y=y, cat=cat, x1=x1, h2=h2, gate=gate, up=up, a=a))
        layers.append(p)
        xc = x2

    loss_blk, dx, dxb = _loss_fwd_bwd(xc, loss_target.reshape(t, d), "loss")
    loss = lax.psum(loss_blk[0, 0], ("x", "y", "c"))

    grads = [dict() for _ in range(depth)]
    recv = [dict() for _ in range(depth)]
    flight_a, token = None, None
    for li in reversed(range(depth)):
        tag = f"l{li}_b_"
        p, s, g = layers[li], saved[li], grads[li]
        dgate, dup = _mm_dact_swiglu(dxb, p["w_down"], s["gate"], s["up"], tag + "mm_dact", after=token)
        g_down = _matmul(s["a"], dxb, "tn", BF16, tag + "mm_dwd")
        dh2 = _matmul(dgate, p["w_gate"], "nn", F32, tag + "mm_dh2g")
        dh2 = _matmul(dup, p["w_up"], "nn", F32, tag + "mm_dh2u", residual=dh2)
        g_gate = _matmul(dgate, s["h2"], "tn", BF16, tag + "mm_dwg")
        g_up = _matmul(dup, s["h2"], "tn", BF16, tag + "mm_dwu")
        if flight_a is not None:
            recv[li + 1]["w_in"], recv[li + 1]["w_out"] = _copies_wait(flight_a, g_up, f"l{li + 1}_b_scatter_a_wait")
        grads_b = [g_gate, g_up, g_down]
        flight_b = _copies_start(grads_b, _scatter_lands("b", grads_b), SCATTER_B,
                                 tag + "scatter_b", after=recv[li + 1]["w_out"] if li + 1 < depth else None)
        dx1, dx1b, g["norm_ffn"] = _rms_bwd(s["x1"], p["norm_ffn"], dh2, dx, tag + "rms2")
        dcat = _matmul(dx1b, p["w_out"], "nt", F32, tag + "mm_dcat", after=flight_b["token"])
        g_out = _matmul(s["cat"], dx1b, "tn", BF16, tag + "mm_dwo")
        do, dy, dz, g["attn_out_gain"], g["ssm_out_gain"] = _mix_bwd(
            dcat, s["o"], s["y"], s["proj"], p["attn_out_gain"], p["ssm_out_gain"], tag + "mix")
        dq, dk, dv, g["q_gain"], g["k_gain"] = _attn_bwd(s["proj"], do, p["q_gain"], p["k_gain"], nb, seq, tag + "attn")
        dxa, dba, dca, ddt, dhp = _ssd_bwd(s["act"], s["proj"], dy, p["hp"], nb, seq, tag + "ssd")
        dxbc, dcw8 = _conv_bwd(s["proj"], jnp.concatenate([dxa, dba, dca], axis=1), p["cw8"], nb, seq, tag + "conv")
        g["conv_w"] = dcw8[0:CONV_K]
        g["conv_b"] = dcw8[CONV_K:CONV_K + 1]
        heads = dhp[:, 0:3, 0:SSM_HG].transpose(1, 0, 2).reshape(3, SSM_HEADS)
        g["dt_bias"], g["a_log"], g["d_skip"] = heads[0:1], heads[1:2], heads[2:3]
        ddt = ddt[:, :LANE] + jnp.roll(ddt[:, LANE:], SSM_HG, axis=1)
        tail = jnp.zeros((t, NPROJ - OFF_DT - LANE), BF16)
        dproj = jnp.concatenate([dq, dk, dv, dz, dxbc, ddt, tail], axis=1)
        recv[li]["w_gate"], recv[li]["w_up"], recv[li]["w_down"] = _copies_wait(flight_b, dproj, tag + "scatter_b_wait")
        dh1 = _matmul(dproj, p["w_in"], "nt", F32, tag + "mm_dh1")
        g_in = _matmul(s["h1"], dproj, "tn", BF16, tag + "mm_dwin")
        flight_a = _copies_start([g_in, g_out], _scatter_lands("a", [g_in, g_out]), SCATTER_A, tag + "scatter_a")
        token = flight_a["token"]
        dx, dxb, g["norm_mix"] = _rms_bwd(s["x"], p["norm_mix"], dh1, dx1, tag + "rms1")
    grad_x = dx.reshape(nb, seq, d)

    out_g, out_d, out_m, out_v = {}, {}, {}, {}

    def update(k, rows, lane_offset=None):
        parts = [recv[li][k] for li in range(depth)]
        out_g[k], out_d[k], out_m[k], out_v[k] = _adamw(parts, w[k], mom[k], var[k], "adamw_" + k, rows, lane_offset)

    def w_in_offset():
        return _my_block() * W_IN_COLS - _w_in_window_start(_my_block())

    update("w_gate", 64)
    update("w_up", 64)
    update("w_down", 64)
    recv[0]["w_in"], recv[0]["w_out"] = _copies_wait(flight_a, out_g["w_down"], "l0_b_scatter_a_wait")
    update("w_in", 128, w_in_offset)
    update("w_out", 128)
    for res in (out_g, out_d, out_m, out_v):
        res["w_gate"], res["w_up"] = jnp.swapaxes(res["w_gate"], 1, 2), jnp.swapaxes(res["w_up"], 1, 2)

    small_g = [jnp.concatenate([grads[li][k] for li in range(depth)], axis=0) for k in SMALL]
    conv_g = jnp.stack([grads[li]["conv_w"] for li in range(depth)]).reshape(depth, CONV_K * CONV_DIM)
    parts = _all_gather(small_g + [conv_g], "gather_small_grads")
    res = _adamw_small(parts[:-1], [w[k] for k in SMALL], [mom[k] for k in SMALL], [var[k] for k in SMALL],
                       "adamw_small")
    for dst, vals in zip((out_g, out_d, out_m, out_v), res):
        dst.update(dict(zip(SMALL, vals)))
    conv_total = _sum_parts(parts[-1], "sum_conv_grads").reshape(depth, CONV_K, CONV_DIM)
    cshard = conv_w.shape[2]
    conv_mine = lax.dynamic_slice_in_dim(conv_total, _my_block() * cshard, cshard, axis=2)
    flat = lambda a: a.reshape(depth, CONV_K * cshard)
    res = _adamw_small([flat(conv_mine)[None]], [flat(conv_w)], [flat(m_conv_w)], [flat(v_conv_w)], "adamw_conv")
    for dst, vals in zip((out_g, out_d, out_m, out_v), res):
        dst["conv_w"] = vals[0].reshape(depth, CONV_K, cshard)

    names = ("norm_mix", "w_in", "q_gain", "k_gain", "conv_w", "conv_b", "dt_bias", "a_log", "d_skip",
             "attn_out_gain", "ssm_out_gain", "w_out", "norm_ffn", "w_gate", "w_up", "w_down")
    return (loss, grad_x, *[out_g[k] for k in names], *[out_d[k] for k in names],
            *[out_m[k] for k in names], *[out_v[k] for k in names])
```

```python
import functools
import math

import jax
import jax.numpy as jnp
from jax import lax
from jax.experimental import pallas as pl
from jax.experimental.pallas import tpu as pltpu

F32 = jnp.float32
BF16 = jnp.bfloat16
MESH = pl.DeviceIdType.MESH

N_DEV = 8
EPS = 1e-6
ATT_HEADS = 8
ATT_DH = 128
ATT_W = ATT_HEADS * ATT_DH
SSM_W = 1024
SSM_P = 64
SSM_N = 128
SSM_GROUPS = 2
SSM_HG = 8
SSM_HEADS = SSM_GROUPS * SSM_HG
CHUNK = 128
CONV_K = 4
CONV_DIM = SSM_W + 2 * SSM_GROUPS * SSM_N
LANE = 128
OFF_Q, OFF_K, OFF_V, OFF_Z, OFF_XS = 0, ATT_W, 2 * ATT_W, 3 * ATT_W, 4 * ATT_W
OFF_B = OFF_XS + SSM_W
OFF_C = OFF_B + SSM_GROUPS * SSM_N
OFF_DT = OFF_C + SSM_GROUPS * SSM_N
NPROJ = 6144
IN_DIM = OFF_DT + SSM_HEADS

ADAM_LR = 0.001
ADAM_B1 = 0.9
ADAM_B2 = 0.999
ADAM_EPS = 1e-08
ADAM_WD = 0.01
ADAM_STEP = 10

VMEM_LIMIT = 56 * 1024 * 1024
MATMUL_OPERAND_BYTES = 26 * 1024 * 1024


def _params(sem=None):
    return pltpu.CompilerParams(dimension_semantics=sem, vmem_limit_bytes=VMEM_LIMIT)


def _pick(dim, target):
    if dim <= target:
        return dim
    best = None
    for t in range(LANE, target + 1, LANE):
        if dim % t == 0:
            best = t
    assert best is not None, (dim, target)
    return best


def _dot(a, b, dims=((1,), (0,))):
    return lax.dot_general(a, b, (dims, ((), ())), preferred_element_type=F32)


def _dot_nt(a, b):
    return _dot(a, b, ((1,), (1,)))


def _dot_tn(a, b):
    return _dot(a, b, ((0,), (0,)))


def _sigmoid(x):
    return 1.0 / (1.0 + jnp.exp(-x))


def _softplus(x):
    return jnp.maximum(x, 0.0) + jnp.log(1.0 + jnp.exp(-jnp.abs(x)))


def _rstd(x):
    return lax.rsqrt(jnp.mean(x * x, axis=-1, keepdims=True) + EPS)


def _matmul(a, b, mode, out_dtype, name, residual=None, after=None, tm=512, tn=1024, tk=2048):
    if mode == "nn":
        (m, k), (k2, n) = a.shape, b.shape
    elif mode == "nt":
        (m, k), (n, k2) = a.shape, b.shape
    else:
        (k, m), (k2, n) = a.shape, b.shape
    assert k == k2, (a.shape, b.shape, mode)
    tm, tn, tk = _pick(m, tm), _pick(n, tn), _pick(k, tk)
    for cand_tn in (tn, _pick(n, tn // 2)):
        if 2 * 2 * (tm * k + k * cand_tn) <= MATMUL_OPERAND_BYTES:
            tn, tk = cand_tn, k
            break
    nk = k // tk
    dims = {"nn": ((1,), (0,)), "nt": ((1,), (1,)), "tn": ((0,), (0,))}[mode]
    has_res = residual is not None

    has_tok = after is not None

    def body(*refs):
        a_ref, b_ref = refs[:2]
        r_ref = refs[2] if has_res else None
        o_ref = refs[2 + has_res + has_tok]
        prod = _dot(a_ref[...], b_ref[...], dims)

        def finish(r):
            if r_ref is not None:
                r = r + r_ref[...]
            o_ref[...] = r.astype(o_ref.dtype)

        if nk == 1:
            finish(prod)
        else:
            acc = refs[-1]
            kk = pl.program_id(2)

            @pl.when(kk == 0)
            def _():
                acc[...] = prod

            @pl.when(kk > 0)
            def _():
                acc[...] += prod

            @pl.when(kk == nk - 1)
            def _():
                finish(acc[...])

    if mode == "tn":
        a_spec = pl.BlockSpec((tk, tm), lambda i, j, kk: (kk, i))
    else:
        a_spec = pl.BlockSpec((tm, tk), lambda i, j, kk: (i, kk))
    if mode == "nt":
        b_spec = pl.BlockSpec((tn, tk), lambda i, j, kk: (j, kk))
    else:
        b_spec = pl.BlockSpec((tk, tn), lambda i, j, kk: (kk, j))
    o_spec = pl.BlockSpec((tm, tn), lambda i, j, kk: (i, j))
    tok_spec = pl.BlockSpec((8, LANE), lambda i, j, kk: (0, 0))
    in_specs = [a_spec, b_spec] + ([o_spec] if has_res else []) + ([tok_spec] if has_tok else [])
    args = (a, b) + ((residual,) if has_res else ()) + ((after,) if has_tok else ())
    return pl.pallas_call(
        body,
        name=name,
        grid=(m // tm, n // tn, nk),
        in_specs=in_specs,
        out_specs=o_spec,
        out_shape=jax.ShapeDtypeStruct((m, n), out_dtype),
        scratch_shapes=[pltpu.VMEM((tm, tn), F32)] if nk > 1 else [],
        compiler_params=_params(("parallel", "parallel", "arbitrary")),
    )(*args)


def _mm_swiglu(h, wg_t, wu_t, name, tm=512, tn=1024):
    m, k = h.shape
    n = wg_t.shape[0]
    tm, tn = _pick(m, tm), _pick(n, tn)

    def body(h_ref, g_ref, u_ref, a_ref, gs_ref, us_ref):
        hv = h_ref[...]
        g = _dot_nt(hv, g_ref[...])
        u = _dot_nt(hv, u_ref[...])
        a_ref[...] = (g * _sigmoid(g) * u).astype(BF16)
        gs_ref[...] = g.astype(BF16)
        us_ref[...] = u.astype(BF16)

    w_spec = pl.BlockSpec((tn, k), lambda i, j: (j, 0))
    o_spec = pl.BlockSpec((tm, tn), lambda i, j: (i, j))
    out = jax.ShapeDtypeStruct((m, n), BF16)
    return pl.pallas_call(
        body, name=name, grid=(m // tm, n // tn),
        in_specs=[pl.BlockSpec((tm, k), lambda i, j: (i, 0)), w_spec, w_spec],
        out_specs=[o_spec] * 3, out_shape=[out] * 3,
        compiler_params=_params(("parallel", "parallel")),
    )(h, wg_t, wu_t)


def _mm_dact_swiglu(dx, wd, gs, us, name, after=None, tm=512, tn=1024):
    m, k = dx.shape
    n = wd.shape[0]
    tm, tn = _pick(m, tm), _pick(n, tn)
    has_tok = after is not None

    def body(*refs):
        dx_ref, wd_ref, g_ref, u_ref = refs[:4]
        dg_ref, du_ref = refs[-2:]
        dact = _dot_nt(dx_ref[...], wd_ref[...])
        g = g_ref[...].astype(F32)
        sg = _sigmoid(g)
        dg_ref[...] = (dact * u_ref[...].astype(F32) * sg * (1.0 + g * (1.0 - sg))).astype(BF16)
        du_ref[...] = (dact * g * sg).astype(BF16)

    o_spec = pl.BlockSpec((tm, tn), lambda i, j: (i, j))
    tok = [pl.BlockSpec((8, LANE), lambda i, j: (0, 0))] if has_tok else []
    out = jax.ShapeDtypeStruct((m, n), BF16)
    return pl.pallas_call(
        body, name=name, grid=(m // tm, n // tn),
        in_specs=[pl.BlockSpec((tm, k), lambda i, j: (i, 0)), pl.BlockSpec((tn, k), lambda i, j: (j, 0)),
                  o_spec, o_spec] + tok,
        out_specs=[o_spec] * 2, out_shape=[out] * 2,
        compiler_params=_params(("parallel", "parallel")),
    )(dx, wd, gs, us, *((after,) if has_tok else ()))


ROWS = 512


def _rms_fwd(x, g, name, after=None):
    t, d = x.shape
    has_tok = after is not None

    def body(*refs):
        x_ref, g_ref, o_ref = refs[0], refs[1], refs[-1]
        xv = x_ref[...]
        o_ref[...] = (xv * _rstd(xv) * g_ref[...]).astype(BF16)

    row = pl.BlockSpec((ROWS, d), lambda i: (i, 0))
    tok = [pl.BlockSpec((8, LANE), lambda i: (0, 0))] if has_tok else []
    return pl.pallas_call(
        body, name=name, grid=(t // ROWS,),
        in_specs=[row, pl.BlockSpec((1, d), lambda i: (0, 0))] + tok,
        out_specs=row, out_shape=jax.ShapeDtypeStruct((t, d), BF16),
        compiler_params=_params(("parallel",)),
    )(x, g, *((after,) if has_tok else ()))


def _rms_bwd(x, g, dh, dres, name):
    t, d = x.shape

    def body(x_ref, g_ref, dh_ref, dr_ref, dx_ref, dxb_ref, dg_ref):
        xv = x_ref[...]
        r = _rstd(xv)
        xh = xv * r
        dhv = dh_ref[...]

        @pl.when(pl.program_id(0) == 0)
        def _():
            dg_ref[...] = jnp.zeros_like(dg_ref)

        dg_ref[...] += jnp.sum(dhv * xh, axis=0, keepdims=True)
        dxh = dhv * g_ref[...]
        dx = r * (dxh - xh * jnp.mean(dxh * xh, axis=-1, keepdims=True)) + dr_ref[...]
        dx_ref[...] = dx
        dxb_ref[...] = dx.astype(BF16)

    row = pl.BlockSpec((ROWS // 2, d), lambda i: (i, 0))
    vec = pl.BlockSpec((1, d), lambda i: (0, 0))
    return pl.pallas_call(
        body, name=name, grid=(t // (ROWS // 2),),
        in_specs=[row, vec, row, row],
        out_specs=[row, row, vec],
        out_shape=[jax.ShapeDtypeStruct((t, d), F32), jax.ShapeDtypeStruct((t, d), BF16),
                   jax.ShapeDtypeStruct((1, d), F32)],
        compiler_params=_params(("arbitrary",)),
    )(x, g, dh, dres)


def _loss_fwd_bwd(y, target, name):
    t, d = y.shape
    inv = 1.0 / d

    def body(y_ref, t_ref, l_ref, dy_ref, dyb_ref):
        e = y_ref[...] - t_ref[...]

        @pl.when(pl.program_id(0) == 0)
        def _():
            l_ref[...] = jnp.zeros_like(l_ref)

        l_ref[...] += 0.5 * inv * jnp.sum(e * e)
        dy = e * inv
        dy_ref[...] = dy
        dyb_ref[...] = dy.astype(BF16)

    row = pl.BlockSpec((ROWS, d), lambda i: (i, 0))
    return pl.pallas_call(
        body, name=name, grid=(t // ROWS,),
        in_specs=[row, row],
        out_specs=[pl.BlockSpec((8, LANE), lambda i: (0, 0)), row, row],
        out_shape=[jax.ShapeDtypeStruct((8, LANE), F32), jax.ShapeDtypeStruct((t, d), F32),
                   jax.ShapeDtypeStruct((t, d), BF16)],
        compiler_params=_params(("arbitrary",)),
    )(y, target)


MIX_ROWS = 256


def _mix_fwd(o, y, proj, ga, gs, name):
    t = o.shape[0]
    gw = SSM_W // SSM_GROUPS

    def body(o_ref, y_ref, z_ref, ga_ref, gs_ref, c_ref):
        ov = o_ref[...]
        c_ref[:, 0:ATT_W] = (ov * _rstd(ov) * ga_ref[...]).astype(BF16)
        zv = z_ref[...]
        yz = y_ref[...] * (zv * _sigmoid(zv))
        for gi in range(SSM_GROUPS):
            seg = yz[:, gi * gw:(gi + 1) * gw]
            c_ref[:, ATT_W + gi * gw:ATT_W + (gi + 1) * gw] = (
                seg * _rstd(seg) * gs_ref[:, gi * gw:(gi + 1) * gw]).astype(BF16)

    half = pl.BlockSpec((MIX_ROWS, ATT_W), lambda i: (i, 0))
    vec = pl.BlockSpec((1, ATT_W), lambda i: (0, 0))
    return pl.pallas_call(
        body, name=name, grid=(t // MIX_ROWS,),
        in_specs=[half, half, pl.BlockSpec((MIX_ROWS, ATT_W), lambda i: (i, OFF_Z // ATT_W)), vec, vec],
        out_specs=pl.BlockSpec((MIX_ROWS, 2 * ATT_W), lambda i: (i, 0)),
        out_shape=jax.ShapeDtypeStruct((t, 2 * ATT_W), BF16),
        compiler_params=_params(("parallel",)),
    )(o, y, proj, ga, gs)


def _mix_bwd(dcat, o, y, proj, ga, gs, name):
    t = o.shape[0]
    gw = SSM_W // SSM_GROUPS

    def body(dc_ref, o_ref, y_ref, z_ref, ga_ref, gs_ref, do_ref, dy_ref, dz_ref, dga_ref, dgs_ref):
        @pl.when(pl.program_id(0) == 0)
        def _():
            dga_ref[...] = jnp.zeros_like(dga_ref)
            dgs_ref[...] = jnp.zeros_like(dgs_ref)

        ov = o_ref[...]
        r = _rstd(ov)
        oh = ov * r
        d_on = dc_ref[:, 0:ATT_W]
        dga_ref[...] += jnp.sum(d_on * oh, axis=0, keepdims=True)
        doh = d_on * ga_ref[...]
        do_ref[...] = r * (doh - oh * jnp.mean(doh * oh, axis=-1, keepdims=True))

        zv = z_ref[...]
        yv = y_ref[...]
        sz = _sigmoid(zv)
        silu = zv * sz
        yz = yv * silu
        for gi in range(SSM_GROUPS):
            sl = slice(gi * gw, (gi + 1) * gw)
            seg = yz[:, sl]
            rg = _rstd(seg)
            yh = seg * rg
            dyn = dc_ref[:, ATT_W + gi * gw:ATT_W + (gi + 1) * gw]
            dgs_ref[:, sl] += jnp.sum(dyn * yh, axis=0, keepdims=True)
            dyh = dyn * gs_ref[:, sl]
            dyz = rg * (dyh - yh * jnp.mean(dyh * yh, axis=-1, keepdims=True))
            dy_ref[:, sl] = dyz * silu[:, sl]
            dz_ref[:, sl] = (dyz * yv[:, sl] * (sz[:, sl] * (1.0 + zv[:, sl] * (1.0 - sz[:, sl])))).astype(BF16)

    half = pl.BlockSpec((MIX_ROWS, ATT_W), lambda i: (i, 0))
    vec = pl.BlockSpec((1, ATT_W), lambda i: (0, 0))
    return pl.pallas_call(
        body, name=name, grid=(t // MIX_ROWS,),
        in_specs=[pl.BlockSpec((MIX_ROWS, 2 * ATT_W), lambda i: (i, 0)), half, half,
                  pl.BlockSpec((MIX_ROWS, ATT_W), lambda i: (i, OFF_Z // ATT_W)), vec, vec],
        out_specs=[half, half, half, vec, vec],
        out_shape=[jax.ShapeDtypeStruct((t, ATT_W), F32), jax.ShapeDtypeStruct((t, SSM_W), F32),
                   jax.ShapeDtypeStruct((t, SSM_W), BF16), jax.ShapeDtypeStruct((1, ATT_W), F32),
                   jax.ShapeDtypeStruct((1, SSM_W), F32)],
        compiler_params=_params(("arbitrary",)),
    )(dcat, o, y, proj, ga, gs)


ATT_QB = 256
ATT_KB = 256
assert ATT_QB == ATT_KB


def _stacked(m):
    return jnp.concatenate([m, m], axis=0)


def _split_sum(x, m2):
    hi = x.astype(BF16)
    lo = (x - hi.astype(F32)).astype(BF16)
    return _dot(jnp.concatenate([hi, lo], axis=1), m2)


def _att_tile(z, mask, m_strict2, carry):
    lse = jnp.log(1.0 + jnp.exp(-jnp.abs(z)))
    lb = jnp.minimum(z, 0.0) - lse
    lrm = -jnp.maximum(z, 0.0) - lse
    if mask is not None:
        lrm = jnp.where(mask, lrm, 0.0)
    w = jnp.exp(lb + _split_sum(lrm, m_strict2) + carry)
    if mask is not None:
        w = jnp.where(mask, w, 0.0)
    return lb, lrm, w


ATT_HP = 2


def _head_spec(seq, off):
    width = ATT_HP * ATT_DH
    per = ATT_HEADS // ATT_HP
    return pl.BlockSpec((seq, width), lambda s: (s // per, off // width + s % per))


def _head_lanes(h):
    return slice(h * ATT_DH, (h + 1) * ATT_DH)


def _attn_fwd(proj, gq, gk, nb, seq, name):
    nq = seq // ATT_QB
    scale = ATT_DH ** -0.5
    heads = range(ATT_HP)

    def body(q_ref, k_ref, v_ref, gq_ref, gk_ref, o_ref, qs, kn, vb):
        for h in heads:
            sl = _head_lanes(h)
            qv = q_ref[:, sl]
            kv = k_ref[:, sl]
            qs[:, sl] = (qv * _rstd(qv) * gq_ref[...] * scale).astype(BF16)
            kn[:, sl] = (kv * _rstd(kv) * gk_ref[...]).astype(BF16)
            vb[:, sl] = v_ref[:, sl].astype(BF16)
        row = lax.broadcasted_iota(jnp.int32, (ATT_QB, ATT_KB), 0)
        col = lax.broadcasted_iota(jnp.int32, (ATT_QB, ATT_KB), 1)
        m_strict2 = _stacked((row > col).astype(BF16))
        diagonal = col < row

        def key_rows(kj):
            return pl.ds(pl.multiple_of(kj * ATT_KB, ATT_KB), ATT_KB)

        def q_loop(qi, _):
            q0 = pl.multiple_of(qi * ATT_QB, ATT_QB)
            q_ts = [qs[pl.ds(q0, ATT_QB), _head_lanes(h)] for h in heads]

            def scores(h, kj):
                return _dot_nt(q_ts[h], kn[key_rows(kj), _head_lanes(h)])

            def tile(c, kj, mask):
                rows = key_rows(kj)
                out = []
                for h in heads:
                    acc, carry, z = c[h]
                    z_next = scores(h, jnp.maximum(kj - 1, 0))
                    _, lrm, w = _att_tile(z, mask, m_strict2, carry)
                    acc = acc + _dot(w.astype(BF16), vb[rows, _head_lanes(h)])
                    out.append((acc, carry + jnp.sum(lrm, axis=-1, keepdims=True), z_next))
                return tuple(out)

            init = tuple((jnp.zeros((ATT_QB, ATT_DH), F32), jnp.zeros((ATT_QB, 1), F32), scores(h, qi)) for h in heads)
            res = lax.fori_loop(1, qi + 1, lambda i, c: tile(c, qi - i, None), tile(init, qi, diagonal))
            for h in heads:
                o_ref[pl.ds(q0, ATT_QB), _head_lanes(h)] = res[h][0]
            return 0

        lax.fori_loop(0, nq, q_loop, 0)

    vec = pl.BlockSpec((1, ATT_DH), lambda s: (0, 0))
    return pl.pallas_call(
        body, name=name, grid=(nb * ATT_HEADS // ATT_HP,),
        in_specs=[_head_spec(seq, OFF_Q), _head_spec(seq, OFF_K), _head_spec(seq, OFF_V), vec, vec],
        out_specs=_head_spec(seq, 0),
        out_shape=jax.ShapeDtypeStruct((nb * seq, ATT_W), F32),
        scratch_shapes=[pltpu.VMEM((seq, ATT_HP * ATT_DH), BF16)] * 3,
        compiler_params=_params(("parallel",)),
    )(proj, proj, proj, gq, gk)


def _attn_bwd(proj, do, gq, gk, nb, seq, name):
    nq = seq // ATT_QB
    nk = seq // ATT_KB
    scale = ATT_DH ** -0.5
    heads = range(ATT_HP)

    def body(q_ref, k_ref, v_ref, do_ref, gq_ref, gk_ref, dq_ref, dk_ref, dv_ref, dgq_ref, dgk_ref,
             qs, kn, vb, dob, dq_acc, dk_acc, dv_acc, gbuf, bbuf):
        @pl.when(pl.program_id(0) == 0)
        def _():
            dgq_ref[...] = jnp.zeros_like(dgq_ref)
            dgk_ref[...] = jnp.zeros_like(dgk_ref)

        for h in heads:
            sl = _head_lanes(h)
            qv = q_ref[:, sl]
            kv = k_ref[:, sl]
            qs[:, sl] = (qv * _rstd(qv) * gq_ref[...] * scale).astype(BF16)
            kn[:, sl] = (kv * _rstd(kv) * gk_ref[...]).astype(BF16)
            vb[:, sl] = v_ref[:, sl].astype(BF16)
            dob[:, sl] = do_ref[:, sl].astype(BF16)
        dk_acc[...] = jnp.zeros_like(dk_acc)
        dv_acc[...] = jnp.zeros_like(dv_acc)
        row = lax.broadcasted_iota(jnp.int32, (ATT_QB, ATT_KB), 0)
        col = lax.broadcasted_iota(jnp.int32, (ATT_QB, ATT_KB), 1)
        m_strict2 = _stacked((row > col).astype(BF16))
        m_prefix2 = _stacked((row < col).astype(BF16))
        diagonal = col < row

        def key_rows(kj):
            return pl.ds(pl.multiple_of(kj * ATT_KB, ATT_KB), ATT_KB)

        def q_loop(qi, _):
            q0 = pl.multiple_of(qi * ATT_QB, ATT_QB)
            q_ts = [qs[pl.ds(q0, ATT_QB), _head_lanes(h)] for h in heads]
            do_ts = [dob[pl.ds(q0, ATT_QB), _head_lanes(h)] for h in heads]

            def scores(h, kj):
                return _dot_nt(q_ts[h], kn[key_rows(kj), _head_lanes(h)])

            def down(c, kj, mask):
                rows = key_rows(kj)
                out = []
                for h in heads:
                    carry, z = c[h]
                    sl = _head_lanes(h)
                    z_next = scores(h, jnp.maximum(kj - 1, 0))
                    lb, lrm, w = _att_tile(z, mask, m_strict2, carry)
                    dw = _dot_nt(do_ts[h], vb[rows, sl])
                    gbuf[h * nk + kj] = w * dw
                    bbuf[h * nk + kj] = jnp.exp(lb)
                    dv_acc[rows, sl] += _dot_tn(w.astype(BF16), do_ts[h])
                    out.append((carry + jnp.sum(lrm, axis=-1, keepdims=True), z_next))
                return tuple(out)

            init = tuple((jnp.zeros((ATT_QB, 1), F32), scores(h, qi)) for h in heads)
            lax.fori_loop(1, qi + 1, lambda i, c: down(c, qi - i, None), down(init, qi, diagonal))

            def up(c, kj, mask):
                rows = key_rows(kj)
                out = []
                for h in heads:
                    acc, carry, within = c[h]
                    sl = _head_lanes(h)
                    g = gbuf[h * nk + kj]
                    beta = bbuf[h * nk + kj]
                    within_next = _split_sum(gbuf[h * nk + jnp.minimum(kj + 1, qi)], m_prefix2)
                    dz = g * (1.0 - beta) - (within + carry) * beta
                    if mask is not None:
                        dz = jnp.where(mask, dz, 0.0)
                    dz = dz.astype(BF16)
                    acc = acc + _dot(dz, kn[rows, sl])
                    dk_acc[rows, sl] += _dot_tn(dz, q_ts[h])
                    out.append((acc, carry + jnp.sum(g, axis=-1, keepdims=True), within_next))
                return tuple(out)

            init = tuple((jnp.zeros((ATT_QB, ATT_DH), F32), jnp.zeros((ATT_QB, 1), F32),
                          _split_sum(gbuf[h * nk], m_prefix2)) for h in heads)
            res = up(lax.fori_loop(0, qi, lambda kj, c: up(c, kj, None), init), qi, diagonal)
            for h in heads:
                dq_acc[pl.ds(q0, ATT_QB), _head_lanes(h)] = res[h][0]
            return 0

        lax.fori_loop(0, nq, q_loop, 0)

        def norm_bwd(xv, gain, dyn):
            r = _rstd(xv)
            xh = xv * r
            dgain = jnp.sum(dyn * xh, axis=0, keepdims=True)
            dxh = dyn * gain
            return r * (dxh - xh * jnp.mean(dxh * xh, axis=-1, keepdims=True)), dgain

        for h in heads:
            sl = _head_lanes(h)
            dq, dgq = norm_bwd(q_ref[:, sl], gq_ref[...], dq_acc[:, sl] * scale)
            dk, dgk = norm_bwd(k_ref[:, sl], gk_ref[...], dk_acc[:, sl])
            dq_ref[:, sl] = dq.astype(BF16)
            dk_ref[:, sl] = dk.astype(BF16)
            dv_ref[:, sl] = dv_acc[:, sl].astype(BF16)
            dgq_ref[...] += dgq
            dgk_ref[...] += dgk

    vec = pl.BlockSpec((1, ATT_DH), lambda s: (0, 0))
    big = jax.ShapeDtypeStruct((nb * seq, ATT_W), BF16)
    small = jax.ShapeDtypeStruct((1, ATT_DH), F32)
    width = ATT_HP * ATT_DH
    return pl.pallas_call(
        body, name=name, grid=(nb * ATT_HEADS // ATT_HP,),
        in_specs=[_head_spec(seq, OFF_Q), _head_spec(seq, OFF_K), _head_spec(seq, OFF_V), _head_spec(seq, 0), vec, vec],
        out_specs=[_head_spec(seq, 0)] * 3 + [vec, vec],
        out_shape=[big, big, big, small, small],
        scratch_shapes=[pltpu.VMEM((seq, width), BF16)] * 4 + [pltpu.VMEM((seq, width), F32)] * 3
        + [pltpu.VMEM((ATT_HP * nk, ATT_QB, ATT_KB), F32)] * 2,
        compiler_params=_params(("arbitrary",)),
    )(proj, proj, proj, do, gq, gk)


CONV_COLS = 256


def _pack_conv(conv_w, conv_b):
    return jnp.concatenate([conv_w, conv_b[None, :], jnp.zeros((3, CONV_DIM), F32)], axis=0)


def _conv_pre(raw, w8, rowi):
    pre = w8[CONV_K:CONV_K + 1, :] + raw * w8[CONV_K - 1:CONV_K, :]
    for k in range(1, CONV_K):
        sh = jnp.where(rowi >= k, pltpu.roll(raw, k, 0), 0.0)
        pre = pre + sh * w8[CONV_K - 1 - k:CONV_K - k, :]
    return pre


def _conv_fwd(proj, cw8, nb, seq, name):
    ncol = CONV_DIM // CONV_COLS

    def body(x_ref, w_ref, o_ref):
        rowi = lax.broadcasted_iota(jnp.int32, (seq, 1), 0)
        pre = _conv_pre(x_ref[...], w_ref[...], rowi)
        o_ref[...] = pre * _sigmoid(pre)

    return pl.pallas_call(
        body, name=name, grid=(nb, ncol),
        in_specs=[pl.BlockSpec((seq, CONV_COLS), lambda b, j: (b, OFF_XS // CONV_COLS + j)),
                  pl.BlockSpec((8, CONV_COLS), lambda b, j: (0, j))],
        out_specs=pl.BlockSpec((seq, CONV_COLS), lambda b, j: (b, j)),
        out_shape=jax.ShapeDtypeStruct((nb * seq, CONV_DIM), F32),
        compiler_params=_params(("parallel", "parallel")),
    )(proj, cw8)


def _conv_bwd(proj, dact, cw8, nb, seq, name):
    ncol = CONV_DIM // CONV_COLS

    def body(x_ref, d_ref, w_ref, dx_ref, dw_ref):
        @pl.when(pl.program_id(1) == 0)
        def _():
            dw_ref[...] = jnp.zeros_like(dw_ref)

        rowi = lax.broadcasted_iota(jnp.int32, (seq, 1), 0)
        raw = x_ref[...]
        w8 = w_ref[...]
        pre = _conv_pre(raw, w8, rowi)
        sg = _sigmoid(pre)
        dpre = d_ref[...] * (sg * (1.0 + pre * (1.0 - sg)))
        dw_ref[CONV_K:CONV_K + 1, :] += jnp.sum(dpre, axis=0, keepdims=True)
        dw_ref[CONV_K - 1:CONV_K, :] += jnp.sum(dpre * raw, axis=0, keepdims=True)
        draw = dpre * w8[CONV_K - 1:CONV_K, :]
        for k in range(1, CONV_K):
            sh = jnp.where(rowi >= k, pltpu.roll(raw, k, 0), 0.0)
            dw_ref[CONV_K - 1 - k:CONV_K - k, :] += jnp.sum(dpre * sh, axis=0, keepdims=True)
            up = jnp.where(rowi < seq - k, pltpu.roll(dpre, seq - k, 0), 0.0)
            draw = draw + up * w8[CONV_K - 1 - k:CONV_K - k, :]
        dx_ref[...] = draw.astype(BF16)

    return pl.pallas_call(
        body, name=name, grid=(ncol, nb),
        in_specs=[pl.BlockSpec((seq, CONV_COLS), lambda j, b: (b, OFF_XS // CONV_COLS + j)),
                  pl.BlockSpec((seq, CONV_COLS), lambda j, b: (b, j)),
                  pl.BlockSpec((8, CONV_COLS), lambda j, b: (0, j))],
        out_specs=[pl.BlockSpec((seq, CONV_COLS), lambda j, b: (b, j)),
                   pl.BlockSpec((8, CONV_COLS), lambda j, b: (0, j))],
        out_shape=[jax.ShapeDtypeStruct((nb * seq, CONV_DIM), BF16), jax.ShapeDtypeStruct((8, CONV_DIM), F32)],
        compiler_params=_params(("parallel", "arbitrary")),
    )(proj, dact, cw8)


def _pack_heads(dt_bias, a_log, d_skip):
    rows = jnp.stack([dt_bias, a_log, d_skip]).reshape(3, SSM_GROUPS, SSM_HG).transpose(1, 0, 2)
    return jnp.pad(rows, ((0, 0), (0, 8 - 3), (0, LANE - SSM_HG)))


def _split3_rows(x):
    hi = x.astype(BF16)
    r1 = x - hi.astype(F32)
    mid = r1.astype(BF16)
    lo = (r1 - mid.astype(F32)).astype(BF16)
    return jnp.concatenate([hi, mid, lo], axis=0)


def _split3_cols(x):
    hi = x.astype(BF16)
    r1 = x - hi.astype(F32)
    mid = r1.astype(BF16)
    lo = (r1 - mid.astype(F32)).astype(BF16)
    return jnp.concatenate([hi, mid, lo], axis=1)


def _split2_rows(x):
    hi = x.astype(BF16)
    return jnp.concatenate([hi, (x - hi.astype(F32)).astype(BF16)], axis=0)


def _ssd_specs(seq):
    gx = SSM_HG * SSM_P
    return dict(
        xs=pl.BlockSpec((seq, gx), lambda g, b: (b, g)),
        bm=pl.BlockSpec((seq, SSM_N), lambda g, b: (b, SSM_W // SSM_N + g)),
        cm=pl.BlockSpec((seq, SSM_N), lambda g, b: (b, SSM_W // SSM_N + SSM_GROUPS + g)),
        dt=pl.BlockSpec((seq, LANE), lambda g, b: (b, OFF_DT // LANE)),
        hp=pl.BlockSpec((1, 8, LANE), lambda g, b: (g, 0, 0)),
        head=pl.BlockSpec((seq, gx), lambda g, b: (b, g)),
        grp=pl.BlockSpec((seq, SSM_N), lambda g, b: (b, g)),
    )


SSM_GX = SSM_HG * SSM_P


def _ssd_masks():
    li = lax.broadcasted_iota(jnp.int32, (CHUNK, CHUNK), 0)
    si = lax.broadcasted_iota(jnp.int32, (CHUNK, CHUNK), 1)
    head = lax.broadcasted_iota(jnp.int32, (LANE, SSM_GX), 0)
    lane = lax.broadcasted_iota(jnp.int32, (LANE, SSM_GX), 1)
    expand = (lane // SSM_P == head).astype(BF16)
    head_t = lax.broadcasted_iota(jnp.int32, (SSM_GX, LANE), 1)
    lane_t = lax.broadcasted_iota(jnp.int32, (SSM_GX, LANE), 0)
    gather = (lane_t // SSM_P == head_t).astype(BF16)
    return dict(
        causal=li >= si, causal_t=si >= li,
        tril3=jnp.concatenate([(si <= li).astype(BF16)] * 3, axis=1),
        triu2=jnp.concatenate([(si >= li).astype(BF16)] * 2, axis=1),
        below2=jnp.concatenate([(si < li).astype(BF16)] * 2, axis=1),
        expand2=_stacked(expand), gather2=_stacked(gather))


def _per_head(x, mk):
    return _split_sum(x, mk["expand2"])


def _head_sums(x, mk):
    return _split_sum(x, mk["gather2"])


def _row8(v):
    return jnp.broadcast_to(v, (8, v.shape[1]))


def _group_dt(dt_ref):
    shift = (LANE - SSM_HG * pl.program_id(0)) % LANE
    return pltpu.roll(dt_ref[...], shift, 1)


def _ssd_fwd(act, proj, hp, nb, seq, name):
    nc = seq // CHUNK

    def body(xs_ref, b_ref, c_ref, dt_ref, hp_ref, y_ref, dt_s, da_s, hst):
        mk = _ssd_masks()
        hpv = hp_ref[0]
        dt = _softplus(_group_dt(dt_ref) + hpv[0:1, :])
        a = -jnp.exp(hpv[1:2, :])
        dsk_row = _per_head(_row8(hpv[2:3, :]), mk)[0:1]
        dt_s[...] = dt
        da_s[...] = dt * a
        hst[...] = jnp.zeros_like(hst)

        def chunk(c, _):
            rows = pl.ds(pl.multiple_of(c * CHUNK, CHUNK), CHUNK)
            acol = _dot(mk["tril3"], _split3_rows(da_s[rows, :]))
            arow = acol.T
            alast = acol[CHUNK - 1:CHUNK, :]
            ea = _per_head(jnp.exp(acol), mk)
            eb = _per_head(jnp.exp(alast - acol), mk)
            el = _per_head(_row8(jnp.exp(alast)), mk)[0:1]
            bb = b_ref[rows, :].astype(BF16)
            cb = c_ref[rows, :].astype(BF16)
            cbm = _dot_nt(cb, bb)
            xc = xs_ref[rows, :]
            u = xc * _per_head(dt_s[rows, :], mk)
            ub = u.astype(BF16)
            ht = hst[...]
            y_ref[rows, :] = ea * _dot(cb, ht.astype(BF16)) + dsk_row * xc
            for j in range(SSM_HG):
                sl = slice(j * SSM_P, (j + 1) * SSM_P)
                decay = jnp.where(mk["causal"], jnp.exp(jnp.minimum(acol[:, j:j + 1] - arow[j:j + 1, :], 0.0)), 0.0)
                y_ref[rows, sl] += _dot((cbm * decay).astype(BF16), ub[:, sl])
            hst[...] = el * ht + _dot_tn(bb, (u * eb).astype(BF16))
            return 0

        lax.fori_loop(0, nc, chunk, 0)

    sp = _ssd_specs(seq)
    return pl.pallas_call(
        body, name=name, grid=(SSM_GROUPS, nb),
        in_specs=[sp["xs"], sp["bm"], sp["cm"], sp["dt"], sp["hp"]],
        out_specs=sp["head"],
        out_shape=jax.ShapeDtypeStruct((nb * seq, SSM_W), F32),
        scratch_shapes=[pltpu.VMEM((seq, LANE), F32)] * 2 + [pltpu.VMEM((SSM_N, SSM_GX), F32)],
        compiler_params=_params(("parallel", "parallel")),
    )(act, act, act, proj, hp)


def _ssd_bwd(act, proj, dy, hp, nb, seq, name):
    nc = seq // CHUNK

    def body(xs_ref, b_ref, c_ref, dt_ref, hp_ref, dy_ref, dxs_ref, db_ref, dc_ref, ddt_ref, dhp_ref,
             dt_s, da_s, ddt_s, hs, lam, du_s):
        @pl.when(pl.program_id(1) == 0)
        def _():
            dhp_ref[...] = jnp.zeros_like(dhp_ref)

        mk = _ssd_masks()
        hpv = hp_ref[0]
        a = -jnp.exp(hpv[1:2, :])
        dsk_row = _per_head(_row8(hpv[2:3, :]), mk)[0:1]
        dt_s[...] = _softplus(_group_dt(dt_ref) + hpv[0:1, :])
        da_s[...] = dt_s[...] * a
        lane = lax.broadcasted_iota(jnp.int32, (1, LANE), 1)

        def chunk_rows(c):
            return pl.ds(pl.multiple_of(c * CHUNK, CHUNK), CHUNK)

        def decays(c):
            acol = _dot(mk["tril3"], _split3_rows(da_s[chunk_rows(c), :]))
            alast = acol[CHUNK - 1:CHUNK, :]
            return acol, alast

        hs[0] = jnp.zeros((SSM_N, SSM_GX), F32)

        def fwd_chunk(c, _):
            rows = chunk_rows(c)
            acol, alast = decays(c)
            eb = _per_head(jnp.exp(alast - acol), mk)
            el = _per_head(_row8(jnp.exp(alast)), mk)[0:1]
            u = xs_ref[rows, :] * _per_head(dt_s[rows, :], mk)
            hs[c + 1] = el * hs[c] + _dot_tn(b_ref[rows, :].astype(BF16), (u * eb).astype(BF16))
            return 0

        lax.fori_loop(0, nc - 1, fwd_chunk, 0)
        lam[...] = jnp.zeros_like(lam)

        def bwd_chunk(i, carry):
            dd_row, da_vec = carry
            c = nc - 1 - i
            rows = chunk_rows(c)
            acol, alast = decays(c)
            arow = acol.T
            ea = _per_head(jnp.exp(acol), mk)
            eb = _per_head(jnp.exp(alast - acol), mk)
            el = _per_head(_row8(jnp.exp(alast)), mk)[0:1]
            dt_all = _per_head(dt_s[rows, :], mk)
            bb = b_ref[rows, :].astype(BF16)
            cb = c_ref[rows, :].astype(BF16)
            cbm = _dot_nt(cb, bb)
            cbt = _dot_nt(bb, cb)
            xc = xs_ref[rows, :]
            dyc = dy_ref[rows, :]
            u = xc * dt_all
            ub = u.astype(BF16)
            dyb = dyc.astype(BF16)
            h_in = hs[c]
            lm = lam[...]
            hb = h_in.astype(BF16)
            lb = lm.astype(BF16)
            y_off = ea * _dot(cb, hb)
            du_off = eb * _dot(bb, lb)
            dye = (ea * dyc).astype(BF16)
            zero = jnp.zeros((CHUNK, CHUNK), F32)
            dcb, dcbt, d_a = zero, zero, zero
            for j in range(SSM_HG):
                sl = slice(j * SSM_P, (j + 1) * SSM_P)
                seg = acol[:, j:j + 1] - arow[j:j + 1, :]
                decay = jnp.where(mk["causal"], jnp.exp(jnp.minimum(seg, 0.0)), 0.0)
                decay_t = jnp.where(mk["causal_t"], jnp.exp(jnp.minimum(-seg, 0.0)), 0.0)
                m = cbm * decay
                mt = cbt * decay_t
                dm = _dot_nt(dyb[:, sl], ub[:, sl])
                dmt = _dot_nt(ub[:, sl], dyb[:, sl])
                dcb = dcb + dm * decay
                dcbt = dcbt + dmt * decay_t
                du_s[:, sl] = _dot(mt.astype(BF16), dyb[:, sl])
                d_a_j = jnp.sum(dm * m, axis=-1, keepdims=True) - jnp.sum(dmt * mt, axis=-1, keepdims=True)
                d_a = jnp.where(lane == j, d_a_j, d_a)
            du = du_s[...] + du_off
            dxs_ref[rows, :] = du * dt_all + dsk_row * dyc
            dc_ref[rows, :] = _dot_nt(dye, hb) + _dot(dcb.astype(BF16), bb)
            db_ref[rows, :] = _dot_nt((eb * u).astype(BF16), lb) + _dot(dcbt.astype(BF16), cb)
            lam[...] = el * lm + _dot_tn(cb, dye)
            d_a = d_a + _head_sums(dyc * y_off, mk)
            f_a = _head_sums(du_off * u, mk)
            c_a = jnp.exp(alast) * _head_sums(_row8(jnp.sum(lm * h_in, axis=0, keepdims=True)), mk)[0:1]
            dda = _dot(mk["triu2"], _split2_rows(d_a)) + _dot(mk["below2"], _split2_rows(f_a)) + c_a
            ddt_s[rows, :] = dda * a + _head_sums(du * xc, mk)
            da_vec = da_vec + jnp.sum(dda * dt_s[rows, :], axis=0, keepdims=True)
            dd_row = dd_row + jnp.sum(dyc * xc, axis=0, keepdims=True)
            return dd_row, da_vec

        init = (jnp.zeros((1, SSM_GX), F32), jnp.zeros((1, LANE), F32))
        dd_row, da_vec = lax.fori_loop(0, nc, bwd_chunk, init)
        ddt_raw = ddt_s[...] * _sigmoid(_group_dt(dt_ref) + hpv[0:1, :])
        ddt_ref[...] = ddt_raw.astype(BF16)
        dhp_ref[0, 0:1, :] += jnp.sum(ddt_raw, axis=0, keepdims=True)
        dhp_ref[0, 1:2, :] += da_vec * a
        dhp_ref[0, 2:3, :] += _head_sums(_row8(dd_row), mk)[0:1]

    sp = _ssd_specs(seq)
    t = nb * seq
    return pl.pallas_call(
        body, name=name, grid=(SSM_GROUPS, nb),
        in_specs=[sp["xs"], sp["bm"], sp["cm"], sp["dt"], sp["hp"], sp["head"]],
        out_specs=[sp["head"], sp["grp"], sp["grp"], sp["grp"], sp["hp"]],
        out_shape=[jax.ShapeDtypeStruct((t, SSM_W), F32), jax.ShapeDtypeStruct((t, SSM_GROUPS * SSM_N), F32),
                   jax.ShapeDtypeStruct((t, SSM_GROUPS * SSM_N), F32),
                   jax.ShapeDtypeStruct((t, SSM_GROUPS * LANE), BF16),
                   jax.ShapeDtypeStruct((SSM_GROUPS, 8, LANE), F32)],
        scratch_shapes=[pltpu.VMEM((seq, LANE), F32)] * 3
        + [pltpu.VMEM((nc, SSM_N, SSM_GX), F32), pltpu.VMEM((SSM_N, SSM_GX), F32), pltpu.VMEM((CHUNK, SSM_GX), F32)],
        compiler_params=_params(("parallel", "arbitrary")),
    )(act, act, act, proj, hp, dy)


ANY = pl.BlockSpec(memory_space=pl.ANY)


def _block_index(p):
    return 4 * p[0] + 2 * p[1] + p[2]


def _all_gather(shards, name):
    n = len(shards)

    def body(*refs):
        ins, outs = refs[:n], refs[n:2 * n]
        send_sems, recv_sems, local_sems = refs[2 * n:]
        x, y, c = lax.axis_index("x"), lax.axis_index("y"), lax.axis_index("c")
        me, sibling = (x, y, c), (x, y, 1 - c)
        chips = [(1 - x, y), (x, 1 - y), (1 - x, 1 - y)]

        def copy(i, k, block, to, src=None):
            dst = outs[i].at[_block_index(block)]
            return pltpu.make_async_remote_copy(
                src_ref=dst if src is None else src, dst_ref=dst,
                send_sem=send_sems.at[i, k], recv_sem=recv_sems.at[i, k],
                device_id=to, device_id_type=MESH)

        mine = [pltpu.make_async_copy(ins[i], outs[i].at[_block_index(me)], local_sems.at[i]) for i in range(n)]
        for cp in mine:
            cp.start()
        first = []
        for i in range(n):
            first.append(copy(i, 0, me, sibling, src=ins[i]))
            first += [copy(i, 1 + j, me, (*chip, c), src=ins[i]) for j, chip in enumerate(chips)]
        for cp in first:
            cp.start()
        passed = []
        for j, chip in enumerate(chips):
            for i in range(n):
                copy(i, 1 + j, (*chip, c), me).wait_recv()
                fwd = copy(i, 4 + j, (*chip, c), sibling)
                fwd.start()
                passed.append(fwd)
        for i in range(n):
            copy(i, 0, sibling, me).wait_recv()
            for j, chip in enumerate(chips):
                copy(i, 4 + j, (*chip, 1 - c), me).wait_recv()
        for cp in first + passed:
            cp.wait_send()
        for cp in mine:
            cp.wait()

    return pl.pallas_call(
        body, name=name,
        in_specs=[ANY] * n, out_specs=[ANY] * n,
        out_shape=[jax.ShapeDtypeStruct((N_DEV,) + s.shape, s.dtype) for s in shards],
        scratch_shapes=[pltpu.SemaphoreType.DMA((n, 7)), pltpu.SemaphoreType.DMA((n, 7)),
                        pltpu.SemaphoreType.DMA((n,))],
    )(*shards)


HBM = pl.BlockSpec(memory_space=pltpu.HBM)
SEM = pl.BlockSpec(memory_space=pltpu.SEMAPHORE)
EFFECT = pltpu.SideEffectType.DATAFLOW_SIDE_EFFECTING


def _my_block():
    return _block_index((lax.axis_index("x"), lax.axis_index("y"), lax.axis_index("c")))


def _peer(k):
    x, y, c = lax.axis_index("x"), lax.axis_index("y"), lax.axis_index("c")
    return (1 - x if k & 4 else x, 1 - y if k & 2 else y, 1 - c if k & 1 else c)


ALL_PEERS = tuple(range(1, N_DEV))
SIBLING = 1
SAME_CORE = (2, 4, 6)


def _plan_copies(plan, src_refs, land_refs, send_sems, recv_sems, peers=ALL_PEERS):
    me = _my_block()
    copies = []
    for e, (si, di, src_view, dst_view, _) in enumerate(plan):
        for k in peers:
            copies.append(pltpu.make_async_remote_copy(
                src_ref=src_view(src_refs[si], _block_index(_peer(k))),
                dst_ref=dst_view(land_refs[di], me),
                send_sem=send_sems[e], recv_sem=recv_sems[e],
                device_id=_peer(k), device_id_type=MESH))
    return copies


def _plan_forwards(plan, land_refs, send_sems, recv_sems):
    copies = []
    for e, (_, di, _, dst_view, _) in enumerate(plan):
        for k in SAME_CORE:
            part = dst_view(land_refs[di], _block_index(_peer(k)))
            copies.append(pltpu.make_async_remote_copy(
                src_ref=part, dst_ref=part, send_sem=send_sems[e], recv_sem=recv_sems[e],
                device_id=_peer(SIBLING), device_id_type=MESH))
    return copies


def _plan_waits(plan, land_refs, send_sems, recv_sems, n=N_DEV - 1):
    waits = []
    for e, (_, di, _, _, parts_view) in enumerate(plan):
        view = parts_view(land_refs[di], n)
        waits.append(pltpu.make_async_remote_copy(
            src_ref=view, dst_ref=view, send_sem=send_sems[e], recv_sem=recv_sems[e],
            device_id=_peer(SIBLING), device_id_type=MESH))
    return waits


def _plan_own(plan, src_refs, land_refs, own_sems):
    me = _my_block()
    return [pltpu.make_async_copy(src_view(src_refs[si], me), dst_view(land_refs[di], me), own_sems[e])
            for e, (si, di, src_view, dst_view, _) in enumerate(plan)]


def _copies_start(srcs, lands, plan, name, after=None):
    ns, nl, ne = len(srcs), len(lands), len(plan)
    extra = [] if after is None else [after]

    nin = ns + nl + len(extra)

    def body(*refs):
        src_refs, land_refs = refs[:ns], refs[ns:ns + nl]
        send_sems, recv_sems = refs[nin:nin + ne], refs[nin + ne:nin + 2 * ne]
        own_sems = refs[nin + 2 * ne:nin + 3 * ne]
        token = refs[-1]
        for cp in _plan_copies(plan, src_refs, land_refs, send_sems, recv_sems):
            cp.start()
        for cp in _plan_own(plan, src_refs, land_refs, own_sems):
            cp.start()
        token[...] = jnp.zeros_like(token)

    thru = [pltpu.HBM(a.shape, a.dtype) for a in list(srcs) + list(lands)]
    res = pl.pallas_call(
        body, name=name,
        in_specs=[HBM] * (ns + nl) + [ANY] * len(extra),
        out_specs=[SEM] * (3 * ne) + [HBM] * (ns + nl) + [pl.BlockSpec(memory_space=pltpu.VMEM)],
        out_shape=[pltpu.SemaphoreType.DMA(())] * (3 * ne) + thru + [jax.ShapeDtypeStruct((8, LANE), F32)],
        input_output_aliases={i: 3 * ne + i for i in range(ns + nl)},
        compiler_params=pltpu.CompilerParams(has_side_effects=EFFECT),
    )(*[pltpu.with_memory_space_constraint(a, pltpu.HBM) for a in list(srcs) + list(lands)], *extra)
    return dict(sems=res[:3 * ne], srcs=res[3 * ne:3 * ne + ns], lands=res[3 * ne + ns:3 * ne + ns + nl],
                token=res[-1], plan=plan)


def _copies_wait(flight, after, name):
    srcs, lands, plan = flight["srcs"], flight["lands"], flight["plan"]
    ns, nl, ne = len(srcs), len(lands), len(plan)

    def body(*refs):
        src_refs, land_refs = refs[:ns], refs[ns:ns + nl]
        send_sems, recv_sems = refs[ns + nl:ns + nl + ne], refs[ns + nl + ne:ns + nl + 2 * ne]
        own_sems = refs[ns + nl + 2 * ne:ns + nl + 3 * ne]
        for cp in _plan_waits(plan, land_refs, send_sems, recv_sems):
            cp.wait_send()
            cp.wait_recv()
        for cp in _plan_own(plan, src_refs, land_refs, own_sems):
            cp.wait()

    after = list(after) if isinstance(after, (list, tuple)) else [after]
    thru = [pltpu.HBM(a.shape, a.dtype) for a in list(srcs) + list(lands)]
    res = pl.pallas_call(
        body, name=name,
        in_specs=[HBM] * (ns + nl) + [SEM] * (3 * ne) + [ANY] * len(after),
        out_specs=[HBM] * (ns + nl),
        out_shape=thru,
        input_output_aliases={i: i for i in range(ns + nl)},
        compiler_params=pltpu.CompilerParams(has_side_effects=EFFECT),
    )(*srcs, *lands, *flight["sems"], *after)
    return list(res[ns:])


def _gather2_start(srcs, lands, plan, name, after=None):
    ns, nl, ne = len(srcs), len(lands), len(plan)
    extra = [] if after is None else [after]
    nin = ns + nl + len(extra)

    def body(*refs):
        src_refs, land_refs = refs[:ns], refs[ns:ns + nl]
        send_sems, recv_sems = refs[nin:nin + ne], refs[nin + ne:nin + 2 * ne]
        own_sems = refs[nin + 2 * ne:nin + 3 * ne]
        for cp in _plan_copies(plan, src_refs, land_refs, send_sems, recv_sems, (SIBLING,) + SAME_CORE):
            cp.start()
        for cp in _plan_own(plan, src_refs, land_refs, own_sems):
            cp.start()
        refs[-1][...] = jnp.zeros_like(refs[-1])

    thru = [pltpu.HBM(a.shape, a.dtype) for a in list(srcs) + list(lands)]
    res = pl.pallas_call(
        body, name=name,
        in_specs=[HBM] * (ns + nl) + [ANY] * len(extra),
        out_specs=[SEM] * (3 * ne) + [HBM] * (ns + nl) + [pl.BlockSpec(memory_space=pltpu.VMEM)],
        out_shape=[pltpu.SemaphoreType.DMA(())] * (3 * ne) + thru + [jax.ShapeDtypeStruct((8, LANE), F32)],
        input_output_aliases={i: 3 * ne + i for i in range(ns + nl)},
        compiler_params=pltpu.CompilerParams(has_side_effects=EFFECT),
    )(*[pltpu.with_memory_space_constraint(a, pltpu.HBM) for a in list(srcs) + list(lands)], *extra)
    return dict(send1=res[:ne], recv1=res[ne:2 * ne], own=res[2 * ne:3 * ne], srcs=res[3 * ne:3 * ne + ns],
                lands=res[3 * ne + ns:3 * ne + ns + nl], token=res[-1], plan=plan)


def _gather2_forward(flight, after, name):
    srcs, lands, plan = flight["srcs"], flight["lands"], flight["plan"]
    ns, nl, ne = len(srcs), len(lands), len(plan)
    nin = ns + nl + ne + 1

    def body(*refs):
        land_refs = refs[ns:ns + nl]
        recv1 = refs[ns + nl:ns + nl + ne]
        send2, recv2 = refs[nin:nin + ne], refs[nin + ne:nin + 2 * ne]
        for cp in _plan_waits(plan, land_refs, send2, recv1, n=1 + len(SAME_CORE)):
            cp.wait_recv()
        for cp in _plan_forwards(plan, land_refs, send2, recv2):
            cp.start()
        refs[-1][...] = jnp.zeros_like(refs[-1])

    thru = [pltpu.HBM(a.shape, a.dtype) for a in list(srcs) + list(lands)]
    res = pl.pallas_call(
        body, name=name,
        in_specs=[HBM] * (ns + nl) + [SEM] * ne + [ANY],
        out_specs=[SEM] * (2 * ne) + [HBM] * (ns + nl) + [pl.BlockSpec(memory_space=pltpu.VMEM)],
        out_shape=[pltpu.SemaphoreType.DMA(())] * (2 * ne) + thru + [jax.ShapeDtypeStruct((8, LANE), F32)],
        input_output_aliases={i: 2 * ne + i for i in range(ns + nl)},
        compiler_params=pltpu.CompilerParams(has_side_effects=EFFECT),
    )(*srcs, *lands, *flight["recv1"], after)
    return dict(flight, send2=res[:ne], recv2=res[ne:2 * ne], srcs=res[2 * ne:2 * ne + ns],
                lands=res[2 * ne + ns:2 * ne + ns + nl], token=res[-1])


def _gather2_wait(flight, after, name):
    srcs, lands, plan = flight["srcs"], flight["lands"], flight["plan"]
    ns, nl, ne = len(srcs), len(lands), len(plan)

    def body(*refs):
        src_refs, land_refs = refs[:ns], refs[ns:ns + nl]
        sems = refs[ns + nl:ns + nl + 4 * ne]
        send1, own, send2, recv2 = sems[:ne], sems[ne:2 * ne], sems[2 * ne:3 * ne], sems[3 * ne:]
        for cp in _plan_waits(plan, land_refs, send1, recv2, n=1 + len(SAME_CORE)):
            cp.wait_send()
        for cp in _plan_waits(plan, land_refs, send2, recv2, n=len(SAME_CORE)):
            cp.wait_send()
            cp.wait_recv()
        for cp in _plan_own(plan, src_refs, land_refs, own):
            cp.wait()

    thru = [pltpu.HBM(a.shape, a.dtype) for a in list(srcs) + list(lands)]
    res = pl.pallas_call(
        body, name=name,
        in_specs=[HBM] * (ns + nl) + [SEM] * (4 * ne) + [ANY],
        out_specs=[HBM] * (ns + nl),
        out_shape=thru,
        input_output_aliases={i: i for i in range(ns + nl)},
        compiler_params=pltpu.CompilerParams(has_side_effects=EFFECT),
    )(*srcs, *lands, *flight["send1"], *flight["own"], *flight["send2"], *flight["recv2"], after)
    return list(res[ns:])


def _adamw_math(w, g, m, v):
    m = ADAM_B1 * m + (1.0 - ADAM_B1) * g
    v = ADAM_B2 * v + (1.0 - ADAM_B2) * (g * g)
    m_hat = m / (1.0 - ADAM_B1 ** ADAM_STEP)
    v_hat = v / (1.0 - ADAM_B2 ** ADAM_STEP)
    delta = -ADAM_LR * (m_hat / (jnp.sqrt(v_hat) + ADAM_EPS) + ADAM_WD * w)
    return delta, m, v


def _adamw(parts, w, m, v, name, rows, lane_offset=None):
    depth, r, c = w.shape
    cp = parts[0].shape[2]
    assert r % rows == 0 and len(parts) == depth

    def body(*refs):
        p_refs = refs[:depth]
        w_ref, m_ref, v_ref, g_ref, d_ref, mo_ref, vo_ref = refs[depth:]
        if lane_offset is not None:
            cw = -(-c // LANE) * LANE
            src = lax.broadcasted_iota(jnp.int32, (cp, cw), 0)
            dst = lax.broadcasted_iota(jnp.int32, (cp, cw), 1)
            pick = (src == dst + lane_offset()).astype(BF16)
            pick3 = jnp.concatenate([pick] * 3, axis=0)
        for li in range(depth):
            @pl.when(pl.program_id(0) == li)
            def _(li=li):
                g = p_refs[li][0].astype(F32)
                for j in range(1, N_DEV):
                    g = g + p_refs[li][j].astype(F32)
                if lane_offset is not None:
                    g = _dot(_split3_cols(g), pick3)
                g = g[:, :c]
                d, mn, vn = _adamw_math(w_ref[...], g, m_ref[...], v_ref[...])
                g_ref[...] = g
                d_ref[...] = d
                mo_ref[...] = mn
                vo_ref[...] = vn

    def part_spec(li):
        return pl.BlockSpec((N_DEV, rows, cp), lambda l, i: (0, jnp.where(l == li, i, 0), 0))

    blk = pl.BlockSpec((None, rows, c), lambda l, i: (l, i, 0))
    out = jax.ShapeDtypeStruct((depth, r, c), F32)
    return pl.pallas_call(
        body, name=name, grid=(depth, r // rows),
        in_specs=[part_spec(li) for li in range(depth)] + [blk, blk, blk],
        out_specs=[blk] * 4, out_shape=[out] * 4,
        compiler_params=_params(("arbitrary", "arbitrary")),
    )(*parts, w, m, v)


def _sum_parts(parts, name):
    _, r, c = parts.shape

    def body(p_ref, o_ref):
        g = p_ref[0]
        for j in range(1, N_DEV):
            g = g + p_ref[j]
        o_ref[...] = g

    return pl.pallas_call(
        body, name=name, out_shape=jax.ShapeDtypeStruct((r, c), F32),
        compiler_params=_params(),
    )(parts)


def _adamw_small(parts, ws, ms, vs, name):
    n = len(ws)

    def body(*refs):
        ins, outs = refs[:4 * n], refs[4 * n:]
        for i in range(n):
            p_ref, w_ref, m_ref, v_ref = ins[i], ins[n + i], ins[2 * n + i], ins[3 * n + i]
            g = p_ref[0]
            for j in range(1, p_ref.shape[0]):
                g = g + p_ref[j]
            d, mn, vn = _adamw_math(w_ref[...], g, m_ref[...], v_ref[...])
            outs[i][...] = g
            outs[n + i][...] = d
            outs[2 * n + i][...] = mn
            outs[3 * n + i][...] = vn

    out = [jax.ShapeDtypeStruct(a.shape, F32) for a in ws] * 4
    res = pl.pallas_call(body, name=name, out_shape=out, compiler_params=_params())(*parts, *ws, *ms, *vs)
    return res[:n], res[n:2 * n], res[2 * n:3 * n], res[3 * n:]


SMALL = ("norm_mix", "q_gain", "k_gain", "conv_b", "dt_bias", "a_log", "d_skip", "attn_out_gain",
         "ssm_out_gain", "norm_ffn")


def _full_cols(gathered):
    _, r, c = gathered.shape
    return gathered.transpose(1, 0, 2).reshape(r, N_DEV * c)


FF_BLK = 768
FF_PAD = N_DEV * FF_BLK
W_IN_COLS = IN_DIM // N_DEV
W_IN_WINDOW = 896


def _w_in_window_start(block):
    return (block * W_IN_COLS // LANE) * LANE


def _w_in_window(ref, block):
    return ref.at[:, pl.ds(pl.multiple_of(_w_in_window_start(block), LANE), W_IN_WINDOW)]


def _whole(ref, block):
    return ref


def _rows_of(size):
    return lambda ref, block: ref.at[pl.ds(pl.multiple_of(block * size, size), size), :]


def _slot(ref, block):
    return ref.at[block]


def _n_slots(ref, n):
    return ref.at[pl.ds(0, n)]


def _n_rows(size):
    return lambda ref, n: ref.at[pl.ds(0, n * size), :]


GATHER_A = [(0, 0, _whole, _slot, _n_slots), (1, 1, _whole, _rows_of(256), _n_rows(256))]
GATHER_B = [(i, i, _whole, _rows_of(FF_BLK), _n_rows(FF_BLK)) for i in range(3)]
SCATTER_A = [(0, 0, _w_in_window, _slot, _n_slots), (1, 1, _rows_of(256), _slot, _n_slots)]
SCATTER_B = [(i, i, _rows_of(FF_BLK), _slot, _n_slots) for i in range(3)]


def _gather_lands(which, shards, d):
    if which == "a":
        return [lax.empty((N_DEV,) + shards[0].shape, BF16), lax.empty((d, d), BF16)]
    return [lax.empty((FF_PAD, d), BF16) for _ in range(3)]


def _scatter_lands(which, grads):
    if which == "a":
        g_in, g_out = grads
        return [lax.empty((N_DEV, g_in.shape[0], W_IN_WINDOW), BF16),
                lax.empty((N_DEV, g_out.shape[0] // N_DEV, g_out.shape[1]), BF16)]
    return [lax.empty((N_DEV, FF_BLK, g.shape[1]), BF16) for g in grads]


def _pad_w_in(full):
    return jnp.pad(full, ((0, 0), (0, NPROJ - IN_DIM)))


def kernel(x, norm_mix, w_in, q_gain, k_gain, conv_w, conv_b, dt_bias, a_log, d_skip, attn_out_gain, ssm_out_gain, w_out, norm_ffn, w_gate, w_up, w_down, loss_target, m_norm_mix, m_w_in, m_q_gain, m_k_gain, m_conv_w, m_conv_b, m_dt_bias, m_a_log, m_d_skip, m_attn_out_gain, m_ssm_out_gain, m_w_out, m_norm_ffn, m_w_gate, m_w_up, m_w_down, v_norm_mix, v_w_in, v_q_gain, v_k_gain, v_conv_w, v_conv_b, v_dt_bias, v_a_log, v_d_skip, v_attn_out_gain, v_ssm_out_gain, v_w_out, v_norm_ffn, v_w_gate, v_w_up, v_w_down):
    nb, seq, d = x.shape
    depth = w_in.shape[0]
    t = nb * seq
    w = dict(norm_mix=norm_mix, w_in=w_in, q_gain=q_gain, k_gain=k_gain, conv_w=conv_w, conv_b=conv_b,
             dt_bias=dt_bias, a_log=a_log, d_skip=d_skip, attn_out_gain=attn_out_gain, ssm_out_gain=ssm_out_gain,
             w_out=w_out, norm_ffn=norm_ffn, w_gate=w_gate, w_up=w_up, w_down=w_down)
    mom = dict(norm_mix=m_norm_mix, w_in=m_w_in, q_gain=m_q_gain, k_gain=m_k_gain, conv_w=m_conv_w, conv_b=m_conv_b,
               dt_bias=m_dt_bias, a_log=m_a_log, d_skip=m_d_skip, attn_out_gain=m_attn_out_gain,
               ssm_out_gain=m_ssm_out_gain, w_out=m_w_out, norm_ffn=m_norm_ffn, w_gate=m_w_gate, w_up=m_w_up,
               w_down=m_w_down)
    var = dict(norm_mix=v_norm_mix, w_in=v_w_in, q_gain=v_q_gain, k_gain=v_k_gain, conv_w=v_conv_w, conv_b=v_conv_b,
               dt_bias=v_dt_bias, a_log=v_a_log, d_skip=v_d_skip, attn_out_gain=v_attn_out_gain,
               ssm_out_gain=v_ssm_out_gain, w_out=v_w_out, norm_ffn=v_norm_ffn, w_gate=v_w_gate, w_up=v_w_up,
               w_down=v_w_down)
    ff = w_gate.shape[2]

    (conv_all,) = _all_gather([conv_w], "gather_conv")

    def shards_a(li):
        return [w_in[li].astype(BF16), w_out[li].astype(BF16)]

    w["w_gate"], mom["w_gate"], var["w_gate"] = (jnp.swapaxes(a, 1, 2) for a in (w_gate, m_w_gate, v_w_gate))
    w["w_up"], mom["w_up"], var["w_up"] = (jnp.swapaxes(a, 1, 2) for a in (w_up, m_w_up, v_w_up))

    def shards_b(li):
        return [jnp.pad(w[k][li].astype(BF16), ((0, FF_BLK - ff), (0, 0))) for k in ("w_gate", "w_up", "w_down")]

    def small_params(li):
        p = {k: w[k][li][None, :] for k in ("norm_mix", "q_gain", "k_gain", "attn_out_gain", "ssm_out_gain", "norm_ffn")}
        conv_full = conv_all[:, li].transpose(1, 0, 2).reshape(CONV_K, CONV_DIM)
        p["cw8"] = _pack_conv(conv_full, conv_b[li])
        p["hp"] = _pack_heads(dt_bias[li], a_log[li], d_skip[li])
        return p

    xc = x.reshape(t, d)
    cur = shards_a(0)
    flight = _gather2_start(cur, _gather_lands("a", cur, d), GATHER_A, "gather_a0")
    flight = _gather2_forward(flight, xc, "gather_a0_fwd")
    lands_a = _gather2_wait(flight, flight["token"], "gather_a0_wait")
    layers, saved = [], []
    for li in range(depth):
        tag = f"l{li}_"
        p = small_params(li)
        p["w_in"] = _pad_w_in(_full_cols(lands_a[0]))
        p["w_out"] = lands_a[1]
        cur = shards_b(li)
        flight = _gather2_start(cur, _gather_lands("b", cur, d), GATHER_B, tag + "gather_b", after=lands_a[1])
        h1 = _rms_fwd(xc, p["norm_mix"], tag + "rms1", after=flight["token"])
        proj = _matmul(h1, p["w_in"], "nn", F32, tag + "mm_in")
        o = _attn_fwd(proj, p["q_gain"], p["k_gain"], nb, seq, tag + "attn")
        flight = _gather2_forward(flight, o, tag + "gather_b_fwd")
        act = _conv_fwd(proj, p["cw8"], nb, seq, tag + "conv")
        y = _ssd_fwd(act, proj, p["hp"], nb, seq, tag + "ssd")
        cat = _mix_fwd(o, y, proj, p["attn_out_gain"], p["ssm_out_gain"], tag + "mix")
        x1 = _matmul(cat, p["w_out"], "nn", F32, tag + "mm_out", residual=xc)
        p["w_gate"], p["w_up"], p["w_down"] = _gather2_wait(flight, x1, tag + "gather_b_wait")
        token = None
        if li + 1 < depth:
            nxt = shards_a(li + 1)
            flight = _gather2_start(nxt, _gather_lands("a", nxt, d), GATHER_A, f"gather_a{li + 1}",
                                    after=p["w_down"])
            token = flight["token"]
        h2 = _rms_fwd(x1, p["norm_ffn"], tag + "rms2", after=token)
        a, gate, up = _mm_swiglu(h2, p["w_gate"], p["w_up"], tag + "mm_gu")
        if li + 1 < depth:
            flight = _gather2_forward(flight, gate, f"gather_a{li + 1}_fwd")
        x2 = _matmul(a, p["w_down"], "nn", F32, tag + "mm_down", residual=x1)
        if li + 1 < depth:
            lands_a = _gather2_wait(flight, x2, f"gather_a{li + 1}_wait")
        saved.append(dict(x=xc, h1=h1, proj=proj, o=o, act=act, y=y, cat=cat, x1=x1, h2=h2, gate=gate, up=up, a=a))
        layers.append(p)
        xc = x2

    loss_blk, dx, dxb = _loss_fwd_bwd(xc, loss_target.reshape(t, d), "loss")
    loss = lax.psum(loss_blk[0, 0], ("x", "y", "c"))

    grads = [dict() for _ in range(depth)]
    recv = [dict() for _ in range(depth)]
    flight_a, token = None, None
    for li in reversed(range(depth)):
        tag = f"l{li}_b_"
        p, s, g = layers[li], saved[li], grads[li]
        dgate, dup = _mm_dact_swiglu(dxb, p["w_down"], s["gate"], s["up"], tag + "mm_dact", after=token)
        g_down = _matmul(s["a"], dxb, "tn", BF16, tag + "mm_dwd")
        dh2 = _matmul(dgate, p["w_gate"], "nn", F32, tag + "mm_dh2g")
        dh2 = _matmul(dup, p["w_up"], "nn", F32, tag + "mm_dh2u", residual=dh2)
        g_gate = _matmul(dgate, s["h2"], "tn", BF16, tag + "mm_dwg")
        g_up = _matmul(dup, s["h2"], "tn", BF16, tag + "mm_dwu")
        if flight_a is not None:
            recv[li + 1]["w_in"], recv[li + 1]["w_out"] = _copies_wait(flight_a, g_up, f"l{li + 1}_b_scatter_a_wait")
        grads_b = [g_gate, g_up, g_down]
        flight_b = _copies_start(grads_b, _scatter_lands("b", grads_b), SCATTER_B,
                                 tag + "scatter_b", after=recv[li + 1]["w_out"] if li + 1 < depth else None)
        dx1, dx1b, g["norm_ffn"] = _rms_bwd(s["x1"], p["norm_ffn"], dh2, dx, tag + "rms2")
        dcat = _matmul(dx1b, p["w_out"], "nt", F32, tag + "mm_dcat", after=flight_b["token"])
        g_out = _matmul(s["cat"], dx1b, "tn", BF16, tag + "mm_dwo")
        do, dy, dz, g["attn_out_gain"], g["ssm_out_gain"] = _mix_bwd(
            dcat, s["o"], s["y"], s["proj"], p["attn_out_gain"], p["ssm_out_gain"], tag + "mix")
        dq, dk, dv, g["q_gain"], g["k_gain"] = _attn_bwd(s["proj"], do, p["q_gain"], p["k_gain"], nb, seq, tag + "attn")
        dxa, dba, dca, ddt, dhp = _ssd_bwd(s["act"], s["proj"], dy, p["hp"], nb, seq, tag + "ssd")
        dxbc, dcw8 = _conv_bwd(s["proj"], jnp.concatenate([dxa, dba, dca], axis=1), p["cw8"], nb, seq, tag + "conv")
        g["conv_w"] = dcw8[0:CONV_K]
        g["conv_b"] = dcw8[CONV_K:CONV_K + 1]
        heads = dhp[:, 0:3, 0:SSM_HG].transpose(1, 0, 2).reshape(3, SSM_HEADS)
        g["dt_bias"], g["a_log"], g["d_skip"] = heads[0:1], heads[1:2], heads[2:3]
        ddt = ddt[:, :LANE] + jnp.roll(ddt[:, LANE:], SSM_HG, axis=1)
        tail = jnp.zeros((t, NPROJ - OFF_DT - LANE), BF16)
        dproj = jnp.concatenate([dq, dk, dv, dz, dxbc, ddt, tail], axis=1)
        recv[li]["w_gate"], recv[li]["w_up"], recv[li]["w_down"] = _copies_wait(flight_b, dproj, tag + "scatter_b_wait")
        dh1 = _matmul(dproj, p["w_in"], "nt", F32, tag + "mm_dh1")
        g_in = _matmul(s["h1"], dproj, "tn", BF16, tag + "mm_dwin")
        flight_a = _copies_start([g_in, g_out], _scatter_lands("a", [g_in, g_out]), SCATTER_A, tag + "scatter_a")
        token = flight_a["token"]
        dx, dxb, g["norm_mix"] = _rms_bwd(s["x"], p["norm_mix"], dh1, dx1, tag + "rms1")
    grad_x = dx.reshape(nb, seq, d)

    out_g, out_d, out_m, out_v = {}, {}, {}, {}

    def update(k, rows, lane_offset=None):
        parts = [recv[li][k] for li in range(depth)]
        out_g[k], out_d[k], out_m[k], out_v[k] = _adamw(parts, w[k], mom[k], var[k], "adamw_" + k, rows, lane_offset)

    def w_in_offset():
        return _my_block() * W_IN_COLS - _w_in_window_start(_my_block())

    update("w_gate", 64)
    update("w_up", 64)
    update("w_down", 64)
    recv[0]["w_in"], recv[0]["w_out"] = _copies_wait(
        flight_a, [out_g["w_gate"], out_g["w_up"], out_g["w_down"], dx], "l0_b_scatter_a_wait")
    update("w_in", 128, w_in_offset)
    update("w_out", 128)
    for res in (out_g, out_d, out_m, out_v):
        res["w_gate"], res["w_up"] = jnp.swapaxes(res["w_gate"], 1, 2), jnp.swapaxes(res["w_up"], 1, 2)

    small_g = [jnp.concatenate([grads[li][k] for li in range(depth)], axis=0) for k in SMALL]
    conv_g = jnp.stack([grads[li]["conv_w"] for li in range(depth)]).reshape(depth, CONV_K * CONV_DIM)
    parts = _all_gather(small_g + [conv_g], "gather_small_grads")
    res = _adamw_small(parts[:-1], [w[k] for k in SMALL], [mom[k] for k in SMALL], [var[k] for k in SMALL],
                       "adamw_small")
    for dst, vals in zip((out_g, out_d, out_m, out_v), res):
        dst.update(dict(zip(SMALL, vals)))
    conv_total = _sum_parts(parts[-1], "sum_conv_grads").reshape(depth, CONV_K, CONV_DIM)
    cshard = conv_w.shape[2]
    conv_mine = lax.dynamic_slice_in_dim(conv_total, _my_block() * cshard, cshard, axis=2)
    flat = lambda a: a.reshape(depth, CONV_K * cshard)
    res = _adamw_small([flat(conv_mine)[None]], [flat(conv_w)], [flat(m_conv_w)], [flat(v_conv_w)], "adamw_conv")
    for dst, vals in zip((out_g, out_d, out_m, out_v), res):
        dst["conv_w"] = vals[0].reshape(depth, CONV_K, cshard)

    names = ("norm_mix", "w_in", "q_gain", "k_gain", "conv_w", "conv_b", "dt_bias", "a_log", "d_skip",
             "attn_out_gain", "ssm_out_gain", "w_out", "norm_ffn", "w_gate", "w_up", "w_down")
    return (loss, grad_x, *[out_g[k] for k in names], *[out_d[k] for k in names],
            *[out_m[k] for k in names], *[out_v[k] for k in names])
```

```python
import functools
import math

import jax
import jax.numpy as jnp
from jax import lax
from jax.experimental import pallas as pl
from jax.experimental.pallas import tpu as pltpu

F32 = jnp.float32
BF16 = jnp.bfloat16
MESH = pl.DeviceIdType.MESH

N_DEV = 8
EPS = 1e-6
ATT_HEADS = 8
ATT_DH = 128
ATT_W = ATT_HEADS * ATT_DH
SSM_W = 1024
SSM_P = 64
SSM_N = 128
SSM_GROUPS = 2
SSM_HG = 8
SSM_HEADS = SSM_GROUPS * SSM_HG
CHUNK = 128
CONV_K = 4
CONV_DIM = SSM_W + 2 * SSM_GROUPS * SSM_N
LANE = 128
OFF_Q, OFF_K, OFF_V, OFF_Z, OFF_XS = 0, ATT_W, 2 * ATT_W, 3 * ATT_W, 4 * ATT_W
OFF_B = OFF_XS + SSM_W
OFF_C = OFF_B + SSM_GROUPS * SSM_N
OFF_DT = OFF_C + SSM_GROUPS * SSM_N
NPROJ = 6144
IN_DIM = OFF_DT + SSM_HEADS

ADAM_LR = 0.001
ADAM_B1 = 0.9
ADAM_B2 = 0.999
ADAM_EPS = 1e-08
ADAM_WD = 0.01
ADAM_STEP = 10

VMEM_LIMIT = 56 * 1024 * 1024
MATMUL_OPERAND_BYTES = 26 * 1024 * 1024


def _params(sem=None):
    return pltpu.CompilerParams(dimension_semantics=sem, vmem_limit_bytes=VMEM_LIMIT)


def _pick(dim, target):
    if dim <= target:
        return dim
    best = None
    for t in range(LANE, target + 1, LANE):
        if dim % t == 0:
            best = t
    assert best is not None, (dim, target)
    return best


def _dot(a, b, dims=((1,), (0,))):
    return lax.dot_general(a, b, (dims, ((), ())), preferred_element_type=F32)


def _dot_nt(a, b):
    return _dot(a, b, ((1,), (1,)))


def _dot_tn(a, b):
    return _dot(a, b, ((0,), (0,)))


def _sigmoid(x):
    return 1.0 / (1.0 + jnp.exp(-x))


def _softplus(x):
    return jnp.maximum(x, 0.0) + jnp.log(1.0 + jnp.exp(-jnp.abs(x)))


def _rstd(x):
    return lax.rsqrt(jnp.mean(x * x, axis=-1, keepdims=True) + EPS)


def _matmul(a, b, mode, out_dtype, name, residual=None, after=None, tm=512, tn=1024, tk=2048):
    if mode == "nn":
        (m, k), (k2, n) = a.shape, b.shape
    elif mode == "nt":
        (m, k), (n, k2) = a.shape, b.shape
    else:
        (k, m), (k2, n) = a.shape, b.shape
    assert k == k2, (a.shape, b.shape, mode)
    tm, tn, tk = _pick(m, tm), _pick(n, tn), _pick(k, tk)
    for cand_tn in (tn, _pick(n, tn // 2)):
        if 2 * 2 * (tm * k + k * cand_tn) <= MATMUL_OPERAND_BYTES:
            tn, tk = cand_tn, k
            break
    nk = k // tk
    dims = {"nn": ((1,), (0,)), "nt": ((1,), (1,)), "tn": ((0,), (0,))}[mode]
    has_res = residual is not None

    has_tok = after is not None

    def body(*refs):
        a_ref, b_ref = refs[:2]
        r_ref = refs[2] if has_res else None
        o_ref = refs[2 + has_res + has_tok]
        prod = _dot(a_ref[...], b_ref[...], dims)

        def finish(r):
            if r_ref is not None:
                r = r + r_ref[...]
            o_ref[...] = r.astype(o_ref.dtype)

        if nk == 1:
            finish(prod)
        else:
            acc = refs[-1]
            kk = pl.program_id(2)

            @pl.when(kk == 0)
            def _():
                acc[...] = prod

            @pl.when(kk > 0)
            def _():
                acc[...] += prod

            @pl.when(kk == nk - 1)
            def _():
                finish(acc[...])

    if mode == "tn":
        a_spec = pl.BlockSpec((tk, tm), lambda i, j, kk: (kk, i))
    else:
        a_spec = pl.BlockSpec((tm, tk), lambda i, j, kk: (i, kk))
    if mode == "nt":
        b_spec = pl.BlockSpec((tn, tk), lambda i, j, kk: (j, kk))
    else:
        b_spec = pl.BlockSpec((tk, tn), lambda i, j, kk: (kk, j))
    o_spec = pl.BlockSpec((tm, tn), lambda i, j, kk: (i, j))
    tok_spec = pl.BlockSpec((8, LANE), lambda i, j, kk: (0, 0))
    in_specs = [a_spec, b_spec] + ([o_spec] if has_res else []) + ([tok_spec] if has_tok else [])
    args = (a, b) + ((residual,) if has_res else ()) + ((after,) if has_tok else ())
    return pl.pallas_call(
        body,
        name=name,
        grid=(m // tm, n // tn, nk),
        in_specs=in_specs,
        out_specs=o_spec,
        out_shape=jax.ShapeDtypeStruct((m, n), out_dtype),
        scratch_shapes=[pltpu.VMEM((tm, tn), F32)] if nk > 1 else [],
        compiler_params=_params(("parallel", "parallel", "arbitrary")),
    )(*args)


def _mm_swiglu(h, wg_t, wu_t, name, tm=512, tn=1024):
    m, k = h.shape
    n = wg_t.shape[0]
    tm, tn = _pick(m, tm), _pick(n, tn)

    def body(h_ref, g_ref, u_ref, a_ref, gs_ref, us_ref):
        hv = h_ref[...]
        g = _dot_nt(hv, g_ref[...])
        u = _dot_nt(hv, u_ref[...])
        a_ref[...] = (g * _sigmoid(g) * u).astype(BF16)
        gs_ref[...] = g.astype(BF16)
        us_ref[...] = u.astype(BF16)

    w_spec = pl.BlockSpec((tn, k), lambda i, j: (j, 0))
    o_spec = pl.BlockSpec((tm, tn), lambda i, j: (i, j))
    out = jax.ShapeDtypeStruct((m, n), BF16)
    return pl.pallas_call(
        body, name=name, grid=(m // tm, n // tn),
        in_specs=[pl.BlockSpec((tm, k), lambda i, j: (i, 0)), w_spec, w_spec],
        out_specs=[o_spec] * 3, out_shape=[out] * 3,
        compiler_params=_params(("parallel", "parallel")),
    )(h, wg_t, wu_t)


def _mm_dact_swiglu(dx, wd, gs, us, name, after=None, tm=512, tn=1024):
    m, k = dx.shape
    n = wd.shape[0]
    tm, tn = _pick(m, tm), _pick(n, tn)
    has_tok = after is not None

    def body(*refs):
        dx_ref, wd_ref, g_ref, u_ref = refs[:4]
        dg_ref, du_ref = refs[-2:]
        dact = _dot_nt(dx_ref[...], wd_ref[...])
        g = g_ref[...].astype(F32)
        sg = _sigmoid(g)
        dg_ref[...] = (dact * u_ref[...].astype(F32) * sg * (1.0 + g * (1.0 - sg))).astype(BF16)
        du_ref[...] = (dact * g * sg).astype(BF16)

    o_spec = pl.BlockSpec((tm, tn), lambda i, j: (i, j))
    tok = [pl.BlockSpec((8, LANE), lambda i, j: (0, 0))] if has_tok else []
    out = jax.ShapeDtypeStruct((m, n), BF16)
    return pl.pallas_call(
        body, name=name, grid=(m // tm, n // tn),
        in_specs=[pl.BlockSpec((tm, k), lambda i, j: (i, 0)), pl.BlockSpec((tn, k), lambda i, j: (j, 0)),
                  o_spec, o_spec] + tok,
        out_specs=[o_spec] * 2, out_shape=[out] * 2,
        compiler_params=_params(("parallel", "parallel")),
    )(dx, wd, gs, us, *((after,) if has_tok else ()))


ROWS = 512


def _rms_fwd(x, g, name, after=None):
    t, d = x.shape
    has_tok = after is not None

    def body(*refs):
        x_ref, g_ref, o_ref = refs[0], refs[1], refs[-1]
        xv = x_ref[...]
        o_ref[...] = (xv * _rstd(xv) * g_ref[...]).astype(BF16)

    row = pl.BlockSpec((ROWS, d), lambda i: (i, 0))
    tok = [pl.BlockSpec((8, LANE), lambda i: (0, 0))] if has_tok else []
    return pl.pallas_call(
        body, name=name, grid=(t // ROWS,),
        in_specs=[row, pl.BlockSpec((1, d), lambda i: (0, 0))] + tok,
        out_specs=row, out_shape=jax.ShapeDtypeStruct((t, d), BF16),
        compiler_params=_params(("parallel",)),
    )(x, g, *((after,) if has_tok else ()))


def _rms_bwd(x, g, dh, dres, name):
    t, d = x.shape

    def body(x_ref, g_ref, dh_ref, dr_ref, dx_ref, dxb_ref, dg_ref):
        xv = x_ref[...]
        r = _rstd(xv)
        xh = xv * r
        dhv = dh_ref[...]

        @pl.when(pl.program_id(0) == 0)
        def _():
            dg_ref[...] = jnp.zeros_like(dg_ref)

        dg_ref[...] += jnp.sum(dhv * xh, axis=0, keepdims=True)
        dxh = dhv * g_ref[...]
        dx = r * (dxh - xh * jnp.mean(dxh * xh, axis=-1, keepdims=True)) + dr_ref[...]
        dx_ref[...] = dx
        dxb_ref[...] = dx.astype(BF16)

    row = pl.BlockSpec((ROWS // 2, d), lambda i: (i, 0))
    vec = pl.BlockSpec((1, d), lambda i: (0, 0))
    return pl.pallas_call(
        body, name=name, grid=(t // (ROWS // 2),),
        in_specs=[row, vec, row, row],
        out_specs=[row, row, vec],
        out_shape=[jax.ShapeDtypeStruct((t, d), F32), jax.ShapeDtypeStruct((t, d), BF16),
                   jax.ShapeDtypeStruct((1, d), F32)],
        compiler_params=_params(("arbitrary",)),
    )(x, g, dh, dres)


def _loss_fwd_bwd(y, target, name):
    t, d = y.shape
    inv = 1.0 / d

    def body(y_ref, t_ref, l_ref, dy_ref, dyb_ref):
        e = y_ref[...] - t_ref[...]

        @pl.when(pl.program_id(0) == 0)
        def _():
            l_ref[...] = jnp.zeros_like(l_ref)

        l_ref[...] += 0.5 * inv * jnp.sum(e * e)
        dy = e * inv
        dy_ref[...] = dy
        dyb_ref[...] = dy.astype(BF16)

    row = pl.BlockSpec((ROWS, d), lambda i: (i, 0))
    return pl.pallas_call(
        body, name=name, grid=(t // ROWS,),
        in_specs=[row, row],
        out_specs=[pl.BlockSpec((8, LANE), lambda i: (0, 0)), row, row],
        out_shape=[jax.ShapeDtypeStruct((8, LANE), F32), jax.ShapeDtypeStruct((t, d), F32),
                   jax.ShapeDtypeStruct((t, d), BF16)],
        compiler_params=_params(("arbitrary",)),
    )(y, target)


MIX_ROWS = 256


def _mix_fwd(o, y, proj, ga, gs, name):
    t = o.shape[0]
    gw = SSM_W // SSM_GROUPS

    def body(o_ref, y_ref, z_ref, ga_ref, gs_ref, c_ref):
        ov = o_ref[...]
        c_ref[:, 0:ATT_W] = (ov * _rstd(ov) * ga_ref[...]).astype(BF16)
        zv = z_ref[...]
        yz = y_ref[...] * (zv * _sigmoid(zv))
        for gi in range(SSM_GROUPS):
            seg = yz[:, gi * gw:(gi + 1) * gw]
            c_ref[:, ATT_W + gi * gw:ATT_W + (gi + 1) * gw] = (
                seg * _rstd(seg) * gs_ref[:, gi * gw:(gi + 1) * gw]).astype(BF16)

    half = pl.BlockSpec((MIX_ROWS, ATT_W), lambda i: (i, 0))
    vec = pl.BlockSpec((1, ATT_W), lambda i: (0, 0))
    return pl.pallas_call(
        body, name=name, grid=(t // MIX_ROWS,),
        in_specs=[half, half, pl.BlockSpec((MIX_ROWS, ATT_W), lambda i: (i, OFF_Z // ATT_W)), vec, vec],
        out_specs=pl.BlockSpec((MIX_ROWS, 2 * ATT_W), lambda i: (i, 0)),
        out_shape=jax.ShapeDtypeStruct((t, 2 * ATT_W), BF16),
        compiler_params=_params(("parallel",)),
    )(o, y, proj, ga, gs)


def _mix_bwd(dcat, o, y, proj, ga, gs, name):
    t = o.shape[0]
    gw = SSM_W // SSM_GROUPS

    def body(dc_ref, o_ref, y_ref, z_ref, ga_ref, gs_ref, do_ref, dy_ref, dz_ref, dga_ref, dgs_ref):
        @pl.when(pl.program_id(0) == 0)
        def _():
            dga_ref[...] = jnp.zeros_like(dga_ref)
            dgs_ref[...] = jnp.zeros_like(dgs_ref)

        ov = o_ref[...]
        r = _rstd(ov)
        oh = ov * r
        d_on = dc_ref[:, 0:ATT_W]
        dga_ref[...] += jnp.sum(d_on * oh, axis=0, keepdims=True)
        doh = d_on * ga_ref[...]
        do_ref[...] = r * (doh - oh * jnp.mean(doh * oh, axis=-1, keepdims=True))

        zv = z_ref[...]
        yv = y_ref[...]
        sz = _sigmoid(zv)
        silu = zv * sz
        yz = yv * silu
        for gi in range(SSM_GROUPS):
            sl = slice(gi * gw, (gi + 1) * gw)
            seg = yz[:, sl]
            rg = _rstd(seg)
            yh = seg * rg
            dyn = dc_ref[:, ATT_W + gi * gw:ATT_W + (gi + 1) * gw]
            dgs_ref[:, sl] += jnp.sum(dyn * yh, axis=0, keepdims=True)
            dyh = dyn * gs_ref[:, sl]
            dyz = rg * (dyh - yh * jnp.mean(dyh * yh, axis=-1, keepdims=True))
            dy_ref[:, sl] = dyz * silu[:, sl]
            dz_ref[:, sl] = (dyz * yv[:, sl] * (sz[:, sl] * (1.0 + zv[:, sl] * (1.0 - sz[:, sl])))).astype(BF16)

    half = pl.BlockSpec((MIX_ROWS, ATT_W), lambda i: (i, 0))
    vec = pl.BlockSpec((1, ATT_W), lambda i: (0, 0))
    return pl.pallas_call(
        body, name=name, grid=(t // MIX_ROWS,),
        in_specs=[pl.BlockSpec((MIX_ROWS, 2 * ATT_W), lambda i: (i, 0)), half, half,
                  pl.BlockSpec((MIX_ROWS, ATT_W), lambda i: (i, OFF_Z // ATT_W)), vec, vec],
        out_specs=[half, half, half, vec, vec],
        out_shape=[jax.ShapeDtypeStruct((t, ATT_W), F32), jax.ShapeDtypeStruct((t, SSM_W), F32),
                   jax.ShapeDtypeStruct((t, SSM_W), BF16), jax.ShapeDtypeStruct((1, ATT_W), F32),
                   jax.ShapeDtypeStruct((1, SSM_W), F32)],
        compiler_params=_params(("arbitrary",)),
    )(dcat, o, y, proj, ga, gs)


ATT_QB = 256
ATT_KB = 256
assert ATT_QB == ATT_KB


def _stacked(m):
    return jnp.concatenate([m, m], axis=0)


def _split_sum(x, m2):
    hi = x.astype(BF16)
    lo = (x - hi.astype(F32)).astype(BF16)
    return _dot(jnp.concatenate([hi, lo], axis=1), m2)


def _att_tile(z, mask, m_strict2, carry):
    lse = jnp.log(1.0 + jnp.exp(-jnp.abs(z)))
    lb = jnp.minimum(z, 0.0) - lse
    lrm = -jnp.maximum(z, 0.0) - lse
    if mask is not None:
        lrm = jnp.where(mask, lrm, 0.0)
    w = jnp.exp(lb + _split_sum(lrm, m_strict2) + carry)
    if mask is not None:
        w = jnp.where(mask, w, 0.0)
    return lb, lrm, w


ATT_HP = 2


def _head_spec(seq, off):
    width = ATT_HP * ATT_DH
    per = ATT_HEADS // ATT_HP
    return pl.BlockSpec((seq, width), lambda s: (s // per, off // width + s % per))


def _head_lanes(h):
    return slice(h * ATT_DH, (h + 1) * ATT_DH)


def _attn_fwd(proj, gq, gk, nb, seq, name):
    nq = seq // ATT_QB
    scale = ATT_DH ** -0.5
    heads = range(ATT_HP)

    def body(q_ref, k_ref, v_ref, gq_ref, gk_ref, o_ref, qs, kn, vb):
        for h in heads:
            sl = _head_lanes(h)
            qv = q_ref[:, sl]
            kv = k_ref[:, sl]
            qs[:, sl] = (qv * _rstd(qv) * gq_ref[...] * scale).astype(BF16)
            kn[:, sl] = (kv * _rstd(kv) * gk_ref[...]).astype(BF16)
            vb[:, sl] = v_ref[:, sl].astype(BF16)
        row = lax.broadcasted_iota(jnp.int32, (ATT_QB, ATT_KB), 0)
        col = lax.broadcasted_iota(jnp.int32, (ATT_QB, ATT_KB), 1)
        m_strict2 = _stacked((row > col).astype(BF16))
        diagonal = col < row

        def key_rows(kj):
            return pl.ds(pl.multiple_of(kj * ATT_KB, ATT_KB), ATT_KB)

        def q_loop(qi, _):
            q0 = pl.multiple_of(qi * ATT_QB, ATT_QB)
            q_ts = [qs[pl.ds(q0, ATT_QB), _head_lanes(h)] for h in heads]

            def scores(h, kj):
                return _dot_nt(q_ts[h], kn[key_rows(kj), _head_lanes(h)])

            def tile(c, kj, mask):
                rows = key_rows(kj)
                out = []
                for h in heads:
                    acc, carry, z = c[h]
                    z_next = scores(h, jnp.maximum(kj - 1, 0))
                    _, lrm, w = _att_tile(z, mask, m_strict2, carry)
                    acc = acc + _dot(w.astype(BF16), vb[rows, _head_lanes(h)])
                    out.append((acc, carry + jnp.sum(lrm, axis=-1, keepdims=True), z_next))
                return tuple(out)

            init = tuple((jnp.zeros((ATT_QB, ATT_DH), F32), jnp.zeros((ATT_QB, 1), F32), scores(h, qi)) for h in heads)
            res = lax.fori_loop(1, qi + 1, lambda i, c: tile(c, qi - i, None), tile(init, qi, diagonal))
            for h in heads:
                o_ref[pl.ds(q0, ATT_QB), _head_lanes(h)] = res[h][0]
            return 0

        lax.fori_loop(0, nq, q_loop, 0)

    vec = pl.BlockSpec((1, ATT_DH), lambda s: (0, 0))
    return pl.pallas_call(
        body, name=name, grid=(nb * ATT_HEADS // ATT_HP,),
        in_specs=[_head_spec(seq, OFF_Q), _head_spec(seq, OFF_K), _head_spec(seq, OFF_V), vec, vec],
        out_specs=_head_spec(seq, 0),
        out_shape=jax.ShapeDtypeStruct((nb * seq, ATT_W), F32),
        scratch_shapes=[pltpu.VMEM((seq, ATT_HP * ATT_DH), BF16)] * 3,
        compiler_params=_params(("parallel",)),
    )(proj, proj, proj, gq, gk)


def _attn_bwd(proj, do, gq, gk, nb, seq, name):
    nq = seq // ATT_QB
    nk = seq // ATT_KB
    scale = ATT_DH ** -0.5
    heads = range(ATT_HP)

    def body(q_ref, k_ref, v_ref, do_ref, gq_ref, gk_ref, dq_ref, dk_ref, dv_ref, dgq_ref, dgk_ref,
             qs, kn, vb, dob, dq_acc, dk_acc, dv_acc, gbuf, bbuf):
        @pl.when(pl.program_id(0) == 0)
        def _():
            dgq_ref[...] = jnp.zeros_like(dgq_ref)
            dgk_ref[...] = jnp.zeros_like(dgk_ref)

        for h in heads:
            sl = _head_lanes(h)
            qv = q_ref[:, sl]
            kv = k_ref[:, sl]
            qs[:, sl] = (qv * _rstd(qv) * gq_ref[...] * scale).astype(BF16)
            kn[:, sl] = (kv * _rstd(kv) * gk_ref[...]).astype(BF16)
            vb[:, sl] = v_ref[:, sl].astype(BF16)
            dob[:, sl] = do_ref[:, sl].astype(BF16)
        dk_acc[...] = jnp.zeros_like(dk_acc)
        dv_acc[...] = jnp.zeros_like(dv_acc)
        row = lax.broadcasted_iota(jnp.int32, (ATT_QB, ATT_KB), 0)
        col = lax.broadcasted_iota(jnp.int32, (ATT_QB, ATT_KB), 1)
        m_strict2 = _stacked((row > col).astype(BF16))
        m_prefix2 = _stacked((row < col).astype(BF16))
        diagonal = col < row

        def key_rows(kj):
            return pl.ds(pl.multiple_of(kj * ATT_KB, ATT_KB), ATT_KB)

        def q_loop(qi, _):
            q0 = pl.multiple_of(qi * ATT_QB, ATT_QB)
            q_ts = [qs[pl.ds(q0, ATT_QB), _head_lanes(h)] for h in heads]
            do_ts = [dob[pl.ds(q0, ATT_QB), _head_lanes(h)] for h in heads]

            def scores(h, kj):
                return _dot_nt(q_ts[h], kn[key_rows(kj), _head_lanes(h)])

            def down(c, kj, mask):
                rows = key_rows(kj)
                out = []
                for h in heads:
                    carry, z = c[h]
                    sl = _head_lanes(h)
                    z_next = scores(h, jnp.maximum(kj - 1, 0))
                    lb, lrm, w = _att_tile(z, mask, m_strict2, carry)
                    dw = _dot_nt(do_ts[h], vb[rows, sl])
                    gbuf[h * nk + kj] = w * dw
                    bbuf[h * nk + kj] = jnp.exp(lb)
                    dv_acc[rows, sl] += _dot_tn(w.astype(BF16), do_ts[h])
                    out.append((carry + jnp.sum(lrm, axis=-1, keepdims=True), z_next))
                return tuple(out)

            init = tuple((jnp.zeros((ATT_QB, 1), F32), scores(h, qi)) for h in heads)
            lax.fori_loop(1, qi + 1, lambda i, c: down(c, qi - i, None), down(init, qi, diagonal))

            def up(c, kj, mask):
                rows = key_rows(kj)
                out = []
                for h in heads:
                    acc, carry, within = c[h]
                    sl = _head_lanes(h)
                    g = gbuf[h * nk + kj]
                    beta = bbuf[h * nk + kj]
                    within_next = _split_sum(gbuf[h * nk + jnp.minimum(kj + 1, qi)], m_prefix2)
                    dz = g * (1.0 - beta) - (within + carry) * beta
                    if mask is not None:
                        dz = jnp.where(mask, dz, 0.0)
                    dz = dz.astype(BF16)
                    acc = acc + _dot(dz, kn[rows, sl])
                    dk_acc[rows, sl] += _dot_tn(dz, q_ts[h])
                    out.append((acc, carry + jnp.sum(g, axis=-1, keepdims=True), within_next))
                return tuple(out)

            init = tuple((jnp.zeros((ATT_QB, ATT_DH), F32), jnp.zeros((ATT_QB, 1), F32),
                          _split_sum(gbuf[h * nk], m_prefix2)) for h in heads)
            res = up(lax.fori_loop(0, qi, lambda kj, c: up(c, kj, None), init), qi, diagonal)
            for h in heads:
                dq_acc[pl.ds(q0, ATT_QB), _head_lanes(h)] = res[h][0]
            return 0

        lax.fori_loop(0, nq, q_loop, 0)

        def norm_bwd(xv, gain, dyn):
            r = _rstd(xv)
            xh = xv * r
            dgain = jnp.sum(dyn * xh, axis=0, keepdims=True)
            dxh = dyn * gain
            return r * (dxh - xh * jnp.mean(dxh * xh, axis=-1, keepdims=True)), dgain

        for h in heads:
            sl = _head_lanes(h)
            dq, dgq = norm_bwd(q_ref[:, sl], gq_ref[...], dq_acc[:, sl] * scale)
            dk, dgk = norm_bwd(k_ref[:, sl], gk_ref[...], dk_acc[:, sl])
            dq_ref[:, sl] = dq.astype(BF16)
            dk_ref[:, sl] = dk.astype(BF16)
            dv_ref[:, sl] = dv_acc[:, sl].astype(BF16)
            dgq_ref[...] += dgq
            dgk_ref[...] += dgk

    vec = pl.BlockSpec((1, ATT_DH), lambda s: (0, 0))
    big = jax.ShapeDtypeStruct((nb * seq, ATT_W), BF16)
    small = jax.ShapeDtypeStruct((1, ATT_DH), F32)
    width = ATT_HP * ATT_DH
    return pl.pallas_call(
        body, name=name, grid=(nb * ATT_HEADS // ATT_HP,),
        in_specs=[_head_spec(seq, OFF_Q), _head_spec(seq, OFF_K), _head_spec(seq, OFF_V), _head_spec(seq, 0), vec, vec],
        out_specs=[_head_spec(seq, 0)] * 3 + [vec, vec],
        out_shape=[big, big, big, small, small],
        scratch_shapes=[pltpu.VMEM((seq, width), BF16)] * 4 + [pltpu.VMEM((seq, width), F32)] * 3
        + [pltpu.VMEM((ATT_HP * nk, ATT_QB, ATT_KB), F32)] * 2,
        compiler_params=_params(("arbitrary",)),
    )(proj, proj, proj, do, gq, gk)


CONV_COLS = 256


def _pack_conv(conv_w, conv_b):
    return jnp.concatenate([conv_w, conv_b[None, :], jnp.zeros((3, CONV_DIM), F32)], axis=0)


def _conv_pre(raw, w8, rowi):
    pre = w8[CONV_K:CONV_K + 1, :] + raw * w8[CONV_K - 1:CONV_K, :]
    for k in range(1, CONV_K):
        sh = jnp.where(rowi >= k, pltpu.roll(raw, k, 0), 0.0)
        pre = pre + sh * w8[CONV_K - 1 - k:CONV_K - k, :]
    return pre


def _conv_fwd(proj, cw8, nb, seq, name, after):
    ncol = CONV_DIM // CONV_COLS

    def body(x_ref, w_ref, tok_ref, o_ref):
        rowi = lax.broadcasted_iota(jnp.int32, (seq, 1), 0)
        pre = _conv_pre(x_ref[...], w_ref[...], rowi)
        o_ref[...] = pre * _sigmoid(pre)

    return pl.pallas_call(
        body, name=name, grid=(nb, ncol),
        in_specs=[pl.BlockSpec((seq, CONV_COLS), lambda b, j: (b, OFF_XS // CONV_COLS + j)),
                  pl.BlockSpec((8, CONV_COLS), lambda b, j: (0, j)),
                  pl.BlockSpec((8, LANE), lambda b, j: (0, 0))],
        out_specs=pl.BlockSpec((seq, CONV_COLS), lambda b, j: (b, j)),
        out_shape=jax.ShapeDtypeStruct((nb * seq, CONV_DIM), F32),
        compiler_params=_params(("parallel", "parallel")),
    )(proj, cw8, after)


def _conv_bwd(proj, dact, cw8, nb, seq, name):
    ncol = CONV_DIM // CONV_COLS

    def body(x_ref, d_ref, w_ref, dx_ref, dw_ref):
        @pl.when(pl.program_id(1) == 0)
        def _():
            dw_ref[...] = jnp.zeros_like(dw_ref)

        rowi = lax.broadcasted_iota(jnp.int32, (seq, 1), 0)
        raw = x_ref[...]
        w8 = w_ref[...]
        pre = _conv_pre(raw, w8, rowi)
        sg = _sigmoid(pre)
        dpre = d_ref[...] * (sg * (1.0 + pre * (1.0 - sg)))
        dw_ref[CONV_K:CONV_K + 1, :] += jnp.sum(dpre, axis=0, keepdims=True)
        dw_ref[CONV_K - 1:CONV_K, :] += jnp.sum(dpre * raw, axis=0, keepdims=True)
        draw = dpre * w8[CONV_K - 1:CONV_K, :]
        for k in range(1, CONV_K):
            sh = jnp.where(rowi >= k, pltpu.roll(raw, k, 0), 0.0)
            dw_ref[CONV_K - 1 - k:CONV_K - k, :] += jnp.sum(dpre * sh, axis=0, keepdims=True)
            up = jnp.where(rowi < seq - k, pltpu.roll(dpre, seq - k, 0), 0.0)
            draw = draw + up * w8[CONV_K - 1 - k:CONV_K - k, :]
        dx_ref[...] = draw.astype(BF16)

    return pl.pallas_call(
        body, name=name, grid=(ncol, nb),
        in_specs=[pl.BlockSpec((seq, CONV_COLS), lambda j, b: (b, OFF_XS // CONV_COLS + j)),
                  pl.BlockSpec((seq, CONV_COLS), lambda j, b: (b, j)),
                  pl.BlockSpec((8, CONV_COLS), lambda j, b: (0, j))],
        out_specs=[pl.BlockSpec((seq, CONV_COLS), lambda j, b: (b, j)),
                   pl.BlockSpec((8, CONV_COLS), lambda j, b: (0, j))],
        out_shape=[jax.ShapeDtypeStruct((nb * seq, CONV_DIM), BF16), jax.ShapeDtypeStruct((8, CONV_DIM), F32)],
        compiler_params=_params(("parallel", "arbitrary")),
    )(proj, dact, cw8)


def _pack_heads(dt_bias, a_log, d_skip):
    rows = jnp.stack([dt_bias, a_log, d_skip]).reshape(3, SSM_GROUPS, SSM_HG).transpose(1, 0, 2)
    return jnp.pad(rows, ((0, 0), (0, 8 - 3), (0, LANE - SSM_HG)))


def _split3_rows(x):
    hi = x.astype(BF16)
    r1 = x - hi.astype(F32)
    mid = r1.astype(BF16)
    lo = (r1 - mid.astype(F32)).astype(BF16)
    return jnp.concatenate([hi, mid, lo], axis=0)


def _split3_cols(x):
    hi = x.astype(BF16)
    r1 = x - hi.astype(F32)
    mid = r1.astype(BF16)
    lo = (r1 - mid.astype(F32)).astype(BF16)
    return jnp.concatenate([hi, mid, lo], axis=1)


def _split2_rows(x):
    hi = x.astype(BF16)
    return jnp.concatenate([hi, (x - hi.astype(F32)).astype(BF16)], axis=0)


def _ssd_specs(seq):
    gx = SSM_HG * SSM_P
    return dict(
        xs=pl.BlockSpec((seq, gx), lambda g, b: (b, g)),
        bm=pl.BlockSpec((seq, SSM_N), lambda g, b: (b, SSM_W // SSM_N + g)),
        cm=pl.BlockSpec((seq, SSM_N), lambda g, b: (b, SSM_W // SSM_N + SSM_GROUPS + g)),
        dt=pl.BlockSpec((seq, LANE), lambda g, b: (b, OFF_DT // LANE)),
        hp=pl.BlockSpec((1, 8, LANE), lambda g, b: (g, 0, 0)),
        head=pl.BlockSpec((seq, gx), lambda g, b: (b, g)),
        grp=pl.BlockSpec((seq, SSM_N), lambda g, b: (b, g)),
    )


SSM_GX = SSM_HG * SSM_P


def _ssd_masks():
    li = lax.broadcasted_iota(jnp.int32, (CHUNK, CHUNK), 0)
    si = lax.broadcasted_iota(jnp.int32, (CHUNK, CHUNK), 1)
    head = lax.broadcasted_iota(jnp.int32, (LANE, SSM_GX), 0)
    lane = lax.broadcasted_iota(jnp.int32, (LANE, SSM_GX), 1)
    expand = (lane // SSM_P == head).astype(BF16)
    head_t = lax.broadcasted_iota(jnp.int32, (SSM_GX, LANE), 1)
    lane_t = lax.broadcasted_iota(jnp.int32, (SSM_GX, LANE), 0)
    gather = (lane_t // SSM_P == head_t).astype(BF16)
    return dict(
        causal=li >= si, causal_t=si >= li,
        tril3=jnp.concatenate([(si <= li).astype(BF16)] * 3, axis=1),
        triu2=jnp.concatenate([(si >= li).astype(BF16)] * 2, axis=1),
        below2=jnp.concatenate([(si < li).astype(BF16)] * 2, axis=1),
        expand2=_stacked(expand), gather2=_stacked(gather))


def _per_head(x, mk):
    return _split_sum(x, mk["expand2"])


def _head_sums(x, mk):
    return _split_sum(x, mk["gather2"])


def _row8(v):
    return jnp.broadcast_to(v, (8, v.shape[1]))


def _group_dt(dt_ref):
    shift = (LANE - SSM_HG * pl.program_id(0)) % LANE
    return pltpu.roll(dt_ref[...], shift, 1)


def _ssd_fwd(act, proj, hp, nb, seq, name):
    nc = seq // CHUNK

    def body(xs_ref, b_ref, c_ref, dt_ref, hp_ref, y_ref, dt_s, da_s, hst):
        mk = _ssd_masks()
        hpv = hp_ref[0]
        dt = _softplus(_group_dt(dt_ref) + hpv[0:1, :])
        a = -jnp.exp(hpv[1:2, :])
        dsk_row = _per_head(_row8(hpv[2:3, :]), mk)[0:1]
        dt_s[...] = dt
        da_s[...] = dt * a
        hst[...] = jnp.zeros_like(hst)

        def chunk(c, _):
            rows = pl.ds(pl.multiple_of(c * CHUNK, CHUNK), CHUNK)
            acol = _dot(mk["tril3"], _split3_rows(da_s[rows, :]))
            arow = acol.T
            alast = acol[CHUNK - 1:CHUNK, :]
            ea = _per_head(jnp.exp(acol), mk)
            eb = _per_head(jnp.exp(alast - acol), mk)
            el = _per_head(_row8(jnp.exp(alast)), mk)[0:1]
            bb = b_ref[rows, :].astype(BF16)
            cb = c_ref[rows, :].astype(BF16)
            cbm = _dot_nt(cb, bb)
            xc = xs_ref[rows, :]
            u = xc * _per_head(dt_s[rows, :], mk)
            ub = u.astype(BF16)
            ht = hst[...]
            y_ref[rows, :] = ea * _dot(cb, ht.astype(BF16)) + dsk_row * xc
            for j in range(SSM_HG):
                sl = slice(j * SSM_P, (j + 1) * SSM_P)
                decay = jnp.where(mk["causal"], jnp.exp(jnp.minimum(acol[:, j:j + 1] - arow[j:j + 1, :], 0.0)), 0.0)
                y_ref[rows, sl] += _dot((cbm * decay).astype(BF16), ub[:, sl])
            hst[...] = el * ht + _dot_tn(bb, (u * eb).astype(BF16))
            return 0

        lax.fori_loop(0, nc, chunk, 0)

    sp = _ssd_specs(seq)
    return pl.pallas_call(
        body, name=name, grid=(SSM_GROUPS, nb),
        in_specs=[sp["xs"], sp["bm"], sp["cm"], sp["dt"], sp["hp"]],
        out_specs=sp["head"],
        out_shape=jax.ShapeDtypeStruct((nb * seq, SSM_W), F32),
        scratch_shapes=[pltpu.VMEM((seq, LANE), F32)] * 2 + [pltpu.VMEM((SSM_N, SSM_GX), F32)],
        compiler_params=_params(("parallel", "parallel")),
    )(act, act, act, proj, hp)


def _ssd_bwd(act, proj, dy, hp, nb, seq, name):
    nc = seq // CHUNK

    def body(xs_ref, b_ref, c_ref, dt_ref, hp_ref, dy_ref, dxs_ref, db_ref, dc_ref, ddt_ref, dhp_ref,
             dt_s, da_s, ddt_s, hs, lam, du_s):
        @pl.when(pl.program_id(1) == 0)
        def _():
            dhp_ref[...] = jnp.zeros_like(dhp_ref)

        mk = _ssd_masks()
        hpv = hp_ref[0]
        a = -jnp.exp(hpv[1:2, :])
        dsk_row = _per_head(_row8(hpv[2:3, :]), mk)[0:1]
        dt_s[...] = _softplus(_group_dt(dt_ref) + hpv[0:1, :])
        da_s[...] = dt_s[...] * a
        lane = lax.broadcasted_iota(jnp.int32, (1, LANE), 1)

        def chunk_rows(c):
            return pl.ds(pl.multiple_of(c * CHUNK, CHUNK), CHUNK)

        def decays(c):
            acol = _dot(mk["tril3"], _split3_rows(da_s[chunk_rows(c), :]))
            alast = acol[CHUNK - 1:CHUNK, :]
            return acol, alast

        hs[0] = jnp.zeros((SSM_N, SSM_GX), F32)

        def fwd_chunk(c, _):
            rows = chunk_rows(c)
            acol, alast = decays(c)
            eb = _per_head(jnp.exp(alast - acol), mk)
            el = _per_head(_row8(jnp.exp(alast)), mk)[0:1]
            u = xs_ref[rows, :] * _per_head(dt_s[rows, :], mk)
            hs[c + 1] = el * hs[c] + _dot_tn(b_ref[rows, :].astype(BF16), (u * eb).astype(BF16))
            return 0

        lax.fori_loop(0, nc - 1, fwd_chunk, 0)
        lam[...] = jnp.zeros_like(lam)

        def bwd_chunk(i, carry):
            dd_row, da_vec = carry
            c = nc - 1 - i
            rows = chunk_rows(c)
            acol, alast = decays(c)
            arow = acol.T
            ea = _per_head(jnp.exp(acol), mk)
            eb = _per_head(jnp.exp(alast - acol), mk)
            el = _per_head(_row8(jnp.exp(alast)), mk)[0:1]
            dt_all = _per_head(dt_s[rows, :], mk)
            bb = b_ref[rows, :].astype(BF16)
            cb = c_ref[rows, :].astype(BF16)
            cbm = _dot_nt(cb, bb)
            cbt = _dot_nt(bb, cb)
            xc = xs_ref[rows, :]
            dyc = dy_ref[rows, :]
            u = xc * dt_all
            ub = u.astype(BF16)
            dyb = dyc.astype(BF16)
            h_in = hs[c]
            lm = lam[...]
            hb = h_in.astype(BF16)
            lb = lm.astype(BF16)
            y_off = ea * _dot(cb, hb)
            du_off = eb * _dot(bb, lb)
            dye = (ea * dyc).astype(BF16)
            zero = jnp.zeros((CHUNK, CHUNK), F32)
            dcb, dcbt, d_a = zero, zero, zero
            for j in range(SSM_HG):
                sl = slice(j * SSM_P, (j + 1) * SSM_P)
                seg = acol[:, j:j + 1] - arow[j:j + 1, :]
                decay = jnp.where(mk["causal"], jnp.exp(jnp.minimum(seg, 0.0)), 0.0)
                decay_t = jnp.where(mk["causal_t"], jnp.exp(jnp.minimum(-seg, 0.0)), 0.0)
                m = cbm * decay
                mt = cbt * decay_t
                dm = _dot_nt(dyb[:, sl], ub[:, sl])
                dmt = _dot_nt(ub[:, sl], dyb[:, sl])
                dcb = dcb + dm * decay
                dcbt = dcbt + dmt * decay_t
                du_s[:, sl] = _dot(mt.astype(BF16), dyb[:, sl])
                d_a_j = jnp.sum(dm * m, axis=-1, keepdims=True) - jnp.sum(dmt * mt, axis=-1, keepdims=True)
                d_a = jnp.where(lane == j, d_a_j, d_a)
            du = du_s[...] + du_off
            dxs_ref[rows, :] = du * dt_all + dsk_row * dyc
            dc_ref[rows, :] = _dot_nt(dye, hb) + _dot(dcb.astype(BF16), bb)
            db_ref[rows, :] = _dot_nt((eb * u).astype(BF16), lb) + _dot(dcbt.astype(BF16), cb)
            lam[...] = el * lm + _dot_tn(cb, dye)
            d_a = d_a + _head_sums(dyc * y_off, mk)
            f_a = _head_sums(du_off * u, mk)
            c_a = jnp.exp(alast) * _head_sums(_row8(jnp.sum(lm * h_in, axis=0, keepdims=True)), mk)[0:1]
            dda = _dot(mk["triu2"], _split2_rows(d_a)) + _dot(mk["below2"], _split2_rows(f_a)) + c_a
            ddt_s[rows, :] = dda * a + _head_sums(du * xc, mk)
            da_vec = da_vec + jnp.sum(dda * dt_s[rows, :], axis=0, keepdims=True)
            dd_row = dd_row + jnp.sum(dyc * xc, axis=0, keepdims=True)
            return dd_row, da_vec

        init = (jnp.zeros((1, SSM_GX), F32), jnp.zeros((1, LANE), F32))
        dd_row, da_vec = lax.fori_loop(0, nc, bwd_chunk, init)
        ddt_raw = ddt_s[...] * _sigmoid(_group_dt(dt_ref) + hpv[0:1, :])
        ddt_ref[...] = ddt_raw.astype(BF16)
        dhp_ref[0, 0:1, :] += jnp.sum(ddt_raw, axis=0, keepdims=True)
        dhp_ref[0, 1:2, :] += da_vec * a
        dhp_ref[0, 2:3, :] += _head_sums(_row8(dd_row), mk)[0:1]

    sp = _ssd_specs(seq)
    t = nb * seq
    return pl.pallas_call(
        body, name=name, grid=(SSM_GROUPS, nb),
        in_specs=[sp["xs"], sp["bm"], sp["cm"], sp["dt"], sp["hp"], sp["head"]],
        out_specs=[sp["head"], sp["grp"], sp["grp"], sp["grp"], sp["hp"]],
        out_shape=[jax.ShapeDtypeStruct((t, SSM_W), F32), jax.ShapeDtypeStruct((t, SSM_GROUPS * SSM_N), F32),
                   jax.ShapeDtypeStruct((t, SSM_GROUPS * SSM_N), F32),
                   jax.ShapeDtypeStruct((t, SSM_GROUPS * LANE), BF16),
                   jax.ShapeDtypeStruct((SSM_GROUPS, 8, LANE), F32)],
        scratch_shapes=[pltpu.VMEM((seq, LANE), F32)] * 3
        + [pltpu.VMEM((nc, SSM_N, SSM_GX), F32), pltpu.VMEM((SSM_N, SSM_GX), F32), pltpu.VMEM((CHUNK, SSM_GX), F32)],
        compiler_params=_params(("parallel", "arbitrary")),
    )(act, act, act, proj, hp, dy)


ANY = pl.BlockSpec(memory_space=pl.ANY)


def _block_index(p):
    return 4 * p[0] + 2 * p[1] + p[2]


def _all_gather(shards, name):
    n = len(shards)

    def body(*refs):
        ins, outs = refs[:n], refs[n:2 * n]
        send_sems, recv_sems, local_sems = refs[2 * n:]
        x, y, c = lax.axis_index("x"), lax.axis_index("y"), lax.axis_index("c")
        me, sibling = (x, y, c), (x, y, 1 - c)
        chips = [(1 - x, y), (x, 1 - y), (1 - x, 1 - y)]

        def copy(i, k, block, to, src=None):
            dst = outs[i].at[_block_index(block)]
            return pltpu.make_async_remote_copy(
                src_ref=dst if src is None else src, dst_ref=dst,
                send_sem=send_sems.at[i, k], recv_sem=recv_sems.at[i, k],
                device_id=to, device_id_type=MESH)

        mine = [pltpu.make_async_copy(ins[i], outs[i].at[_block_index(me)], local_sems.at[i]) for i in range(n)]
        for cp in mine:
            cp.start()
        first = []
        for i in range(n):
            first.append(copy(i, 0, me, sibling, src=ins[i]))
            first += [copy(i, 1 + j, me, (*chip, c), src=ins[i]) for j, chip in enumerate(chips)]
        for cp in first:
            cp.start()
        passed = []
        for j, chip in enumerate(chips):
            for i in range(n):
                copy(i, 1 + j, (*chip, c), me).wait_recv()
                fwd = copy(i, 4 + j, (*chip, c), sibling)
                fwd.start()
                passed.append(fwd)
        for i in range(n):
            copy(i, 0, sibling, me).wait_recv()
            for j, chip in enumerate(chips):
                copy(i, 4 + j, (*chip, 1 - c), me).wait_recv()
        for cp in first + passed:
            cp.wait_send()
        for cp in mine:
            cp.wait()

    return pl.pallas_call(
        body, name=name,
        in_specs=[ANY] * n, out_specs=[ANY] * n,
        out_shape=[jax.ShapeDtypeStruct((N_DEV,) + s.shape, s.dtype) for s in shards],
        scratch_shapes=[pltpu.SemaphoreType.DMA((n, 7)), pltpu.SemaphoreType.DMA((n, 7)),
                        pltpu.SemaphoreType.DMA((n,))],
    )(*shards)


HBM = pl.BlockSpec(memory_space=pltpu.HBM)
SEM = pl.BlockSpec(memory_space=pltpu.SEMAPHORE)
EFFECT = pltpu.SideEffectType.DATAFLOW_SIDE_EFFECTING


def _my_block():
    return _block_index((lax.axis_index("x"), lax.axis_index("y"), lax.axis_index("c")))


def _peer(k):
    x, y, c = lax.axis_index("x"), lax.axis_index("y"), lax.axis_index("c")
    return (1 - x if k & 4 else x, 1 - y if k & 2 else y, 1 - c if k & 1 else c)


ALL_PEERS = tuple(range(1, N_DEV))
SIBLING = 1
SAME_CORE = (2, 4, 6)


def _plan_copies(plan, src_refs, land_refs, send_sems, recv_sems, peers=ALL_PEERS):
    me = _my_block()
    copies = []
    for e, (si, di, src_view, dst_view, _) in enumerate(plan):
        for k in peers:
            copies.append(pltpu.make_async_remote_copy(
                src_ref=src_view(src_refs[si], _block_index(_peer(k))),
                dst_ref=dst_view(land_refs[di], me),
                send_sem=send_sems[e], recv_sem=recv_sems[e],
                device_id=_peer(k), device_id_type=MESH))
    return copies


def _plan_forwards(plan, land_refs, send_sems, recv_sems):
    copies = []
    for e, (_, di, _, dst_view, _) in enumerate(plan):
        for k in SAME_CORE:
            part = dst_view(land_refs[di], _block_index(_peer(k)))
            copies.append(pltpu.make_async_remote_copy(
                src_ref=part, dst_ref=part, send_sem=send_sems[e], recv_sem=recv_sems[e],
                device_id=_peer(SIBLING), device_id_type=MESH))
    return copies


def _plan_waits(plan, land_refs, send_sems, recv_sems, n=N_DEV - 1):
    waits = []
    for e, (_, di, _, _, parts_view) in enumerate(plan):
        view = parts_view(land_refs[di], n)
        waits.append(pltpu.make_async_remote_copy(
            src_ref=view, dst_ref=view, send_sem=send_sems[e], recv_sem=recv_sems[e],
            device_id=_peer(SIBLING), device_id_type=MESH))
    return waits


def _plan_own(plan, src_refs, land_refs, own_sems):
    me = _my_block()
    return [pltpu.make_async_copy(src_view(src_refs[si], me), dst_view(land_refs[di], me), own_sems[e])
            for e, (si, di, src_view, dst_view, _) in enumerate(plan)]


def _copies_start(srcs, lands, plan, name, after=None):
    ns, nl, ne = len(srcs), len(lands), len(plan)
    extra = [] if after is None else [after]

    nin = ns + nl + len(extra)

    def body(*refs):
        src_refs, land_refs = refs[:ns], refs[ns:ns + nl]
        send_sems, recv_sems = refs[nin:nin + ne], refs[nin + ne:nin + 2 * ne]
        own_sems = refs[nin + 2 * ne:nin + 3 * ne]
        token = refs[-1]
        for cp in _plan_copies(plan, src_refs, land_refs, send_sems, recv_sems):
            cp.start()
        for cp in _plan_own(plan, src_refs, land_refs, own_sems):
            cp.start()
        token[...] = jnp.zeros_like(token)

    thru = [pltpu.HBM(a.shape, a.dtype) for a in list(srcs) + list(lands)]
    res = pl.pallas_call(
        body, name=name,
        in_specs=[HBM] * (ns + nl) + [ANY] * len(extra),
        out_specs=[SEM] * (3 * ne) + [HBM] * (ns + nl) + [pl.BlockSpec(memory_space=pltpu.VMEM)],
        out_shape=[pltpu.SemaphoreType.DMA(())] * (3 * ne) + thru + [jax.ShapeDtypeStruct((8, LANE), F32)],
        input_output_aliases={i: 3 * ne + i for i in range(ns + nl)},
        compiler_params=pltpu.CompilerParams(has_side_effects=EFFECT),
    )(*[pltpu.with_memory_space_constraint(a, pltpu.HBM) for a in list(srcs) + list(lands)], *extra)
    return dict(sems=res[:3 * ne], srcs=res[3 * ne:3 * ne + ns], lands=res[3 * ne + ns:3 * ne + ns + nl],
                token=res[-1], plan=plan)


def _copies_wait(flight, after, name):
    srcs, lands, plan = flight["srcs"], flight["lands"], flight["plan"]
    ns, nl, ne = len(srcs), len(lands), len(plan)

    def body(*refs):
        src_refs, land_refs = refs[:ns], refs[ns:ns + nl]
        send_sems, recv_sems = refs[ns + nl:ns + nl + ne], refs[ns + nl + ne:ns + nl + 2 * ne]
        own_sems = refs[ns + nl + 2 * ne:ns + nl + 3 * ne]
        for cp in _plan_waits(plan, land_refs, send_sems, recv_sems):
            cp.wait_send()
            cp.wait_recv()
        for cp in _plan_own(plan, src_refs, land_refs, own_sems):
            cp.wait()

    after = list(after) if isinstance(after, (list, tuple)) else [after]
    thru = [pltpu.HBM(a.shape, a.dtype) for a in list(srcs) + list(lands)]
    res = pl.pallas_call(
        body, name=name,
        in_specs=[HBM] * (ns + nl) + [SEM] * (3 * ne) + [ANY] * len(after),
        out_specs=[HBM] * (ns + nl),
        out_shape=thru,
        input_output_aliases={i: i for i in range(ns + nl)},
        compiler_params=pltpu.CompilerParams(has_side_effects=EFFECT),
    )(*srcs, *lands, *flight["sems"], *after)
    return list(res[ns:])


def _gather2_start(srcs, lands, plan, name, after=None):
    ns, nl, ne = len(srcs), len(lands), len(plan)
    extra = [] if after is None else [after]
    nin = ns + nl + len(extra)

    def body(*refs):
        src_refs, land_refs = refs[:ns], refs[ns:ns + nl]
        send_sems, recv_sems = refs[nin:nin + ne], refs[nin + ne:nin + 2 * ne]
        own_sems = refs[nin + 2 * ne:nin + 3 * ne]
        for cp in _plan_copies(plan, src_refs, land_refs, send_sems, recv_sems, (SIBLING,) + SAME_CORE):
            cp.start()
        for cp in _plan_own(plan, src_refs, land_refs, own_sems):
            cp.start()
        refs[-1][...] = jnp.zeros_like(refs[-1])

    thru = [pltpu.HBM(a.shape, a.dtype) for a in list(srcs) + list(lands)]
    res = pl.pallas_call(
        body, name=name,
        in_specs=[HBM] * (ns + nl) + [ANY] * len(extra),
        out_specs=[SEM] * (3 * ne) + [HBM] * (ns + nl) + [pl.BlockSpec(memory_space=pltpu.VMEM)],
        out_shape=[pltpu.SemaphoreType.DMA(())] * (3 * ne) + thru + [jax.ShapeDtypeStruct((8, LANE), F32)],
        input_output_aliases={i: 3 * ne + i for i in range(ns + nl)},
        compiler_params=pltpu.CompilerParams(has_side_effects=EFFECT),
    )(*[pltpu.with_memory_space_constraint(a, pltpu.HBM) for a in list(srcs) + list(lands)], *extra)
    return dict(send1=res[:ne], recv1=res[ne:2 * ne], own=res[2 * ne:3 * ne], srcs=res[3 * ne:3 * ne + ns],
                lands=res[3 * ne + ns:3 * ne + ns + nl], token=res[-1], plan=plan)


def _gather2_forward(flight, after, name):
    srcs, lands, plan = flight["srcs"], flight["lands"], flight["plan"]
    ns, nl, ne = len(srcs), len(lands), len(plan)
    nin = ns + nl + ne + 1

    def body(*refs):
        land_refs = refs[ns:ns + nl]
        recv1 = refs[ns + nl:ns + nl + ne]
        send2, recv2 = refs[nin:nin + ne], refs[nin + ne:nin + 2 * ne]
        for cp in _plan_waits(plan, land_refs, send2, recv1, n=1 + len(SAME_CORE)):
            cp.wait_recv()
        for cp in _plan_forwards(plan, land_refs, send2, recv2):
            cp.start()
        refs[-1][...] = jnp.zeros_like(refs[-1])

    thru = [pltpu.HBM(a.shape, a.dtype) for a in list(srcs) + list(lands)]
    res = pl.pallas_call(
        body, name=name,
        in_specs=[HBM] * (ns + nl) + [SEM] * ne + [ANY],
        out_specs=[SEM] * (2 * ne) + [HBM] * (ns + nl) + [pl.BlockSpec(memory_space=pltpu.VMEM)],
        out_shape=[pltpu.SemaphoreType.DMA(())] * (2 * ne) + thru + [jax.ShapeDtypeStruct((8, LANE), F32)],
        input_output_aliases={i: 2 * ne + i for i in range(ns + nl)},
        compiler_params=pltpu.CompilerParams(has_side_effects=EFFECT),
    )(*srcs, *lands, *flight["recv1"], after)
    return dict(flight, send2=res[:ne], recv2=res[ne:2 * ne], srcs=res[2 * ne:2 * ne + ns],
                lands=res[2 * ne + ns:2 * ne + ns + nl], token=res[-1])


def _gather2_wait(flight, after, name):
    srcs, lands, plan = flight["srcs"], flight["lands"], flight["plan"]
    ns, nl, ne = len(srcs), len(lands), len(plan)

    def body(*refs):
        src_refs, land_refs = refs[:ns], refs[ns:ns + nl]
        sems = refs[ns + nl:ns + nl + 4 * ne]
        send1, own, send2, recv2 = sems[:ne], sems[ne:2 * ne], sems[2 * ne:3 * ne], sems[3 * ne:]
        for cp in _plan_waits(plan, land_refs, send1, recv2, n=1 + len(SAME_CORE)):
            cp.wait_send()
        for cp in _plan_waits(plan, land_refs, send2, recv2, n=len(SAME_CORE)):
            cp.wait_send()
            cp.wait_recv()
        for cp in _plan_own(plan, src_refs, land_refs, own):
            cp.wait()

    thru = [pltpu.HBM(a.shape, a.dtype) for a in list(srcs) + list(lands)]
    res = pl.pallas_call(
        body, name=name,
        in_specs=[HBM] * (ns + nl) + [SEM] * (4 * ne) + [ANY],
        out_specs=[HBM] * (ns + nl),
        out_shape=thru,
        input_output_aliases={i: i for i in range(ns + nl)},
        compiler_params=pltpu.CompilerParams(has_side_effects=EFFECT),
    )(*srcs, *lands, *flight["send1"], *flight["own"], *flight["send2"], *flight["recv2"], after)
    return list(res[ns:])


def _adamw_math(w, g, m, v):
    m = ADAM_B1 * m + (1.0 - ADAM_B1) * g
    v = ADAM_B2 * v + (1.0 - ADAM_B2) * (g * g)
    m_hat = m / (1.0 - ADAM_B1 ** ADAM_STEP)
    v_hat = v / (1.0 - ADAM_B2 ** ADAM_STEP)
    delta = -ADAM_LR * (m_hat / (jnp.sqrt(v_hat) + ADAM_EPS) + ADAM_WD * w)
    return delta, m, v


def _adamw(parts, w, m, v, name, rows, lane_offset=None):
    depth, r, c = w.shape
    cp = parts[0].shape[2]
    assert r % rows == 0 and len(parts) == depth

    def body(*refs):
        p_refs = refs[:depth]
        w_ref, m_ref, v_ref, g_ref, d_ref, mo_ref, vo_ref = refs[depth:]
        if lane_offset is not None:
            cw = -(-c // LANE) * LANE
            src = lax.broadcasted_iota(jnp.int32, (cp, cw), 0)
            dst = lax.broadcasted_iota(jnp.int32, (cp, cw), 1)
            pick = (src == dst + lane_offset()).astype(BF16)
            pick3 = jnp.concatenate([pick] * 3, axis=0)
        for li in range(depth):
            @pl.when(pl.program_id(0) == li)
            def _(li=li):
                g = p_refs[li][0].astype(F32)
                for j in range(1, N_DEV):
                    g = g + p_refs[li][j].astype(F32)
                if lane_offset is not None:
                    g = _dot(_split3_cols(g), pick3)
                g = g[:, :c]
                d, mn, vn = _adamw_math(w_ref[...], g, m_ref[...], v_ref[...])
                g_ref[...] = g
                d_ref[...] = d
                mo_ref[...] = mn
                vo_ref[...] = vn

    def part_spec(li):
        return pl.BlockSpec((N_DEV, rows, cp), lambda l, i: (0, jnp.where(l == li, i, 0), 0))

    blk = pl.BlockSpec((None, rows, c), lambda l, i: (l, i, 0))
    out = jax.ShapeDtypeStruct((depth, r, c), F32)
    return pl.pallas_call(
        body, name=name, grid=(depth, r // rows),
        in_specs=[part_spec(li) for li in range(depth)] + [blk, blk, blk],
        out_specs=[blk] * 4, out_shape=[out] * 4,
        compiler_params=_params(("arbitrary", "arbitrary")),
    )(*parts, w, m, v)


def _sum_parts(parts, name):
    _, r, c = parts.shape

    def body(p_ref, o_ref):
        g = p_ref[0]
        for j in range(1, N_DEV):
            g = g + p_ref[j]
        o_ref[...] = g

    return pl.pallas_call(
        body, name=name, out_shape=jax.ShapeDtypeStruct((r, c), F32),
        compiler_params=_params(),
    )(parts)


def _adamw_small(parts, ws, ms, vs, name):
    n = len(ws)

    def body(*refs):
        ins, outs = refs[:4 * n], refs[4 * n:]
        for i in range(n):
            p_ref, w_ref, m_ref, v_ref = ins[i], ins[n + i], ins[2 * n + i], ins[3 * n + i]
            g = p_ref[0]
            for j in range(1, p_ref.shape[0]):
                g = g + p_ref[j]
            d, mn, vn = _adamw_math(w_ref[...], g, m_ref[...], v_ref[...])
            outs[i][...] = g
            outs[n + i][...] = d
            outs[2 * n + i][...] = mn
            outs[3 * n + i][...] = vn

    out = [jax.ShapeDtypeStruct(a.shape, F32) for a in ws] * 4
    res = pl.pallas_call(body, name=name, out_shape=out, compiler_params=_params())(*parts, *ws, *ms, *vs)
    return res[:n], res[n:2 * n], res[2 * n:3 * n], res[3 * n:]


SMALL = ("norm_mix", "q_gain", "k_gain", "conv_b", "dt_bias", "a_log", "d_skip", "attn_out_gain",
         "ssm_out_gain", "norm_ffn")


def _full_cols(gathered):
    _, r, c = gathered.shape
    return gathered.transpose(1, 0, 2).reshape(r, N_DEV * c)


FF_BLK = 768
FF_PAD = N_DEV * FF_BLK
W_IN_COLS = IN_DIM // N_DEV
W_IN_WINDOW = 896


def _w_in_window_start(block):
    return (block * W_IN_COLS // LANE) * LANE


def _w_in_window(ref, block):
    return ref.at[:, pl.ds(pl.multiple_of(_w_in_window_start(block), LANE), W_IN_WINDOW)]


def _whole(ref, block):
    return ref


def _rows_of(size):
    return lambda ref, block: ref.at[pl.ds(pl.multiple_of(block * size, size), size), :]


def _slot(ref, block):
    return ref.at[block]


def _n_slots(ref, n):
    return ref.at[pl.ds(0, n)]


def _n_rows(size):
    return lambda ref, n: ref.at[pl.ds(0, n * size), :]


GATHER_A = [(0, 0, _whole, _slot, _n_slots), (1, 1, _whole, _rows_of(256), _n_rows(256))]
GATHER_B = [(i, i, _whole, _rows_of(FF_BLK), _n_rows(FF_BLK)) for i in range(3)]
SCATTER_A = [(0, 0, _w_in_window, _slot, _n_slots), (1, 1, _rows_of(256), _slot, _n_slots)]
SCATTER_B = [(i, i, _rows_of(FF_BLK), _slot, _n_slots) for i in range(3)]


def _gather_lands(which, shards, d):
    if which == "a":
        return [lax.empty((N_DEV,) + shards[0].shape, BF16), lax.empty((d, d), BF16)]
    return [lax.empty((FF_PAD, d), BF16) for _ in range(3)]


def _scatter_lands(which, grads):
    if which == "a":
        g_in, g_out = grads
        return [lax.empty((N_DEV, g_in.shape[0], W_IN_WINDOW), BF16),
                lax.empty((N_DEV, g_out.shape[0] // N_DEV, g_out.shape[1]), BF16)]
    return [lax.empty((N_DEV, FF_BLK, g.shape[1]), BF16) for g in grads]


def _pad_w_in(full):
    return jnp.pad(full, ((0, 0), (0, NPROJ - IN_DIM)))


def kernel(x, norm_mix, w_in, q_gain, k_gain, conv_w, conv_b, dt_bias, a_log, d_skip, attn_out_gain, ssm_out_gain, w_out, norm_ffn, w_gate, w_up, w_down, loss_target, m_norm_mix, m_w_in, m_q_gain, m_k_gain, m_conv_w, m_conv_b, m_dt_bias, m_a_log, m_d_skip, m_attn_out_gain, m_ssm_out_gain, m_w_out, m_norm_ffn, m_w_gate, m_w_up, m_w_down, v_norm_mix, v_w_in, v_q_gain, v_k_gain, v_conv_w, v_conv_b, v_dt_bias, v_a_log, v_d_skip, v_attn_out_gain, v_ssm_out_gain, v_w_out, v_norm_ffn, v_w_gate, v_w_up, v_w_down):
    nb, seq, d = x.shape
    depth = w_in.shape[0]
    t = nb * seq
    w = dict(norm_mix=norm_mix, w_in=w_in, q_gain=q_gain, k_gain=k_gain, conv_w=conv_w, conv_b=conv_b,
             dt_bias=dt_bias, a_log=a_log, d_skip=d_skip, attn_out_gain=attn_out_gain, ssm_out_gain=ssm_out_gain,
             w_out=w_out, norm_ffn=norm_ffn, w_gate=w_gate, w_up=w_up, w_down=w_down)
    mom = dict(norm_mix=m_norm_mix, w_in=m_w_in, q_gain=m_q_gain, k_gain=m_k_gain, conv_w=m_conv_w, conv_b=m_conv_b,
               dt_bias=m_dt_bias, a_log=m_a_log, d_skip=m_d_skip, attn_out_gain=m_attn_out_gain,
               ssm_out_gain=m_ssm_out_gain, w_out=m_w_out, norm_ffn=m_norm_ffn, w_gate=m_w_gate, w_up=m_w_up,
               w_down=m_w_down)
    var = dict(norm_mix=v_norm_mix, w_in=v_w_in, q_gain=v_q_gain, k_gain=v_k_gain, conv_w=v_conv_w, conv_b=v_conv_b,
               dt_bias=v_dt_bias, a_log=v_a_log, d_skip=v_d_skip, attn_out_gain=v_attn_out_gain,
               ssm_out_gain=v_ssm_out_gain, w_out=v_w_out, norm_ffn=v_norm_ffn, w_gate=v_w_gate, w_up=v_w_up,
               w_down=v_w_down)
    ff = w_gate.shape[2]

    (conv_all,) = _all_gather([conv_w], "gather_conv")

    def shards_a(li):
        return [w_in[li].astype(BF16), w_out[li].astype(BF16)]

    w["w_gate"], mom["w_gate"], var["w_gate"] = (jnp.swapaxes(a, 1, 2) for a in (w_gate, m_w_gate, v_w_gate))
    w["w_up"], mom["w_up"], var["w_up"] = (jnp.swapaxes(a, 1, 2) for a in (w_up, m_w_up, v_w_up))

    def shards_b(li):
        return [jnp.pad(w[k][li].astype(BF16), ((0, FF_BLK - ff), (0, 0))) for k in ("w_gate", "w_up", "w_down")]

    def small_params(li):
        p = {k: w[k][li][None, :] for k in ("norm_mix", "q_gain", "k_gain", "attn_out_gain", "ssm_out_gain", "norm_ffn")}
        conv_full = conv_all[:, li].transpose(1, 0, 2).reshape(CONV_K, CONV_DIM)
        p["cw8"] = _pack_conv(conv_full, conv_b[li])
        p["hp"] = _pack_heads(dt_bias[li], a_log[li], d_skip[li])
        return p

    xc = x.reshape(t, d)
    cur = shards_a(0)
    flight = _gather2_start(cur, _gather_lands("a", cur, d), GATHER_A, "gather_a0")
    flight = _gather2_forward(flight, xc, "gather_a0_fwd")
    lands_a = _gather2_wait(flight, flight["token"], "gather_a0_wait")
    layers, saved = [], []
    for li in range(depth):
        tag = f"l{li}_"
        p = small_params(li)
        p["w_in"] = _pad_w_in(_full_cols(lands_a[0]))
        p["w_out"] = lands_a[1]
        cur = shards_b(li)
        flight = _gather2_start(cur, _gather_lands("b", cur, d), GATHER_B, tag + "gather_b", after=lands_a[1])
        h1 = _rms_fwd(xc, p["norm_mix"], tag + "rms1", after=flight["token"])
        proj = _matmul(h1, p["w_in"], "nn", F32, tag + "mm_in")
        o = _attn_fwd(proj, p["q_gain"], p["k_gain"], nb, seq, tag + "attn")
        flight = _gather2_forward(flight, o, tag + "gather_b_fwd")
        act = _conv_fwd(proj, p["cw8"], nb, seq, tag + "conv", after=flight["token"])
        y = _ssd_fwd(act, proj, p["hp"], nb, seq, tag + "ssd")
        cat = _mix_fwd(o, y, proj, p["attn_out_gain"], p["ssm_out_gain"], tag + "mix")
        x1 = _matmul(cat, p["w_out"], "nn", F32, tag + "mm_out", residual=xc)
        p["w_gate"], p["w_up"], p["w_down"] = _gather2_wait(flight, x1, tag + "gather_b_wait")
        token = None
        if li + 1 < depth:
            nxt = shards_a(li + 1)
            flight = _gather2_start(nxt, _gather_lands("a", nxt, d), GATHER_A, f"gather_a{li + 1}",
                                    after=p["w_down"])
            token = flight["token"]
        h2 = _rms_fwd(x1, p["norm_ffn"], tag + "rms2", after=token)
        a, gate, up = _mm_swiglu(h2, p["w_gate"], p["w_up"], tag + "mm_gu")
        token = None
        if li + 1 < depth:
            flight = _gather2_forward(flight, gate, f"gather_a{li + 1}_fwd")
            token = flight["token"]
        x2 = _matmul(a, p["w_down"], "nn", F32, tag + "mm_down", residual=x1, after=token)
        if li + 1 < depth:
            lands_a = _gather2_wait(flight, x2, f"gather_a{li + 1}_wait")
        saved.append(dict(x=xc, h1=h1, proj=proj, o=o, act=act, y=y, cat=cat, x1=x1, h2=h2, gate=gate, up=up, a=a))
        layers.append(p)
        xc = x2

    loss_blk, dx, dxb = _loss_fwd_bwd(xc, loss_target.reshape(t, d), "loss")
    loss = lax.psum(loss_blk[0, 0], ("x", "y", "c"))

    grads = [dict() for _ in range(depth)]
    recv = [dict() for _ in range(depth)]
    flight_a, token = None, None
    for li in reversed(range(depth)):
        tag = f"l{li}_b_"
        p, s, g = layers[li], saved[li], grads[li]
        dgate, dup = _mm_dact_swiglu(dxb, p["w_down"], s["gate"], s["up"], tag + "mm_dact", after=token)
        g_down = _matmul(s["a"], dxb, "tn", BF16, tag + "mm_dwd")
        dh2 = _matmul(dgate, p["w_gate"], "nn", F32, tag + "mm_dh2g")
        dh2 = _matmul(dup, p["w_up"], "nn", F32, tag + "mm_dh2u", residual=dh2)
        g_gate = _matmul(dgate, s["h2"], "tn", BF16, tag + "mm_dwg")
        g_up = _matmul(dup, s["h2"], "tn", BF16, tag + "mm_dwu")
        if flight_a is not None:
            recv[li + 1]["w_in"], recv[li + 1]["w_out"] = _copies_wait(flight_a, g_up, f"l{li + 1}_b_scatter_a_wait")
        grads_b = [g_gate, g_up, g_down]
        flight_b = _copies_start(grads_b, _scatter_lands("b", grads_b), SCATTER_B,
                                 tag + "scatter_b", after=recv[li + 1]["w_out"] if li + 1 < depth else None)
        dx1, dx1b, g["norm_ffn"] = _rms_bwd(s["x1"], p["norm_ffn"], dh2, dx, tag + "rms2")
        dcat = _matmul(dx1b, p["w_out"], "nt", F32, tag + "mm_dcat", after=flight_b["token"])
        g_out = _matmul(s["cat"], dx1b, "tn", BF16, tag + "mm_dwo")
        do, dy, dz, g["attn_out_gain"], g["ssm_out_gain"] = _mix_bwd(
            dcat, s["o"], s["y"], s["proj"], p["attn_out_gain"], p["ssm_out_gain"], tag + "mix")
        dq, dk, dv, g["q_gain"], g["k_gain"] = _attn_bwd(s["proj"], do, p["q_gain"], p["k_gain"], nb, seq, tag + "attn")
        dxa, dba, dca, ddt, dhp = _ssd_bwd(s["act"], s["proj"], dy, p["hp"], nb, seq, tag + "ssd")
        dxbc, dcw8 = _conv_bwd(s["proj"], jnp.concatenate([dxa, dba, dca], axis=1), p["cw8"], nb, seq, tag + "conv")
        g["conv_w"] = dcw8[0:CONV_K]
        g["conv_b"] = dcw8[CONV_K:CONV_K + 1]
        heads = dhp[:, 0:3, 0:SSM_HG].transpose(1, 0, 2).reshape(3, SSM_HEADS)
        g["dt_bias"], g["a_log"], g["d_skip"] = heads[0:1], heads[1:2], heads[2:3]
        ddt = ddt[:, :LANE] + jnp.roll(ddt[:, LANE:], SSM_HG, axis=1)
        tail = jnp.zeros((t, NPROJ - OFF_DT - LANE), BF16)
        dproj = jnp.concatenate([dq, dk, dv, dz, dxbc, ddt, tail], axis=1)
        recv[li]["w_gate"], recv[li]["w_up"], recv[li]["w_down"] = _copies_wait(flight_b, dproj, tag + "scatter_b_wait")
        dh1 = _matmul(dproj, p["w_in"], "nt", F32, tag + "mm_dh1")
        g_in = _matmul(s["h1"], dproj, "tn", BF16, tag + "mm_dwin")
        flight_a = _copies_start([g_in, g_out], _scatter_lands("a", [g_in, g_out]), SCATTER_A, tag + "scatter_a")
        token = flight_a["token"]
        dx, dxb, g["norm_mix"] = _rms_bwd(s["x"], p["norm_mix"], dh1, dx1, tag + "rms1")
    grad_x = dx.reshape(nb, seq, d)

    out_g, out_d, out_m, out_v = {}, {}, {}, {}

    def update(k, rows, lane_offset=None):
        parts = [recv[li][k] for li in range(depth)]
        out_g[k], out_d[k], out_m[k], out_v[k] = _adamw(parts, w[k], mom[k], var[k], "adamw_" + k, rows, lane_offset)

    def w_in_offset():
        return _my_block() * W_IN_COLS - _w_in_window_start(_my_block())

    update("w_gate", 64)
    update("w_up", 64)
    update("w_down", 64)
    recv[0]["w_in"], recv[0]["w_out"] = _copies_wait(
        flight_a, [out_g["w_gate"], out_g["w_up"], out_g["w_down"], dx], "l0_b_scatter_a_wait")
    update("w_in", 128, w_in_offset)
    update("w_out", 128)
    for res in (out_g, out_d, out_m, out_v):
        res["w_gate"], res["w_up"] = jnp.swapaxes(res["w_gate"], 1, 2), jnp.swapaxes(res["w_up"], 1, 2)

    small_g = [jnp.concatenate([grads[li][k] for li in range(depth)], axis=0) for k in SMALL]
    conv_g = jnp.stack([grads[li]["conv_w"] for li in range(depth)]).reshape(depth, CONV_K * CONV_DIM)
    parts = _all_gather(small_g + [conv_g], "gather_small_grads")
    res = _adamw_small(parts[:-1], [w[k] for k in SMALL], [mom[k] for k in SMALL], [var[k] for k in SMALL],
                       "adamw_small")
    for dst, vals in zip((out_g, out_d, out_m, out_v), res):
        dst.update(dict(zip(SMALL, vals)))
    conv_total = _sum_parts(parts[-1], "sum_conv_grads").reshape(depth, CONV_K, CONV_DIM)
    cshard = conv_w.shape[2]
    conv_mine = lax.dynamic_slice_in_dim(conv_total, _my_block() * cshard, cshard, axis=2)
    flat = lambda a: a.reshape(depth, CONV_K * cshard)
    res = _adamw_small([flat(conv_mine)[None]], [flat(conv_w)], [flat(m_conv_w)], [flat(v_conv_w)], "adamw_conv")
    for dst, vals in zip((out_g, out_d, out_m, out_v), res):
        dst["conv_w"] = vals[0].reshape(depth, CONV_K, cshard)

    names = ("norm_mix", "w_in", "q_gain", "k_gain", "conv_w", "conv_b", "dt_bias", "a_log", "d_skip",
             "attn_out_gain", "ssm_out_gain", "w_out", "norm_ffn", "w_gate", "w_up", "w_down")
    return (loss, grad_x, *[out_g[k] for k in names], *[out_d[k] for k in names],
            *[out_m[k] for k in names], *[out_v[k] for k in names])
```

```python
import functools
import math

import jax
import jax.numpy as jnp
from jax import lax
from jax.experimental import pallas as pl
from jax.experimental.pallas import tpu as pltpu

F32 = jnp.float32
BF16 = jnp.bfloat16
MESH = pl.DeviceIdType.MESH

N_DEV = 8
EPS = 1e-6
ATT_HEADS = 8
ATT_DH = 128
ATT_W = ATT_HEADS * ATT_DH
SSM_W = 1024
SSM_P = 64
SSM_N = 128
SSM_GROUPS = 2
SSM_HG = 8
SSM_HEADS = SSM_GROUPS * SSM_HG
CHUNK = 128
CONV_K = 4
CONV_DIM = SSM_W + 2 * SSM_GROUPS * SSM_N
LANE = 128
OFF_Q, OFF_K, OFF_V, OFF_Z, OFF_XS = 0, ATT_W, 2 * ATT_W, 3 * ATT_W, 4 * ATT_W
OFF_B = OFF_XS + SSM_W
OFF_C = OFF_B + SSM_GROUPS * SSM_N
OFF_DT = OFF_C + SSM_GROUPS * SSM_N
NPROJ = 6144
IN_DIM = OFF_DT + SSM_HEADS

ADAM_LR = 0.001
ADAM_B1 = 0.9
ADAM_B2 = 0.999
ADAM_EPS = 1e-08
ADAM_WD = 0.01
ADAM_STEP = 10

VMEM_LIMIT = 56 * 1024 * 1024
MATMUL_OPERAND_BYTES = 26 * 1024 * 1024


def _params(sem=None):
    return pltpu.CompilerParams(dimension_semantics=sem, vmem_limit_bytes=VMEM_LIMIT)


def _pick(dim, target):
    if dim <= target:
        return dim
    best = None
    for t in range(LANE, target + 1, LANE):
        if dim % t == 0:
            best = t
    assert best is not None, (dim, target)
    return best


def _dot(a, b, dims=((1,), (0,))):
    return lax.dot_general(a, b, (dims, ((), ())), preferred_element_type=F32)


def _dot_nt(a, b):
    return _dot(a, b, ((1,), (1,)))


def _dot_tn(a, b):
    return _dot(a, b, ((0,), (0,)))


def _sigmoid(x):
    return 1.0 / (1.0 + jnp.exp(-x))


def _softplus(x):
    return jnp.maximum(x, 0.0) + jnp.log(1.0 + jnp.exp(-jnp.abs(x)))


def _rstd(x):
    return lax.rsqrt(jnp.mean(x * x, axis=-1, keepdims=True) + EPS)


def _matmul(a, b, mode, out_dtype, name, residual=None, after=None, tm=512, tn=1024, tk=2048):
    if mode == "nn":
        (m, k), (k2, n) = a.shape, b.shape
    elif mode == "nt":
        (m, k), (n, k2) = a.shape, b.shape
    else:
        (k, m), (k2, n) = a.shape, b.shape
    assert k == k2, (a.shape, b.shape, mode)
    tm, tn, tk = _pick(m, tm), _pick(n, tn), _pick(k, tk)
    for cand_tn in (tn, _pick(n, tn // 2)):
        if 2 * 2 * (tm * k + k * cand_tn) <= MATMUL_OPERAND_BYTES:
            tn, tk = cand_tn, k
            break
    nk = k // tk
    dims = {"nn": ((1,), (0,)), "nt": ((1,), (1,)), "tn": ((0,), (0,))}[mode]
    has_res = residual is not None

    has_tok = after is not None

    def body(*refs):
        a_ref, b_ref = refs[:2]
        r_ref = refs[2] if has_res else None
        o_ref = refs[2 + has_res + has_tok]
        prod = _dot(a_ref[...], b_ref[...], dims)

        def finish(r):
            if r_ref is not None:
                r = r + r_ref[...]
            o_ref[...] = r.astype(o_ref.dtype)

        if nk == 1:
            finish(prod)
        else:
            acc = refs[-1]
            kk = pl.program_id(2)

            @pl.when(kk == 0)
            def _():
                acc[...] = prod

            @pl.when(kk > 0)
            def _():
                acc[...] += prod

            @pl.when(kk == nk - 1)
            def _():
                finish(acc[...])

    if mode == "tn":
        a_spec = pl.BlockSpec((tk, tm), lambda i, j, kk: (kk, i))
    else:
        a_spec = pl.BlockSpec((tm, tk), lambda i, j, kk: (i, kk))
    if mode == "nt":
        b_spec = pl.BlockSpec((tn, tk), lambda i, j, kk: (j, kk))
    else:
        b_spec = pl.BlockSpec((tk, tn), lambda i, j, kk: (kk, j))
    o_spec = pl.BlockSpec((tm, tn), lambda i, j, kk: (i, j))
    tok_spec = pl.BlockSpec((8, LANE), lambda i, j, kk: (0, 0))
    in_specs = [a_spec, b_spec] + ([o_spec] if has_res else []) + ([tok_spec] if has_tok else [])
    args = (a, b) + ((residual,) if has_res else ()) + ((after,) if has_tok else ())
    return pl.pallas_call(
        body,
        name=name,
        grid=(m // tm, n // tn, nk),
        in_specs=in_specs,
        out_specs=o_spec,
        out_shape=jax.ShapeDtypeStruct((m, n), out_dtype),
        scratch_shapes=[pltpu.VMEM((tm, tn), F32)] if nk > 1 else [],
        compiler_params=_params(("parallel", "parallel", "arbitrary")),
    )(*args)


def _mm_swiglu(h, wg_t, wu_t, name, tm=512, tn=1024):
    m, k = h.shape
    n = wg_t.shape[0]
    tm, tn = _pick(m, tm), _pick(n, tn)

    def body(h_ref, g_ref, u_ref, a_ref, gs_ref, us_ref):
        hv = h_ref[...]
        g = _dot_nt(hv, g_ref[...])
        u = _dot_nt(hv, u_ref[...])
        a_ref[...] = (g * _sigmoid(g) * u).astype(BF16)
        gs_ref[...] = g.astype(BF16)
        us_ref[...] = u.astype(BF16)

    w_spec = pl.BlockSpec((tn, k), lambda i, j: (j, 0))
    o_spec = pl.BlockSpec((tm, tn), lambda i, j: (i, j))
    out = jax.ShapeDtypeStruct((m, n), BF16)
    return pl.pallas_call(
        body, name=name, grid=(m // tm, n // tn),
        in_specs=[pl.BlockSpec((tm, k), lambda i, j: (i, 0)), w_spec, w_spec],
        out_specs=[o_spec] * 3, out_shape=[out] * 3,
        compiler_params=_params(("parallel", "parallel")),
    )(h, wg_t, wu_t)


def _mm_dact_swiglu(dx, wd, gs, us, name, after=None, tm=512, tn=1024):
    m, k = dx.shape
    n = wd.shape[0]
    tm, tn = _pick(m, tm), _pick(n, tn)
    has_tok = after is not None

    def body(*refs):
        dx_ref, wd_ref, g_ref, u_ref = refs[:4]
        dg_ref, du_ref = refs[-2:]
        dact = _dot_nt(dx_ref[...], wd_ref[...])
        g = g_ref[...].astype(F32)
        sg = _sigmoid(g)
        dg_ref[...] = (dact * u_ref[...].astype(F32) * sg * (1.0 + g * (1.0 - sg))).astype(BF16)
        du_ref[...] = (dact * g * sg).astype(BF16)

    o_spec = pl.BlockSpec((tm, tn), lambda i, j: (i, j))
    tok = [pl.BlockSpec((8, LANE), lambda i, j: (0, 0))] if has_tok else []
    out = jax.ShapeDtypeStruct((m, n), BF16)
    return pl.pallas_call(
        body, name=name, grid=(m // tm, n // tn),
        in_specs=[pl.BlockSpec((tm, k), lambda i, j: (i, 0)), pl.BlockSpec((tn, k), lambda i, j: (j, 0)),
                  o_spec, o_spec] + tok,
        out_specs=[o_spec] * 2, out_shape=[out] * 2,
        compiler_params=_params(("parallel", "parallel")),
    )(dx, wd, gs, us, *((after,) if has_tok else ()))


ROWS = 512


def _rms_fwd(x, g, name, after=None):
    t, d = x.shape
    has_tok = after is not None

    def body(*refs):
        x_ref, g_ref, o_ref = refs[0], refs[1], refs[-1]
        xv = x_ref[...]
        o_ref[...] = (xv * _rstd(xv) * g_ref[...]).astype(BF16)

    row = pl.BlockSpec((ROWS, d), lambda i: (i, 0))
    tok = [pl.BlockSpec((8, LANE), lambda i: (0, 0))] if has_tok else []
    return pl.pallas_call(
        body, name=name, grid=(t // ROWS,),
        in_specs=[row, pl.BlockSpec((1, d), lambda i: (0, 0))] + tok,
        out_specs=row, out_shape=jax.ShapeDtypeStruct((t, d), BF16),
        compiler_params=_params(("parallel",)),
    )(x, g, *((after,) if has_tok else ()))


def _rms_bwd(x, g, dh, dres, name):
    t, d = x.shape

    def body(x_ref, g_ref, dh_ref, dr_ref, dx_ref, dxb_ref, dg_ref):
        xv = x_ref[...]
        r = _rstd(xv)
        xh = xv * r
        dhv = dh_ref[...]

        @pl.when(pl.program_id(0) == 0)
        def _():
            dg_ref[...] = jnp.zeros_like(dg_ref)

        dg_ref[...] += jnp.sum(dhv * xh, axis=0, keepdims=True)
        dxh = dhv * g_ref[...]
        dx = r * (dxh - xh * jnp.mean(dxh * xh, axis=-1, keepdims=True)) + dr_ref[...]
        dx_ref[...] = dx
        dxb_ref[...] = dx.astype(BF16)

    row = pl.BlockSpec((ROWS // 2, d), lambda i: (i, 0))
    vec = pl.BlockSpec((1, d), lambda i: (0, 0))
    return pl.pallas_call(
        body, name=name, grid=(t // (ROWS // 2),),
        in_specs=[row, vec, row, row],
        out_specs=[row, row, vec],
        out_shape=[jax.ShapeDtypeStruct((t, d), F32), jax.ShapeDtypeStruct((t, d), BF16),
                   jax.ShapeDtypeStruct((1, d), F32)],
        compiler_params=_params(("arbitrary",)),
    )(x, g, dh, dres)


def _loss_fwd_bwd(y, target, name):
    t, d = y.shape
    inv = 1.0 / d

    def body(y_ref, t_ref, l_ref, dy_ref, dyb_ref):
        e = y_ref[...] - t_ref[...]

        @pl.when(pl.program_id(0) == 0)
        def _():
            l_ref[...] = jnp.zeros_like(l_ref)

        l_ref[...] += 0.5 * inv * jnp.sum(e * e)
        dy = e * inv
        dy_ref[...] = dy
        dyb_ref[...] = dy.astype(BF16)

    row = pl.BlockSpec((ROWS, d), lambda i: (i, 0))
    return pl.pallas_call(
        body, name=name, grid=(t // ROWS,),
        in_specs=[row, row],
        out_specs=[pl.BlockSpec((8, LANE), lambda i: (0, 0)), row, row],
        out_shape=[jax.ShapeDtypeStruct((8, LANE), F32), jax.ShapeDtypeStruct((t, d), F32),
                   jax.ShapeDtypeStruct((t, d), BF16)],
        compiler_params=_params(("arbitrary",)),
    )(y, target)


MIX_ROWS = 256


def _mix_fwd(o, y, proj, ga, gs, name):
    t = o.shape[0]
    gw = SSM_W // SSM_GROUPS

    def body(o_ref, y_ref, z_ref, ga_ref, gs_ref, c_ref):
        ov = o_ref[...]
        c_ref[:, 0:ATT_W] = (ov * _rstd(ov) * ga_ref[...]).astype(BF16)
        zv = z_ref[...]
        yz = y_ref[...] * (zv * _sigmoid(zv))
        for gi in range(SSM_GROUPS):
            seg = yz[:, gi * gw:(gi + 1) * gw]
            c_ref[:, ATT_W + gi * gw:ATT_W + (gi + 1) * gw] = (
                seg * _rstd(seg) * gs_ref[:, gi * gw:(gi + 1) * gw]).astype(BF16)

    half = pl.BlockSpec((MIX_ROWS, ATT_W), lambda i: (i, 0))
    vec = pl.BlockSpec((1, ATT_W), lambda i: (0, 0))
    return pl.pallas_call(
        body, name=name, grid=(t // MIX_ROWS,),
        in_specs=[half, half, pl.BlockSpec((MIX_ROWS, ATT_W), lambda i: (i, OFF_Z // ATT_W)), vec, vec],
        out_specs=pl.BlockSpec((MIX_ROWS, 2 * ATT_W), lambda i: (i, 0)),
        out_shape=jax.ShapeDtypeStruct((t, 2 * ATT_W), BF16),
        compiler_params=_params(("parallel",)),
    )(o, y, proj, ga, gs)


def _mix_bwd(dcat, o, y, proj, ga, gs, name):
    t = o.shape[0]
    gw = SSM_W // SSM_GROUPS

    def body(dc_ref, o_ref, y_ref, z_ref, ga_ref, gs_ref, do_ref, dy_ref, dz_ref, dga_ref, dgs_ref):
        @pl.when(pl.program_id(0) == 0)
        def _():
            dga_ref[...] = jnp.zeros_like(dga_ref)
            dgs_ref[...] = jnp.zeros_like(dgs_ref)

        ov = o_ref[...]
        r = _rstd(ov)
        oh = ov * r
        d_on = dc_ref[:, 0:ATT_W]
        dga_ref[...] += jnp.sum(d_on * oh, axis=0, keepdims=True)
        doh = d_on * ga_ref[...]
        do_ref[...] = r * (doh - oh * jnp.mean(doh * oh, axis=-1, keepdims=True))

        zv = z_ref[...]
        yv = y_ref[...]
        sz = _sigmoid(zv)
        silu = zv * sz
        yz = yv * silu
        for gi in range(SSM_GROUPS):
            sl = slice(gi * gw, (gi + 1) * gw)
            seg = yz[:, sl]
            rg = _rstd(seg)
            yh = seg * rg
            dyn = dc_ref[:, ATT_W + gi * gw:ATT_W + (gi + 1) * gw]
            dgs_ref[:, sl] += jnp.sum(dyn * yh, axis=0, keepdims=True)
            dyh = dyn * gs_ref[:, sl]
            dyz = rg * (dyh - yh * jnp.mean(dyh * yh, axis=-1, keepdims=True))
            dy_ref[:, sl] = dyz * silu[:, sl]
            dz_ref[:, sl] = (dyz * yv[:, sl] * (sz[:, sl] * (1.0 + zv[:, sl] * (1.0 - sz[:, sl])))).astype(BF16)

    half = pl.BlockSpec((MIX_ROWS, ATT_W), lambda i: (i, 0))
    vec = pl.BlockSpec((1, ATT_W), lambda i: (0, 0))
    return pl.pallas_call(
        body, name=name, grid=(t // MIX_ROWS,),
        in_specs=[pl.BlockSpec((MIX_ROWS, 2 * ATT_W), lambda i: (i, 0)), half, half,
                  pl.BlockSpec((MIX_ROWS, ATT_W), lambda i: (i, OFF_Z // ATT_W)), vec, vec],
        out_specs=[half, half, half, vec, vec],
        out_shape=[jax.ShapeDtypeStruct((t, ATT_W), F32), jax.ShapeDtypeStruct((t, SSM_W), F32),
                   jax.ShapeDtypeStruct((t, SSM_W), BF16), jax.ShapeDtypeStruct((1, ATT_W), F32),
                   jax.ShapeDtypeStruct((1, SSM_W), F32)],
        compiler_params=_params(("arbitrary",)),
    )(dcat, o, y, proj, ga, gs)


ATT_QB = 256
ATT_KB = 256
assert ATT_QB == ATT_KB


def _stacked(m):
    return jnp.concatenate([m, m], axis=0)


def _split_sum(x, m2):
    hi = x.astype(BF16)
    lo = (x - hi.astype(F32)).astype(BF16)
    return _dot(jnp.concatenate([hi, lo], axis=1), m2)


def _att_tile(z, mask, m_strict2, carry):
    lse = jnp.log(1.0 + jnp.exp(-jnp.abs(z)))
    lb = jnp.minimum(z, 0.0) - lse
    lrm = -jnp.maximum(z, 0.0) - lse
    if mask is not None:
        lrm = jnp.where(mask, lrm, 0.0)
    w = jnp.exp(lb + _split_sum(lrm, m_strict2) + carry)
    if mask is not None:
        w = jnp.where(mask, w, 0.0)
    return lb, lrm, w


ATT_HP = 2
ATT_HP_FWD = 4


def _head_spec(seq, off, hp=ATT_HP):
    width = hp * ATT_DH
    per = ATT_HEADS // hp
    return pl.BlockSpec((seq, width), lambda s: (s // per, off // width + s % per))


def _head_lanes(h):
    return slice(h * ATT_DH, (h + 1) * ATT_DH)


def _attn_fwd(proj, gq, gk, nb, seq, name):
    nq = seq // ATT_QB
    scale = ATT_DH ** -0.5
    hp = ATT_HP_FWD
    heads = range(hp)

    def body(q_ref, k_ref, v_ref, gq_ref, gk_ref, o_ref, qs, kn, vb):
        for h in heads:
            sl = _head_lanes(h)
            qv = q_ref[:, sl]
            kv = k_ref[:, sl]
            qs[:, sl] = (qv * _rstd(qv) * gq_ref[...] * scale).astype(BF16)
            kn[:, sl] = (kv * _rstd(kv) * gk_ref[...]).astype(BF16)
            vb[:, sl] = v_ref[:, sl].astype(BF16)
        row = lax.broadcasted_iota(jnp.int32, (ATT_QB, ATT_KB), 0)
        col = lax.broadcasted_iota(jnp.int32, (ATT_QB, ATT_KB), 1)
        m_strict2 = _stacked((row > col).astype(BF16))
        diagonal = col < row

        def key_rows(kj):
            return pl.ds(pl.multiple_of(kj * ATT_KB, ATT_KB), ATT_KB)

        def q_loop(qi, _):
            q0 = pl.multiple_of(qi * ATT_QB, ATT_QB)
            q_ts = [qs[pl.ds(q0, ATT_QB), _head_lanes(h)] for h in heads]

            def scores(h, kj):
                return _dot_nt(q_ts[h], kn[key_rows(kj), _head_lanes(h)])

            def tile(c, kj, mask):
                rows = key_rows(kj)
                out = []
                for h in heads:
                    acc, carry, z = c[h]
                    z_next = scores(h, jnp.maximum(kj - 1, 0))
                    _, lrm, w = _att_tile(z, mask, m_strict2, carry)
                    acc = acc + _dot(w.astype(BF16), vb[rows, _head_lanes(h)])
                    out.append((acc, carry + jnp.sum(lrm, axis=-1, keepdims=True), z_next))
                return tuple(out)

            init = tuple((jnp.zeros((ATT_QB, ATT_DH), F32), jnp.zeros((ATT_QB, 1), F32), scores(h, qi)) for h in heads)
            res = lax.fori_loop(1, qi + 1, lambda i, c: tile(c, qi - i, None), tile(init, qi, diagonal))
            for h in heads:
                o_ref[pl.ds(q0, ATT_QB), _head_lanes(h)] = res[h][0]
            return 0

        lax.fori_loop(0, nq, q_loop, 0)

    vec = pl.BlockSpec((1, ATT_DH), lambda s: (0, 0))
    return pl.pallas_call(
        body, name=name, grid=(nb * ATT_HEADS // hp,),
        in_specs=[_head_spec(seq, OFF_Q, hp), _head_spec(seq, OFF_K, hp), _head_spec(seq, OFF_V, hp), vec, vec],
        out_specs=_head_spec(seq, 0, hp),
        out_shape=jax.ShapeDtypeStruct((nb * seq, ATT_W), F32),
        scratch_shapes=[pltpu.VMEM((seq, hp * ATT_DH), BF16)] * 3,
        compiler_params=_params(("parallel",)),
    )(proj, proj, proj, gq, gk)


def _attn_bwd(proj, do, gq, gk, nb, seq, name):
    nq = seq // ATT_QB
    nk = seq // ATT_KB
    scale = ATT_DH ** -0.5
    heads = range(ATT_HP)

    def body(q_ref, k_ref, v_ref, do_ref, gq_ref, gk_ref, dq_ref, dk_ref, dv_ref, dgq_ref, dgk_ref,
             qs, kn, vb, dob, dq_acc, dk_acc, dv_acc, gbuf, bbuf):
        @pl.when(pl.program_id(0) == 0)
        def _():
            dgq_ref[...] = jnp.zeros_like(dgq_ref)
            dgk_ref[...] = jnp.zeros_like(dgk_ref)

        for h in heads:
            sl = _head_lanes(h)
            qv = q_ref[:, sl]
            kv = k_ref[:, sl]
            qs[:, sl] = (qv * _rstd(qv) * gq_ref[...] * scale).astype(BF16)
            kn[:, sl] = (kv * _rstd(kv) * gk_ref[...]).astype(BF16)
            vb[:, sl] = v_ref[:, sl].astype(BF16)
            dob[:, sl] = do_ref[:, sl].astype(BF16)
        dk_acc[...] = jnp.zeros_like(dk_acc)
        dv_acc[...] = jnp.zeros_like(dv_acc)
        row = lax.broadcasted_iota(jnp.int32, (ATT_QB, ATT_KB), 0)
        col = lax.broadcasted_iota(jnp.int32, (ATT_QB, ATT_KB), 1)
        m_strict2 = _stacked((row > col).astype(BF16))
        m_prefix2 = _stacked((row < col).astype(BF16))
        diagonal = col < row

        def key_rows(kj):
            return pl.ds(pl.multiple_of(kj * ATT_KB, ATT_KB), ATT_KB)

        def q_loop(qi, _):
            q0 = pl.multiple_of(qi * ATT_QB, ATT_QB)
            q_ts = [qs[pl.ds(q0, ATT_QB), _head_lanes(h)] for h in heads]
            do_ts = [dob[pl.ds(q0, ATT_QB), _head_lanes(h)] for h in heads]

            def scores(h, kj):
                return _dot_nt(q_ts[h], kn[key_rows(kj), _head_lanes(h)])

            def down(c, kj, mask):
                rows = key_rows(kj)
                out = []
                for h in heads:
                    carry, z = c[h]
                    sl = _head_lanes(h)
                    z_next = scores(h, jnp.maximum(kj - 1, 0))
                    lb, lrm, w = _att_tile(z, mask, m_strict2, carry)
                    dw = _dot_nt(do_ts[h], vb[rows, sl])
                    gbuf[h * nk + kj] = w * dw
                    bbuf[h * nk + kj] = jnp.exp(lb)
                    dv_acc[rows, sl] += _dot_tn(w.astype(BF16), do_ts[h])
                    out.append((carry + jnp.sum(lrm, axis=-1, keepdims=True), z_next))
                return tuple(out)

            init = tuple((jnp.zeros((ATT_QB, 1), F32), scores(h, qi)) for h in heads)
            lax.fori_loop(1, qi + 1, lambda i, c: down(c, qi - i, None), down(init, qi, diagonal))

            def up(c, kj, mask):
                rows = key_rows(kj)
                out = []
                for h in heads:
                    acc, carry, within = c[h]
                    sl = _head_lanes(h)
                    g = gbuf[h * nk + kj]
                    beta = bbuf[h * nk + kj]
                    within_next = _split_sum(gbuf[h * nk + jnp.minimum(kj + 1, qi)], m_prefix2)
                    dz = g * (1.0 - beta) - (within + carry) * beta
                    if mask is not None:
                        dz = jnp.where(mask, dz, 0.0)
                    dz = dz.astype(BF16)
                    acc = acc + _dot(dz, kn[rows, sl])
                    dk_acc[rows, sl] += _dot_tn(dz, q_ts[h])
                    out.append((acc, carry + jnp.sum(g, axis=-1, keepdims=True), within_next))
                return tuple(out)

            init = tuple((jnp.zeros((ATT_QB, ATT_DH), F32), jnp.zeros((ATT_QB, 1), F32),
                          _split_sum(gbuf[h * nk], m_prefix2)) for h in heads)
            res = up(lax.fori_loop(0, qi, lambda kj, c: up(c, kj, None), init), qi, diagonal)
            for h in heads:
                dq_acc[pl.ds(q0, ATT_QB), _head_lanes(h)] = res[h][0]
            return 0

        lax.fori_loop(0, nq, q_loop, 0)

        def norm_bwd(xv, gain, dyn):
            r = _rstd(xv)
            xh = xv * r
            dgain = jnp.sum(dyn * xh, axis=0, keepdims=True)
            dxh = dyn * gain
            return r * (dxh - xh * jnp.mean(dxh * xh, axis=-1, keepdims=True)), dgain

        for h in heads:
            sl = _head_lanes(h)
            dq, dgq = norm_bwd(q_ref[:, sl], gq_ref[...], dq_acc[:, sl] * scale)
            dk, dgk = norm_bwd(k_ref[:, sl], gk_ref[...], dk_acc[:, sl])
            dq_ref[:, sl] = dq.astype(BF16)
            dk_ref[:, sl] = dk.astype(BF16)
            dv_ref[:, sl] = dv_acc[:, sl].astype(BF16)
            dgq_ref[...] += dgq
            dgk_ref[...] += dgk

    vec = pl.BlockSpec((1, ATT_DH), lambda s: (0, 0))
    big = jax.ShapeDtypeStruct((nb * seq, ATT_W), BF16)
    small = jax.ShapeDtypeStruct((1, ATT_DH), F32)
    width = ATT_HP * ATT_DH
    return pl.pallas_call(
        body, name=name, grid=(nb * ATT_HEADS // ATT_HP,),
        in_specs=[_head_spec(seq, OFF_Q), _head_spec(seq, OFF_K), _head_spec(seq, OFF_V), _head_spec(seq, 0), vec, vec],
        out_specs=[_head_spec(seq, 0)] * 3 + [vec, vec],
        out_shape=[big, big, big, small, small],
        scratch_shapes=[pltpu.VMEM((seq, width), BF16)] * 4 + [pltpu.VMEM((seq, width), F32)] * 3
        + [pltpu.VMEM((ATT_HP * nk, ATT_QB, ATT_KB), F32)] * 2,
        compiler_params=_params(("arbitrary",)),
    )(proj, proj, proj, do, gq, gk)


CONV_COLS = 256


def _pack_conv(conv_w, conv_b):
    return jnp.concatenate([conv_w, conv_b[None, :], jnp.zeros((3, CONV_DIM), F32)], axis=0)


def _conv_pre(raw, w8, rowi):
    pre = w8[CONV_K:CONV_K + 1, :] + raw * w8[CONV_K - 1:CONV_K, :]
    for k in range(1, CONV_K):
        sh = jnp.where(rowi >= k, pltpu.roll(raw, k, 0), 0.0)
        pre = pre + sh * w8[CONV_K - 1 - k:CONV_K - k, :]
    return pre


def _conv_fwd(proj, cw8, nb, seq, name, after):
    ncol = CONV_DIM // CONV_COLS

    def body(x_ref, w_ref, tok_ref, o_ref):
        rowi = lax.broadcasted_iota(jnp.int32, (seq, 1), 0)
        pre = _conv_pre(x_ref[...], w_ref[...], rowi)
        o_ref[...] = pre * _sigmoid(pre)

    return pl.pallas_call(
        body, name=name, grid=(nb, ncol),
        in_specs=[pl.BlockSpec((seq, CONV_COLS), lambda b, j: (b, OFF_XS // CONV_COLS + j)),
                  pl.BlockSpec((8, CONV_COLS), lambda b, j: (0, j)),
                  pl.BlockSpec((8, LANE), lambda b, j: (0, 0))],
        out_specs=pl.BlockSpec((seq, CONV_COLS), lambda b, j: (b, j)),
        out_shape=jax.ShapeDtypeStruct((nb * seq, CONV_DIM), F32),
        compiler_params=_params(("parallel", "parallel")),
    )(proj, cw8, after)


def _conv_bwd(proj, dact, cw8, nb, seq, name):
    ncol = CONV_DIM // CONV_COLS

    def body(x_ref, d_ref, w_ref, dx_ref, dw_ref):
        @pl.when(pl.program_id(1) == 0)
        def _():
            dw_ref[...] = jnp.zeros_like(dw_ref)

        rowi = lax.broadcasted_iota(jnp.int32, (seq, 1), 0)
        raw = x_ref[...]
        w8 = w_ref[...]
        pre = _conv_pre(raw, w8, rowi)
        sg = _sigmoid(pre)
        dpre = d_ref[...] * (sg * (1.0 + pre * (1.0 - sg)))
        dw_ref[CONV_K:CONV_K + 1, :] += jnp.sum(dpre, axis=0, keepdims=True)
        dw_ref[CONV_K - 1:CONV_K, :] += jnp.sum(dpre * raw, axis=0, keepdims=True)
        draw = dpre * w8[CONV_K - 1:CONV_K, :]
        for k in range(1, CONV_K):
            sh = jnp.where(rowi >= k, pltpu.roll(raw, k, 0), 0.0)
            dw_ref[CONV_K - 1 - k:CONV_K - k, :] += jnp.sum(dpre * sh, axis=0, keepdims=True)
            up = jnp.where(rowi < seq - k, pltpu.roll(dpre, seq - k, 0), 0.0)
            draw = draw + up * w8[CONV_K - 1 - k:CONV_K - k, :]
        dx_ref[...] = draw.astype(BF16)

    return pl.pallas_call(
        body, name=name, grid=(ncol, nb),
        in_specs=[pl.BlockSpec((seq, CONV_COLS), lambda j, b: (b, OFF_XS // CONV_COLS + j)),
                  pl.BlockSpec((seq, CONV_COLS), lambda j, b: (b, j)),
                  pl.BlockSpec((8, CONV_COLS), lambda j, b: (0, j))],
        out_specs=[pl.BlockSpec((seq, CONV_COLS), lambda j, b: (b, j)),
                   pl.BlockSpec((8, CONV_COLS), lambda j, b: (0, j))],
        out_shape=[jax.ShapeDtypeStruct((nb * seq, CONV_DIM), BF16), jax.ShapeDtypeStruct((8, CONV_DIM), F32)],
        compiler_params=_params(("parallel", "arbitrary")),
    )(proj, dact, cw8)


def _pack_heads(dt_bias, a_log, d_skip):
    rows = jnp.stack([dt_bias, a_log, d_skip]).reshape(3, SSM_GROUPS, SSM_HG).transpose(1, 0, 2)
    return jnp.pad(rows, ((0, 0), (0, 8 - 3), (0, LANE - SSM_HG)))


def _split3_rows(x):
    hi = x.astype(BF16)
    r1 = x - hi.astype(F32)
    mid = r1.astype(BF16)
    lo = (r1 - mid.astype(F32)).astype(BF16)
    return jnp.concatenate([hi, mid, lo], axis=0)


def _split3_cols(x):
    hi = x.astype(BF16)
    r1 = x - hi.astype(F32)
    mid = r1.astype(BF16)
    lo = (r1 - mid.astype(F32)).astype(BF16)
    return jnp.concatenate([hi, mid, lo], axis=1)


def _split2_rows(x):
    hi = x.astype(BF16)
    return jnp.concatenate([hi, (x - hi.astype(F32)).astype(BF16)], axis=0)


def _ssd_specs(seq):
    gx = SSM_HG * SSM_P
    return dict(
        xs=pl.BlockSpec((seq, gx), lambda g, b: (b, g)),
        bm=pl.BlockSpec((seq, SSM_N), lambda g, b: (b, SSM_W // SSM_N + g)),
        cm=pl.BlockSpec((seq, SSM_N), lambda g, b: (b, SSM_W // SSM_N + SSM_GROUPS + g)),
        dt=pl.BlockSpec((seq, LANE), lambda g, b: (b, OFF_DT // LANE)),
        hp=pl.BlockSpec((1, 8, LANE), lambda g, b: (g, 0, 0)),
        head=pl.BlockSpec((seq, gx), lambda g, b: (b, g)),
        grp=pl.BlockSpec((seq, SSM_N), lambda g, b: (b, g)),
    )


SSM_GX = SSM_HG * SSM_P


def _ssd_masks():
    li = lax.broadcasted_iota(jnp.int32, (CHUNK, CHUNK), 0)
    si = lax.broadcasted_iota(jnp.int32, (CHUNK, CHUNK), 1)
    head = lax.broadcasted_iota(jnp.int32, (LANE, SSM_GX), 0)
    lane = lax.broadcasted_iota(jnp.int32, (LANE, SSM_GX), 1)
    expand = (lane // SSM_P == head).astype(BF16)
    head_t = lax.broadcasted_iota(jnp.int32, (SSM_GX, LANE), 1)
    lane_t = lax.broadcasted_iota(jnp.int32, (SSM_GX, LANE), 0)
    gather = (lane_t // SSM_P == head_t).astype(BF16)
    return dict(
        causal=li >= si, causal_t=si >= li,
        tril3=jnp.concatenate([(si <= li).astype(BF16)] * 3, axis=1),
        triu2=jnp.concatenate([(si >= li).astype(BF16)] * 2, axis=1),
        below2=jnp.concatenate([(si < li).astype(BF16)] * 2, axis=1),
        expand2=_stacked(expand), gather2=_stacked(gather))


def _per_head(x, mk):
    return _split_sum(x, mk["expand2"])


def _head_sums(x, mk):
    return _split_sum(x, mk["gather2"])


def _row8(v):
    return jnp.broadcast_to(v, (8, v.shape[1]))


def _group_dt(dt_ref):
    shift = (LANE - SSM_HG * pl.program_id(0)) % LANE
    return pltpu.roll(dt_ref[...], shift, 1)


def _ssd_fwd(act, proj, hp, nb, seq, name):
    nc = seq // CHUNK

    def body(xs_ref, b_ref, c_ref, dt_ref, hp_ref, y_ref, dt_s, da_s, hst):
        mk = _ssd_masks()
        hpv = hp_ref[0]
        dt = _softplus(_group_dt(dt_ref) + hpv[0:1, :])
        a = -jnp.exp(hpv[1:2, :])
        dsk_row = _per_head(_row8(hpv[2:3, :]), mk)[0:1]
        dt_s[...] = dt
        da_s[...] = dt * a
        hst[...] = jnp.zeros_like(hst)

        def chunk(c, _):
            rows = pl.ds(pl.multiple_of(c * CHUNK, CHUNK), CHUNK)
            acol = _dot(mk["tril3"], _split3_rows(da_s[rows, :]))
            arow = acol.T
            alast = acol[CHUNK - 1:CHUNK, :]
            ea = _per_head(jnp.exp(acol), mk)
            eb = _per_head(jnp.exp(alast - acol), mk)
            el = _per_head(_row8(jnp.exp(alast)), mk)[0:1]
            bb = b_ref[rows, :].astype(BF16)
            cb = c_ref[rows, :].astype(BF16)
            cbm = _dot_nt(cb, bb)
            xc = xs_ref[rows, :]
            u = xc * _per_head(dt_s[rows, :], mk)
            ub = u.astype(BF16)
            ht = hst[...]
            y_ref[rows, :] = ea * _dot(cb, ht.astype(BF16)) + dsk_row * xc
            for j in range(SSM_HG):
                sl = slice(j * SSM_P, (j + 1) * SSM_P)
                decay = jnp.where(mk["causal"], jnp.exp(jnp.minimum(acol[:, j:j + 1] - arow[j:j + 1, :], 0.0)), 0.0)
                y_ref[rows, sl] += _dot((cbm * decay).astype(BF16), ub[:, sl])
            hst[...] = el * ht + _dot_tn(bb, (u * eb).astype(BF16))
            return 0

        lax.fori_loop(0, nc, chunk, 0)

    sp = _ssd_specs(seq)
    return pl.pallas_call(
        body, name=name, grid=(SSM_GROUPS, nb),
        in_specs=[sp["xs"], sp["bm"], sp["cm"], sp["dt"], sp["hp"]],
        out_specs=sp["head"],
        out_shape=jax.ShapeDtypeStruct((nb * seq, SSM_W), F32),
        scratch_shapes=[pltpu.VMEM((seq, LANE), F32)] * 2 + [pltpu.VMEM((SSM_N, SSM_GX), F32)],
        compiler_params=_params(("parallel", "parallel")),
    )(act, act, act, proj, hp)


def _ssd_bwd(act, proj, dy, hp, nb, seq, name):
    nc = seq // CHUNK

    def body(xs_ref, b_ref, c_ref, dt_ref, hp_ref, dy_ref, dxs_ref, db_ref, dc_ref, ddt_ref, dhp_ref,
             dt_s, da_s, ddt_s, hs, lam, du_s):
        @pl.when(pl.program_id(1) == 0)
        def _():
            dhp_ref[...] = jnp.zeros_like(dhp_ref)

        mk = _ssd_masks()
        hpv = hp_ref[0]
        a = -jnp.exp(hpv[1:2, :])
        dsk_row = _per_head(_row8(hpv[2:3, :]), mk)[0:1]
        dt_s[...] = _softplus(_group_dt(dt_ref) + hpv[0:1, :])
        da_s[...] = dt_s[...] * a
        lane = lax.broadcasted_iota(jnp.int32, (1, LANE), 1)

        def chunk_rows(c):
            return pl.ds(pl.multiple_of(c * CHUNK, CHUNK), CHUNK)

        def decays(c):
            acol = _dot(mk["tril3"], _split3_rows(da_s[chunk_rows(c), :]))
            alast = acol[CHUNK - 1:CHUNK, :]
            return acol, alast

        hs[0] = jnp.zeros((SSM_N, SSM_GX), F32)

        def fwd_chunk(c, _):
            rows = chunk_rows(c)
            acol, alast = decays(c)
            eb = _per_head(jnp.exp(alast - acol), mk)
            el = _per_head(_row8(jnp.exp(alast)), mk)[0:1]
            u = xs_ref[rows, :] * _per_head(dt_s[rows, :], mk)
            hs[c + 1] = el * hs[c] + _dot_tn(b_ref[rows, :].astype(BF16), (u * eb).astype(BF16))
            return 0

        lax.fori_loop(0, nc - 1, fwd_chunk, 0)
        lam[...] = jnp.zeros_like(lam)

        def bwd_chunk(i, carry):
            dd_row, da_vec = carry
            c = nc - 1 - i
            rows = chunk_rows(c)
            acol, alast = decays(c)
            arow = acol.T
            ea = _per_head(jnp.exp(acol), mk)
            eb = _per_head(jnp.exp(alast - acol), mk)
            el = _per_head(_row8(jnp.exp(alast)), mk)[0:1]
            dt_all = _per_head(dt_s[rows, :], mk)
            bb = b_ref[rows, :].astype(BF16)
            cb = c_ref[rows, :].astype(BF16)
            cbm = _dot_nt(cb, bb)
            cbt = _dot_nt(bb, cb)
            xc = xs_ref[rows, :]
            dyc = dy_ref[rows, :]
            u = xc * dt_all
            ub = u.astype(BF16)
            dyb = dyc.astype(BF16)
            h_in = hs[c]
            lm = lam[...]
            hb = h_in.astype(BF16)
            lb = lm.astype(BF16)
            y_off = ea * _dot(cb, hb)
            du_off = eb * _dot(bb, lb)
            dye = (ea * dyc).astype(BF16)
            zero = jnp.zeros((CHUNK, CHUNK), F32)
            dcb, dcbt, d_a = zero, zero, zero
            for j in range(SSM_HG):
                sl = slice(j * SSM_P, (j + 1) * SSM_P)
                seg = acol[:, j:j + 1] - arow[j:j + 1, :]
                decay = jnp.where(mk["causal"], jnp.exp(jnp.minimum(seg, 0.0)), 0.0)
                decay_t = jnp.where(mk["causal_t"], jnp.exp(jnp.minimum(-seg, 0.0)), 0.0)
                m = cbm * decay
                mt = cbt * decay_t
                dm = _dot_nt(dyb[:, sl], ub[:, sl])
                dmt = _dot_nt(ub[:, sl], dyb[:, sl])
                dcb = dcb + dm * decay
                dcbt = dcbt + dmt * decay_t
                du_s[:, sl] = _dot(mt.astype(BF16), dyb[:, sl])
                d_a_j = jnp.sum(dm * m, axis=-1, keepdims=True) - jnp.sum(dmt * mt, axis=-1, keepdims=True)
                d_a = jnp.where(lane == j, d_a_j, d_a)
            du = du_s[...] + du_off
            dxs_ref[rows, :] = du * dt_all + dsk_row * dyc
            dc_ref[rows, :] = _dot_nt(dye, hb) + _dot(dcb.astype(BF16), bb)
            db_ref[rows, :] = _dot_nt((eb * u).astype(BF16), lb) + _dot(dcbt.astype(BF16), cb)
            lam[...] = el * lm + _dot_tn(cb, dye)
            d_a = d_a + _head_sums(dyc * y_off, mk)
            f_a = _head_sums(du_off * u, mk)
            c_a = jnp.exp(alast) * _head_sums(_row8(jnp.sum(lm * h_in, axis=0, keepdims=True)), mk)[0:1]
            dda = _dot(mk["triu2"], _split2_rows(d_a)) + _dot(mk["below2"], _split2_rows(f_a)) + c_a
            ddt_s[rows, :] = dda * a + _head_sums(du * xc, mk)
            da_vec = da_vec + jnp.sum(dda * dt_s[rows, :], axis=0, keepdims=True)
            dd_row = dd_row + jnp.sum(dyc * xc, axis=0, keepdims=True)
            return dd_row, da_vec

        init = (jnp.zeros((1, SSM_GX), F32), jnp.zeros((1, LANE), F32))
        dd_row, da_vec = lax.fori_loop(0, nc, bwd_chunk, init)
        ddt_raw = ddt_s[...] * _sigmoid(_group_dt(dt_ref) + hpv[0:1, :])
        ddt_ref[...] = ddt_raw.astype(BF16)
        dhp_ref[0, 0:1, :] += jnp.sum(ddt_raw, axis=0, keepdims=True)
        dhp_ref[0, 1:2, :] += da_vec * a
        dhp_ref[0, 2:3, :] += _head_sums(_row8(dd_row), mk)[0:1]

    sp = _ssd_specs(seq)
    t = nb * seq
    return pl.pallas_call(
        body, name=name, grid=(SSM_GROUPS, nb),
        in_specs=[sp["xs"], sp["bm"], sp["cm"], sp["dt"], sp["hp"], sp["head"]],
        out_specs=[sp["head"], sp["grp"], sp["grp"], sp["grp"], sp["hp"]],
        out_shape=[jax.ShapeDtypeStruct((t, SSM_W), F32), jax.ShapeDtypeStruct((t, SSM_GROUPS * SSM_N), F32),
                   jax.ShapeDtypeStruct((t, SSM_GROUPS * SSM_N), F32),
                   jax.ShapeDtypeStruct((t, SSM_GROUPS * LANE), BF16),
                   jax.ShapeDtypeStruct((SSM_GROUPS, 8, LANE), F32)],
        scratch_shapes=[pltpu.VMEM((seq, LANE), F32)] * 3
        + [pltpu.VMEM((nc, SSM_N, SSM_GX), F32), pltpu.VMEM((SSM_N, SSM_GX), F32), pltpu.VMEM((CHUNK, SSM_GX), F32)],
        compiler_params=_params(("parallel", "arbitrary")),
    )(act, act, act, proj, hp, dy)


ANY = pl.BlockSpec(memory_space=pl.ANY)


def _block_index(p):
    return 4 * p[0] + 2 * p[1] + p[2]


def _all_gather(shards, name):
    n = len(shards)

    def body(*refs):
        ins, outs = refs[:n], refs[n:2 * n]
        send_sems, recv_sems, local_sems = refs[2 * n:]
        x, y, c = lax.axis_index("x"), lax.axis_index("y"), lax.axis_index("c")
        me, sibling = (x, y, c), (x, y, 1 - c)
        chips = [(1 - x, y), (x, 1 - y), (1 - x, 1 - y)]

        def copy(i, k, block, to, src=None):
            dst = outs[i].at[_block_index(block)]
            return pltpu.make_async_remote_copy(
                src_ref=dst if src is None else src, dst_ref=dst,
                send_sem=send_sems.at[i, k], recv_sem=recv_sems.at[i, k],
                device_id=to, device_id_type=MESH)

        mine = [pltpu.make_async_copy(ins[i], outs[i].at[_block_index(me)], local_sems.at[i]) for i in range(n)]
        for cp in mine:
            cp.start()
        first = []
        for i in range(n):
            first.append(copy(i, 0, me, sibling, src=ins[i]))
            first += [copy(i, 1 + j, me, (*chip, c), src=ins[i]) for j, chip in enumerate(chips)]
        for cp in first:
            cp.start()
        passed = []
        for j, chip in enumerate(chips):
            for i in range(n):
                copy(i, 1 + j, (*chip, c), me).wait_recv()
                fwd = copy(i, 4 + j, (*chip, c), sibling)
                fwd.start()
                passed.append(fwd)
        for i in range(n):
            copy(i, 0, sibling, me).wait_recv()
            for j, chip in enumerate(chips):
                copy(i, 4 + j, (*chip, 1 - c), me).wait_recv()
        for cp in first + passed:
            cp.wait_send()
        for cp in mine:
            cp.wait()

    return pl.pallas_call(
        body, name=name,
        in_specs=[ANY] * n, out_specs=[ANY] * n,
        out_shape=[jax.ShapeDtypeStruct((N_DEV,) + s.shape, s.dtype) for s in shards],
        scratch_shapes=[pltpu.SemaphoreType.DMA((n, 7)), pltpu.SemaphoreType.DMA((n, 7)),
                        pltpu.SemaphoreType.DMA((n,))],
    )(*shards)


HBM = pl.BlockSpec(memory_space=pltpu.HBM)
SEM = pl.BlockSpec(memory_space=pltpu.SEMAPHORE)
EFFECT = pltpu.SideEffectType.DATAFLOW_SIDE_EFFECTING


def _my_block():
    return _block_index((lax.axis_index("x"), lax.axis_index("y"), lax.axis_index("c")))


def _peer(k):
    x, y, c = lax.axis_index("x"), lax.axis_index("y"), lax.axis_index("c")
    return (1 - x if k & 4 else x, 1 - y if k & 2 else y, 1 - c if k & 1 else c)


ALL_PEERS = tuple(range(1, N_DEV))
SIBLING = 1
SAME_CORE = (2, 4, 6)


def _plan_copies(plan, src_refs, land_refs, send_sems, recv_sems, peers=ALL_PEERS):
    me = _my_block()
    copies = []
    for e, (si, di, src_view, dst_view, _) in enumerate(plan):
        for k in peers:
            copies.append(pltpu.make_async_remote_copy(
                src_ref=src_view(src_refs[si], _block_index(_peer(k))),
                dst_ref=dst_view(land_refs[di], me),
                send_sem=send_sems[e], recv_sem=recv_sems[e],
                device_id=_peer(k), device_id_type=MESH))
    return copies


def _plan_forwards(plan, land_refs, send_sems, recv_sems):
    copies = []
    for e, (_, di, _, dst_view, _) in enumerate(plan):
        for k in SAME_CORE:
            part = dst_view(land_refs[di], _block_index(_peer(k)))
            copies.append(pltpu.make_async_remote_copy(
                src_ref=part, dst_ref=part, send_sem=send_sems[e], recv_sem=recv_sems[e],
                device_id=_peer(SIBLING), device_id_type=MESH))
    return copies


def _plan_waits(plan, land_refs, send_sems, recv_sems, n=N_DEV - 1):
    waits = []
    for e, (_, di, _, _, parts_view) in enumerate(plan):
        view = parts_view(land_refs[di], n)
        waits.append(pltpu.make_async_remote_copy(
            src_ref=view, dst_ref=view, send_sem=send_sems[e], recv_sem=recv_sems[e],
            device_id=_peer(SIBLING), device_id_type=MESH))
    return waits


def _plan_own(plan, src_refs, land_refs, own_sems):
    me = _my_block()
    return [pltpu.make_async_copy(src_view(src_refs[si], me), dst_view(land_refs[di], me), own_sems[e])
            for e, (si, di, src_view, dst_view, _) in enumerate(plan)]


def _copies_start(srcs, lands, plan, name, after=None):
    ns, nl, ne = len(srcs), len(lands), len(plan)
    extra = [] if after is None else [after]

    nin = ns + nl + len(extra)

    def body(*refs):
        src_refs, land_refs = refs[:ns], refs[ns:ns + nl]
        send_sems, recv_sems = refs[nin:nin + ne], refs[nin + ne:nin + 2 * ne]
        own_sems = refs[nin + 2 * ne:nin + 3 * ne]
        token = refs[-1]
        for cp in _plan_copies(plan, src_refs, land_refs, send_sems, recv_sems):
            cp.start()
        for cp in _plan_own(plan, src_refs, land_refs, own_sems):
            cp.start()
        token[...] = jnp.zeros_like(token)

    thru = [pltpu.HBM(a.shape, a.dtype) for a in list(srcs) + list(lands)]
    res = pl.pallas_call(
        body, name=name,
        in_specs=[HBM] * (ns + nl) + [ANY] * len(extra),
        out_specs=[SEM] * (3 * ne) + [HBM] * (ns + nl) + [pl.BlockSpec(memory_space=pltpu.VMEM)],
        out_shape=[pltpu.SemaphoreType.DMA(())] * (3 * ne) + thru + [jax.ShapeDtypeStruct((8, LANE), F32)],
        input_output_aliases={i: 3 * ne + i for i in range(ns + nl)},
        compiler_params=pltpu.CompilerParams(has_side_effects=EFFECT),
    )(*[pltpu.with_memory_space_constraint(a, pltpu.HBM) for a in list(srcs) + list(lands)], *extra)
    return dict(sems=res[:3 * ne], srcs=res[3 * ne:3 * ne + ns], lands=res[3 * ne + ns:3 * ne + ns + nl],
                token=res[-1], plan=plan)


def _copies_wait(flight, after, name):
    srcs, lands, plan = flight["srcs"], flight["lands"], flight["plan"]
    ns, nl, ne = len(srcs), len(lands), len(plan)

    def body(*refs):
        src_refs, land_refs = refs[:ns], refs[ns:ns + nl]
        send_sems, recv_sems = refs[ns + nl:ns + nl + ne], refs[ns + nl + ne:ns + nl + 2 * ne]
        own_sems = refs[ns + nl + 2 * ne:ns + nl + 3 * ne]
        for cp in _plan_waits(plan, land_refs, send_sems, recv_sems):
            cp.wait_send()
            cp.wait_recv()
        for cp in _plan_own(plan, src_refs, land_refs, own_sems):
            cp.wait()

    after = list(after) if isinstance(after, (list, tuple)) else [after]
    thru = [pltpu.HBM(a.shape, a.dtype) for a in list(srcs) + list(lands)]
    res = pl.pallas_call(
        body, name=name,
        in_specs=[HBM] * (ns + nl) + [SEM] * (3 * ne) + [ANY] * len(after),
        out_specs=[HBM] * (ns + nl),
        out_shape=thru,
        input_output_aliases={i: i for i in range(ns + nl)},
        compiler_params=pltpu.CompilerParams(has_side_effects=EFFECT),
    )(*srcs, *lands, *flight["sems"], *after)
    return list(res[ns:])


def _gather2_start(srcs, lands, plan, name, after=None):
    ns, nl, ne = len(srcs), len(lands), len(plan)
    extra = [] if after is None else [after]
    nin = ns + nl + len(extra)

    def body(*refs):
        src_refs, land_refs = refs[:ns], refs[ns:ns + nl]
        send_sems, recv_sems = refs[nin:nin + ne], refs[nin + ne:nin + 2 * ne]
        own_sems = refs[nin + 2 * ne:nin + 3 * ne]
        for cp in _plan_copies(plan, src_refs, land_refs, send_sems, recv_sems, (SIBLING,) + SAME_CORE):
            cp.start()
        for cp in _plan_own(plan, src_refs, land_refs, own_sems):
            cp.start()
        refs[-1][...] = jnp.zeros_like(refs[-1])

    thru = [pltpu.HBM(a.shape, a.dtype) for a in list(srcs) + list(lands)]
    res = pl.pallas_call(
        body, name=name,
        in_specs=[HBM] * (ns + nl) + [ANY] * len(extra),
        out_specs=[SEM] * (3 * ne) + [HBM] * (ns + nl) + [pl.BlockSpec(memory_space=pltpu.VMEM)],
        out_shape=[pltpu.SemaphoreType.DMA(())] * (3 * ne) + thru + [jax.ShapeDtypeStruct((8, LANE), F32)],
        input_output_aliases={i: 3 * ne + i for i in range(ns + nl)},
        compiler_params=pltpu.CompilerParams(has_side_effects=EFFECT),
    )(*[pltpu.with_memory_space_constraint(a, pltpu.HBM) for a in list(srcs) + list(lands)], *extra)
    return dict(send1=res[:ne], recv1=res[ne:2 * ne], own=res[2 * ne:3 * ne], srcs=res[3 * ne:3 * ne + ns],
                lands=res[3 * ne + ns:3 * ne + ns + nl], token=res[-1], plan=plan)


def _gather2_forward(flight, after, name):
    srcs, lands, plan = flight["srcs"], flight["lands"], flight["plan"]
    ns, nl, ne = len(srcs), len(lands), len(plan)
    nin = ns + nl + ne + 1

    def body(*refs):
        land_refs = refs[ns:ns + nl]
        recv1 = refs[ns + nl:ns + nl + ne]
        send2, recv2 = refs[nin:nin + ne], refs[nin + ne:nin + 2 * ne]
        for cp in _plan_waits(plan, land_refs, send2, recv1, n=1 + len(SAME_CORE)):
            cp.wait_recv()
        for cp in _plan_forwards(plan, land_refs, send2, recv2):
            cp.start()
        refs[-1][...] = jnp.zeros_like(refs[-1])

    thru = [pltpu.HBM(a.shape, a.dtype) for a in list(srcs) + list(lands)]
    res = pl.pallas_call(
        body, name=name,
        in_specs=[HBM] * (ns + nl) + [SEM] * ne + [ANY],
        out_specs=[SEM] * (2 * ne) + [HBM] * (ns + nl) + [pl.BlockSpec(memory_space=pltpu.VMEM)],
        out_shape=[pltpu.SemaphoreType.DMA(())] * (2 * ne) + thru + [jax.ShapeDtypeStruct((8, LANE), F32)],
        input_output_aliases={i: 2 * ne + i for i in range(ns + nl)},
        compiler_params=pltpu.CompilerParams(has_side_effects=EFFECT),
    )(*srcs, *lands, *flight["recv1"], after)
    return dict(flight, send2=res[:ne], recv2=res[ne:2 * ne], srcs=res[2 * ne:2 * ne + ns],
                lands=res[2 * ne + ns:2 * ne + ns + nl], token=res[-1])


def _gather2_wait(flight, after, name):
    srcs, lands, plan = flight["srcs"], flight["lands"], flight["plan"]
    ns, nl, ne = len(srcs), len(lands), len(plan)

    def body(*refs):
        src_refs, land_refs = refs[:ns], refs[ns:ns + nl]
        sems = refs[ns + nl:ns + nl + 4 * ne]
        send1, own, send2, recv2 = sems[:ne], sems[ne:2 * ne], sems[2 * ne:3 * ne], sems[3 * ne:]
        for cp in _plan_waits(plan, land_refs, send1, recv2, n=1 + len(SAME_CORE)):
            cp.wait_send()
        for cp in _plan_waits(plan, land_refs, send2, recv2, n=len(SAME_CORE)):
            cp.wait_send()
            cp.wait_recv()
        for cp in _plan_own(plan, src_refs, land_refs, own):
            cp.wait()

    thru = [pltpu.HBM(a.shape, a.dtype) for a in list(srcs) + list(lands)]
    res = pl.pallas_call(
        body, name=name,
        in_specs=[HBM] * (ns + nl) + [SEM] * (4 * ne) + [ANY],
        out_specs=[HBM] * (ns + nl),
        out_shape=thru,
        input_output_aliases={i: i for i in range(ns + nl)},
        compiler_params=pltpu.CompilerParams(has_side_effects=EFFECT),
    )(*srcs, *lands, *flight["send1"], *flight["own"], *flight["send2"], *flight["recv2"], after)
    return list(res[ns:])


def _adamw_math(w, g, m, v):
    m = ADAM_B1 * m + (1.0 - ADAM_B1) * g
    v = ADAM_B2 * v + (1.0 - ADAM_B2) * (g * g)
    m_hat = m / (1.0 - ADAM_B1 ** ADAM_STEP)
    v_hat = v / (1.0 - ADAM_B2 ** ADAM_STEP)
    delta = -ADAM_LR * (m_hat / (jnp.sqrt(v_hat) + ADAM_EPS) + ADAM_WD * w)
    return delta, m, v


def _adamw(parts, w, m, v, name, rows, lane_offset=None):
    depth, r, c = w.shape
    cp = parts[0].shape[2]
    assert r % rows == 0 and len(parts) == depth

    def body(*refs):
        p_refs = refs[:depth]
        w_ref, m_ref, v_ref, g_ref, d_ref, mo_ref, vo_ref = refs[depth:]
        if lane_offset is not None:
            cw = -(-c // LANE) * LANE
            src = lax.broadcasted_iota(jnp.int32, (cp, cw), 0)
            dst = lax.broadcasted_iota(jnp.int32, (cp, cw), 1)
            pick = (src == dst + lane_offset()).astype(BF16)
            pick3 = jnp.concatenate([pick] * 3, axis=0)
        for li in range(depth):
            @pl.when(pl.program_id(0) == li)
            def _(li=li):
                g = p_refs[li][0].astype(F32)
                for j in range(1, N_DEV):
                    g = g + p_refs[li][j].astype(F32)
                if lane_offset is not None:
                    g = _dot(_split3_cols(g), pick3)
                g = g[:, :c]
                d, mn, vn = _adamw_math(w_ref[...], g, m_ref[...], v_ref[...])
                g_ref[...] = g
                d_ref[...] = d
                mo_ref[...] = mn
                vo_ref[...] = vn

    def part_spec(li):
        return pl.BlockSpec((N_DEV, rows, cp), lambda l, i: (0, jnp.where(l == li, i, 0), 0))

    blk = pl.BlockSpec((None, rows, c), lambda l, i: (l, i, 0))
    out = jax.ShapeDtypeStruct((depth, r, c), F32)
    return pl.pallas_call(
        body, name=name, grid=(depth, r // rows),
        in_specs=[part_spec(li) for li in range(depth)] + [blk, blk, blk],
        out_specs=[blk] * 4, out_shape=[out] * 4,
        compiler_params=_params(("arbitrary", "arbitrary")),
    )(*parts, w, m, v)


def _sum_parts(parts, name):
    _, r, c = parts.shape

    def body(p_ref, o_ref):
        g = p_ref[0]
        for j in range(1, N_DEV):
            g = g + p_ref[j]
        o_ref[...] = g

    return pl.pallas_call(
        body, name=name, out_shape=jax.ShapeDtypeStruct((r, c), F32),
        compiler_params=_params(),
    )(parts)


def _adamw_small(parts, ws, ms, vs, name):
    n = len(ws)

    def body(*refs):
        ins, outs = refs[:4 * n], refs[4 * n:]
        for i in range(n):
            p_ref, w_ref, m_ref, v_ref = ins[i], ins[n + i], ins[2 * n + i], ins[3 * n + i]
            g = p_ref[0]
            for j in range(1, p_ref.shape[0]):
                g = g + p_ref[j]
            d, mn, vn = _adamw_math(w_ref[...], g, m_ref[...], v_ref[...])
            outs[i][...] = g
            outs[n + i][...] = d
            outs[2 * n + i][...] = mn
            outs[3 * n + i][...] = vn

    out = [jax.ShapeDtypeStruct(a.shape, F32) for a in ws] * 4
    res = pl.pallas_call(body, name=name, out_shape=out, compiler_params=_params())(*parts, *ws, *ms, *vs)
    return res[:n], res[n:2 * n], res[2 * n:3 * n], res[3 * n:]


SMALL = ("norm_mix", "q_gain", "k_gain", "conv_b", "dt_bias", "a_log", "d_skip", "attn_out_gain",
         "ssm_out_gain", "norm_ffn")


def _full_cols(gathered):
    _, r, c = gathered.shape
    return gathered.transpose(1, 0, 2).reshape(r, N_DEV * c)


FF_BLK = 768
FF_PAD = N_DEV * FF_BLK
W_IN_COLS = IN_DIM // N_DEV
W_IN_WINDOW = 896


def _w_in_window_start(block):
    return (block * W_IN_COLS // LANE) * LANE


def _w_in_window(ref, block):
    return ref.at[:, pl.ds(pl.multiple_of(_w_in_window_start(block), LANE), W_IN_WINDOW)]


def _whole(ref, block):
    return ref


def _rows_of(size):
    return lambda ref, block: ref.at[pl.ds(pl.multiple_of(block * size, size), size), :]


def _slot(ref, block):
    return ref.at[block]


def _n_slots(ref, n):
    return ref.at[pl.ds(0, n)]


def _n_rows(size):
    return lambda ref, n: ref.at[pl.ds(0, n * size), :]


GATHER_A = [(0, 0, _whole, _slot, _n_slots), (1, 1, _whole, _rows_of(256), _n_rows(256))]
GATHER_B = [(i, i, _whole, _rows_of(FF_BLK), _n_rows(FF_BLK)) for i in range(3)]
SCATTER_A = [(0, 0, _w_in_window, _slot, _n_slots), (1, 1, _rows_of(256), _slot, _n_slots)]
SCATTER_B = [(i, i, _rows_of(FF_BLK), _slot, _n_slots) for i in range(3)]


def _gather_lands(which, shards, d):
    if which == "a":
        return [lax.empty((N_DEV,) + shards[0].shape, BF16), lax.empty((d, d), BF16)]
    return [lax.empty((FF_PAD, d), BF16) for _ in range(3)]


def _scatter_lands(which, grads):
    if which == "a":
        g_in, g_out = grads
        return [lax.empty((N_DEV, g_in.shape[0], W_IN_WINDOW), BF16),
                lax.empty((N_DEV, g_out.shape[0] // N_DEV, g_out.shape[1]), BF16)]
    return [lax.empty((N_DEV, FF_BLK, g.shape[1]), BF16) for g in grads]


def _pad_w_in(full):
    return jnp.pad(full, ((0, 0), (0, NPROJ - IN_DIM)))


def kernel(x, norm_mix, w_in, q_gain, k_gain, conv_w, conv_b, dt_bias, a_log, d_skip, attn_out_gain, ssm_out_gain, w_out, norm_ffn, w_gate, w_up, w_down, loss_target, m_norm_mix, m_w_in, m_q_gain, m_k_gain, m_conv_w, m_conv_b, m_dt_bias, m_a_log, m_d_skip, m_attn_out_gain, m_ssm_out_gain, m_w_out, m_norm_ffn, m_w_gate, m_w_up, m_w_down, v_norm_mix, v_w_in, v_q_gain, v_k_gain, v_conv_w, v_conv_b, v_dt_bias, v_a_log, v_d_skip, v_attn_out_gain, v_ssm_out_gain, v_w_out, v_norm_ffn, v_w_gate, v_w_up, v_w_down):
    nb, seq, d = x.shape
    depth = w_in.shape[0]
    t = nb * seq
    w = dict(norm_mix=norm_mix, w_in=w_in, q_gain=q_gain, k_gain=k_gain, conv_w=conv_w, conv_b=conv_b,
             dt_bias=dt_bias, a_log=a_log, d_skip=d_skip, attn_out_gain=attn_out_gain, ssm_out_gain=ssm_out_gain,
             w_out=w_out, norm_ffn=norm_ffn, w_gate=w_gate, w_up=w_up, w_down=w_down)
    mom = dict(norm_mix=m_norm_mix, w_in=m_w_in, q_gain=m_q_gain, k_gain=m_k_gain, conv_w=m_conv_w, conv_b=m_conv_b,
               dt_bias=m_dt_bias, a_log=m_a_log, d_skip=m_d_skip, attn_out_gain=m_attn_out_gain,
               ssm_out_gain=m_ssm_out_gain, w_out=m_w_out, norm_ffn=m_norm_ffn, w_gate=m_w_gate, w_up=m_w_up,
               w_down=m_w_down)
    var = dict(norm_mix=v_norm_mix, w_in=v_w_in, q_gain=v_q_gain, k_gain=v_k_gain, conv_w=v_conv_w, conv_b=v_conv_b,
               dt_bias=v_dt_bias, a_log=v_a_log, d_skip=v_d_skip, attn_out_gain=v_attn_out_gain,
               ssm_out_gain=v_ssm_out_gain, w_out=v_w_out, norm_ffn=v_norm_ffn, w_gate=v_w_gate, w_up=v_w_up,
               w_down=v_w_down)
    ff = w_gate.shape[2]

    (conv_all,) = _all_gather([conv_w], "gather_conv")

    def shards_a(li):
        return [w_in[li].astype(BF16), w_out[li].astype(BF16)]

    w["w_gate"], mom["w_gate"], var["w_gate"] = (jnp.swapaxes(a, 1, 2) for a in (w_gate, m_w_gate, v_w_gate))
    w["w_up"], mom["w_up"], var["w_up"] = (jnp.swapaxes(a, 1, 2) for a in (w_up, m_w_up, v_w_up))

    def shards_b(li):
        return [jnp.pad(w[k][li].astype(BF16), ((0, FF_BLK - ff), (0, 0))) for k in ("w_gate", "w_up", "w_down")]

    def small_params(li):
        p = {k: w[k][li][None, :] for k in ("norm_mix", "q_gain", "k_gain", "attn_out_gain", "ssm_out_gain", "norm_ffn")}
        conv_full = conv_all[:, li].transpose(1, 0, 2).reshape(CONV_K, CONV_DIM)
        p["cw8"] = _pack_conv(conv_full, conv_b[li])
        p["hp"] = _pack_heads(dt_bias[li], a_log[li], d_skip[li])
        return p

    xc = x.reshape(t, d)
    cur = shards_a(0)
    flight = _gather2_start(cur, _gather_lands("a", cur, d), GATHER_A, "gather_a0")
    flight = _gather2_forward(flight, xc, "gather_a0_fwd")
    lands_a = _gather2_wait(flight, flight["token"], "gather_a0_wait")
    layers, saved = [], []
    for li in range(depth):
        tag = f"l{li}_"
        p = small_params(li)
        p["w_in"] = _pad_w_in(_full_cols(lands_a[0]))
        p["w_out"] = lands_a[1]
        cur = shards_b(li)
        flight = _gather2_start(cur, _gather_lands("b", cur, d), GATHER_B, tag + "gather_b", after=lands_a[1])
        h1 = _rms_fwd(xc, p["norm_mix"], tag + "rms1", after=flight["token"])
        proj = _matmul(h1, p["w_in"], "nn", F32, tag + "mm_in")
        o = _attn_fwd(proj, p["q_gain"], p["k_gain"], nb, seq, tag + "attn")
        flight = _gather2_forward(flight, o, tag + "gather_b_fwd")
        act = _conv_fwd(proj, p["cw8"], nb, seq, tag + "conv", after=flight["token"])
        y = _ssd_fwd(act, proj, p["hp"], nb, seq, tag + "ssd")
        cat = _mix_fwd(o, y, proj, p["attn_out_gain"], p["ssm_out_gain"], tag + "mix")
        x1 = _matmul(cat, p["w_out"], "nn", F32, tag + "mm_out", residual=xc)
        p["w_gate"], p["w_up"], p["w_down"] = _gather2_wait(flight, x1, tag + "gather_b_wait")
        token = None
        if li + 1 < depth:
            nxt = shards_a(li + 1)
            flight = _gather2_start(nxt, _gather_lands("a", nxt, d), GATHER_A, f"gather_a{li + 1}",
                                    after=p["w_down"])
            token = flight["token"]
        h2 = _rms_fwd(x1, p["norm_ffn"], tag + "rms2", after=token)
        a, gate, up = _mm_swiglu(h2, p["w_gate"], p["w_up"], tag + "mm_gu")
        token = None
        if li + 1 < depth:
            flight = _gather2_forward(flight, gate, f"gather_a{li + 1}_fwd")
            token = flight["token"]
        x2 = _matmul(a, p["w_down"], "nn", F32, tag + "mm_down", residual=x1, after=token)
        if li + 1 < depth:
            lands_a = _gather2_wait(flight, x2, f"gather_a{li + 1}_wait")
        saved.append(dict(x=xc, h1=h1, proj=proj, o=o, act=act, y=y, cat=cat, x1=x1, h2=h2, gate=gate, up=up, a=a))
        layers.append(p)
        xc = x2

    loss_blk, dx, dxb = _loss_fwd_bwd(xc, loss_target.reshape(t, d), "loss")
    loss = lax.psum(loss_blk[0, 0], ("x", "y", "c"))

    grads = [dict() for _ in range(depth)]
    recv = [dict() for _ in range(depth)]
    flight_a, token = None, None
    for li in reversed(range(depth)):
        tag = f"l{li}_b_"
        p, s, g = layers[li], saved[li], grads[li]
        dgate, dup = _mm_dact_swiglu(dxb, p["w_down"], s["gate"], s["up"], tag + "mm_dact", after=token)
        g_down = _matmul(s["a"], dxb, "tn", BF16, tag + "mm_dwd")
        dh2 = _matmul(dgate, p["w_gate"], "nn", F32, tag + "mm_dh2g")
        dh2 = _matmul(dup, p["w_up"], "nn", F32, tag + "mm_dh2u", residual=dh2)
        g_gate = _matmul(dgate, s["h2"], "tn", BF16, tag + "mm_dwg")
        g_up = _matmul(dup, s["h2"], "tn", BF16, tag + "mm_dwu")
        if flight_a is not None:
            recv[li + 1]["w_in"], recv[li + 1]["w_out"] = _copies_wait(flight_a, g_up, f"l{li + 1}_b_scatter_a_wait")
        grads_b = [g_gate, g_up, g_down]
        flight_b = _copies_start(grads_b, _scatter_lands("b", grads_b), SCATTER_B,
                                 tag + "scatter_b", after=recv[li + 1]["w_out"] if li + 1 < depth else None)
        dx1, dx1b, g["norm_ffn"] = _rms_bwd(s["x1"], p["norm_ffn"], dh2, dx, tag + "rms2")
        dcat = _matmul(dx1b, p["w_out"], "nt", F32, tag + "mm_dcat", after=flight_b["token"])
        g_out = _matmul(s["cat"], dx1b, "tn", BF16, tag + "mm_dwo")
        do, dy, dz, g["attn_out_gain"], g["ssm_out_gain"] = _mix_bwd(
            dcat, s["o"], s["y"], s["proj"], p["attn_out_gain"], p["ssm_out_gain"], tag + "mix")
        dq, dk, dv, g["q_gain"], g["k_gain"] = _attn_bwd(s["proj"], do, p["q_gain"], p["k_gain"], nb, seq, tag + "attn")
        dxa, dba, dca, ddt, dhp = _ssd_bwd(s["act"], s["proj"], dy, p["hp"], nb, seq, tag + "ssd")
        dxbc, dcw8 = _conv_bwd(s["proj"], jnp.concatenate([dxa, dba, dca], axis=1), p["cw8"], nb, seq, tag + "conv")
        g["conv_w"] = dcw8[0:CONV_K]
        g["conv_b"] = dcw8[CONV_K:CONV_K + 1]
        heads = dhp[:, 0:3, 0:SSM_HG].transpose(1, 0, 2).reshape(3, SSM_HEADS)
        g["dt_bias"], g["a_log"], g["d_skip"] = heads[0:1], heads[1:2], heads[2:3]
        ddt = ddt[:, :LANE] + jnp.roll(ddt[:, LANE:], SSM_HG, axis=1)
        tail = jnp.zeros((t, NPROJ - OFF_DT - LANE), BF16)
        dproj = jnp.concatenate([dq, dk, dv, dz, dxbc, ddt, tail], axis=1)
        recv[li]["w_gate"], recv[li]["w_up"], recv[li]["w_down"] = _copies_wait(flight_b, dproj, tag + "scatter_b_wait")
        dh1 = _matmul(dproj, p["w_in"], "nt", F32, tag + "mm_dh1")
        g_in = _matmul(s["h1"], dproj, "tn", BF16, tag + "mm_dwin")
        flight_a = _copies_start([g_in, g_out], _scatter_lands("a", [g_in, g_out]), SCATTER_A, tag + "scatter_a")
        token = flight_a["token"]
        dx, dxb, g["norm_mix"] = _rms_bwd(s["x"], p["norm_mix"], dh1, dx1, tag + "rms1")
    grad_x = dx.reshape(nb, seq, d)

    out_g, out_d, out_m, out_v = {}, {}, {}, {}

    def update(k, rows, lane_offset=None):
        parts = [recv[li][k] for li in range(depth)]
        out_g[k], out_d[k], out_m[k], out_v[k] = _adamw(parts, w[k], mom[k], var[k], "adamw_" + k, rows, lane_offset)

    def w_in_offset():
        return _my_block() * W_IN_COLS - _w_in_window_start(_my_block())

    update("w_gate", 64)
    update("w_up", 64)
    update("w_down", 64)
    recv[0]["w_in"], recv[0]["w_out"] = _copies_wait(
        flight_a, [out_g["w_gate"], out_g["w_up"], out_g["w_down"], dx], "l0_b_scatter_a_wait")
    update("w_in", 128, w_in_offset)
    update("w_out", 128)
    for res in (out_g, out_d, out_m, out_v):
        res["w_gate"], res["w_up"] = jnp.swapaxes(res["w_gate"], 1, 2), jnp.swapaxes(res["w_up"], 1, 2)

    small_g = [jnp.concatenate([grads[li][k] for li in range(depth)], axis=0) for k in SMALL]
    conv_g = jnp.stack([grads[li]["conv_w"] for li in range(depth)]).reshape(depth, CONV_K * CONV_DIM)
    parts = _all_gather(small_g + [conv_g], "gather_small_grads")
    res = _adamw_small(parts[:-1], [w[k] for k in SMALL], [mom[k] for k in SMALL], [var[k] for k in SMALL],
                       "adamw_small")
    for dst, vals in zip((out_g, out_d, out_m, out_v), res):
        dst.update(dict(zip(SMALL, vals)))
    conv_total = _sum_parts(parts[-1], "sum_conv_grads").reshape(depth, CONV_K, CONV_DIM)
    cshard = conv_w.shape[2]
    conv_mine = lax.dynamic_slice_in_dim(conv_total, _my_block() * cshard, cshard, axis=2)
    flat = lambda a: a.reshape(depth, CONV_K * cshard)
    res = _adamw_small([flat(conv_mine)[None]], [flat(conv_w)], [flat(m_conv_w)], [flat(v_conv_w)], "adamw_conv")
    for dst, vals in zip((out_g, out_d, out_m, out_v), res):
        dst["conv_w"] = vals[0].reshape(depth, CONV_K, cshard)

    names = ("norm_mix", "w_in", "q_gain", "k_gain", "conv_w", "conv_b", "dt_bias", "a_log", "d_skip",
             "attn_out_gain", "ssm_out_gain", "w_out", "norm_ffn", "w_gate", "w_up", "w_down")
    return (loss, grad_x, *[out_g[k] for k in names], *[out_d[k] for k in names],
            *[out_m[k] for k in names], *[out_v[k] for k in names])
```

```python
import functools
import math

import jax
import jax.numpy as jnp
from jax import lax
from jax.experimental import pallas as pl
from jax.experimental.pallas import tpu as pltpu

F32 = jnp.float32
BF16 = jnp.bfloat16
MESH = pl.DeviceIdType.MESH

N_DEV = 8
EPS = 1e-6
ATT_HEADS = 8
ATT_DH = 128
ATT_W = ATT_HEADS * ATT_DH
SSM_W = 1024
SSM_P = 64
SSM_N = 128
SSM_GROUPS = 2
SSM_HG = 8
SSM_HEADS = SSM_GROUPS * SSM_HG
CHUNK = 128
CONV_K = 4
CONV_DIM = SSM_W + 2 * SSM_GROUPS * SSM_N
LANE = 128
OFF_Q, OFF_K, OFF_V, OFF_Z, OFF_XS = 0, ATT_W, 2 * ATT_W, 3 * ATT_W, 4 * ATT_W
OFF_B = OFF_XS + SSM_W
OFF_C = OFF_B + SSM_GROUPS * SSM_N
OFF_DT = OFF_C + SSM_GROUPS * SSM_N
NPROJ = 6144
IN_DIM = OFF_DT + SSM_HEADS

ADAM_LR = 0.001
ADAM_B1 = 0.9
ADAM_B2 = 0.999
ADAM_EPS = 1e-08
ADAM_WD = 0.01
ADAM_STEP = 10

VMEM_LIMIT = 56 * 1024 * 1024
MATMUL_OPERAND_BYTES = 26 * 1024 * 1024


def _params(sem=None):
    return pltpu.CompilerParams(dimension_semantics=sem, vmem_limit_bytes=VMEM_LIMIT)


def _pick(dim, target):
    if dim <= target:
        return dim
    best = None
    for t in range(LANE, target + 1, LANE):
        if dim % t == 0:
            best = t
    assert best is not None, (dim, target)
    return best


def _dot(a, b, dims=((1,), (0,))):
    return lax.dot_general(a, b, (dims, ((), ())), preferred_element_type=F32)


def _dot_nt(a, b):
    return _dot(a, b, ((1,), (1,)))


def _dot_tn(a, b):
    return _dot(a, b, ((0,), (0,)))


def _sigmoid(x):
    return 1.0 / (1.0 + jnp.exp(-x))


def _softplus(x):
    return jnp.maximum(x, 0.0) + jnp.log(1.0 + jnp.exp(-jnp.abs(x)))


def _rstd(x):
    return lax.rsqrt(jnp.mean(x * x, axis=-1, keepdims=True) + EPS)


def _matmul(a, b, mode, out_dtype, name, residual=None, after=None, tm=512, tn=1024, tk=2048):
    if mode == "nn":
        (m, k), (k2, n) = a.shape, b.shape
    elif mode == "nt":
        (m, k), (n, k2) = a.shape, b.shape
    else:
        (k, m), (k2, n) = a.shape, b.shape
    assert k == k2, (a.shape, b.shape, mode)
    tm, tn, tk = _pick(m, tm), _pick(n, tn), _pick(k, tk)
    for cand_tn in (tn, _pick(n, tn // 2)):
        if 2 * 2 * (tm * k + k * cand_tn) <= MATMUL_OPERAND_BYTES:
            tn, tk = cand_tn, k
            break
    nk = k // tk
    dims = {"nn": ((1,), (0,)), "nt": ((1,), (1,)), "tn": ((0,), (0,))}[mode]
    has_res = residual is not None

    has_tok = after is not None

    def body(*refs):
        a_ref, b_ref = refs[:2]
        r_ref = refs[2] if has_res else None
        o_ref = refs[2 + has_res + has_tok]
        prod = _dot(a_ref[...], b_ref[...], dims)

        def finish(r):
            if r_ref is not None:
                r = r + r_ref[...]
            o_ref[...] = r.astype(o_ref.dtype)

        if nk == 1:
            finish(prod)
        else:
            acc = refs[-1]
            kk = pl.program_id(2)

            @pl.when(kk == 0)
            def _():
                acc[...] = prod

            @pl.when(kk > 0)
            def _():
                acc[...] += prod

            @pl.when(kk == nk - 1)
            def _():
                finish(acc[...])

    if mode == "tn":
        a_spec = pl.BlockSpec((tk, tm), lambda i, j, kk: (kk, i))
    else:
        a_spec = pl.BlockSpec((tm, tk), lambda i, j, kk: (i, kk))
    if mode == "nt":
        b_spec = pl.BlockSpec((tn, tk), lambda i, j, kk: (j, kk))
    else:
        b_spec = pl.BlockSpec((tk, tn), lambda i, j, kk: (kk, j))
    o_spec = pl.BlockSpec((tm, tn), lambda i, j, kk: (i, j))
    tok_spec = pl.BlockSpec((8, LANE), lambda i, j, kk: (0, 0))
    in_specs = [a_spec, b_spec] + ([o_spec] if has_res else []) + ([tok_spec] if has_tok else [])
    args = (a, b) + ((residual,) if has_res else ()) + ((after,) if has_tok else ())
    return pl.pallas_call(
        body,
        name=name,
        grid=(m // tm, n // tn, nk),
        in_specs=in_specs,
        out_specs=o_spec,
        out_shape=jax.ShapeDtypeStruct((m, n), out_dtype),
        scratch_shapes=[pltpu.VMEM((tm, tn), F32)] if nk > 1 else [],
        compiler_params=_params(("parallel", "parallel", "arbitrary")),
    )(*args)


def _mm_swiglu(h, wg_t, wu_t, name, tm=512, tn=1024):
    m, k = h.shape
    n = wg_t.shape[0]
    tm, tn = _pick(m, tm), _pick(n, tn)

    def body(h_ref, g_ref, u_ref, a_ref, gs_ref, us_ref):
        hv = h_ref[...]
        g = _dot_nt(hv, g_ref[...])
        u = _dot_nt(hv, u_ref[...])
        a_ref[...] = (g * _sigmoid(g) * u).astype(BF16)
        gs_ref[...] = g.astype(BF16)
        us_ref[...] = u.astype(BF16)

    w_spec = pl.BlockSpec((tn, k), lambda i, j: (j, 0))
    o_spec = pl.BlockSpec((tm, tn), lambda i, j: (i, j))
    out = jax.ShapeDtypeStruct((m, n), BF16)
    return pl.pallas_call(
        body, name=name, grid=(m // tm, n // tn),
        in_specs=[pl.BlockSpec((tm, k), lambda i, j: (i, 0)), w_spec, w_spec],
        out_specs=[o_spec] * 3, out_shape=[out] * 3,
        compiler_params=_params(("parallel", "parallel")),
    )(h, wg_t, wu_t)


def _mm_dact_swiglu(dx, wd, gs, us, name, after=None, tm=512, tn=1024):
    m, k = dx.shape
    n = wd.shape[0]
    tm, tn = _pick(m, tm), _pick(n, tn)
    has_tok = after is not None

    def body(*refs):
        dx_ref, wd_ref, g_ref, u_ref = refs[:4]
        dg_ref, du_ref = refs[-2:]
        dact = _dot_nt(dx_ref[...], wd_ref[...])
        g = g_ref[...].astype(F32)
        sg = _sigmoid(g)
        dg_ref[...] = (dact * u_ref[...].astype(F32) * sg * (1.0 + g * (1.0 - sg))).astype(BF16)
        du_ref[...] = (dact * g * sg).astype(BF16)

    o_spec = pl.BlockSpec((tm, tn), lambda i, j: (i, j))
    tok = [pl.BlockSpec((8, LANE), lambda i, j: (0, 0))] if has_tok else []
    out = jax.ShapeDtypeStruct((m, n), BF16)
    return pl.pallas_call(
        body, name=name, grid=(m // tm, n // tn),
        in_specs=[pl.BlockSpec((tm, k), lambda i, j: (i, 0)), pl.BlockSpec((tn, k), lambda i, j: (j, 0)),
                  o_spec, o_spec] + tok,
        out_specs=[o_spec] * 2, out_shape=[out] * 2,
        compiler_params=_params(("parallel", "parallel")),
    )(dx, wd, gs, us, *((after,) if has_tok else ()))


ROWS = 512


def _rms_fwd(x, g, name, after=None):
    t, d = x.shape
    has_tok = after is not None

    def body(*refs):
        x_ref, g_ref, o_ref = refs[0], refs[1], refs[-1]
        xv = x_ref[...]
        o_ref[...] = (xv * _rstd(xv) * g_ref[...]).astype(BF16)

    row = pl.BlockSpec((ROWS, d), lambda i: (i, 0))
    tok = [pl.BlockSpec((8, LANE), lambda i: (0, 0))] if has_tok else []
    return pl.pallas_call(
        body, name=name, grid=(t // ROWS,),
        in_specs=[row, pl.BlockSpec((1, d), lambda i: (0, 0))] + tok,
        out_specs=row, out_shape=jax.ShapeDtypeStruct((t, d), BF16),
        compiler_params=_params(("parallel",)),
    )(x, g, *((after,) if has_tok else ()))


def _rms_bwd(x, g, dh, dres, name):
    t, d = x.shape

    def body(x_ref, g_ref, dh_ref, dr_ref, dx_ref, dxb_ref, dg_ref):
        xv = x_ref[...]
        r = _rstd(xv)
        xh = xv * r
        dhv = dh_ref[...]

        @pl.when(pl.program_id(0) == 0)
        def _():
            dg_ref[...] = jnp.zeros_like(dg_ref)

        dg_ref[...] += jnp.sum(dhv * xh, axis=0, keepdims=True)
        dxh = dhv * g_ref[...]
        dx = r * (dxh - xh * jnp.mean(dxh * xh, axis=-1, keepdims=True)) + dr_ref[...]
        dx_ref[...] = dx
        dxb_ref[...] = dx.astype(BF16)

    row = pl.BlockSpec((ROWS // 2, d), lambda i: (i, 0))
    vec = pl.BlockSpec((1, d), lambda i: (0, 0))
    return pl.pallas_call(
        body, name=name, grid=(t // (ROWS // 2),),
        in_specs=[row, vec, row, row],
        out_specs=[row, row, vec],
        out_shape=[jax.ShapeDtypeStruct((t, d), F32), jax.ShapeDtypeStruct((t, d), BF16),
                   jax.ShapeDtypeStruct((1, d), F32)],
        compiler_params=_params(("arbitrary",)),
    )(x, g, dh, dres)


def _loss_fwd_bwd(y, target, name):
    t, d = y.shape
    inv = 1.0 / d

    def body(y_ref, t_ref, l_ref, dy_ref, dyb_ref):
        e = y_ref[...] - t_ref[...]

        @pl.when(pl.program_id(0) == 0)
        def _():
            l_ref[...] = jnp.zeros_like(l_ref)

        l_ref[...] += 0.5 * inv * jnp.sum(e * e)
        dy = e * inv
        dy_ref[...] = dy
        dyb_ref[...] = dy.astype(BF16)

    row = pl.BlockSpec((ROWS, d), lambda i: (i, 0))
    return pl.pallas_call(
        body, name=name, grid=(t // ROWS,),
        in_specs=[row, row],
        out_specs=[pl.BlockSpec((8, LANE), lambda i: (0, 0)), row, row],
        out_shape=[jax.ShapeDtypeStruct((8, LANE), F32), jax.ShapeDtypeStruct((t, d), F32),
                   jax.ShapeDtypeStruct((t, d), BF16)],
        compiler_params=_params(("arbitrary",)),
    )(y, target)


MIX_ROWS = 256


def _mix_fwd(o, y, proj, ga, gs, name):
    t = o.shape[0]
    gw = SSM_W // SSM_GROUPS

    def body(o_ref, y_ref, z_ref, ga_ref, gs_ref, c_ref):
        ov = o_ref[...]
        c_ref[:, 0:ATT_W] = (ov * _rstd(ov) * ga_ref[...]).astype(BF16)
        zv = z_ref[...]
        yz = y_ref[...] * (zv * _sigmoid(zv))
        for gi in range(SSM_GROUPS):
            seg = yz[:, gi * gw:(gi + 1) * gw]
            c_ref[:, ATT_W + gi * gw:ATT_W + (gi + 1) * gw] = (
                seg * _rstd(seg) * gs_ref[:, gi * gw:(gi + 1) * gw]).astype(BF16)

    half = pl.BlockSpec((MIX_ROWS, ATT_W), lambda i: (i, 0))
    vec = pl.BlockSpec((1, ATT_W), lambda i: (0, 0))
    return pl.pallas_call(
        body, name=name, grid=(t // MIX_ROWS,),
        in_specs=[half, half, pl.BlockSpec((MIX_ROWS, ATT_W), lambda i: (i, OFF_Z // ATT_W)), vec, vec],
        out_specs=pl.BlockSpec((MIX_ROWS, 2 * ATT_W), lambda i: (i, 0)),
        out_shape=jax.ShapeDtypeStruct((t, 2 * ATT_W), BF16),
        compiler_params=_params(("parallel",)),
    )(o, y, proj, ga, gs)


def _mix_bwd(dcat, o, y, proj, ga, gs, name):
    t = o.shape[0]
    gw = SSM_W // SSM_GROUPS

    def body(dc_ref, o_ref, y_ref, z_ref, ga_ref, gs_ref, do_ref, dy_ref, dz_ref, dga_ref, dgs_ref):
        @pl.when(pl.program_id(0) == 0)
        def _():
            dga_ref[...] = jnp.zeros_like(dga_ref)
            dgs_ref[...] = jnp.zeros_like(dgs_ref)

        ov = o_ref[...]
        r = _rstd(ov)
        oh = ov * r
        d_on = dc_ref[:, 0:ATT_W]
        dga_ref[...] += jnp.sum(d_on * oh, axis=0, keepdims=True)
        doh = d_on * ga_ref[...]
        do_ref[...] = r * (doh - oh * jnp.mean(doh * oh, axis=-1, keepdims=True))

        zv = z_ref[...]
        yv = y_ref[...]
        sz = _sigmoid(zv)
        silu = zv * sz
        yz = yv * silu
        for gi in range(SSM_GROUPS):
            sl = slice(gi * gw, (gi + 1) * gw)
            seg = yz[:, sl]
            rg = _rstd(seg)
            yh = seg * rg
            dyn = dc_ref[:, ATT_W + gi * gw:ATT_W + (gi + 1) * gw]
            dgs_ref[:, sl] += jnp.sum(dyn * yh, axis=0, keepdims=True)
            dyh = dyn * gs_ref[:, sl]
            dyz = rg * (dyh - yh * jnp.mean(dyh * yh, axis=-1, keepdims=True))
            dy_ref[:, sl] = dyz * silu[:, sl]
            dz_ref[:, sl] = (dyz * yv[:, sl] * (sz[:, sl] * (1.0 + zv[:, sl] * (1.0 - sz[:, sl])))).astype(BF16)

    half = pl.BlockSpec((MIX_ROWS, ATT_W), lambda i: (i, 0))
    vec = pl.BlockSpec((1, ATT_W), lambda i: (0, 0))
    return pl.pallas_call(
        body, name=name, grid=(t // MIX_ROWS,),
        in_specs=[pl.BlockSpec((MIX_ROWS, 2 * ATT_W), lambda i: (i, 0)), half, half,
                  pl.BlockSpec((MIX_ROWS, ATT_W), lambda i: (i, OFF_Z // ATT_W)), vec, vec],
        out_specs=[half, half, half, vec, vec],
        out_shape=[jax.ShapeDtypeStruct((t, ATT_W), F32), jax.ShapeDtypeStruct((t, SSM_W), F32),
                   jax.ShapeDtypeStruct((t, SSM_W), BF16), jax.ShapeDtypeStruct((1, ATT_W), F32),
                   jax.ShapeDtypeStruct((1, SSM_W), F32)],
        compiler_params=_params(("arbitrary",)),
    )(dcat, o, y, proj, ga, gs)


ATT_QB = 256
ATT_KB = 256
assert ATT_QB == ATT_KB


def _stacked(m):
    return jnp.concatenate([m, m], axis=0)


def _split_sum(x, m2):
    hi = x.astype(BF16)
    lo = (x - hi.astype(F32)).astype(BF16)
    return _dot(jnp.concatenate([hi, lo], axis=1), m2)


def _att_tile(z, mask, m_strict2, carry):
    lse = jnp.log(1.0 + jnp.exp(-jnp.abs(z)))
    lb = jnp.minimum(z, 0.0) - lse
    lrm = -jnp.maximum(z, 0.0) - lse
    if mask is not None:
        lrm = jnp.where(mask, lrm, 0.0)
    w = jnp.exp(lb + _split_sum(lrm, m_strict2) + carry)
    if mask is not None:
        w = jnp.where(mask, w, 0.0)
    return lb, lrm, w


ATT_HP = 2
ATT_HP_FWD = 4


def _head_spec(seq, off, hp=ATT_HP):
    width = hp * ATT_DH
    per = ATT_HEADS // hp
    return pl.BlockSpec((seq, width), lambda s: (s // per, off // width + s % per))


def _head_lanes(h):
    return slice(h * ATT_DH, (h + 1) * ATT_DH)


def _attn_fwd(proj, gq, gk, nb, seq, name):
    nq = seq // ATT_QB
    scale = ATT_DH ** -0.5
    hp = ATT_HP_FWD
    heads = range(hp)

    def body(q_ref, k_ref, v_ref, gq_ref, gk_ref, o_ref, qs, kn, vb):
        for h in heads:
            sl = _head_lanes(h)
            qv = q_ref[:, sl]
            kv = k_ref[:, sl]
            qs[:, sl] = (qv * _rstd(qv) * gq_ref[...] * scale).astype(BF16)
            kn[:, sl] = (kv * _rstd(kv) * gk_ref[...]).astype(BF16)
            vb[:, sl] = v_ref[:, sl].astype(BF16)
        row = lax.broadcasted_iota(jnp.int32, (ATT_QB, ATT_KB), 0)
        col = lax.broadcasted_iota(jnp.int32, (ATT_QB, ATT_KB), 1)
        m_strict2 = _stacked((row > col).astype(BF16))
        diagonal = col < row

        def key_rows(kj):
            return pl.ds(pl.multiple_of(kj * ATT_KB, ATT_KB), ATT_KB)

        def q_loop(qi, _):
            q0 = pl.multiple_of(qi * ATT_QB, ATT_QB)
            q_ts = [qs[pl.ds(q0, ATT_QB), _head_lanes(h)] for h in heads]

            def scores(h, kj):
                return _dot_nt(q_ts[h], kn[key_rows(kj), _head_lanes(h)])

            def tile(c, kj, mask):
                rows = key_rows(kj)
                out = []
                for h in heads:
                    acc, carry, z = c[h]
                    z_next = scores(h, jnp.maximum(kj - 1, 0))
                    _, lrm, w = _att_tile(z, mask, m_strict2, carry)
                    acc = acc + _dot(w.astype(BF16), vb[rows, _head_lanes(h)])
                    out.append((acc, carry + jnp.sum(lrm, axis=-1, keepdims=True), z_next))
                return tuple(out)

            init = tuple((jnp.zeros((ATT_QB, ATT_DH), F32), jnp.zeros((ATT_QB, 1), F32), scores(h, qi)) for h in heads)
            res = lax.fori_loop(1, qi + 1, lambda i, c: tile(c, qi - i, None), tile(init, qi, diagonal))
            for h in heads:
                o_ref[pl.ds(q0, ATT_QB), _head_lanes(h)] = res[h][0]
            return 0

        lax.fori_loop(0, nq, q_loop, 0)

    vec = pl.BlockSpec((1, ATT_DH), lambda s: (0, 0))
    return pl.pallas_call(
        body, name=name, grid=(nb * ATT_HEADS // hp,),
        in_specs=[_head_spec(seq, OFF_Q, hp), _head_spec(seq, OFF_K, hp), _head_spec(seq, OFF_V, hp), vec, vec],
        out_specs=_head_spec(seq, 0, hp),
        out_shape=jax.ShapeDtypeStruct((nb * seq, ATT_W), F32),
        scratch_shapes=[pltpu.VMEM((seq, hp * ATT_DH), BF16)] * 3,
        compiler_params=_params(("parallel",)),
    )(proj, proj, proj, gq, gk)


def _attn_bwd(proj, do, gq, gk, nb, seq, name):
    nq = seq // ATT_QB
    nk = seq // ATT_KB
    scale = ATT_DH ** -0.5
    heads = range(ATT_HP)

    def body(q_ref, k_ref, v_ref, do_ref, gq_ref, gk_ref, dq_ref, dk_ref, dv_ref, dgq_ref, dgk_ref,
             qs, kn, vb, dob, dq_acc, dk_acc, dv_acc, gbuf, bbuf):
        @pl.when(pl.program_id(0) == 0)
        def _():
            dgq_ref[...] = jnp.zeros_like(dgq_ref)
            dgk_ref[...] = jnp.zeros_like(dgk_ref)

        for h in heads:
            sl = _head_lanes(h)
            qv = q_ref[:, sl]
            kv = k_ref[:, sl]
            qs[:, sl] = (qv * _rstd(qv) * gq_ref[...] * scale).astype(BF16)
            kn[:, sl] = (kv * _rstd(kv) * gk_ref[...]).astype(BF16)
            vb[:, sl] = v_ref[:, sl].astype(BF16)
            dob[:, sl] = do_ref[:, sl].astype(BF16)
        dk_acc[...] = jnp.zeros_like(dk_acc)
        dv_acc[...] = jnp.zeros_like(dv_acc)
        row = lax.broadcasted_iota(jnp.int32, (ATT_QB, ATT_KB), 0)
        col = lax.broadcasted_iota(jnp.int32, (ATT_QB, ATT_KB), 1)
        m_strict2 = _stacked((row > col).astype(BF16))
        m_prefix2 = _stacked((row < col).astype(BF16))
        diagonal = col < row

        def key_rows(kj):
            return pl.ds(pl.multiple_of(kj * ATT_KB, ATT_KB), ATT_KB)

        def q_loop(qi, _):
            q0 = pl.multiple_of(qi * ATT_QB, ATT_QB)
            q_ts = [qs[pl.ds(q0, ATT_QB), _head_lanes(h)] for h in heads]
            do_ts = [dob[pl.ds(q0, ATT_QB), _head_lanes(h)] for h in heads]

            def scores(h, kj):
                return _dot_nt(q_ts[h], kn[key_rows(kj), _head_lanes(h)])

            def down(c, kj, mask):
                rows = key_rows(kj)
                out = []
                for h in heads:
                    carry, z = c[h]
                    sl = _head_lanes(h)
                    z_next = scores(h, jnp.maximum(kj - 1, 0))
                    lb, lrm, w = _att_tile(z, mask, m_strict2, carry)
                    dw = _dot_nt(do_ts[h], vb[rows, sl])
                    gbuf[h * nk + kj] = w * dw
                    bbuf[h * nk + kj] = jnp.exp(lb)
                    dv_acc[rows, sl] += _dot_tn(w.astype(BF16), do_ts[h])
                    out.append((carry + jnp.sum(lrm, axis=-1, keepdims=True), z_next))
                return tuple(out)

            init = tuple((jnp.zeros((ATT_QB, 1), F32), scores(h, qi)) for h in heads)
            lax.fori_loop(1, qi + 1, lambda i, c: down(c, qi - i, None), down(init, qi, diagonal))

            def up(c, kj, mask):
                rows = key_rows(kj)
                out = []
                for h in heads:
                    acc, carry, within = c[h]
                    sl = _head_lanes(h)
                    g = gbuf[h * nk + kj]
                    beta = bbuf[h * nk + kj]
                    within_next = _split_sum(gbuf[h * nk + jnp.minimum(kj + 1, qi)], m_prefix2)
                    dz = g * (1.0 - beta) - (within + carry) * beta
                    if mask is not None:
                        dz = jnp.where(mask, dz, 0.0)
                    dz = dz.astype(BF16)
                    acc = acc + _dot(dz, kn[rows, sl])
                    dk_acc[rows, sl] += _dot_tn(dz, q_ts[h])
                    out.append((acc, carry + jnp.sum(g, axis=-1, keepdims=True), within_next))
                return tuple(out)

            init = tuple((jnp.zeros((ATT_QB, ATT_DH), F32), jnp.zeros((ATT_QB, 1), F32),
                          _split_sum(gbuf[h * nk], m_prefix2)) for h in heads)
            res = up(lax.fori_loop(0, qi, lambda kj, c: up(c, kj, None), init), qi, diagonal)
            for h in heads:
                dq_acc[pl.ds(q0, ATT_QB), _head_lanes(h)] = res[h][0]
            return 0

        lax.fori_loop(0, nq, q_loop, 0)

        def norm_bwd(xv, gain, dyn):
            r = _rstd(xv)
            xh = xv * r
            dgain = jnp.sum(dyn * xh, axis=0, keepdims=True)
            dxh = dyn * gain
            return r * (dxh - xh * jnp.mean(dxh * xh, axis=-1, keepdims=True)), dgain

        for h in heads:
            sl = _head_lanes(h)
            dq, dgq = norm_bwd(q_ref[:, sl], gq_ref[...], dq_acc[:, sl] * scale)
            dk, dgk = norm_bwd(k_ref[:, sl], gk_ref[...], dk_acc[:, sl])
            dq_ref[:, sl] = dq.astype(BF16)
            dk_ref[:, sl] = dk.astype(BF16)
            dv_ref[:, sl] = dv_acc[:, sl].astype(BF16)
            dgq_ref[...] += dgq
            dgk_ref[...] += dgk

    vec = pl.BlockSpec((1, ATT_DH), lambda s: (0, 0))
    big = jax.ShapeDtypeStruct((nb * seq, ATT_W), BF16)
    small = jax.ShapeDtypeStruct((1, ATT_DH), F32)
    width = ATT_HP * ATT_DH
    return pl.pallas_call(
        body, name=name, grid=(nb * ATT_HEADS // ATT_HP,),
        in_specs=[_head_spec(seq, OFF_Q), _head_spec(seq, OFF_K), _head_spec(seq, OFF_V), _head_spec(seq, 0), vec, vec],
        out_specs=[_head_spec(seq, 0)] * 3 + [vec, vec],
        out_shape=[big, big, big, small, small],
        scratch_shapes=[pltpu.VMEM((seq, width), BF16)] * 4 + [pltpu.VMEM((seq, width), F32)] * 3
        + [pltpu.VMEM((ATT_HP * nk, ATT_QB, ATT_KB), F32)] * 2,
        compiler_params=_params(("arbitrary",)),
    )(proj, proj, proj, do, gq, gk)


CONV_COLS = 256


def _pack_conv(conv_w, conv_b):
    return jnp.concatenate([conv_w, conv_b[None, :], jnp.zeros((3, CONV_DIM), F32)], axis=0)


def _conv_pre(raw, w8, rowi):
    pre = w8[CONV_K:CONV_K + 1, :] + raw * w8[CONV_K - 1:CONV_K, :]
    for k in range(1, CONV_K):
        sh = jnp.where(rowi >= k, pltpu.roll(raw, k, 0), 0.0)
        pre = pre + sh * w8[CONV_K - 1 - k:CONV_K - k, :]
    return pre


def _conv_fwd(proj, cw8, nb, seq, name, after):
    ncol = CONV_DIM // CONV_COLS

    def body(x_ref, w_ref, tok_ref, o_ref):
        rowi = lax.broadcasted_iota(jnp.int32, (seq, 1), 0)
        pre = _conv_pre(x_ref[...], w_ref[...], rowi)
        o_ref[...] = pre * _sigmoid(pre)

    return pl.pallas_call(
        body, name=name, grid=(nb, ncol),
        in_specs=[pl.BlockSpec((seq, CONV_COLS), lambda b, j: (b, OFF_XS // CONV_COLS + j)),
                  pl.BlockSpec((8, CONV_COLS), lambda b, j: (0, j)),
                  pl.BlockSpec((8, LANE), lambda b, j: (0, 0))],
        out_specs=pl.BlockSpec((seq, CONV_COLS), lambda b, j: (b, j)),
        out_shape=jax.ShapeDtypeStruct((nb * seq, CONV_DIM), F32),
        compiler_params=_params(("parallel", "parallel")),
    )(proj, cw8, after)


def _conv_bwd(proj, dact, cw8, nb, seq, name):
    ncol = CONV_DIM // CONV_COLS

    def body(x_ref, d_ref, w_ref, dx_ref, dw_ref):
        @pl.when(pl.program_id(1) == 0)
        def _():
            dw_ref[...] = jnp.zeros_like(dw_ref)

        rowi = lax.broadcasted_iota(jnp.int32, (seq, 1), 0)
        raw = x_ref[...]
        w8 = w_ref[...]
        delayed = [raw] + [jnp.where(rowi >= k, pltpu.roll(raw, k, 0), 0.0) for k in range(1, CONV_K)]
        pre = w8[CONV_K:CONV_K + 1, :]
        for k in range(CONV_K):
            pre = pre + delayed[k] * w8[CONV_K - 1 - k:CONV_K - k, :]
        sg = _sigmoid(pre)
        dpre = d_ref[...] * (sg * (1.0 + pre * (1.0 - sg)))
        dw_ref[CONV_K:CONV_K + 1, :] += jnp.sum(dpre, axis=0, keepdims=True)
        draw = dpre * w8[CONV_K - 1:CONV_K, :]
        for k in range(CONV_K):
            dw_ref[CONV_K - 1 - k:CONV_K - k, :] += jnp.sum(dpre * delayed[k], axis=0, keepdims=True)
            if k > 0:
                up = jnp.where(rowi < seq - k, pltpu.roll(dpre, seq - k, 0), 0.0)
                draw = draw + up * w8[CONV_K - 1 - k:CONV_K - k, :]
        dx_ref[...] = draw.astype(BF16)

    return pl.pallas_call(
        body, name=name, grid=(ncol, nb),
        in_specs=[pl.BlockSpec((seq, CONV_COLS), lambda j, b: (b, OFF_XS // CONV_COLS + j)),
                  pl.BlockSpec((seq, CONV_COLS), lambda j, b: (b, j)),
                  pl.BlockSpec((8, CONV_COLS), lambda j, b: (0, j))],
        out_specs=[pl.BlockSpec((seq, CONV_COLS), lambda j, b: (b, j)),
                   pl.BlockSpec((8, CONV_COLS), lambda j, b: (0, j))],
        out_shape=[jax.ShapeDtypeStruct((nb * seq, CONV_DIM), BF16), jax.ShapeDtypeStruct((8, CONV_DIM), F32)],
        compiler_params=_params(("parallel", "arbitrary")),
    )(proj, dact, cw8)


def _pack_heads(dt_bias, a_log, d_skip):
    rows = jnp.stack([dt_bias, a_log, d_skip]).reshape(3, SSM_GROUPS, SSM_HG).transpose(1, 0, 2)
    return jnp.pad(rows, ((0, 0), (0, 8 - 3), (0, LANE - SSM_HG)))


def _split3_rows(x):
    hi = x.astype(BF16)
    r1 = x - hi.astype(F32)
    mid = r1.astype(BF16)
    lo = (r1 - mid.astype(F32)).astype(BF16)
    return jnp.concatenate([hi, mid, lo], axis=0)


def _split3_cols(x):
    hi = x.astype(BF16)
    r1 = x - hi.astype(F32)
    mid = r1.astype(BF16)
    lo = (r1 - mid.astype(F32)).astype(BF16)
    return jnp.concatenate([hi, mid, lo], axis=1)


def _split2_rows(x):
    hi = x.astype(BF16)
    return jnp.concatenate([hi, (x - hi.astype(F32)).astype(BF16)], axis=0)


def _ssd_specs(seq):
    gx = SSM_HG * SSM_P
    return dict(
        xs=pl.BlockSpec((seq, gx), lambda g, b: (b, g)),
        bm=pl.BlockSpec((seq, SSM_N), lambda g, b: (b, SSM_W // SSM_N + g)),
        cm=pl.BlockSpec((seq, SSM_N), lambda g, b: (b, SSM_W // SSM_N + SSM_GROUPS + g)),
        dt=pl.BlockSpec((seq, LANE), lambda g, b: (b, OFF_DT // LANE)),
        hp=pl.BlockSpec((1, 8, LANE), lambda g, b: (g, 0, 0)),
        head=pl.BlockSpec((seq, gx), lambda g, b: (b, g)),
        grp=pl.BlockSpec((seq, SSM_N), lambda g, b: (b, g)),
    )


SSM_GX = SSM_HG * SSM_P


def _ssd_masks():
    li = lax.broadcasted_iota(jnp.int32, (CHUNK, CHUNK), 0)
    si = lax.broadcasted_iota(jnp.int32, (CHUNK, CHUNK), 1)
    head = lax.broadcasted_iota(jnp.int32, (LANE, SSM_GX), 0)
    lane = lax.broadcasted_iota(jnp.int32, (LANE, SSM_GX), 1)
    expand = (lane // SSM_P == head).astype(BF16)
    head_t = lax.broadcasted_iota(jnp.int32, (SSM_GX, LANE), 1)
    lane_t = lax.broadcasted_iota(jnp.int32, (SSM_GX, LANE), 0)
    gather = (lane_t // SSM_P == head_t).astype(BF16)
    return dict(
        causal=li >= si, causal_t=si >= li,
        tril3=jnp.concatenate([(si <= li).astype(BF16)] * 3, axis=1),
        triu2=jnp.concatenate([(si >= li).astype(BF16)] * 2, axis=1),
        below2=jnp.concatenate([(si < li).astype(BF16)] * 2, axis=1),
        expand2=_stacked(expand), gather2=_stacked(gather))


def _per_head(x, mk):
    return _split_sum(x, mk["expand2"])


def _head_sums(x, mk):
    return _split_sum(x, mk["gather2"])


def _row8(v):
    return jnp.broadcast_to(v, (8, v.shape[1]))


def _group_dt(dt_ref):
    shift = (LANE - SSM_HG * pl.program_id(0)) % LANE
    return pltpu.roll(dt_ref[...], shift, 1)


def _ssd_fwd(act, proj, hp, nb, seq, name):
    nc = seq // CHUNK

    def body(xs_ref, b_ref, c_ref, dt_ref, hp_ref, y_ref, dt_s, da_s, hst):
        mk = _ssd_masks()
        hpv = hp_ref[0]
        dt = _softplus(_group_dt(dt_ref) + hpv[0:1, :])
        a = -jnp.exp(hpv[1:2, :])
        dsk_row = _per_head(_row8(hpv[2:3, :]), mk)[0:1]
        dt_s[...] = dt
        da_s[...] = dt * a
        hst[...] = jnp.zeros_like(hst)

        def chunk(c, _):
            rows = pl.ds(pl.multiple_of(c * CHUNK, CHUNK), CHUNK)
            acol = _dot(mk["tril3"], _split3_rows(da_s[rows, :]))
            arow = acol.T
            alast = acol[CHUNK - 1:CHUNK, :]
            ea = _per_head(jnp.exp(acol), mk)
            eb = _per_head(jnp.exp(alast - acol), mk)
            el = _per_head(_row8(jnp.exp(alast)), mk)[0:1]
            bb = b_ref[rows, :].astype(BF16)
            cb = c_ref[rows, :].astype(BF16)
            cbm = _dot_nt(cb, bb)
            xc = xs_ref[rows, :]
            u = xc * _per_head(dt_s[rows, :], mk)
            ub = u.astype(BF16)
            ht = hst[...]
            y_ref[rows, :] = ea * _dot(cb, ht.astype(BF16)) + dsk_row * xc
            for j in range(SSM_HG):
                sl = slice(j * SSM_P, (j + 1) * SSM_P)
                decay = jnp.where(mk["causal"], jnp.exp(jnp.minimum(acol[:, j:j + 1] - arow[j:j + 1, :], 0.0)), 0.0)
                y_ref[rows, sl] += _dot((cbm * decay).astype(BF16), ub[:, sl])
            hst[...] = el * ht + _dot_tn(bb, (u * eb).astype(BF16))
            return 0

        lax.fori_loop(0, nc, chunk, 0)

    sp = _ssd_specs(seq)
    return pl.pallas_call(
        body, name=name, grid=(SSM_GROUPS, nb),
        in_specs=[sp["xs"], sp["bm"], sp["cm"], sp["dt"], sp["hp"]],
        out_specs=sp["head"],
        out_shape=jax.ShapeDtypeStruct((nb * seq, SSM_W), F32),
        scratch_shapes=[pltpu.VMEM((seq, LANE), F32)] * 2 + [pltpu.VMEM((SSM_N, SSM_GX), F32)],
        compiler_params=_params(("parallel", "parallel")),
    )(act, act, act, proj, hp)


def _ssd_bwd(act, proj, dy, hp, nb, seq, name):
    nc = seq // CHUNK

    def body(xs_ref, b_ref, c_ref, dt_ref, hp_ref, dy_ref, dxs_ref, db_ref, dc_ref, ddt_ref, dhp_ref,
             dt_s, da_s, ddt_s, hs, lam, du_s):
        @pl.when(pl.program_id(1) == 0)
        def _():
            dhp_ref[...] = jnp.zeros_like(dhp_ref)

        mk = _ssd_masks()
        hpv = hp_ref[0]
        a = -jnp.exp(hpv[1:2, :])
        dsk_row = _per_head(_row8(hpv[2:3, :]), mk)[0:1]
        dt_s[...] = _softplus(_group_dt(dt_ref) + hpv[0:1, :])
        da_s[...] = dt_s[...] * a
        lane = lax.broadcasted_iota(jnp.int32, (1, LANE), 1)

        def chunk_rows(c):
            return pl.ds(pl.multiple_of(c * CHUNK, CHUNK), CHUNK)

        def decays(c):
            acol = _dot(mk["tril3"], _split3_rows(da_s[chunk_rows(c), :]))
            alast = acol[CHUNK - 1:CHUNK, :]
            return acol, alast

        hs[0] = jnp.zeros((SSM_N, SSM_GX), F32)

        def fwd_chunk(c, _):
            rows = chunk_rows(c)
            acol, alast = decays(c)
            eb = _per_head(jnp.exp(alast - acol), mk)
            el = _per_head(_row8(jnp.exp(alast)), mk)[0:1]
            u = xs_ref[rows, :] * _per_head(dt_s[rows, :], mk)
            hs[c + 1] = el * hs[c] + _dot_tn(b_ref[rows, :].astype(BF16), (u * eb).astype(BF16))
            return 0

        lax.fori_loop(0, nc - 1, fwd_chunk, 0)
        lam[...] = jnp.zeros_like(lam)

        def bwd_chunk(i, carry):
            dd_row, da_vec = carry
            c = nc - 1 - i
            rows = chunk_rows(c)
            acol, alast = decays(c)
            arow = acol.T
            ea = _per_head(jnp.exp(acol), mk)
            eb = _per_head(jnp.exp(alast - acol), mk)
            el = _per_head(_row8(jnp.exp(alast)), mk)[0:1]
            dt_all = _per_head(dt_s[rows, :], mk)
            bb = b_ref[rows, :].astype(BF16)
            cb = c_ref[rows, :].astype(BF16)
            cbm = _dot_nt(cb, bb)
            cbt = _dot_nt(bb, cb)
            xc = xs_ref[rows, :]
            dyc = dy_ref[rows, :]
            u = xc * dt_all
            ub = u.astype(BF16)
            dyb = dyc.astype(BF16)
            h_in = hs[c]
            lm = lam[...]
            hb = h_in.astype(BF16)
            lb = lm.astype(BF16)
            y_off = ea * _dot(cb, hb)
            du_off = eb * _dot(bb, lb)
            dye = (ea * dyc).astype(BF16)
            zero = jnp.zeros((CHUNK, CHUNK), F32)
            dcb, dcbt, d_a = zero, zero, zero
            for j in range(SSM_HG):
                sl = slice(j * SSM_P, (j + 1) * SSM_P)
                seg = acol[:, j:j + 1] - arow[j:j + 1, :]
                decay = jnp.where(mk["causal"], jnp.exp(jnp.minimum(seg, 0.0)), 0.0)
                decay_t = jnp.where(mk["causal_t"], jnp.exp(jnp.minimum(-seg, 0.0)), 0.0)
                m = cbm * decay
                mt = cbt * decay_t
                dm = _dot_nt(dyb[:, sl], ub[:, sl])
                dmt = _dot_nt(ub[:, sl], dyb[:, sl])
                dcb = dcb + dm * decay
                dcbt = dcbt + dmt * decay_t
                du_s[:, sl] = _dot(mt.astype(BF16), dyb[:, sl])
                d_a_j = jnp.sum(dm * m, axis=-1, keepdims=True) - jnp.sum(dmt * mt, axis=-1, keepdims=True)
                d_a = jnp.where(lane == j, d_a_j, d_a)
            du = du_s[...] + du_off
            dxs_ref[rows, :] = du * dt_all + dsk_row * dyc
            dc_ref[rows, :] = _dot_nt(dye, hb) + _dot(dcb.astype(BF16), bb)
            db_ref[rows, :] = _dot_nt((eb * u).astype(BF16), lb) + _dot(dcbt.astype(BF16), cb)
            lam[...] = el * lm + _dot_tn(cb, dye)
            d_a = d_a + _head_sums(dyc * y_off, mk)
            f_a = _head_sums(du_off * u, mk)
            c_a = jnp.exp(alast) * _head_sums(_row8(jnp.sum(lm * h_in, axis=0, keepdims=True)), mk)[0:1]
            dda = _dot(mk["triu2"], _split2_rows(d_a)) + _dot(mk["below2"], _split2_rows(f_a)) + c_a
            ddt_s[rows, :] = dda * a + _head_sums(du * xc, mk)
            da_vec = da_vec + jnp.sum(dda * dt_s[rows, :], axis=0, keepdims=True)
            dd_row = dd_row + jnp.sum(dyc * xc, axis=0, keepdims=True)
            return dd_row, da_vec

        init = (jnp.zeros((1, SSM_GX), F32), jnp.zeros((1, LANE), F32))
        dd_row, da_vec = lax.fori_loop(0, nc, bwd_chunk, init)
        ddt_raw = ddt_s[...] * _sigmoid(_group_dt(dt_ref) + hpv[0:1, :])
        ddt_ref[...] = ddt_raw.astype(BF16)
        dhp_ref[0, 0:1, :] += jnp.sum(ddt_raw, axis=0, keepdims=True)
        dhp_ref[0, 1:2, :] += da_vec * a
        dhp_ref[0, 2:3, :] += _head_sums(_row8(dd_row), mk)[0:1]

    sp = _ssd_specs(seq)
    t = nb * seq
    return pl.pallas_call(
        body, name=name, grid=(SSM_GROUPS, nb),
        in_specs=[sp["xs"], sp["bm"], sp["cm"], sp["dt"], sp["hp"], sp["head"]],
        out_specs=[sp["head"], sp["grp"], sp["grp"], sp["grp"], sp["hp"]],
        out_shape=[jax.ShapeDtypeStruct((t, SSM_W), F32), jax.ShapeDtypeStruct((t, SSM_GROUPS * SSM_N), F32),
                   jax.ShapeDtypeStruct((t, SSM_GROUPS * SSM_N), F32),
                   jax.ShapeDtypeStruct((t, SSM_GROUPS * LANE), BF16),
                   jax.ShapeDtypeStruct((SSM_GROUPS, 8, LANE), F32)],
        scratch_shapes=[pltpu.VMEM((seq, LANE), F32)] * 3
        + [pltpu.VMEM((nc, SSM_N, SSM_GX), F32), pltpu.VMEM((SSM_N, SSM_GX), F32), pltpu.VMEM((CHUNK, SSM_GX), F32)],
        compiler_params=_params(("parallel", "arbitrary")),
    )(act, act, act, proj, hp, dy)


ANY = pl.BlockSpec(memory_space=pl.ANY)


def _block_index(p):
    return 4 * p[0] + 2 * p[1] + p[2]


def _all_gather(shards, name):
    n = len(shards)

    def body(*refs):
        ins, outs = refs[:n], refs[n:2 * n]
        send_sems, recv_sems, local_sems = refs[2 * n:]
        x, y, c = lax.axis_index("x"), lax.axis_index("y"), lax.axis_index("c")
        me, sibling = (x, y, c), (x, y, 1 - c)
        chips = [(1 - x, y), (x, 1 - y), (1 - x, 1 - y)]

        def copy(i, k, block, to, src=None):
            dst = outs[i].at[_block_index(block)]
            return pltpu.make_async_remote_copy(
                src_ref=dst if src is None else src, dst_ref=dst,
                send_sem=send_sems.at[i, k], recv_sem=recv_sems.at[i, k],
                device_id=to, device_id_type=MESH)

        mine = [pltpu.make_async_copy(ins[i], outs[i].at[_block_index(me)], local_sems.at[i]) for i in range(n)]
        for cp in mine:
            cp.start()
        first = []
        for i in range(n):
            first.append(copy(i, 0, me, sibling, src=ins[i]))
            first += [copy(i, 1 + j, me, (*chip, c), src=ins[i]) for j, chip in enumerate(chips)]
        for cp in first:
            cp.start()
        passed = []
        for j, chip in enumerate(chips):
            for i in range(n):
                copy(i, 1 + j, (*chip, c), me).wait_recv()
                fwd = copy(i, 4 + j, (*chip, c), sibling)
                fwd.start()
                passed.append(fwd)
        for i in range(n):
            copy(i, 0, sibling, me).wait_recv()
            for j, chip in enumerate(chips):
                copy(i, 4 + j, (*chip, 1 - c), me).wait_recv()
        for cp in first + passed:
            cp.wait_send()
        for cp in mine:
            cp.wait()

    return pl.pallas_call(
        body, name=name,
        in_specs=[ANY] * n, out_specs=[ANY] * n,
        out_shape=[jax.ShapeDtypeStruct((N_DEV,) + s.shape, s.dtype) for s in shards],
        scratch_shapes=[pltpu.SemaphoreType.DMA((n, 7)), pltpu.SemaphoreType.DMA((n, 7)),
                        pltpu.SemaphoreType.DMA((n,))],
    )(*shards)


HBM = pl.BlockSpec(memory_space=pltpu.HBM)
SEM = pl.BlockSpec(memory_space=pltpu.SEMAPHORE)
EFFECT = pltpu.SideEffectType.DATAFLOW_SIDE_EFFECTING


def _my_block():
    return _block_index((lax.axis_index("x"), lax.axis_index("y"), lax.axis_index("c")))


def _peer(k):
    x, y, c = lax.axis_index("x"), lax.axis_index("y"), lax.axis_index("c")
    return (1 - x if k & 4 else x, 1 - y if k & 2 else y, 1 - c if k & 1 else c)


ALL_PEERS = tuple(range(1, N_DEV))
SIBLING = 1
SAME_CORE = (2, 4, 6)


def _plan_copies(plan, src_refs, land_refs, send_sems, recv_sems, peers=ALL_PEERS):
    me = _my_block()
    copies = []
    for e, (si, di, src_view, dst_view, _) in enumerate(plan):
        for k in peers:
            copies.append(pltpu.make_async_remote_copy(
                src_ref=src_view(src_refs[si], _block_index(_peer(k))),
                dst_ref=dst_view(land_refs[di], me),
                send_sem=send_sems[e], recv_sem=recv_sems[e],
                device_id=_peer(k), device_id_type=MESH))
    return copies


def _plan_forwards(plan, land_refs, send_sems, recv_sems):
    copies = []
    for e, (_, di, _, dst_view, _) in enumerate(plan):
        for k in SAME_CORE:
            part = dst_view(land_refs[di], _block_index(_peer(k)))
            copies.append(pltpu.make_async_remote_copy(
                src_ref=part, dst_ref=part, send_sem=send_sems[e], recv_sem=recv_sems[e],
                device_id=_peer(SIBLING), device_id_type=MESH))
    return copies


def _plan_waits(plan, land_refs, send_sems, recv_sems, n=N_DEV - 1):
    waits = []
    for e, (_, di, _, _, parts_view) in enumerate(plan):
        view = parts_view(land_refs[di], n)
        waits.append(pltpu.make_async_remote_copy(
            src_ref=view, dst_ref=view, send_sem=send_sems[e], recv_sem=recv_sems[e],
            device_id=_peer(SIBLING), device_id_type=MESH))
    return waits


def _plan_own(plan, src_refs, land_refs, own_sems):
    me = _my_block()
    return [pltpu.make_async_copy(src_view(src_refs[si], me), dst_view(land_refs[di], me), own_sems[e])
            for e, (si, di, src_view, dst_view, _) in enumerate(plan)]


def _copies_start(srcs, lands, plan, name, after=None):
    ns, nl, ne = len(srcs), len(lands), len(plan)
    extra = [] if after is None else [after]

    nin = ns + nl + len(extra)

    def body(*refs):
        src_refs, land_refs = refs[:ns], refs[ns:ns + nl]
        send_sems, recv_sems = refs[nin:nin + ne], refs[nin + ne:nin + 2 * ne]
        own_sems = refs[nin + 2 * ne:nin + 3 * ne]
        token = refs[-1]
        for cp in _plan_copies(plan, src_refs, land_refs, send_sems, recv_sems):
            cp.start()
        for cp in _plan_own(plan, src_refs, land_refs, own_sems):
            cp.start()
        token[...] = jnp.zeros_like(token)

    thru = [pltpu.HBM(a.shape, a.dtype) for a in list(srcs) + list(lands)]
    res = pl.pallas_call(
        body, name=name,
        in_specs=[HBM] * (ns + nl) + [ANY] * len(extra),
        out_specs=[SEM] * (3 * ne) + [HBM] * (ns + nl) + [pl.BlockSpec(memory_space=pltpu.VMEM)],
        out_shape=[pltpu.SemaphoreType.DMA(())] * (3 * ne) + thru + [jax.ShapeDtypeStruct((8, LANE), F32)],
        input_output_aliases={i: 3 * ne + i for i in range(ns + nl)},
        compiler_params=pltpu.CompilerParams(has_side_effects=EFFECT),
    )(*[pltpu.with_memory_space_constraint(a, pltpu.HBM) for a in list(srcs) + list(lands)], *extra)
    return dict(sems=res[:3 * ne], srcs=res[3 * ne:3 * ne + ns], lands=res[3 * ne + ns:3 * ne + ns + nl],
                token=res[-1], plan=plan)


def _copies_wait(flight, after, name):
    srcs, lands, plan = flight["srcs"], flight["lands"], flight["plan"]
    ns, nl, ne = len(srcs), len(lands), len(plan)

    def body(*refs):
        src_refs, land_refs = refs[:ns], refs[ns:ns + nl]
        send_sems, recv_sems = refs[ns + nl:ns + nl + ne], refs[ns + nl + ne:ns + nl + 2 * ne]
        own_sems = refs[ns + nl + 2 * ne:ns + nl + 3 * ne]
        for cp in _plan_waits(plan, land_refs, send_sems, recv_sems):
            cp.wait_send()
            cp.wait_recv()
        for cp in _plan_own(plan, src_refs, land_refs, own_sems):
            cp.wait()

    after = list(after) if isinstance(after, (list, tuple)) else [after]
    thru = [pltpu.HBM(a.shape, a.dtype) for a in list(srcs) + list(lands)]
    res = pl.pallas_call(
        body, name=name,
        in_specs=[HBM] * (ns + nl) + [SEM] * (3 * ne) + [ANY] * len(after),
        out_specs=[HBM] * (ns + nl),
        out_shape=thru,
        input_output_aliases={i: i for i in range(ns + nl)},
        compiler_params=pltpu.CompilerParams(has_side_effects=EFFECT),
    )(*srcs, *lands, *flight["sems"], *after)
    return list(res[ns:])


def _gather2_start(srcs, lands, plan, name, after=None):
    ns, nl, ne = len(srcs), len(lands), len(plan)
    extra = [] if after is None else [after]
    nin = ns + nl + len(extra)

    def body(*refs):
        src_refs, land_refs = refs[:ns], refs[ns:ns + nl]
        send_sems, recv_sems = refs[nin:nin + ne], refs[nin + ne:nin + 2 * ne]
        own_sems = refs[nin + 2 * ne:nin + 3 * ne]
        for cp in _plan_copies(plan, src_refs, land_refs, send_sems, recv_sems, (SIBLING,) + SAME_CORE):
            cp.start()
        for cp in _plan_own(plan, src_refs, land_refs, own_sems):
            cp.start()
        refs[-1][...] = jnp.zeros_like(refs[-1])

    thru = [pltpu.HBM(a.shape, a.dtype) for a in list(srcs) + list(lands)]
    res = pl.pallas_call(
        body, name=name,
        in_specs=[HBM] * (ns + nl) + [ANY] * len(extra),
        out_specs=[SEM] * (3 * ne) + [HBM] * (ns + nl) + [pl.BlockSpec(memory_space=pltpu.VMEM)],
        out_shape=[pltpu.SemaphoreType.DMA(())] * (3 * ne) + thru + [jax.ShapeDtypeStruct((8, LANE), F32)],
        input_output_aliases={i: 3 * ne + i for i in range(ns + nl)},
        compiler_params=pltpu.CompilerParams(has_side_effects=EFFECT),
    )(*[pltpu.with_memory_space_constraint(a, pltpu.HBM) for a in list(srcs) + list(lands)], *extra)
    return dict(send1=res[:ne], recv1=res[ne:2 * ne], own=res[2 * ne:3 * ne], srcs=res[3 * ne:3 * ne + ns],
                lands=res[3 * ne + ns:3 * ne + ns + nl], token=res[-1], plan=plan)


def _gather2_forward(flight, after, name):
    srcs, lands, plan = flight["srcs"], flight["lands"], flight["plan"]
    ns, nl, ne = len(srcs), len(lands), len(plan)
    nin = ns + nl + ne + 1

    def body(*refs):
        land_refs = refs[ns:ns + nl]
        recv1 = refs[ns + nl:ns + nl + ne]
        send2, recv2 = refs[nin:nin + ne], refs[nin + ne:nin + 2 * ne]
        for cp in _plan_waits(plan, land_refs, send2, recv1, n=1 + len(SAME_CORE)):
            cp.wait_recv()
        for cp in _plan_forwards(plan, land_refs, send2, recv2):
            cp.start()
        refs[-1][...] = jnp.zeros_like(refs[-1])

    thru = [pltpu.HBM(a.shape, a.dtype) for a in list(srcs) + list(lands)]
    res = pl.pallas_call(
        body, name=name,
        in_specs=[HBM] * (ns + nl) + [SEM] * ne + [ANY],
        out_specs=[SEM] * (2 * ne) + [HBM] * (ns + nl) + [pl.BlockSpec(memory_space=pltpu.VMEM)],
        out_shape=[pltpu.SemaphoreType.DMA(())] * (2 * ne) + thru + [jax.ShapeDtypeStruct((8, LANE), F32)],
        input_output_aliases={i: 2 * ne + i for i in range(ns + nl)},
        compiler_params=pltpu.CompilerParams(has_side_effects=EFFECT),
    )(*srcs, *lands, *flight["recv1"], after)
    return dict(flight, send2=res[:ne], recv2=res[ne:2 * ne], srcs=res[2 * ne:2 * ne + ns],
                lands=res[2 * ne + ns:2 * ne + ns + nl], token=res[-1])


def _gather2_wait(flight, after, name):
    srcs, lands, plan = flight["srcs"], flight["lands"], flight["plan"]
    ns, nl, ne = len(srcs), len(lands), len(plan)

    def body(*refs):
        src_refs, land_refs = refs[:ns], refs[ns:ns + nl]
        sems = refs[ns + nl:ns + nl + 4 * ne]
        send1, own, send2, recv2 = sems[:ne], sems[ne:2 * ne], sems[2 * ne:3 * ne], sems[3 * ne:]
        for cp in _plan_waits(plan, land_refs, send1, recv2, n=1 + len(SAME_CORE)):
            cp.wait_send()
        for cp in _plan_waits(plan, land_refs, send2, recv2, n=len(SAME_CORE)):
            cp.wait_send()
            cp.wait_recv()
        for cp in _plan_own(plan, src_refs, land_refs, own):
            cp.wait()

    thru = [pltpu.HBM(a.shape, a.dtype) for a in list(srcs) + list(lands)]
    res = pl.pallas_call(
        body, name=name,
        in_specs=[HBM] * (ns + nl) + [SEM] * (4 * ne) + [ANY],
        out_specs=[HBM] * (ns + nl),
        out_shape=thru,
        input_output_aliases={i: i for i in range(ns + nl)},
        compiler_params=pltpu.CompilerParams(has_side_effects=EFFECT),
    )(*srcs, *lands, *flight["send1"], *flight["own"], *flight["send2"], *flight["recv2"], after)
    return list(res[ns:])


def _adamw_math(w, g, m, v):
    m = ADAM_B1 * m + (1.0 - ADAM_B1) * g
    v = ADAM_B2 * v + (1.0 - ADAM_B2) * (g * g)
    m_hat = m / (1.0 - ADAM_B1 ** ADAM_STEP)
    v_hat = v / (1.0 - ADAM_B2 ** ADAM_STEP)
    delta = -ADAM_LR * (m_hat / (jnp.sqrt(v_hat) + ADAM_EPS) + ADAM_WD * w)
    return delta, m, v


def _adamw(parts, w, m, v, name, rows, lane_offset=None):
    depth, r, c = w.shape
    cp = parts[0].shape[2]
    assert r % rows == 0 and len(parts) == depth

    def body(*refs):
        p_refs = refs[:depth]
        w_ref, m_ref, v_ref, g_ref, d_ref, mo_ref, vo_ref = refs[depth:]
        if lane_offset is not None:
            cw = -(-c // LANE) * LANE
            src = lax.broadcasted_iota(jnp.int32, (cp, cw), 0)
            dst = lax.broadcasted_iota(jnp.int32, (cp, cw), 1)
            pick = (src == dst + lane_offset()).astype(BF16)
            pick3 = jnp.concatenate([pick] * 3, axis=0)
        for li in range(depth):
            @pl.when(pl.program_id(0) == li)
            def _(li=li):
                g = p_refs[li][0].astype(F32)
                for j in range(1, N_DEV):
                    g = g + p_refs[li][j].astype(F32)
                if lane_offset is not None:
                    g = _dot(_split3_cols(g), pick3)
                g = g[:, :c]
                d, mn, vn = _adamw_math(w_ref[...], g, m_ref[...], v_ref[...])
                g_ref[...] = g
                d_ref[...] = d
                mo_ref[...] = mn
                vo_ref[...] = vn

    def part_spec(li):
        return pl.BlockSpec((N_DEV, rows, cp), lambda l, i: (0, jnp.where(l == li, i, 0), 0))

    blk = pl.BlockSpec((None, rows, c), lambda l, i: (l, i, 0))
    out = jax.ShapeDtypeStruct((depth, r, c), F32)
    return pl.pallas_call(
        body, name=name, grid=(depth, r // rows),
        in_specs=[part_spec(li) for li in range(depth)] + [blk, blk, blk],
        out_specs=[blk] * 4, out_shape=[out] * 4,
        compiler_params=_params(("arbitrary", "arbitrary")),
    )(*parts, w, m, v)


def _sum_parts(parts, name):
    _, r, c = parts.shape

    def body(p_ref, o_ref):
        g = p_ref[0]
        for j in range(1, N_DEV):
            g = g + p_ref[j]
        o_ref[...] = g

    return pl.pallas_call(
        body, name=name, out_shape=jax.ShapeDtypeStruct((r, c), F32),
        compiler_params=_params(),
    )(parts)


def _adamw_small(parts, ws, ms, vs, name):
    n = len(ws)

    def body(*refs):
        ins, outs = refs[:4 * n], refs[4 * n:]
        for i in range(n):
            p_ref, w_ref, m_ref, v_ref = ins[i], ins[n + i], ins[2 * n + i], ins[3 * n + i]
            g = p_ref[0]
            for j in range(1, p_ref.shape[0]):
                g = g + p_ref[j]
            d, mn, vn = _adamw_math(w_ref[...], g, m_ref[...], v_ref[...])
            outs[i][...] = g
            outs[n + i][...] = d
            outs[2 * n + i][...] = mn
            outs[3 * n + i][...] = vn

    out = [jax.ShapeDtypeStruct(a.shape, F32) for a in ws] * 4
    res = pl.pallas_call(body, name=name, out_shape=out, compiler_params=_params())(*parts, *ws, *ms, *vs)
    return res[:n], res[n:2 * n], res[2 * n:3 * n], res[3 * n:]


SMALL = ("norm_mix", "q_gain", "k_gain", "conv_b", "dt_bias", "a_log", "d_skip", "attn_out_gain",
         "ssm_out_gain", "norm_ffn")


def _full_cols(gathered):
    _, r, c = gathered.shape
    return gathered.transpose(1, 0, 2).reshape(r, N_DEV * c)


FF_BLK = 768
FF_PAD = N_DEV * FF_BLK
W_IN_COLS = IN_DIM // N_DEV
W_IN_WINDOW = 896


def _w_in_window_start(block):
    return (block * W_IN_COLS // LANE) * LANE


def _w_in_window(ref, block):
    return ref.at[:, pl.ds(pl.multiple_of(_w_in_window_start(block), LANE), W_IN_WINDOW)]


def _whole(ref, block):
    return ref


def _rows_of(size):
    return lambda ref, block: ref.at[pl.ds(pl.multiple_of(block * size, size), size), :]


def _slot(ref, block):
    return ref.at[block]


def _n_slots(ref, n):
    return ref.at[pl.ds(0, n)]


def _n_rows(size):
    return lambda ref, n: ref.at[pl.ds(0, n * size), :]


GATHER_A = [(0, 0, _whole, _slot, _n_slots), (1, 1, _whole, _rows_of(256), _n_rows(256))]
GATHER_B = [(i, i, _whole, _rows_of(FF_BLK), _n_rows(FF_BLK)) for i in range(3)]
SCATTER_A = [(0, 0, _w_in_window, _slot, _n_slots), (1, 1, _rows_of(256), _slot, _n_slots)]
SCATTER_B = [(i, i, _rows_of(FF_BLK), _slot, _n_slots) for i in range(3)]


def _gather_lands(which, shards, d):
    if which == "a":
        return [lax.empty((N_DEV,) + shards[0].shape, BF16), lax.empty((d, d), BF16)]
    return [lax.empty((FF_PAD, d), BF16) for _ in range(3)]


def _scatter_lands(which, grads):
    if which == "a":
        g_in, g_out = grads
        return [lax.empty((N_DEV, g_in.shape[0], W_IN_WINDOW), BF16),
                lax.empty((N_DEV, g_out.shape[0] // N_DEV, g_out.shape[1]), BF16)]
    return [lax.empty((N_DEV, FF_BLK, g.shape[1]), BF16) for g in grads]


def _pad_w_in(full):
    return jnp.pad(full, ((0, 0), (0, NPROJ - IN_DIM)))


def kernel(x, norm_mix, w_in, q_gain, k_gain, conv_w, conv_b, dt_bias, a_log, d_skip, attn_out_gain, ssm_out_gain, w_out, norm_ffn, w_gate, w_up, w_down, loss_target, m_norm_mix, m_w_in, m_q_gain, m_k_gain, m_conv_w, m_conv_b, m_dt_bias, m_a_log, m_d_skip, m_attn_out_gain, m_ssm_out_gain, m_w_out, m_norm_ffn, m_w_gate, m_w_up, m_w_down, v_norm_mix, v_w_in, v_q_gain, v_k_gain, v_conv_w, v_conv_b, v_dt_bias, v_a_log, v_d_skip, v_attn_out_gain, v_ssm_out_gain, v_w_out, v_norm_ffn, v_w_gate, v_w_up, v_w_down):
    nb, seq, d = x.shape
    depth = w_in.shape[0]
    t = nb * seq
    w = dict(norm_mix=norm_mix, w_in=w_in, q_gain=q_gain, k_gain=k_gain, conv_w=conv_w, conv_b=conv_b,
             dt_bias=dt_bias, a_log=a_log, d_skip=d_skip, attn_out_gain=attn_out_gain, ssm_out_gain=ssm_out_gain,
             w_out=w_out, norm_ffn=norm_ffn, w_gate=w_gate, w_up=w_up, w_down=w_down)
    mom = dict(norm_mix=m_norm_mix, w_in=m_w_in, q_gain=m_q_gain, k_gain=m_k_gain, conv_w=m_conv_w, conv_b=m_conv_b,
               dt_bias=m_dt_bias, a_log=m_a_log, d_skip=m_d_skip, attn_out_gain=m_attn_out_gain,
               ssm_out_gain=m_ssm_out_gain, w_out=m_w_out, norm_ffn=m_norm_ffn, w_gate=m_w_gate, w_up=m_w_up,
               w_down=m_w_down)
    var = dict(norm_mix=v_norm_mix, w_in=v_w_in, q_gain=v_q_gain, k_gain=v_k_gain, conv_w=v_conv_w, conv_b=v_conv_b,
               dt_bias=v_dt_bias, a_log=v_a_log, d_skip=v_d_skip, attn_out_gain=v_attn_out_gain,
               ssm_out_gain=v_ssm_out_gain, w_out=v_w_out, norm_ffn=v_norm_ffn, w_gate=v_w_gate, w_up=v_w_up,
               w_down=v_w_down)
    ff = w_gate.shape[2]

    (conv_all,) = _all_gather([conv_w], "gather_conv")

    def shards_a(li):
        return [w_in[li].astype(BF16), w_out[li].astype(BF16)]

    w["w_gate"], mom["w_gate"], var["w_gate"] = (jnp.swapaxes(a, 1, 2) for a in (w_gate, m_w_gate, v_w_gate))
    w["w_up"], mom["w_up"], var["w_up"] = (jnp.swapaxes(a, 1, 2) for a in (w_up, m_w_up, v_w_up))

    def shards_b(li):
        return [jnp.pad(w[k][li].astype(BF16), ((0, FF_BLK - ff), (0, 0))) for k in ("w_gate", "w_up", "w_down")]

    def small_params(li):
        p = {k: w[k][li][None, :] for k in ("norm_mix", "q_gain", "k_gain", "attn_out_gain", "ssm_out_gain", "norm_ffn")}
        conv_full = conv_all[:, li].transpose(1, 0, 2).reshape(CONV_K, CONV_DIM)
        p["cw8"] = _pack_conv(conv_full, conv_b[li])
        p["hp"] = _pack_heads(dt_bias[li], a_log[li], d_skip[li])
        return p

    xc = x.reshape(t, d)
    cur = shards_a(0)
    flight = _gather2_start(cur, _gather_lands("a", cur, d), GATHER_A, "gather_a0")
    h1_first = _rms_fwd(xc, norm_mix[0][None, :], "l0_rms1", after=flight["token"])
    flight = _gather2_forward(flight, h1_first, "gather_a0_fwd")
    lands_a = _gather2_wait(flight, flight["token"], "gather_a0_wait")
    layers, saved = [], []
    for li in range(depth):
        tag = f"l{li}_"
        p = small_params(li)
        p["w_in"] = _pad_w_in(_full_cols(lands_a[0]))
        p["w_out"] = lands_a[1]
        cur = shards_b(li)
        flight = _gather2_start(cur, _gather_lands("b", cur, d), GATHER_B, tag + "gather_b", after=lands_a[1])
        if li == 0:
            h1 = h1_first
            proj = _matmul(h1, p["w_in"], "nn", F32, tag + "mm_in", after=flight["token"])
        else:
            h1 = _rms_fwd(xc, p["norm_mix"], tag + "rms1", after=flight["token"])
            proj = _matmul(h1, p["w_in"], "nn", F32, tag + "mm_in")
        o = _attn_fwd(proj, p["q_gain"], p["k_gain"], nb, seq, tag + "attn")
        flight = _gather2_forward(flight, o, tag + "gather_b_fwd")
        act = _conv_fwd(proj, p["cw8"], nb, seq, tag + "conv", after=flight["token"])
        y = _ssd_fwd(act, proj, p["hp"], nb, seq, tag + "ssd")
        cat = _mix_fwd(o, y, proj, p["attn_out_gain"], p["ssm_out_gain"], tag + "mix")
        x1 = _matmul(cat, p["w_out"], "nn", F32, tag + "mm_out", residual=xc)
        p["w_gate"], p["w_up"], p["w_down"] = _gather2_wait(flight, x1, tag + "gather_b_wait")
        token = None
        if li + 1 < depth:
            nxt = shards_a(li + 1)
            flight = _gather2_start(nxt, _gather_lands("a", nxt, d), GATHER_A, f"gather_a{li + 1}",
                                    after=p["w_down"])
            token = flight["token"]
        h2 = _rms_fwd(x1, p["norm_ffn"], tag + "rms2", after=token)
        a, gate, up = _mm_swiglu(h2, p["w_gate"], p["w_up"], tag + "mm_gu")
        token = None
        if li + 1 < depth:
            flight = _gather2_forward(flight, gate, f"gather_a{li + 1}_fwd")
            token = flight["token"]
        x2 = _matmul(a, p["w_down"], "nn", F32, tag + "mm_down", residual=x1, after=token)
        if li + 1 < depth:
            lands_a = _gather2_wait(flight, x2, f"gather_a{li + 1}_wait")
        saved.append(dict(x=xc, h1=h1, proj=proj, o=o, act=act, y=y, cat=cat, x1=x1, h2=h2, gate=gate, up=up, a=a))
        layers.append(p)
        xc = x2

    loss_blk, dx, dxb = _loss_fwd_bwd(xc, loss_target.reshape(t, d), "loss")
    loss = lax.psum(loss_blk[0, 0], ("x", "y", "c"))

    grads = [dict() for _ in range(depth)]
    recv = [dict() for _ in range(depth)]
    flight_a, token = None, None
    for li in reversed(range(depth)):
        tag = f"l{li}_b_"
        p, s, g = layers[li], saved[li], grads[li]
        dgate, dup = _mm_dact_swiglu(dxb, p["w_down"], s["gate"], s["up"], tag + "mm_dact", after=token)
        g_down = _matmul(s["a"], dxb, "tn", BF16, tag + "mm_dwd")
        dh2 = _matmul(dgate, p["w_gate"], "nn", F32, tag + "mm_dh2g")
        dh2 = _matmul(dup, p["w_up"], "nn", F32, tag + "mm_dh2u", residual=dh2)
        g_gate = _matmul(dgate, s["h2"], "tn", BF16, tag + "mm_dwg")
        g_up = _matmul(dup, s["h2"], "tn", BF16, tag + "mm_dwu")
        if flight_a is not None:
            recv[li + 1]["w_in"], recv[li + 1]["w_out"] = _copies_wait(flight_a, g_up, f"l{li + 1}_b_scatter_a_wait")
        grads_b = [g_gate, g_up, g_down]
        flight_b = _copies_start(grads_b, _scatter_lands("b", grads_b), SCATTER_B,
                                 tag + "scatter_b", after=recv[li + 1]["w_out"] if li + 1 < depth else None)
        dx1, dx1b, g["norm_ffn"] = _rms_bwd(s["x1"], p["norm_ffn"], dh2, dx, tag + "rms2")
        dcat = _matmul(dx1b, p["w_out"], "nt", F32, tag + "mm_dcat", after=flight_b["token"])
        g_out = _matmul(s["cat"], dx1b, "tn", BF16, tag + "mm_dwo")
        do, dy, dz, g["attn_out_gain"], g["ssm_out_gain"] = _mix_bwd(
            dcat, s["o"], s["y"], s["proj"], p["attn_out_gain"], p["ssm_out_gain"], tag + "mix")
        dq, dk, dv, g["q_gain"], g["k_gain"] = _attn_bwd(s["proj"], do, p["q_gain"], p["k_gain"], nb, seq, tag + "attn")
        dxa, dba, dca, ddt, dhp = _ssd_bwd(s["act"], s["proj"], dy, p["hp"], nb, seq, tag + "ssd")
        dxbc, dcw8 = _conv_bwd(s["proj"], jnp.concatenate([dxa, dba, dca], axis=1), p["cw8"], nb, seq, tag + "conv")
        g["conv_w"] = dcw8[0:CONV_K]
        g["conv_b"] = dcw8[CONV_K:CONV_K + 1]
        heads = dhp[:, 0:3, 0:SSM_HG].transpose(1, 0, 2).reshape(3, SSM_HEADS)
        g["dt_bias"], g["a_log"], g["d_skip"] = heads[0:1], heads[1:2], heads[2:3]
        ddt = ddt[:, :LANE] + jnp.roll(ddt[:, LANE:], SSM_HG, axis=1)
        tail = jnp.zeros((t, NPROJ - OFF_DT - LANE), BF16)
        dproj = jnp.concatenate([dq, dk, dv, dz, dxbc, ddt, tail], axis=1)
        recv[li]["w_gate"], recv[li]["w_up"], recv[li]["w_down"] = _copies_wait(flight_b, dproj, tag + "scatter_b_wait")
        dh1 = _matmul(dproj, p["w_in"], "nt", F32, tag + "mm_dh1")
        g_in = _matmul(s["h1"], dproj, "tn", BF16, tag + "mm_dwin")
        flight_a = _copies_start([g_in, g_out], _scatter_lands("a", [g_in, g_out]), SCATTER_A, tag + "scatter_a")
        token = flight_a["token"]
        dx, dxb, g["norm_mix"] = _rms_bwd(s["x"], p["norm_mix"], dh1, dx1, tag + "rms1")
    grad_x = dx.reshape(nb, seq, d)

    out_g, out_d, out_m, out_v = {}, {}, {}, {}

    def update(k, rows, lane_offset=None):
        parts = [recv[li][k] for li in range(depth)]
        out_g[k], out_d[k], out_m[k], out_v[k] = _adamw(parts, w[k], mom[k], var[k], "adamw_" + k, rows, lane_offset)

    def w_in_offset():
        return _my_block() * W_IN_COLS - _w_in_window_start(_my_block())

    update("w_gate", 64)
    update("w_up", 64)
    update("w_down", 64)
    recv[0]["w_in"], recv[0]["w_out"] = _copies_wait(
        flight_a, [out_g["w_gate"], out_g["w_up"], out_g["w_down"], dx], "l0_b_scatter_a_wait")
    update("w_in", 128, w_in_offset)
    update("w_out", 128)
    for res in (out_g, out_d, out_m, out_v):
        res["w_gate"], res["w_up"] = jnp.swapaxes(res["w_gate"], 1, 2), jnp.swapaxes(res["w_up"], 1, 2)

    small_g = [jnp.concatenate([grads[li][k] for li in range(depth)], axis=0) for k in SMALL]
    conv_g = jnp.stack([grads[li]["conv_w"] for li in range(depth)]).reshape(depth, CONV_K * CONV_DIM)
    parts = _all_gather(small_g + [conv_g], "gather_small_grads")
    res = _adamw_small(parts[:-1], [w[k] for k in SMALL], [mom[k] for k in SMALL], [var[k] for k in SMALL],
                       "adamw_small")
    for dst, vals in zip((out_g, out_d, out_m, out_v), res):
        dst.update(dict(zip(SMALL, vals)))
    conv_total = _sum_parts(parts[-1], "sum_conv_grads").reshape(depth, CONV_K, CONV_DIM)
    cshard = conv_w.shape[2]
    conv_mine = lax.dynamic_slice_in_dim(conv_total, _my_block() * cshard, cshard, axis=2)
    flat = lambda a: a.reshape(depth, CONV_K * cshard)
    res = _adamw_small([flat(conv_mine)[None]], [flat(conv_w)], [flat(m_conv_w)], [flat(v_conv_w)], "adamw_conv")
    for dst, vals in zip((out_g, out_d, out_m, out_v), res):
        dst["conv_w"] = vals[0].reshape(depth, CONV_K, cshard)

    names = ("norm_mix", "w_in", "q_gain", "k_gain", "conv_w", "conv_b", "dt_bias", "a_log", "d_skip",
             "attn_out_gain", "ssm_out_gain", "w_out", "norm_ffn", "w_gate", "w_up", "w_down")
    return (loss, grad_x, *[out_g[k] for k in names], *[out_d[k] for k in names],
            *[out_m[k] for k in names], *[out_v[k] for k in names])
```

```python
import functools
import math

import jax
import jax.numpy as jnp
from jax import lax
from jax.experimental import pallas as pl
from jax.experimental.pallas import tpu as pltpu

F32 = jnp.float32
BF16 = jnp.bfloat16
MESH = pl.DeviceIdType.MESH

N_DEV = 8
EPS = 1e-6
ATT_HEADS = 8
ATT_DH = 128
ATT_W = ATT_HEADS * ATT_DH
SSM_W = 1024
SSM_P = 64
SSM_N = 128
SSM_GROUPS = 2
SSM_HG = 8
SSM_HEADS = SSM_GROUPS * SSM_HG
CHUNK = 128
CONV_K = 4
CONV_DIM = SSM_W + 2 * SSM_GROUPS * SSM_N
LANE = 128
OFF_Q, OFF_K, OFF_V, OFF_Z, OFF_XS = 0, ATT_W, 2 * ATT_W, 3 * ATT_W, 4 * ATT_W
OFF_B = OFF_XS + SSM_W
OFF_C = OFF_B + SSM_GROUPS * SSM_N
OFF_DT = OFF_C + SSM_GROUPS * SSM_N
NPROJ = 6144
IN_DIM = OFF_DT + SSM_HEADS

ADAM_LR = 0.001
ADAM_B1 = 0.9
ADAM_B2 = 0.999
ADAM_EPS = 1e-08
ADAM_WD = 0.01
ADAM_STEP = 10

VMEM_LIMIT = 56 * 1024 * 1024
MATMUL_OPERAND_BYTES = 26 * 1024 * 1024


def _params(sem=None):
    return pltpu.CompilerParams(dimension_semantics=sem, vmem_limit_bytes=VMEM_LIMIT)


def _pick(dim, target):
    if dim <= target:
        return dim
    best = None
    for t in range(LANE, target + 1, LANE):
        if dim % t == 0:
            best = t
    assert best is not None, (dim, target)
    return best


def _dot(a, b, dims=((1,), (0,))):
    return lax.dot_general(a, b, (dims, ((), ())), preferred_element_type=F32)


def _dot_nt(a, b):
    return _dot(a, b, ((1,), (1,)))


def _dot_tn(a, b):
    return _dot(a, b, ((0,), (0,)))


def _sigmoid(x):
    return 1.0 / (1.0 + jnp.exp(-x))


def _softplus(x):
    return jnp.maximum(x, 0.0) + jnp.log(1.0 + jnp.exp(-jnp.abs(x)))


def _rstd(x):
    return lax.rsqrt(jnp.mean(x * x, axis=-1, keepdims=True) + EPS)


def _matmul(a, b, mode, out_dtype, name, residual=None, after=None, tm=512, tn=1024, tk=2048):
    if mode == "nn":
        (m, k), (k2, n) = a.shape, b.shape
    elif mode == "nt":
        (m, k), (n, k2) = a.shape, b.shape
    else:
        (k, m), (k2, n) = a.shape, b.shape
    assert k == k2, (a.shape, b.shape, mode)
    tm, tn, tk = _pick(m, tm), _pick(n, tn), _pick(k, tk)
    for cand_tn in (tn, _pick(n, tn // 2)):
        if 2 * 2 * (tm * k + k * cand_tn) <= MATMUL_OPERAND_BYTES:
            tn, tk = cand_tn, k
            break
    nk = k // tk
    dims = {"nn": ((1,), (0,)), "nt": ((1,), (1,)), "tn": ((0,), (0,))}[mode]
    has_res = residual is not None

    has_tok = after is not None

    def body(*refs):
        a_ref, b_ref = refs[:2]
        r_ref = refs[2] if has_res else None
        o_ref = refs[2 + has_res + has_tok]
        prod = _dot(a_ref[...], b_ref[...], dims)

        def finish(r):
            if r_ref is not None:
                r = r + r_ref[...]
            o_ref[...] = r.astype(o_ref.dtype)

        if nk == 1:
            finish(prod)
        else:
            acc = refs[-1]
            kk = pl.program_id(2)

            @pl.when(kk == 0)
            def _():
                acc[...] = prod

            @pl.when(kk > 0)
            def _():
                acc[...] += prod

            @pl.when(kk == nk - 1)
            def _():
                finish(acc[...])

    if mode == "tn":
        a_spec = pl.BlockSpec((tk, tm), lambda i, j, kk: (kk, i))
    else:
        a_spec = pl.BlockSpec((tm, tk), lambda i, j, kk: (i, kk))
    if mode == "nt":
        b_spec = pl.BlockSpec((tn, tk), lambda i, j, kk: (j, kk))
    else:
        b_spec = pl.BlockSpec((tk, tn), lambda i, j, kk: (kk, j))
    o_spec = pl.BlockSpec((tm, tn), lambda i, j, kk: (i, j))
    tok_spec = pl.BlockSpec((8, LANE), lambda i, j, kk: (0, 0))
    in_specs = [a_spec, b_spec] + ([o_spec] if has_res else []) + ([tok_spec] if has_tok else [])
    args = (a, b) + ((residual,) if has_res else ()) + ((after,) if has_tok else ())
    return pl.pallas_call(
        body,
        name=name,
        grid=(m // tm, n // tn, nk),
        in_specs=in_specs,
        out_specs=o_spec,
        out_shape=jax.ShapeDtypeStruct((m, n), out_dtype),
        scratch_shapes=[pltpu.VMEM((tm, tn), F32)] if nk > 1 else [],
        compiler_params=_params(("parallel", "parallel", "arbitrary")),
    )(*args)


def _mm_swiglu(h, wg_t, wu_t, name, tm=512, tn=1024):
    m, k = h.shape
    n = wg_t.shape[0]
    tm, tn = _pick(m, tm), _pick(n, tn)

    def body(h_ref, g_ref, u_ref, a_ref, gs_ref, us_ref):
        hv = h_ref[...]
        g = _dot_nt(hv, g_ref[...])
        u = _dot_nt(hv, u_ref[...])
        a_ref[...] = (g * _sigmoid(g) * u).astype(BF16)
        gs_ref[...] = g.astype(BF16)
        us_ref[...] = u.astype(BF16)

    w_spec = pl.BlockSpec((tn, k), lambda i, j: (j, 0))
    o_spec = pl.BlockSpec((tm, tn), lambda i, j: (i, j))
    out = jax.ShapeDtypeStruct((m, n), BF16)
    return pl.pallas_call(
        body, name=name, grid=(m // tm, n // tn),
        in_specs=[pl.BlockSpec((tm, k), lambda i, j: (i, 0)), w_spec, w_spec],
        out_specs=[o_spec] * 3, out_shape=[out] * 3,
        compiler_params=_params(("parallel", "parallel")),
    )(h, wg_t, wu_t)


def _mm_dact_swiglu(dx, wd, gs, us, name, after=None, tm=512, tn=1024):
    m, k = dx.shape
    n = wd.shape[0]
    tm, tn = _pick(m, tm), _pick(n, tn)
    has_tok = after is not None

    def body(*refs):
        dx_ref, wd_ref, g_ref, u_ref = refs[:4]
        dg_ref, du_ref = refs[-2:]
        dact = _dot_nt(dx_ref[...], wd_ref[...])
        g = g_ref[...].astype(F32)
        sg = _sigmoid(g)
        dg_ref[...] = (dact * u_ref[...].astype(F32) * sg * (1.0 + g * (1.0 - sg))).astype(BF16)
        du_ref[...] = (dact * g * sg).astype(BF16)

    o_spec = pl.BlockSpec((tm, tn), lambda i, j: (i, j))
    tok = [pl.BlockSpec((8, LANE), lambda i, j: (0, 0))] if has_tok else []
    out = jax.ShapeDtypeStruct((m, n), BF16)
    return pl.pallas_call(
        body, name=name, grid=(m // tm, n // tn),
        in_specs=[pl.BlockSpec((tm, k), lambda i, j: (i, 0)), pl.BlockSpec((tn, k), lambda i, j: (j, 0)),
                  o_spec, o_spec] + tok,
        out_specs=[o_spec] * 2, out_shape=[out] * 2,
        compiler_params=_params(("parallel", "parallel")),
    )(dx, wd, gs, us, *((after,) if has_tok else ()))


ROWS = 512


def _rms_fwd(x, g, name, after=None):
    t, d = x.shape
    has_tok = after is not None

    def body(*refs):
        x_ref, g_ref, o_ref = refs[0], refs[1], refs[-1]
        xv = x_ref[...]
        o_ref[...] = (xv * _rstd(xv) * g_ref[...]).astype(BF16)

    row = pl.BlockSpec((ROWS, d), lambda i: (i, 0))
    tok = [pl.BlockSpec((8, LANE), lambda i: (0, 0))] if has_tok else []
    return pl.pallas_call(
        body, name=name, grid=(t // ROWS,),
        in_specs=[row, pl.BlockSpec((1, d), lambda i: (0, 0))] + tok,
        out_specs=row, out_shape=jax.ShapeDtypeStruct((t, d), BF16),
        compiler_params=_params(("parallel",)),
    )(x, g, *((after,) if has_tok else ()))


def _rms_bwd(x, g, dh, dres, name):
    t, d = x.shape

    def body(x_ref, g_ref, dh_ref, dr_ref, dx_ref, dxb_ref, dg_ref):
        xv = x_ref[...]
        r = _rstd(xv)
        xh = xv * r
        dhv = dh_ref[...]

        @pl.when(pl.program_id(0) == 0)
        def _():
            dg_ref[...] = jnp.zeros_like(dg_ref)

        dg_ref[...] += jnp.sum(dhv * xh, axis=0, keepdims=True)
        dxh = dhv * g_ref[...]
        dx = r * (dxh - xh * jnp.mean(dxh * xh, axis=-1, keepdims=True)) + dr_ref[...]
        dx_ref[...] = dx
        dxb_ref[...] = dx.astype(BF16)

    row = pl.BlockSpec((ROWS // 2, d), lambda i: (i, 0))
    vec = pl.BlockSpec((1, d), lambda i: (0, 0))
    return pl.pallas_call(
        body, name=name, grid=(t // (ROWS // 2),),
        in_specs=[row, vec, row, row],
        out_specs=[row, row, vec],
        out_shape=[jax.ShapeDtypeStruct((t, d), F32), jax.ShapeDtypeStruct((t, d), BF16),
                   jax.ShapeDtypeStruct((1, d), F32)],
        compiler_params=_params(("arbitrary",)),
    )(x, g, dh, dres)


def _loss_fwd_bwd(y, target, name):
    t, d = y.shape
    inv = 1.0 / d

    def body(y_ref, t_ref, l_ref, dy_ref, dyb_ref):
        e = y_ref[...] - t_ref[...]

        @pl.when(pl.program_id(0) == 0)
        def _():
            l_ref[...] = jnp.zeros_like(l_ref)

        l_ref[...] += 0.5 * inv * jnp.sum(e * e)
        dy = e * inv
        dy_ref[...] = dy
        dyb_ref[...] = dy.astype(BF16)

    row = pl.BlockSpec((ROWS, d), lambda i: (i, 0))
    return pl.pallas_call(
        body, name=name, grid=(t // ROWS,),
        in_specs=[row, row],
        out_specs=[pl.BlockSpec((8, LANE), lambda i: (0, 0)), row, row],
        out_shape=[jax.ShapeDtypeStruct((8, LANE), F32), jax.ShapeDtypeStruct((t, d), F32),
                   jax.ShapeDtypeStruct((t, d), BF16)],
        compiler_params=_params(("arbitrary",)),
    )(y, target)


MIX_ROWS = 256


def _mix_fwd(o, y, proj, ga, gs, name):
    t = o.shape[0]
    gw = SSM_W // SSM_GROUPS

    def body(o_ref, y_ref, z_ref, ga_ref, gs_ref, c_ref):
        ov = o_ref[...]
        c_ref[:, 0:ATT_W] = (ov * _rstd(ov) * ga_ref[...]).astype(BF16)
        zv = z_ref[...]
        yz = y_ref[...] * (zv * _sigmoid(zv))
        for gi in range(SSM_GROUPS):
            seg = yz[:, gi * gw:(gi + 1) * gw]
            c_ref[:, ATT_W + gi * gw:ATT_W + (gi + 1) * gw] = (
                seg * _rstd(seg) * gs_ref[:, gi * gw:(gi + 1) * gw]).astype(BF16)

    half = pl.BlockSpec((MIX_ROWS, ATT_W), lambda i: (i, 0))
    vec = pl.BlockSpec((1, ATT_W), lambda i: (0, 0))
    return pl.pallas_call(
        body, name=name, grid=(t // MIX_ROWS,),
        in_specs=[half, half, pl.BlockSpec((MIX_ROWS, ATT_W), lambda i: (i, OFF_Z // ATT_W)), vec, vec],
        out_specs=pl.BlockSpec((MIX_ROWS, 2 * ATT_W), lambda i: (i, 0)),
        out_shape=jax.ShapeDtypeStruct((t, 2 * ATT_W), BF16),
        compiler_params=_params(("parallel",)),
    )(o, y, proj, ga, gs)


def _mix_bwd(dcat, o, y, proj, ga, gs, name):
    t = o.shape[0]
    gw = SSM_W // SSM_GROUPS

    def body(dc_ref, o_ref, y_ref, z_ref, ga_ref, gs_ref, do_ref, dy_ref, dz_ref, dga_ref, dgs_ref):
        @pl.when(pl.program_id(0) == 0)
        def _():
            dga_ref[...] = jnp.zeros_like(dga_ref)
            dgs_ref[...] = jnp.zeros_like(dgs_ref)

        ov = o_ref[...]
        r = _rstd(ov)
        oh = ov * r
        d_on = dc_ref[:, 0:ATT_W]
        dga_ref[...] += jnp.sum(d_on * oh, axis=0, keepdims=True)
        doh = d_on * ga_ref[...]
        do_ref[...] = r * (doh - oh * jnp.mean(doh * oh, axis=-1, keepdims=True))

        zv = z_ref[...]
        yv = y_ref[...]
        sz = _sigmoid(zv)
        silu = zv * sz
        yz = yv * silu
        for gi in range(SSM_GROUPS):
            sl = slice(gi * gw, (gi + 1) * gw)
            seg = yz[:, sl]
            rg = _rstd(seg)
            yh = seg * rg
            dyn = dc_ref[:, ATT_W + gi * gw:ATT_W + (gi + 1) * gw]
            dgs_ref[:, sl] += jnp.sum(dyn * yh, axis=0, keepdims=True)
            dyh = dyn * gs_ref[:, sl]
            dyz = rg * (dyh - yh * jnp.mean(dyh * yh, axis=-1, keepdims=True))
            dy_ref[:, sl] = dyz * silu[:, sl]
            dz_ref[:, sl] = (dyz * yv[:, sl] * (sz[:, sl] * (1.0 + zv[:, sl] * (1.0 - sz[:, sl])))).astype(BF16)

    half = pl.BlockSpec((MIX_ROWS, ATT_W), lambda i: (i, 0))
    vec = pl.BlockSpec((1, ATT_W), lambda i: (0, 0))
    return pl.pallas_call(
        body, name=name, grid=(t // MIX_ROWS,),
        in_specs=[pl.BlockSpec((MIX_ROWS, 2 * ATT_W), lambda i: (i, 0)), half, half,
                  pl.BlockSpec((MIX_ROWS, ATT_W), lambda i: (i, OFF_Z // ATT_W)), vec, vec],
        out_specs=[half, half, half, vec, vec],
        out_shape=[jax.ShapeDtypeStruct((t, ATT_W), F32), jax.ShapeDtypeStruct((t, SSM_W), F32),
                   jax.ShapeDtypeStruct((t, SSM_W), BF16), jax.ShapeDtypeStruct((1, ATT_W), F32),
                   jax.ShapeDtypeStruct((1, SSM_W), F32)],
        compiler_params=_params(("arbitrary",)),
    )(dcat, o, y, proj, ga, gs)


ATT_QB = 256
ATT_KB = 256
assert ATT_QB == ATT_KB


def _stacked(m):
    return jnp.concatenate([m, m], axis=0)


def _split_sum(x, m2):
    hi = x.astype(BF16)
    lo = (x - hi.astype(F32)).astype(BF16)
    return _dot(jnp.concatenate([hi, lo], axis=1), m2)


def _att_tile(z, mask, m_strict2, carry):
    lse = jnp.log(1.0 + jnp.exp(-jnp.abs(z)))
    lb = jnp.minimum(z, 0.0) - lse
    lrm = -jnp.maximum(z, 0.0) - lse
    if mask is not None:
        lrm = jnp.where(mask, lrm, 0.0)
    w = jnp.exp(lb + _split_sum(lrm, m_strict2) + carry)
    if mask is not None:
        w = jnp.where(mask, w, 0.0)
    return lb, lrm, w


ATT_HP = 2
ATT_HP_FWD = 4


def _head_spec(seq, off, hp=ATT_HP):
    width = hp * ATT_DH
    per = ATT_HEADS // hp
    return pl.BlockSpec((seq, width), lambda s: (s // per, off // width + s % per))


def _head_lanes(h):
    return slice(h * ATT_DH, (h + 1) * ATT_DH)


def _attn_fwd(proj, gq, gk, nb, seq, name):
    nq = seq // ATT_QB
    scale = ATT_DH ** -0.5
    hp = ATT_HP_FWD
    heads = range(hp)

    def body(q_ref, k_ref, v_ref, gq_ref, gk_ref, o_ref, qs, kn, vb):
        for h in heads:
            sl = _head_lanes(h)
            qv = q_ref[:, sl]
            kv = k_ref[:, sl]
            qs[:, sl] = (qv * _rstd(qv) * gq_ref[...] * scale).astype(BF16)
            kn[:, sl] = (kv * _rstd(kv) * gk_ref[...]).astype(BF16)
            vb[:, sl] = v_ref[:, sl].astype(BF16)
        row = lax.broadcasted_iota(jnp.int32, (ATT_QB, ATT_KB), 0)
        col = lax.broadcasted_iota(jnp.int32, (ATT_QB, ATT_KB), 1)
        m_strict2 = _stacked((row > col).astype(BF16))
        diagonal = col < row

        def key_rows(kj):
            return pl.ds(pl.multiple_of(kj * ATT_KB, ATT_KB), ATT_KB)

        def q_loop(qi, _):
            q0 = pl.multiple_of(qi * ATT_QB, ATT_QB)
            q_ts = [qs[pl.ds(q0, ATT_QB), _head_lanes(h)] for h in heads]

            def scores(h, kj):
                return _dot_nt(q_ts[h], kn[key_rows(kj), _head_lanes(h)])

            def tile(c, kj, mask):
                rows = key_rows(kj)
                out = []
                for h in heads:
                    acc, carry, z = c[h]
                    z_next = scores(h, jnp.maximum(kj - 1, 0))
                    _, lrm, w = _att_tile(z, mask, m_strict2, carry)
                    acc = acc + _dot(w.astype(BF16), vb[rows, _head_lanes(h)])
                    out.append((acc, carry + jnp.sum(lrm, axis=-1, keepdims=True), z_next))
                return tuple(out)

            init = tuple((jnp.zeros((ATT_QB, ATT_DH), F32), jnp.zeros((ATT_QB, 1), F32), scores(h, qi)) for h in heads)
            res = lax.fori_loop(1, qi + 1, lambda i, c: tile(c, qi - i, None), tile(init, qi, diagonal))
            for h in heads:
                o_ref[pl.ds(q0, ATT_QB), _head_lanes(h)] = res[h][0]
            return 0

        lax.fori_loop(0, nq, q_loop, 0)

    vec = pl.BlockSpec((1, ATT_DH), lambda s: (0, 0))
    return pl.pallas_call(
        body, name=name, grid=(nb * ATT_HEADS // hp,),
        in_specs=[_head_spec(seq, OFF_Q, hp), _head_spec(seq, OFF_K, hp), _head_spec(seq, OFF_V, hp), vec, vec],
        out_specs=_head_spec(seq, 0, hp),
        out_shape=jax.ShapeDtypeStruct((nb * seq, ATT_W), F32),
        scratch_shapes=[pltpu.VMEM((seq, hp * ATT_DH), BF16)] * 3,
        compiler_params=_params(("parallel",)),
    )(proj, proj, proj, gq, gk)


def _attn_bwd(proj, do, gq, gk, nb, seq, name):
    nq = seq // ATT_QB
    nk = seq // ATT_KB
    scale = ATT_DH ** -0.5
    heads = range(ATT_HP)

    def body(q_ref, k_ref, v_ref, do_ref, gq_ref, gk_ref, dq_ref, dk_ref, dv_ref, dgq_ref, dgk_ref,
             qs, kn, vb, dob, dq_acc, dk_acc, dv_acc, gbuf, bbuf):
        @pl.when(pl.program_id(0) == 0)
        def _():
            dgq_ref[...] = jnp.zeros_like(dgq_ref)
            dgk_ref[...] = jnp.zeros_like(dgk_ref)

        for h in heads:
            sl = _head_lanes(h)
            qv = q_ref[:, sl]
            kv = k_ref[:, sl]
            qs[:, sl] = (qv * _rstd(qv) * gq_ref[...] * scale).astype(BF16)
            kn[:, sl] = (kv * _rstd(kv) * gk_ref[...]).astype(BF16)
            vb[:, sl] = v_ref[:, sl].astype(BF16)
            dob[:, sl] = do_ref[:, sl].astype(BF16)
        dk_acc[...] = jnp.zeros_like(dk_acc)
        dv_acc[...] = jnp.zeros_like(dv_acc)
        row = lax.broadcasted_iota(jnp.int32, (ATT_QB, ATT_KB), 0)
        col = lax.broadcasted_iota(jnp.int32, (ATT_QB, ATT_KB), 1)
        m_strict2 = _stacked((row > col).astype(BF16))
        m_prefix2 = _stacked((row < col).astype(BF16))
        diagonal = col < row

        def key_rows(kj):
            return pl.ds(pl.multiple_of(kj * ATT_KB, ATT_KB), ATT_KB)

        def q_loop(qi, _):
            q0 = pl.multiple_of(qi * ATT_QB, ATT_QB)
            q_ts = [qs[pl.ds(q0, ATT_QB), _head_lanes(h)] for h in heads]
            do_ts = [dob[pl.ds(q0, ATT_QB), _head_lanes(h)] for h in heads]

            def scores(h, kj):
                return _dot_nt(q_ts[h], kn[key_rows(kj), _head_lanes(h)])

            def down(c, kj, mask):
                rows = key_rows(kj)
                out = []
                for h in heads:
                    carry, z = c[h]
                    sl = _head_lanes(h)
                    z_next = scores(h, jnp.maximum(kj - 1, 0))
                    lb, lrm, w = _att_tile(z, mask, m_strict2, carry)
                    dw = _dot_nt(do_ts[h], vb[rows, sl])
                    gbuf[h * nk + kj] = w * dw
                    bbuf[h * nk + kj] = jnp.exp(lb)
                    dv_acc[rows, sl] += _dot_tn(w.astype(BF16), do_ts[h])
                    out.append((carry + jnp.sum(lrm, axis=-1, keepdims=True), z_next))
                return tuple(out)

            init = tuple((jnp.zeros((ATT_QB, 1), F32), scores(h, qi)) for h in heads)
            lax.fori_loop(1, qi + 1, lambda i, c: down(c, qi - i, None), down(init, qi, diagonal))

            def up(c, kj, mask):
                rows = key_rows(kj)
                out = []
                for h in heads:
                    acc, carry, within = c[h]
                    sl = _head_lanes(h)
                    g = gbuf[h * nk + kj]
                    beta = bbuf[h * nk + kj]
                    within_next = _split_sum(gbuf[h * nk + jnp.minimum(kj + 1, qi)], m_prefix2)
                    dz = g - (g + within + carry) * beta
                    if mask is not None:
                        dz = jnp.where(mask, dz, 0.0)
                    dz = dz.astype(BF16)
                    acc = acc + _dot(dz, kn[rows, sl])
                    dk_acc[rows, sl] += _dot_tn(dz, q_ts[h])
                    out.append((acc, carry + jnp.sum(g, axis=-1, keepdims=True), within_next))
                return tuple(out)

            init = tuple((jnp.zeros((ATT_QB, ATT_DH), F32), jnp.zeros((ATT_QB, 1), F32),
                          _split_sum(gbuf[h * nk], m_prefix2)) for h in heads)
            res = up(lax.fori_loop(0, qi, lambda kj, c: up(c, kj, None), init), qi, diagonal)
            for h in heads:
                dq_acc[pl.ds(q0, ATT_QB), _head_lanes(h)] = res[h][0]
            return 0

        lax.fori_loop(0, nq, q_loop, 0)

        def norm_bwd(xv, gain, dyn):
            r = _rstd(xv)
            xh = xv * r
            dgain = jnp.sum(dyn * xh, axis=0, keepdims=True)
            dxh = dyn * gain
            return r * (dxh - xh * jnp.mean(dxh * xh, axis=-1, keepdims=True)), dgain

        for h in heads:
            sl = _head_lanes(h)
            dq, dgq = norm_bwd(q_ref[:, sl], gq_ref[...], dq_acc[:, sl] * scale)
            dk, dgk = norm_bwd(k_ref[:, sl], gk_ref[...], dk_acc[:, sl])
            dq_ref[:, sl] = dq.astype(BF16)
            dk_ref[:, sl] = dk.astype(BF16)
            dv_ref[:, sl] = dv_acc[:, sl].astype(BF16)
            dgq_ref[...] += dgq
            dgk_ref[...] += dgk

    vec = pl.BlockSpec((1, ATT_DH), lambda s: (0, 0))
    big = jax.ShapeDtypeStruct((nb * seq, ATT_W), BF16)
    small = jax.ShapeDtypeStruct((1, ATT_DH), F32)
    width = ATT_HP * ATT_DH
    return pl.pallas_call(
        body, name=name, grid=(nb * ATT_HEADS // ATT_HP,),
        in_specs=[_head_spec(seq, OFF_Q), _head_spec(seq, OFF_K), _head_spec(seq, OFF_V), _head_spec(seq, 0), vec, vec],
        out_specs=[_head_spec(seq, 0)] * 3 + [vec, vec],
        out_shape=[big, big, big, small, small],
        scratch_shapes=[pltpu.VMEM((seq, width), BF16)] * 4 + [pltpu.VMEM((seq, width), F32)] * 3
        + [pltpu.VMEM((ATT_HP * nk, ATT_QB, ATT_KB), F32)] * 2,
        compiler_params=_params(("arbitrary",)),
    )(proj, proj, proj, do, gq, gk)


CONV_COLS = 256


def _pack_conv(conv_w, conv_b):
    return jnp.concatenate([conv_w, conv_b[None, :], jnp.zeros((3, CONV_DIM), F32)], axis=0)


def _conv_pre(raw, w8, rowi):
    pre = w8[CONV_K:CONV_K + 1, :] + raw * w8[CONV_K - 1:CONV_K, :]
    for k in range(1, CONV_K):
        sh = jnp.where(rowi >= k, pltpu.roll(raw, k, 0), 0.0)
        pre = pre + sh * w8[CONV_K - 1 - k:CONV_K - k, :]
    return pre


def _conv_fwd(proj, cw8, nb, seq, name, after):
    ncol = CONV_DIM // CONV_COLS

    def body(x_ref, w_ref, tok_ref, o_ref):
        rowi = lax.broadcasted_iota(jnp.int32, (seq, 1), 0)
        pre = _conv_pre(x_ref[...], w_ref[...], rowi)
        o_ref[...] = pre * _sigmoid(pre)

    return pl.pallas_call(
        body, name=name, grid=(nb, ncol),
        in_specs=[pl.BlockSpec((seq, CONV_COLS), lambda b, j: (b, OFF_XS // CONV_COLS + j)),
                  pl.BlockSpec((8, CONV_COLS), lambda b, j: (0, j)),
                  pl.BlockSpec((8, LANE), lambda b, j: (0, 0))],
        out_specs=pl.BlockSpec((seq, CONV_COLS), lambda b, j: (b, j)),
        out_shape=jax.ShapeDtypeStruct((nb * seq, CONV_DIM), F32),
        compiler_params=_params(("parallel", "parallel")),
    )(proj, cw8, after)


def _conv_bwd(proj, dact, cw8, nb, seq, name):
    ncol = CONV_DIM // CONV_COLS

    def body(x_ref, d_ref, w_ref, dx_ref, dw_ref):
        @pl.when(pl.program_id(1) == 0)
        def _():
            dw_ref[...] = jnp.zeros_like(dw_ref)

        rowi = lax.broadcasted_iota(jnp.int32, (seq, 1), 0)
        raw = x_ref[...]
        w8 = w_ref[...]
        pre = _conv_pre(raw, w8, rowi)
        sg = _sigmoid(pre)
        dpre = d_ref[...] * (sg * (1.0 + pre * (1.0 - sg)))
        dw_ref[CONV_K:CONV_K + 1, :] += jnp.sum(dpre, axis=0, keepdims=True)
        dw_ref[CONV_K - 1:CONV_K, :] += jnp.sum(dpre * raw, axis=0, keepdims=True)
        draw = dpre * w8[CONV_K - 1:CONV_K, :]
        for k in range(1, CONV_K):
            sh = jnp.where(rowi >= k, pltpu.roll(raw, k, 0), 0.0)
            dw_ref[CONV_K - 1 - k:CONV_K - k, :] += jnp.sum(dpre * sh, axis=0, keepdims=True)
            up = jnp.where(rowi < seq - k, pltpu.roll(dpre, seq - k, 0), 0.0)
            draw = draw + up * w8[CONV_K - 1 - k:CONV_K - k, :]
        dx_ref[...] = draw.astype(BF16)

    return pl.pallas_call(
        body, name=name, grid=(ncol, nb),
        in_specs=[pl.BlockSpec((seq, CONV_COLS), lambda j, b: (b, OFF_XS // CONV_COLS + j)),
                  pl.BlockSpec((seq, CONV_COLS), lambda j, b: (b, j)),
                  pl.BlockSpec((8, CONV_COLS), lambda j, b: (0, j))],
        out_specs=[pl.BlockSpec((seq, CONV_COLS), lambda j, b: (b, j)),
                   pl.BlockSpec((8, CONV_COLS), lambda j, b: (0, j))],
        out_shape=[jax.ShapeDtypeStruct((nb * seq, CONV_DIM), BF16), jax.ShapeDtypeStruct((8, CONV_DIM), F32)],
        compiler_params=_params(("parallel", "arbitrary")),
    )(proj, dact, cw8)


def _pack_heads(dt_bias, a_log, d_skip):
    rows = jnp.stack([dt_bias, a_log, d_skip]).reshape(3, SSM_GROUPS, SSM_HG).transpose(1, 0, 2)
    return jnp.pad(rows, ((0, 0), (0, 8 - 3), (0, LANE - SSM_HG)))


def _split3_rows(x):
    hi = x.astype(BF16)
    r1 = x - hi.astype(F32)
    mid = r1.astype(BF16)
    lo = (r1 - mid.astype(F32)).astype(BF16)
    return jnp.concatenate([hi, mid, lo], axis=0)


def _split3_cols(x):
    hi = x.astype(BF16)
    r1 = x - hi.astype(F32)
    mid = r1.astype(BF16)
    lo = (r1 - mid.astype(F32)).astype(BF16)
    return jnp.concatenate([hi, mid, lo], axis=1)


def _split2_rows(x):
    hi = x.astype(BF16)
    return jnp.concatenate([hi, (x - hi.astype(F32)).astype(BF16)], axis=0)


def _ssd_specs(seq):
    gx = SSM_HG * SSM_P
    return dict(
        xs=pl.BlockSpec((seq, gx), lambda g, b: (b, g)),
        bm=pl.BlockSpec((seq, SSM_N), lambda g, b: (b, SSM_W // SSM_N + g)),
        cm=pl.BlockSpec((seq, SSM_N), lambda g, b: (b, SSM_W // SSM_N + SSM_GROUPS + g)),
        dt=pl.BlockSpec((seq, LANE), lambda g, b: (b, OFF_DT // LANE)),
        hp=pl.BlockSpec((1, 8, LANE), lambda g, b: (g, 0, 0)),
        head=pl.BlockSpec((seq, gx), lambda g, b: (b, g)),
        grp=pl.BlockSpec((seq, SSM_N), lambda g, b: (b, g)),
    )


SSM_GX = SSM_HG * SSM_P


def _ssd_masks():
    li = lax.broadcasted_iota(jnp.int32, (CHUNK, CHUNK), 0)
    si = lax.broadcasted_iota(jnp.int32, (CHUNK, CHUNK), 1)
    head = lax.broadcasted_iota(jnp.int32, (LANE, SSM_GX), 0)
    lane = lax.broadcasted_iota(jnp.int32, (LANE, SSM_GX), 1)
    expand = (lane // SSM_P == head).astype(BF16)
    head_t = lax.broadcasted_iota(jnp.int32, (SSM_GX, LANE), 1)
    lane_t = lax.broadcasted_iota(jnp.int32, (SSM_GX, LANE), 0)
    gather = (lane_t // SSM_P == head_t).astype(BF16)
    return dict(
        causal=li >= si, causal_t=si >= li,
        tril3=jnp.concatenate([(si <= li).astype(BF16)] * 3, axis=1),
        triu2=jnp.concatenate([(si >= li).astype(BF16)] * 2, axis=1),
        below2=jnp.concatenate([(si < li).astype(BF16)] * 2, axis=1),
        expand2=_stacked(expand), gather2=_stacked(gather))


def _per_head(x, mk):
    return _split_sum(x, mk["expand2"])


def _head_sums(x, mk):
    return _split_sum(x, mk["gather2"])


def _row8(v):
    return jnp.broadcast_to(v, (8, v.shape[1]))


def _group_dt(dt_ref):
    shift = (LANE - SSM_HG * pl.program_id(0)) % LANE
    return pltpu.roll(dt_ref[...], shift, 1)


def _ssd_fwd(act, proj, hp, nb, seq, name):
    nc = seq // CHUNK

    def body(xs_ref, b_ref, c_ref, dt_ref, hp_ref, y_ref, dt_s, da_s, hst):
        mk = _ssd_masks()
        hpv = hp_ref[0]
        dt = _softplus(_group_dt(dt_ref) + hpv[0:1, :])
        a = -jnp.exp(hpv[1:2, :])
        dsk_row = _per_head(_row8(hpv[2:3, :]), mk)[0:1]
        dt_s[...] = dt
        da_s[...] = dt * a
        hst[...] = jnp.zeros_like(hst)

        def chunk(c, _):
            rows = pl.ds(pl.multiple_of(c * CHUNK, CHUNK), CHUNK)
            acol = _dot(mk["tril3"], _split3_rows(da_s[rows, :]))
            arow = acol.T
            alast = acol[CHUNK - 1:CHUNK, :]
            ea = _per_head(jnp.exp(acol), mk)
            eb = _per_head(jnp.exp(alast - acol), mk)
            el = _per_head(_row8(jnp.exp(alast)), mk)[0:1]
            bb = b_ref[rows, :].astype(BF16)
            cb = c_ref[rows, :].astype(BF16)
            cbm = _dot_nt(cb, bb)
            xc = xs_ref[rows, :]
            u = xc * _per_head(dt_s[rows, :], mk)
            ub = u.astype(BF16)
            ht = hst[...]
            y_ref[rows, :] = ea * _dot(cb, ht.astype(BF16)) + dsk_row * xc
            for j in range(SSM_HG):
                sl = slice(j * SSM_P, (j + 1) * SSM_P)
                decay = jnp.where(mk["causal"], jnp.exp(jnp.minimum(acol[:, j:j + 1] - arow[j:j + 1, :], 0.0)), 0.0)
                y_ref[rows, sl] += _dot((cbm * decay).astype(BF16), ub[:, sl])
            hst[...] = el * ht + _dot_tn(bb, (u * eb).astype(BF16))
            return 0

        lax.fori_loop(0, nc, chunk, 0)

    sp = _ssd_specs(seq)
    return pl.pallas_call(
        body, name=name, grid=(SSM_GROUPS, nb),
        in_specs=[sp["xs"], sp["bm"], sp["cm"], sp["dt"], sp["hp"]],
        out_specs=sp["head"],
        out_shape=jax.ShapeDtypeStruct((nb * seq, SSM_W), F32),
        scratch_shapes=[pltpu.VMEM((seq, LANE), F32)] * 2 + [pltpu.VMEM((SSM_N, SSM_GX), F32)],
        compiler_params=_params(("parallel", "parallel")),
    )(act, act, act, proj, hp)


def _ssd_bwd(act, proj, dy, hp, nb, seq, name):
    nc = seq // CHUNK

    def body(xs_ref, b_ref, c_ref, dt_ref, hp_ref, dy_ref, dxs_ref, db_ref, dc_ref, ddt_ref, dhp_ref,
             dt_s, da_s, ddt_s, hs, lam, du_s):
        @pl.when(pl.program_id(1) == 0)
        def _():
            dhp_ref[...] = jnp.zeros_like(dhp_ref)

        mk = _ssd_masks()
        hpv = hp_ref[0]
        a = -jnp.exp(hpv[1:2, :])
        dsk_row = _per_head(_row8(hpv[2:3, :]), mk)[0:1]
        dt_s[...] = _softplus(_group_dt(dt_ref) + hpv[0:1, :])
        da_s[...] = dt_s[...] * a
        lane = lax.broadcasted_iota(jnp.int32, (1, LANE), 1)

        def chunk_rows(c):
            return pl.ds(pl.multiple_of(c * CHUNK, CHUNK), CHUNK)

        def decays(c):
            acol = _dot(mk["tril3"], _split3_rows(da_s[chunk_rows(c), :]))
            alast = acol[CHUNK - 1:CHUNK, :]
            return acol, alast

        hs[0] = jnp.zeros((SSM_N, SSM_GX), F32)

        def fwd_chunk(c, _):
            rows = chunk_rows(c)
            acol, alast = decays(c)
            eb = _per_head(jnp.exp(alast - acol), mk)
            el = _per_head(_row8(jnp.exp(alast)), mk)[0:1]
            u = xs_ref[rows, :] * _per_head(dt_s[rows, :], mk)
            hs[c + 1] = el * hs[c] + _dot_tn(b_ref[rows, :].astype(BF16), (u * eb).astype(BF16))
            return 0

        lax.fori_loop(0, nc - 1, fwd_chunk, 0)
        lam[...] = jnp.zeros_like(lam)

        def bwd_chunk(i, carry):
            dd_row, da_vec = carry
            c = nc - 1 - i
            rows = chunk_rows(c)
            acol, alast = decays(c)
            arow = acol.T
            ea = _per_head(jnp.exp(acol), mk)
            eb = _per_head(jnp.exp(alast - acol), mk)
            el = _per_head(_row8(jnp.exp(alast)), mk)[0:1]
            dt_all = _per_head(dt_s[rows, :], mk)
            bb = b_ref[rows, :].astype(BF16)
            cb = c_ref[rows, :].astype(BF16)
            cbm = _dot_nt(cb, bb)
            cbt = _dot_nt(bb, cb)
            xc = xs_ref[rows, :]
            dyc = dy_ref[rows, :]
            u = xc * dt_all
            ub = u.astype(BF16)
            dyb = dyc.astype(BF16)
            h_in = hs[c]
            lm = lam[...]
            hb = h_in.astype(BF16)
            lb = lm.astype(BF16)
            y_off = ea * _dot(cb, hb)
            du_off = eb * _dot(bb, lb)
            dye = (ea * dyc).astype(BF16)
            zero = jnp.zeros((CHUNK, CHUNK), F32)
            dcb, dcbt, d_a = zero, zero, zero
            for j in range(SSM_HG):
                sl = slice(j * SSM_P, (j + 1) * SSM_P)
                seg = acol[:, j:j + 1] - arow[j:j + 1, :]
                decay = jnp.where(mk["causal"], jnp.exp(jnp.minimum(seg, 0.0)), 0.0)
                decay_t = jnp.where(mk["causal_t"], jnp.exp(jnp.minimum(-seg, 0.0)), 0.0)
                m = cbm * decay
                mt = cbt * decay_t
                dm = _dot_nt(dyb[:, sl], ub[:, sl])
                dmt = _dot_nt(ub[:, sl], dyb[:, sl])
                dcb = dcb + dm * decay
                dcbt = dcbt + dmt * decay_t
                du_s[:, sl] = _dot(mt.astype(BF16), dyb[:, sl])
                d_a_j = jnp.sum(dm * m, axis=-1, keepdims=True) - jnp.sum(dmt * mt, axis=-1, keepdims=True)
                d_a = jnp.where(lane == j, d_a_j, d_a)
            du = du_s[...] + du_off
            dxs_ref[rows, :] = du * dt_all + dsk_row * dyc
            dc_ref[rows, :] = _dot_nt(dye, hb) + _dot(dcb.astype(BF16), bb)
            db_ref[rows, :] = _dot_nt((eb * u).astype(BF16), lb) + _dot(dcbt.astype(BF16), cb)
            lam[...] = el * lm + _dot_tn(cb, dye)
            d_a = d_a + _head_sums(dyc * y_off, mk)
            f_a = _head_sums(du_off * u, mk)
            c_a = jnp.exp(alast) * _head_sums(_row8(jnp.sum(lm * h_in, axis=0, keepdims=True)), mk)[0:1]
            dda = _dot(mk["triu2"], _split2_rows(d_a)) + _dot(mk["below2"], _split2_rows(f_a)) + c_a
            ddt_s[rows, :] = dda * a + _head_sums(du * xc, mk)
            da_vec = da_vec + jnp.sum(dda * dt_s[rows, :], axis=0, keepdims=True)
            dd_row = dd_row + jnp.sum(dyc * xc, axis=0, keepdims=True)
            return dd_row, da_vec

        init = (jnp.zeros((1, SSM_GX), F32), jnp.zeros((1, LANE), F32))
        dd_row, da_vec = lax.fori_loop(0, nc, bwd_chunk, init)
        ddt_raw = ddt_s[...] * _sigmoid(_group_dt(dt_ref) + hpv[0:1, :])
        ddt_ref[...] = ddt_raw.astype(BF16)
        dhp_ref[0, 0:1, :] += jnp.sum(ddt_raw, axis=0, keepdims=True)
        dhp_ref[0, 1:2, :] += da_vec * a
        dhp_ref[0, 2:3, :] += _head_sums(_row8(dd_row), mk)[0:1]

    sp = _ssd_specs(seq)
    t = nb * seq
    return pl.pallas_call(
        body, name=name, grid=(SSM_GROUPS, nb),
        in_specs=[sp["xs"], sp["bm"], sp["cm"], sp["dt"], sp["hp"], sp["head"]],
        out_specs=[sp["head"], sp["grp"], sp["grp"], sp["grp"], sp["hp"]],
        out_shape=[jax.ShapeDtypeStruct((t, SSM_W), F32), jax.ShapeDtypeStruct((t, SSM_GROUPS * SSM_N), F32),
                   jax.ShapeDtypeStruct((t, SSM_GROUPS * SSM_N), F32),
                   jax.ShapeDtypeStruct((t, SSM_GROUPS * LANE), BF16),
                   jax.ShapeDtypeStruct((SSM_GROUPS, 8, LANE), F32)],
        scratch_shapes=[pltpu.VMEM((seq, LANE), F32)] * 3
        + [pltpu.VMEM((nc, SSM_N, SSM_GX), F32), pltpu.VMEM((SSM_N, SSM_GX), F32), pltpu.VMEM((CHUNK, SSM_GX), F32)],
        compiler_params=_params(("parallel", "arbitrary")),
    )(act, act, act, proj, hp, dy)


ANY = pl.BlockSpec(memory_space=pl.ANY)


def _block_index(p):
    return 4 * p[0] + 2 * p[1] + p[2]


def _all_gather(shards, name):
    n = len(shards)

    def body(*refs):
        ins, outs = refs[:n], refs[n:2 * n]
        send_sems, recv_sems, local_sems = refs[2 * n:]
        x, y, c = lax.axis_index("x"), lax.axis_index("y"), lax.axis_index("c")
        me, sibling = (x, y, c), (x, y, 1 - c)
        chips = [(1 - x, y), (x, 1 - y), (1 - x, 1 - y)]

        def copy(i, k, block, to, src=None):
            dst = outs[i].at[_block_index(block)]
            return pltpu.make_async_remote_copy(
                src_ref=dst if src is None else src, dst_ref=dst,
                send_sem=send_sems.at[i, k], recv_sem=recv_sems.at[i, k],
                device_id=to, device_id_type=MESH)

        mine = [pltpu.make_async_copy(ins[i], outs[i].at[_block_index(me)], local_sems.at[i]) for i in range(n)]
        for cp in mine:
            cp.start()
        first = []
        for i in range(n):
            first.append(copy(i, 0, me, sibling, src=ins[i]))
            first += [copy(i, 1 + j, me, (*chip, c), src=ins[i]) for j, chip in enumerate(chips)]
        for cp in first:
            cp.start()
        passed = []
        for j, chip in enumerate(chips):
            for i in range(n):
                copy(i, 1 + j, (*chip, c), me).wait_recv()
                fwd = copy(i, 4 + j, (*chip, c), sibling)
                fwd.start()
                passed.append(fwd)
        for i in range(n):
            copy(i, 0, sibling, me).wait_recv()
            for j, chip in enumerate(chips):
                copy(i, 4 + j, (*chip, 1 - c), me).wait_recv()
        for cp in first + passed:
            cp.wait_send()
        for cp in mine:
            cp.wait()

    return pl.pallas_call(
        body, name=name,
        in_specs=[ANY] * n, out_specs=[ANY] * n,
        out_shape=[jax.ShapeDtypeStruct((N_DEV,) + s.shape, s.dtype) for s in shards],
        scratch_shapes=[pltpu.SemaphoreType.DMA((n, 7)), pltpu.SemaphoreType.DMA((n, 7)),
                        pltpu.SemaphoreType.DMA((n,))],
    )(*shards)


HBM = pl.BlockSpec(memory_space=pltpu.HBM)
SEM = pl.BlockSpec(memory_space=pltpu.SEMAPHORE)
EFFECT = pltpu.SideEffectType.DATAFLOW_SIDE_EFFECTING


def _my_block():
    return _block_index((lax.axis_index("x"), lax.axis_index("y"), lax.axis_index("c")))


def _peer(k):
    x, y, c = lax.axis_index("x"), lax.axis_index("y"), lax.axis_index("c")
    return (1 - x if k & 4 else x, 1 - y if k & 2 else y, 1 - c if k & 1 else c)


ALL_PEERS = tuple(range(1, N_DEV))
SIBLING = 1
SAME_CORE = (2, 4, 6)


def _plan_copies(plan, src_refs, land_refs, send_sems, recv_sems, peers=ALL_PEERS):
    me = _my_block()
    copies = []
    for e, (si, di, src_view, dst_view, _) in enumerate(plan):
        for k in peers:
            copies.append(pltpu.make_async_remote_copy(
                src_ref=src_view(src_refs[si], _block_index(_peer(k))),
                dst_ref=dst_view(land_refs[di], me),
                send_sem=send_sems[e], recv_sem=recv_sems[e],
                device_id=_peer(k), device_id_type=MESH))
    return copies


def _plan_forwards(plan, land_refs, send_sems, recv_sems):
    copies = []
    for e, (_, di, _, dst_view, _) in enumerate(plan):
        for k in SAME_CORE:
            part = dst_view(land_refs[di], _block_index(_peer(k)))
            copies.append(pltpu.make_async_remote_copy(
                src_ref=part, dst_ref=part, send_sem=send_sems[e], recv_sem=recv_sems[e],
                device_id=_peer(SIBLING), device_id_type=MESH))
    return copies


def _plan_waits(plan, land_refs, send_sems, recv_sems, n=N_DEV - 1):
    waits = []
    for e, (_, di, _, _, parts_view) in enumerate(plan):
        view = parts_view(land_refs[di], n)
        waits.append(pltpu.make_async_remote_copy(
            src_ref=view, dst_ref=view, send_sem=send_sems[e], recv_sem=recv_sems[e],
            device_id=_peer(SIBLING), device_id_type=MESH))
    return waits


def _plan_own(plan, src_refs, land_refs, own_sems):
    me = _my_block()
    return [pltpu.make_async_copy(src_view(src_refs[si], me), dst_view(land_refs[di], me), own_sems[e])
            for e, (si, di, src_view, dst_view, _) in enumerate(plan)]


def _copies_start(srcs, lands, plan, name, after=None):
    ns, nl, ne = len(srcs), len(lands), len(plan)
    extra = [] if after is None else [after]

    nin = ns + nl + len(extra)

    def body(*refs):
        src_refs, land_refs = refs[:ns], refs[ns:ns + nl]
        send_sems, recv_sems = refs[nin:nin + ne], refs[nin + ne:nin + 2 * ne]
        own_sems = refs[nin + 2 * ne:nin + 3 * ne]
        token = refs[-1]
        for cp in _plan_copies(plan, src_refs, land_refs, send_sems, recv_sems):
            cp.start()
        for cp in _plan_own(plan, src_refs, land_refs, own_sems):
            cp.start()
        token[...] = jnp.zeros_like(token)

    thru = [pltpu.HBM(a.shape, a.dtype) for a in list(srcs) + list(lands)]
    res = pl.pallas_call(
        body, name=name,
        in_specs=[HBM] * (ns + nl) + [ANY] * len(extra),
        out_specs=[SEM] * (3 * ne) + [HBM] * (ns + nl) + [pl.BlockSpec(memory_space=pltpu.VMEM)],
        out_shape=[pltpu.SemaphoreType.DMA(())] * (3 * ne) + thru + [jax.ShapeDtypeStruct((8, LANE), F32)],
        input_output_aliases={i: 3 * ne + i for i in range(ns + nl)},
        compiler_params=pltpu.CompilerParams(has_side_effects=EFFECT),
    )(*[pltpu.with_memory_space_constraint(a, pltpu.HBM) for a in list(srcs) + list(lands)], *extra)
    return dict(sems=res[:3 * ne], srcs=res[3 * ne:3 * ne + ns], lands=res[3 * ne + ns:3 * ne + ns + nl],
                token=res[-1], plan=plan)


def _copies_wait(flight, after, name):
    srcs, lands, plan = flight["srcs"], flight["lands"], flight["plan"]
    ns, nl, ne = len(srcs), len(lands), len(plan)

    def body(*refs):
        src_refs, land_refs = refs[:ns], refs[ns:ns + nl]
        send_sems, recv_sems = refs[ns + nl:ns + nl + ne], refs[ns + nl + ne:ns + nl + 2 * ne]
        own_sems = refs[ns + nl + 2 * ne:ns + nl + 3 * ne]
        for cp in _plan_waits(plan, land_refs, send_sems, recv_sems):
            cp.wait_send()
            cp.wait_recv()
        for cp in _plan_own(plan, src_refs, land_refs, own_sems):
            cp.wait()

    after = list(after) if isinstance(after, (list, tuple)) else [after]
    thru = [pltpu.HBM(a.shape, a.dtype) for a in list(srcs) + list(lands)]
    res = pl.pallas_call(
        body, name=name,
        in_specs=[HBM] * (ns + nl) + [SEM] * (3 * ne) + [ANY] * len(after),
        out_specs=[HBM] * (ns + nl),
        out_shape=thru,
        input_output_aliases={i: i for i in range(ns + nl)},
        compiler_params=pltpu.CompilerParams(has_side_effects=EFFECT),
    )(*srcs, *lands, *flight["sems"], *after)
    return list(res[ns:])


def _gather2_start(srcs, lands, plan, name, after=None):
    ns, nl, ne = len(srcs), len(lands), len(plan)
    extra = [] if after is None else [after]
    nin = ns + nl + len(extra)

    def body(*refs):
        src_refs, land_refs = refs[:ns], refs[ns:ns + nl]
        send_sems, recv_sems = refs[nin:nin + ne], refs[nin + ne:nin + 2 * ne]
        own_sems = refs[nin + 2 * ne:nin + 3 * ne]
        for cp in _plan_copies(plan, src_refs, land_refs, send_sems, recv_sems, (SIBLING,) + SAME_CORE):
            cp.start()
        for cp in _plan_own(plan, src_refs, land_refs, own_sems):
            cp.start()
        refs[-1][...] = jnp.zeros_like(refs[-1])

    thru = [pltpu.HBM(a.shape, a.dtype) for a in list(srcs) + list(lands)]
    res = pl.pallas_call(
        body, name=name,
        in_specs=[HBM] * (ns + nl) + [ANY] * len(extra),
        out_specs=[SEM] * (3 * ne) + [HBM] * (ns + nl) + [pl.BlockSpec(memory_space=pltpu.VMEM)],
        out_shape=[pltpu.SemaphoreType.DMA(())] * (3 * ne) + thru + [jax.ShapeDtypeStruct((8, LANE), F32)],
        input_output_aliases={i: 3 * ne + i for i in range(ns + nl)},
        compiler_params=pltpu.CompilerParams(has_side_effects=EFFECT),
    )(*[pltpu.with_memory_space_constraint(a, pltpu.HBM) for a in list(srcs) + list(lands)], *extra)
    return dict(send1=res[:ne], recv1=res[ne:2 * ne], own=res[2 * ne:3 * ne], srcs=res[3 * ne:3 * ne + ns],
                lands=res[3 * ne + ns:3 * ne + ns + nl], token=res[-1], plan=plan)


def _gather2_forward(flight, after, name):
    srcs, lands, plan = flight["srcs"], flight["lands"], flight["plan"]
    ns, nl, ne = len(srcs), len(lands), len(plan)
    nin = ns + nl + ne + 1

    def body(*refs):
        land_refs = refs[ns:ns + nl]
        recv1 = refs[ns + nl:ns + nl + ne]
        send2, recv2 = refs[nin:nin + ne], refs[nin + ne:nin + 2 * ne]
        for cp in _plan_waits(plan, land_refs, send2, recv1, n=1 + len(SAME_CORE)):
            cp.wait_recv()
        for cp in _plan_forwards(plan, land_refs, send2, recv2):
            cp.start()
        refs[-1][...] = jnp.zeros_like(refs[-1])

    thru = [pltpu.HBM(a.shape, a.dtype) for a in list(srcs) + list(lands)]
    res = pl.pallas_call(
        body, name=name,
        in_specs=[HBM] * (ns + nl) + [SEM] * ne + [ANY],
        out_specs=[SEM] * (2 * ne) + [HBM] * (ns + nl) + [pl.BlockSpec(memory_space=pltpu.VMEM)],
        out_shape=[pltpu.SemaphoreType.DMA(())] * (2 * ne) + thru + [jax.ShapeDtypeStruct((8, LANE), F32)],
        input_output_aliases={i: 2 * ne + i for i in range(ns + nl)},
        compiler_params=pltpu.CompilerParams(has_side_effects=EFFECT),
    )(*srcs, *lands, *flight["recv1"], after)
    return dict(flight, send2=res[:ne], recv2=res[ne:2 * ne], srcs=res[2 * ne:2 * ne + ns],
                lands=res[2 * ne + ns:2 * ne + ns + nl], token=res[-1])


def _gather2_wait(flight, after, name):
    srcs, lands, plan = flight["srcs"], flight["lands"], flight["plan"]
    ns, nl, ne = len(srcs), len(lands), len(plan)

    def body(*refs):
        src_refs, land_refs = refs[:ns], refs[ns:ns + nl]
        sems = refs[ns + nl:ns + nl + 4 * ne]
        send1, own, send2, recv2 = sems[:ne], sems[ne:2 * ne], sems[2 * ne:3 * ne], sems[3 * ne:]
        for cp in _plan_waits(plan, land_refs, send1, recv2, n=1 + len(SAME_CORE)):
            cp.wait_send()
        for cp in _plan_waits(plan, land_refs, send2, recv2, n=len(SAME_CORE)):
            cp.wait_send()
            cp.wait_recv()
        for cp in _plan_own(plan, src_refs, land_refs, own):
            cp.wait()

    thru = [pltpu.HBM(a.shape, a.dtype) for a in list(srcs) + list(lands)]
    res = pl.pallas_call(
        body, name=name,
        in_specs=[HBM] * (ns + nl) + [SEM] * (4 * ne) + [ANY],
        out_specs=[HBM] * (ns + nl),
        out_shape=thru,
        input_output_aliases={i: i for i in range(ns + nl)},
        compiler_params=pltpu.CompilerParams(has_side_effects=EFFECT),
    )(*srcs, *lands, *flight["send1"], *flight["own"], *flight["send2"], *flight["recv2"], after)
    return list(res[ns:])


def _adamw_math(w, g, m, v):
    m = ADAM_B1 * m + (1.0 - ADAM_B1) * g
    v = ADAM_B2 * v + (1.0 - ADAM_B2) * (g * g)
    m_hat = m / (1.0 - ADAM_B1 ** ADAM_STEP)
    v_hat = v / (1.0 - ADAM_B2 ** ADAM_STEP)
    delta = -ADAM_LR * (m_hat / (jnp.sqrt(v_hat) + ADAM_EPS) + ADAM_WD * w)
    return delta, m, v


def _adamw(parts, w, m, v, name, rows, lane_offset=None):
    depth, r, c = w.shape
    cp = parts[0].shape[2]
    assert r % rows == 0 and len(parts) == depth

    def body(*refs):
        p_refs = refs[:depth]
        w_ref, m_ref, v_ref, g_ref, d_ref, mo_ref, vo_ref = refs[depth:]
        if lane_offset is not None:
            cw = -(-c // LANE) * LANE
            src = lax.broadcasted_iota(jnp.int32, (cp, cw), 0)
            dst = lax.broadcasted_iota(jnp.int32, (cp, cw), 1)
            pick = (src == dst + lane_offset()).astype(BF16)
            pick3 = jnp.concatenate([pick] * 3, axis=0)
        for li in range(depth):
            @pl.when(pl.program_id(0) == li)
            def _(li=li):
                g = p_refs[li][0].astype(F32)
                for j in range(1, N_DEV):
                    g = g + p_refs[li][j].astype(F32)
                if lane_offset is not None:
                    g = _dot(_split3_cols(g), pick3)
                g = g[:, :c]
                d, mn, vn = _adamw_math(w_ref[...], g, m_ref[...], v_ref[...])
                g_ref[...] = g
                d_ref[...] = d
                mo_ref[...] = mn
                vo_ref[...] = vn

    def part_spec(li):
        return pl.BlockSpec((N_DEV, rows, cp), lambda l, i: (0, jnp.where(l == li, i, 0), 0))

    blk = pl.BlockSpec((None, rows, c), lambda l, i: (l, i, 0))
    out = jax.ShapeDtypeStruct((depth, r, c), F32)
    return pl.pallas_call(
        body, name=name, grid=(depth, r // rows),
        in_specs=[part_spec(li) for li in range(depth)] + [blk, blk, blk],
        out_specs=[blk] * 4, out_shape=[out] * 4,
        compiler_params=_params(("arbitrary", "arbitrary")),
    )(*parts, w, m, v)


def _sum_parts(parts, name):
    _, r, c = parts.shape

    def body(p_ref, o_ref):
        g = p_ref[0]
        for j in range(1, N_DEV):
            g = g + p_ref[j]
        o_ref[...] = g

    return pl.pallas_call(
        body, name=name, out_shape=jax.ShapeDtypeStruct((r, c), F32),
        compiler_params=_params(),
    )(parts)


def _adamw_small(parts, ws, ms, vs, name):
    n = len(ws)

    def body(*refs):
        ins, outs = refs[:4 * n], refs[4 * n:]
        for i in range(n):
            p_ref, w_ref, m_ref, v_ref = ins[i], ins[n + i], ins[2 * n + i], ins[3 * n + i]
            g = p_ref[0]
            for j in range(1, p_ref.shape[0]):
                g = g + p_ref[j]
            d, mn, vn = _adamw_math(w_ref[...], g, m_ref[...], v_ref[...])
            outs[i][...] = g
            outs[n + i][...] = d
            outs[2 * n + i][...] = mn
            outs[3 * n + i][...] = vn

    out = [jax.ShapeDtypeStruct(a.shape, F32) for a in ws] * 4
    res = pl.pallas_call(body, name=name, out_shape=out, compiler_params=_params())(*parts, *ws, *ms, *vs)
    return res[:n], res[n:2 * n], res[2 * n:3 * n], res[3 * n:]


SMALL = ("norm_mix", "q_gain", "k_gain", "conv_b", "dt_bias", "a_log", "d_skip", "attn_out_gain",
         "ssm_out_gain", "norm_ffn")


def _full_cols(gathered):
    _, r, c = gathered.shape
    return gathered.transpose(1, 0, 2).reshape(r, N_DEV * c)


FF_BLK = 768
FF_PAD = N_DEV * FF_BLK
W_IN_COLS = IN_DIM // N_DEV
W_IN_WINDOW = 896


def _w_in_window_start(block):
    return (block * W_IN_COLS // LANE) * LANE


def _w_in_window(ref, block):
    return ref.at[:, pl.ds(pl.multiple_of(_w_in_window_start(block), LANE), W_IN_WINDOW)]


def _whole(ref, block):
    return ref


def _rows_of(size):
    return lambda ref, block: ref.at[pl.ds(pl.multiple_of(block * size, size), size), :]


def _slot(ref, block):
    return ref.at[block]


def _n_slots(ref, n):
    return ref.at[pl.ds(0, n)]


def _n_rows(size):
    return lambda ref, n: ref.at[pl.ds(0, n * size), :]


GATHER_A = [(0, 0, _whole, _slot, _n_slots), (1, 1, _whole, _rows_of(256), _n_rows(256))]
GATHER_B = [(i, i, _whole, _rows_of(FF_BLK), _n_rows(FF_BLK)) for i in range(3)]
SCATTER_A = [(0, 0, _w_in_window, _slot, _n_slots), (1, 1, _rows_of(256), _slot, _n_slots)]
SCATTER_B = [(i, i, _rows_of(FF_BLK), _slot, _n_slots) for i in range(3)]


def _gather_lands(which, shards, d):
    if which == "a":
        return [lax.empty((N_DEV,) + shards[0].shape, BF16), lax.empty((d, d), BF16)]
    return [lax.empty((FF_PAD, d), BF16) for _ in range(3)]


def _scatter_lands(which, grads):
    if which == "a":
        g_in, g_out = grads
        return [lax.empty((N_DEV, g_in.shape[0], W_IN_WINDOW), BF16),
                lax.empty((N_DEV, g_out.shape[0] // N_DEV, g_out.shape[1]), BF16)]
    return [lax.empty((N_DEV, FF_BLK, g.shape[1]), BF16) for g in grads]


def _pad_w_in(full):
    return jnp.pad(full, ((0, 0), (0, NPROJ - IN_DIM)))


def kernel(x, norm_mix, w_in, q_gain, k_gain, conv_w, conv_b, dt_bias, a_log, d_skip, attn_out_gain, ssm_out_gain, w_out, norm_ffn, w_gate, w_up, w_down, loss_target, m_norm_mix, m_w_in, m_q_gain, m_k_gain, m_conv_w, m_conv_b, m_dt_bias, m_a_log, m_d_skip, m_attn_out_gain, m_ssm_out_gain, m_w_out, m_norm_ffn, m_w_gate, m_w_up, m_w_down, v_norm_mix, v_w_in, v_q_gain, v_k_gain, v_conv_w, v_conv_b, v_dt_bias, v_a_log, v_d_skip, v_attn_out_gain, v_ssm_out_gain, v_w_out, v_norm_ffn, v_w_gate, v_w_up, v_w_down):
    nb, seq, d = x.shape
    depth = w_in.shape[0]
    t = nb * seq
    w = dict(norm_mix=norm_mix, w_in=w_in, q_gain=q_gain, k_gain=k_gain, conv_w=conv_w, conv_b=conv_b,
             dt_bias=dt_bias, a_log=a_log, d_skip=d_skip, attn_out_gain=attn_out_gain, ssm_out_gain=ssm_out_gain,
             w_out=w_out, norm_ffn=norm_ffn, w_gate=w_gate, w_up=w_up, w_down=w_down)
    mom = dict(norm_mix=m_norm_mix, w_in=m_w_in, q_gain=m_q_gain, k_gain=m_k_gain, conv_w=m_conv_w, conv_b=m_conv_b,
               dt_bias=m_dt_bias, a_log=m_a_log, d_skip=m_d_skip, attn_out_gain=m_attn_out_gain,
               ssm_out_gain=m_ssm_out_gain, w_out=m_w_out, norm_ffn=m_norm_ffn, w_gate=m_w_gate, w_up=m_w_up,
               w_down=m_w_down)
    var = dict(norm_mix=v_norm_mix, w_in=v_w_in, q_gain=v_q_gain, k_gain=v_k_gain, conv_w=v_conv_w, conv_b=v_conv_b,
               dt_bias=v_dt_bias, a_log=v_a_log, d_skip=v_d_skip, attn_out_gain=v_attn_out_gain,
               ssm_out_gain=v_ssm_out_gain, w_out=v_w_out, norm_ffn=v_norm_ffn, w_gate=v_w_gate, w_up=v_w_up,
               w_down=v_w_down)
    ff = w_gate.shape[2]

    (conv_all,) = _all_gather([conv_w], "gather_conv")

    def shards_a(li):
        return [w_in[li].astype(BF16), w_out[li].astype(BF16)]

    w["w_gate"], mom["w_gate"], var["w_gate"] = (jnp.swapaxes(a, 1, 2) for a in (w_gate, m_w_gate, v_w_gate))
    w["w_up"], mom["w_up"], var["w_up"] = (jnp.swapaxes(a, 1, 2) for a in (w_up, m_w_up, v_w_up))

    def shards_b(li):
        return [jnp.pad(w[k][li].astype(BF16), ((0, FF_BLK - ff), (0, 0))) for k in ("w_gate", "w_up", "w_down")]

    def small_params(li):
        p = {k: w[k][li][None, :] for k in ("norm_mix", "q_gain", "k_gain", "attn_out_gain", "ssm_out_gain", "norm_ffn")}
        conv_full = conv_all[:, li].transpose(1, 0, 2).reshape(CONV_K, CONV_DIM)
        p["cw8"] = _pack_conv(conv_full, conv_b[li])
        p["hp"] = _pack_heads(dt_bias[li], a_log[li], d_skip[li])
        return p

    xc = x.reshape(t, d)
    cur = shards_a(0)
    flight = _gather2_start(cur, _gather_lands("a", cur, d), GATHER_A, "gather_a0")
    flight = _gather2_forward(flight, xc, "gather_a0_fwd")
    lands_a = _gather2_wait(flight, flight["token"], "gather_a0_wait")
    layers, saved = [], []
    for li in range(depth):
        tag = f"l{li}_"
        p = small_params(li)
        p["w_in"] = _pad_w_in(_full_cols(lands_a[0]))
        p["w_out"] = lands_a[1]
        cur = shards_b(li)
        flight = _gather2_start(cur, _gather_lands("b", cur, d), GATHER_B, tag + "gather_b", after=lands_a[1])
        h1 = _rms_fwd(xc, p["norm_mix"], tag + "rms1", after=flight["token"])
        proj = _matmul(h1, p["w_in"], "nn", F32, tag + "mm_in")
        o = _attn_fwd(proj, p["q_gain"], p["k_gain"], nb, seq, tag + "attn")
        flight = _gather2_forward(flight, o, tag + "gather_b_fwd")
        act = _conv_fwd(proj, p["cw8"], nb, seq, tag + "conv", after=flight["token"])
        y = _ssd_fwd(act, proj, p["hp"], nb, seq, tag + "ssd")
        cat = _mix_fwd(o, y, proj, p["attn_out_gain"], p["ssm_out_gain"], tag + "mix")
        x1 = _matmul(cat, p["w_out"], "nn", F32, tag + "mm_out", residual=xc)
        p["w_gate"], p["w_up"], p["w_down"] = _gather2_wait(flight, x1, tag + "gather_b_wait")
        token = None
        if li + 1 < depth:
            nxt = shards_a(li + 1)
            flight = _gather2_start(nxt, _gather_lands("a", nxt, d), GATHER_A, f"gather_a{li + 1}",
                                    after=p["w_down"])
            token = flight["token"]
        h2 = _rms_fwd(x1, p["norm_ffn"], tag + "rms2", after=token)
        a, gate, up = _mm_swiglu(h2, p["w_gate"], p["w_up"], tag + "mm_gu")
        token = None
        if li + 1 < depth:
            flight = _gather2_forward(flight, gate, f"gather_a{li + 1}_fwd")
            token = flight["token"]
        x2 = _matmul(a, p["w_down"], "nn", F32, tag + "mm_down", residual=x1, after=token)
        if li + 1 < depth:
            lands_a = _gather2_wait(flight, x2, f"gather_a{li + 1}_wait")
        saved.append(dict(x=xc, h1=h1, proj=proj, o=o, act=act, y=y, cat=cat, x1=x1, h2=h2, gate=gate, up=up, a=a))
        layers.append(p)
        xc = x2

    loss_blk, dx, dxb = _loss_fwd_bwd(xc, loss_target.reshape(t, d), "loss")
    loss = lax.psum(loss_blk[0, 0], ("x", "y", "c"))

    grads = [dict() for _ in range(depth)]
    recv = [dict() for _ in range(depth)]
    flight_a, token = None, None
    for li in reversed(range(depth)):
        tag = f"l{li}_b_"
        p, s, g = layers[li], saved[li], grads[li]
        dgate, dup = _mm_dact_swiglu(dxb, p["w_down"], s["gate"], s["up"], tag + "mm_dact", after=token)
        g_down = _matmul(s["a"], dxb, "tn", BF16, tag + "mm_dwd")
        dh2 = _matmul(dgate, p["w_gate"], "nn", F32, tag + "mm_dh2g")
        dh2 = _matmul(dup, p["w_up"], "nn", F32, tag + "mm_dh2u", residual=dh2)
        g_gate = _matmul(dgate, s["h2"], "tn", BF16, tag + "mm_dwg")
        g_up = _matmul(dup, s["h2"], "tn", BF16, tag + "mm_dwu")
        if flight_a is not None:
            recv[li + 1]["w_in"], recv[li + 1]["w_out"] = _copies_wait(flight_a, g_up, f"l{li + 1}_b_scatter_a_wait")
        grads_b = [g_gate, g_up, g_down]
        flight_b = _copies_start(grads_b, _scatter_lands("b", grads_b), SCATTER_B,
                                 tag + "scatter_b", after=recv[li + 1]["w_out"] if li + 1 < depth else None)
        dx1, dx1b, g["norm_ffn"] = _rms_bwd(s["x1"], p["norm_ffn"], dh2, dx, tag + "rms2")
        dcat = _matmul(dx1b, p["w_out"], "nt", F32, tag + "mm_dcat", after=flight_b["token"])
        g_out = _matmul(s["cat"], dx1b, "tn", BF16, tag + "mm_dwo")
        do, dy, dz, g["attn_out_gain"], g["ssm_out_gain"] = _mix_bwd(
            dcat, s["o"], s["y"], s["proj"], p["attn_out_gain"], p["ssm_out_gain"], tag + "mix")
        dq, dk, dv, g["q_gain"], g["k_gain"] = _attn_bwd(s["proj"], do, p["q_gain"], p["k_gain"], nb, seq, tag + "attn")
        dxa, dba, dca, ddt, dhp = _ssd_bwd(s["act"], s["proj"], dy, p["hp"], nb, seq, tag + "ssd")
        dxbc, dcw8 = _conv_bwd(s["proj"], jnp.concatenate([dxa, dba, dca], axis=1), p["cw8"], nb, seq, tag + "conv")
        g["conv_w"] = dcw8[0:CONV_K]
        g["conv_b"] = dcw8[CONV_K:CONV_K + 1]
        heads = dhp[:, 0:3, 0:SSM_HG].transpose(1, 0, 2).reshape(3, SSM_HEADS)
        g["dt_bias"], g["a_log"], g["d_skip"] = heads[0:1], heads[1:2], heads[2:3]
        ddt = ddt[:, :LANE] + jnp.roll(ddt[:, LANE:], SSM_HG, axis=1)
        tail = jnp.zeros((t, NPROJ - OFF_DT - LANE), BF16)
        dproj = jnp.concatenate([dq, dk, dv, dz, dxbc, ddt, tail], axis=1)
        recv[li]["w_gate"], recv[li]["w_up"], recv[li]["w_down"] = _copies_wait(flight_b, dproj, tag + "scatter_b_wait")
        dh1 = _matmul(dproj, p["w_in"], "nt", F32, tag + "mm_dh1")
        g_in = _matmul(s["h1"], dproj, "tn", BF16, tag + "mm_dwin")
        flight_a = _copies_start([g_in, g_out], _scatter_lands("a", [g_in, g_out]), SCATTER_A, tag + "scatter_a")
        token = flight_a["token"]
        dx, dxb, g["norm_mix"] = _rms_bwd(s["x"], p["norm_mix"], dh1, dx1, tag + "rms1")
    grad_x = dx.reshape(nb, seq, d)

    out_g, out_d, out_m, out_v = {}, {}, {}, {}

    def update(k, rows, lane_offset=None):
        parts = [recv[li][k] for li in range(depth)]
        out_g[k], out_d[k], out_m[k], out_v[k] = _adamw(parts, w[k], mom[k], var[k], "adamw_" + k, rows, lane_offset)

    def w_in_offset():
        return _my_block() * W_IN_COLS - _w_in_window_start(_my_block())

    update("w_gate", 64)
    update("w_up", 64)
    update("w_down", 64)
    recv[0]["w_in"], recv[0]["w_out"] = _copies_wait(
        flight_a, [out_g["w_gate"], out_g["w_up"], out_g["w_down"], dx], "l0_b_scatter_a_wait")
    update("w_in", 128, w_in_offset)
    update("w_out", 128)
    for res in (out_g, out_d, out_m, out_v):
        res["w_gate"], res["w_up"] = jnp.swapaxes(res["w_gate"], 1, 2), jnp.swapaxes(res["w_up"], 1, 2)

    small_g = [jnp.concatenate([grads[li][k] for li in range(depth)], axis=0) for k in SMALL]
    conv_g = jnp.stack([grads[li]["conv_w"] for li in range(depth)]).reshape(depth, CONV_K * CONV_DIM)
    parts = _all_gather(small_g + [conv_g], "gather_small_grads")
    res = _adamw_small(parts[:-1], [w[k] for k in SMALL], [mom[k] for k in SMALL], [var[k] for k in SMALL],
                       "adamw_small")
    for dst, vals in zip((out_g, out_d, out_m, out_v), res):
        dst.update(dict(zip(SMALL, vals)))
    conv_total = _sum_parts(parts[-1], "sum_conv_grads").reshape(depth, CONV_K, CONV_DIM)
    cshard = conv_w.shape[2]
    conv_mine = lax.dynamic_slice_in_dim(conv_total, _my_block() * cshard, cshard, axis=2)
    flat = lambda a: a.reshape(depth, CONV_K * cshard)
    res = _adamw_small([flat(conv_mine)[None]], [flat(conv_w)], [flat(m_conv_w)], [flat(v_conv_w)], "adamw_conv")
    for dst, vals in zip((out_g, out_d, out_m, out_v), res):
        dst["conv_w"] = vals[0].reshape(depth, CONV_K, cshard)

    names = ("norm_mix", "w_in", "q_gain", "k_gain", "conv_w", "conv_b", "dt_bias", "a_log", "d_skip",
             "attn_out_gain", "ssm_out_gain", "w_out", "norm_ffn", "w_gate", "w_up", "w_down")
    return (loss, grad_x, *[out_g[k] for k in names], *[out_d[k] for k in names],
            *[out_m[k] for k in names], *[out_v[k] for k in names])
```
